```python
import jax
import jax.numpy as jnp
from jax import lax
import numpy as np

D_MODEL = 1024
BATCH = 16
SEQ = 2048
DEPTH = 1

GRID_W = 64
CTX_LEN = 256
EPS = 1e-6

D_MIX = D_MODEL
D_A = D_MIX // 2
D_B = D_MIX - D_A
CHUNK_A = 128
ROWS_PER_CHUNK = CHUNK_A // GRID_W
HEADS_A = 4
HEAD_DIM_A = D_A // HEADS_A
HEADS_B = 4
HEAD_DIM_B = D_B // HEADS_B
CONV_W = 5
DELTA_CHUNK = 64
N_DIR = 2
N_IN = 2 * D_A + 4 * D_B + 2 * N_DIR * HEADS_B
N_GROUPS = 4
EXPERTS_PER_GROUP = 4
N_EXPERTS = N_GROUPS * EXPERTS_PER_GROUP
TOP_K = 2
D_EXPERT = 512
MOE_BLOCK = 128

kernel_name = 'hymba_chunkmlp_gdn_hmoe_dit'


def rmsnorm(x, g):
    xf = x.astype(jnp.float32)
    y = xf * lax.rsqrt(jnp.mean(xf * xf, axis=-1, keepdims=True) + EPS)
    return (y * g).astype(x.dtype)


def layernorm(x, g, b):
    xf = x.astype(jnp.float32)
    mu = jnp.mean(xf, axis=-1, keepdims=True)
    var = jnp.mean(jnp.square(xf - mu), axis=-1, keepdims=True)
    return ((xf - mu) * lax.rsqrt(var + EPS) * g + b).astype(x.dtype)


def l2norm(x):
    return x * lax.rsqrt(jnp.sum(x * x, axis=-1, keepdims=True) + EPS)


def modulation(cond, w_ada, b_ada):
    m = jax.nn.silu(cond) @ w_ada + b_ada
    return jnp.split(m[..., None, :], 6, axis=-1)


def centred_dwconv(x, w):
    pad = (CONV_W - 1) // 2
    return lax.conv_general_dilated(
        x, w[:, None, :].astype(x.dtype), window_strides=(1,),
        padding=[(pad, CONV_W - 1 - pad)],
        dimension_numbers=('NWC', 'WIO', 'NWC'),
        feature_group_count=x.shape[-1])


def chunk_mlp_mixer(p, n_chunks, ln_g, ln_b, w_s, b_s):
    B_, L_, _ = p.shape
    u, v = jnp.split(jax.nn.gelu(p), 2, axis=-1)
    v = layernorm(v, ln_g, ln_b)
    vc = v.reshape(B_, n_chunks, CHUNK_A, HEADS_A, HEAD_DIM_A)
    s = jnp.einsum('hij,bnjhd->bnihd', w_s, vc) + b_s.T[None, None, :, :, None]
    return u * s.reshape(B_, L_, D_A)


def chunk_gated_delta(q, k, v, g, beta, s0):
    R, L_, DK = k.shape
    DV = v.shape[-1]
    C = DELTA_CHUNK
    N = L_ // C
    q = q.reshape(R, N, C, DK)
    k = k.reshape(R, N, C, DK)
    v = v.reshape(R, N, C, DV)
    beta = beta.reshape(R, N, C)
    gc = jnp.cumsum(g.reshape(R, N, C), axis=-1)
    incl = jnp.tril(jnp.ones((C, C), dtype=bool))
    strict = jnp.tril(jnp.ones((C, C), dtype=bool), -1)
    diff = gc[..., :, None] - gc[..., None, :]
    decay = jnp.where(incl, jnp.exp(jnp.where(incl, diff, 0.0)), 0.0)
    kb = k * beta[..., None]
    lmat = jnp.where(strict, jnp.einsum('rnid,rnjd->rnij', kb, k) * decay, 0.0)
    amat = lmat + jnp.eye(C, dtype=jnp.float32)
    rhs = jnp.concatenate([v * beta[..., None], kb * jnp.exp(gc)[..., None]], axis=-1)
    sol = lax.linalg.triangular_solve(amat, rhs, left_side=True, lower=True, unit_diagonal=True)
    u, w = sol[..., :DV], sol[..., DV:]
    qk = jnp.where(incl, jnp.einsum('rnid,rnjd->rnij', q, k) * decay, 0.0)
    qd = q * jnp.exp(gc)[..., None]
    kd = k * jnp.exp(gc[..., -1:] - gc)[..., None]
    glast = jnp.exp(gc[..., -1])

    def step(s, xs):
        qk_c, qd_c, w_c, u_c, kd_c, gl_c = xs
        v_new = u_c - jnp.einsum('rcd,rde->rce', w_c, s)
        o_c = jnp.einsum('rcd,rde->rce', qd_c, s) + jnp.einsum('rcj,rje->rce', qk_c, v_new)
        s = s * gl_c[:, None, None] + jnp.einsum('rcd,rce->rde', kd_c, v_new)
        return s, o_c

    xs = tuple(jnp.moveaxis(t, 1, 0) for t in (qk, qd, w, u, kd, glast))
    s_final, o = lax.scan(step, s0, xs)
    return jnp.moveaxis(o, 0, 1).reshape(R, L_, DV), s_final


def delta_mixer(p, s0, conv_w, a_log, dt_bias, onorm_g):
    B_, L_, _ = p.shape
    qkv = jax.nn.silu(centred_dwconv(p[..., :3 * D_B], conv_w))
    z = p[..., 3 * D_B:4 * D_B]
    off = 4 * D_B
    a = p[..., off:off + N_DIR * HEADS_B].astype(jnp.float32).reshape(B_, L_, N_DIR, HEADS_B)
    b = p[..., off + N_DIR * HEADS_B:].astype(jnp.float32).reshape(B_, L_, N_DIR, HEADS_B)

    def to_heads(t):
        return t.astype(jnp.float32).reshape(B_, L_, HEADS_B, HEAD_DIM_B).transpose(0, 2, 1, 3)

    q, k, v = jnp.split(qkv, 3, axis=-1)
    q = l2norm(to_heads(q)) * HEAD_DIM_B ** -0.5
    k = l2norm(to_heads(k))
    v = to_heads(v)
    g = (-jnp.exp(a_log) * jax.nn.softplus(a + dt_bias)).transpose(2, 0, 3, 1)
    beta = jax.nn.sigmoid(b).transpose(2, 0, 3, 1)

    def bidir(t):
        return jnp.stack([t, jnp.flip(t, axis=2)]).reshape((N_DIR * B_ * HEADS_B,) + t.shape[2:])

    def per_dir(t):
        return jnp.stack([t[0], jnp.flip(t[1], axis=-1)]).reshape(N_DIR * B_ * HEADS_B, L_)

    o, s_final = chunk_gated_delta(bidir(q), bidir(k), bidir(v), per_dir(g), per_dir(beta), s0)
    o = o.reshape(N_DIR, B_, HEADS_B, L_, HEAD_DIM_B)
    o = o[0] + jnp.flip(o[1], axis=2)
    y = rmsnorm(o, onorm_g) * jax.nn.silu(to_heads(z))
    return y.transpose(0, 2, 1, 3).reshape(B_, L_, D_B).astype(p.dtype), s_final


def hier_moe(h, w_group, b_group, w_router, b_router, w_gate_up, w_down):
    B_, L_, D = h.shape
    t = h.reshape(-1, D)
    T = t.shape[0]
    glog = (t @ w_group + b_group).astype(jnp.float32)
    gprob = jax.nn.softmax(glog, axis=-1)
    g_sel = jnp.argmax(glog, axis=-1)
    p_g = jnp.max(gprob, axis=-1, keepdims=True)
    elog = (t @ w_router + b_router).astype(jnp.float32).reshape(T, N_GROUPS, EXPERTS_PER_GROUP)
    idx = jnp.broadcast_to(g_sel[:, None, None], (T, 1, EXPERTS_PER_GROUP))
    elog_sel = jnp.take_along_axis(elog, idx, axis=1)[:, 0]
    top_p, top_i = lax.top_k(jax.nn.softmax(elog_sel, axis=-1), TOP_K)
    top_p = top_p / jnp.sum(top_p, axis=-1, keepdims=True) * p_g
    expert_idx = g_sel[:, None] * EXPERTS_PER_GROUP + top_i
    combine = jnp.sum(jax.nn.one_hot(expert_idx, N_EXPERTS, dtype=jnp.float32) * top_p[..., None], axis=1)

    def block(args):
        tb, cb = args
        gate, up = jnp.split(jnp.einsum('td,edf->tef', tb, w_gate_up), 2, axis=-1)
        act = jax.nn.silu(gate) * up * cb[..., None].astype(tb.dtype)
        return jnp.einsum('tef,efd->td', act, w_down)

    out = lax.map(block, (t.reshape(-1, MOE_BLOCK, D), combine.reshape(-1, MOE_BLOCK, N_EXPERTS)))
    return out.reshape(B_, L_, D)


def setup_inputs(seed: int = 0) -> dict:
    key = jax.random.key(seed)
    ks = jax.random.split(key, 26)
    f32 = jnp.float32

    def nrm(k, shape, scale):
        return scale * jax.random.normal(k, shape, f32)

    dt = jnp.exp(jax.random.uniform(ks[14], (DEPTH, N_DIR, HEADS_B), f32,
                                    float(np.log(1e-3)), float(np.log(1e-1))))
    return {
        'x': nrm(ks[0], (BATCH, SEQ, D_MODEL), 1.0),
        'c': nrm(ks[1], (BATCH, D_MODEL), 1.0),
        'ctx': nrm(ks[2], (BATCH, CTX_LEN, D_MODEL), 1.0),
        'c_ctx': nrm(ks[3], (D_MODEL,), 1.0),
        'w_ada': nrm(ks[4], (DEPTH, D_MODEL, 6 * D_MODEL), 0.5 * D_MODEL ** -0.5),
        'b_ada': nrm(ks[5], (DEPTH, 6 * D_MODEL), 0.01),
        'norm1_g': 1.0 + nrm(ks[6], (DEPTH, D_MODEL), 0.01),
        'w_in': nrm(ks[7], (DEPTH, D_MODEL, N_IN), D_MODEL ** -0.5),
        'ln_a_g': 1.0 + nrm(ks[8], (DEPTH, D_A), 0.01),
        'ln_a_b': nrm(ks[9], (DEPTH, D_A), 0.01),
        'w_spatial': nrm(ks[10], (DEPTH, HEADS_A, CHUNK_A, CHUNK_A), CHUNK_A ** -0.5),
        'b_spatial': 1.0 + nrm(ks[11], (DEPTH, HEADS_A, CHUNK_A), 0.01),
        'conv_qkv': nrm(ks[12], (DEPTH, CONV_W, 3 * D_B), CONV_W ** -0.5),
        'a_log': jnp.log(jax.random.uniform(ks[13], (DEPTH, N_DIR, HEADS_B), f32, 1.0, 16.0)),
        'dt_bias': dt + jnp.log(-jnp.expm1(-dt)),
        'onorm_g': 1.0 + nrm(ks[15], (DEPTH, HEAD_DIM_B), 0.01),
        'w_out': nrm(ks[16], (DEPTH, D_MIX, D_MODEL), D_MIX ** -0.5),
        'norm2_g': 1.0 + nrm(ks[17], (DEPTH, D_MODEL), 0.01),
        'w_group': nrm(ks[18], (DEPTH, D_MODEL, N_GROUPS), D_MODEL ** -0.5),
        'b_group': nrm(ks[19], (DEPTH, N_GROUPS), 0.01),
        'w_router': nrm(ks[20], (DEPTH, D_MODEL, N_EXPERTS), D_MODEL ** -0.5),
        'b_router': nrm(ks[21], (DEPTH, N_EXPERTS), 0.01),
        'w_gate_up': nrm(ks[22], (DEPTH, N_EXPERTS, D_MODEL, 2 * D_EXPERT), D_MODEL ** -0.5),
        'w_down': nrm(ks[23], (DEPTH, N_EXPERTS, D_EXPERT, D_MODEL), D_EXPERT ** -0.5),
        'final_g': 1.0 + nrm(ks[24], (D_MODEL,), 0.01),
    }


def reference(x, c, ctx, c_ctx, w_ada, b_ada, norm1_g, w_in, ln_a_g, ln_a_b, w_spatial, b_spatial,
              conv_qkv, a_log, dt_bias, onorm_g, w_out, norm2_g, w_group, b_group, w_router, b_router,
              w_gate_up, w_down, final_g):
    B_, L_, _ = x.shape
    ROWS = L_ // GRID_W
    n_lat_chunks = ROWS // ROWS_PER_CHUNK
    n_ctx_chunks = ctx.shape[1] // CHUNK_A
    s_zero = jnp.zeros((N_DIR * B_ * HEADS_B, HEAD_DIM_B, HEAD_DIM_B), jnp.float32)
    h_ctx = ctx
    for l in range(DEPTH):
        last = l == DEPTH - 1
        sh1, sc1, gt1, sh2, sc2, gt2 = modulation(c, w_ada[l], b_ada[l])
        csh1, csc1, cgt1, csh2, csc2, cgt2 = modulation(c_ctx, w_ada[l], b_ada[l])

        p_lat = (rmsnorm(x, norm1_g[l]) * (1.0 + sc1) + sh1) @ w_in[l]
        p_ctx = (rmsnorm(h_ctx, norm1_g[l]) * (1.0 + csc1) + csh1) @ w_in[l]
        yb_ctx, s_ctx = delta_mixer(p_ctx[..., 2 * D_A:], s_zero, conv_qkv[l], a_log[l], dt_bias[l], onorm_g[l])
        yb_lat, _ = delta_mixer(p_lat[..., 2 * D_A:], s_ctx, conv_qkv[l], a_log[l], dt_bias[l], onorm_g[l])
        ya_lat = chunk_mlp_mixer(p_lat[..., :2 * D_A], n_lat_chunks, ln_a_g[l], ln_a_b[l], w_spatial[l], b_spatial[l])
        x = x + gt1 * (jnp.concatenate([ya_lat, yb_lat], axis=-1) @ w_out[l])
        if not last:
            ya_ctx = chunk_mlp_mixer(p_ctx[..., :2 * D_A], n_ctx_chunks, ln_a_g[l], ln_a_b[l], w_spatial[l], b_spatial[l])
            h_ctx = h_ctx + cgt1 * (jnp.concatenate([ya_ctx, yb_ctx], axis=-1) @ w_out[l])

        x = x + gt2 * hier_moe(rmsnorm(x, norm2_g[l]) * (1.0 + sc2) + sh2, w_group[l], b_group[l],
                               w_router[l], b_router[l], w_gate_up[l], w_down[l])
        if not last:
            h_ctx = h_ctx + cgt2 * hier_moe(rmsnorm(h_ctx, norm2_g[l]) * (1.0 + csc2) + csh2, w_group[l], b_group[l],
                                            w_router[l], b_router[l], w_gate_up[l], w_down[l])
    return rmsnorm(x, final_g)
```

```python
import functools

import jax
import jax.numpy as jnp
import numpy as np
from jax import lax
from jax.experimental import pallas as pl
from jax.experimental.pallas import tpu as pltpu

F32 = jnp.float32
BF16 = jnp.bfloat16
EPS = 1e-6

D = 1024
D_A = 512
D_B = 512
HEADS = 4
HD = 128
CHUNK = 128
CONV_W = 5
N_QKV = 3 * D_B
N_MAIN = 2 * D_A + 4 * D_B
N_AB = 16
N_GROUPS = 4
EPG = 4
N_EXPERTS = 16
D_EXPERT = 512
N_PAIRS = 6
N_BUCKETS = N_GROUPS * N_PAIRS
PAIR_A = (0, 0, 0, 1, 1, 2)
PAIR_B = (1, 2, 3, 2, 3, 3)

TM_IN = 256
HALO = 8
TM_OUT = 512
BM_MOE = 256
TM_FIN = 256
VMEM_LIMIT = 56 * 1024 * 1024

HI = lax.Precision.HIGHEST


def _dot(a, b, precision=None):
    return jnp.dot(a, b, preferred_element_type=F32, precision=precision)


def _dot_nt(a, b):
    return lax.dot_general(a, b, (((1,), (1,)), ((), ())), preferred_element_type=F32)


def _dot_tn(a, b):
    return lax.dot_general(a, b, (((0,), (0,)), ((), ())), preferred_element_type=F32)


def _sigmoid(x):
    return 1.0 / (1.0 + jnp.exp(-x))


def _silu(x):
    return x * _sigmoid(x)


def _softplus(x):
    return jnp.maximum(x, 0.0) + jnp.log(1.0 + jnp.exp(-jnp.abs(x)))


def _gelu_tanh(x):
    return 0.5 * x * (1.0 + jnp.tanh(np.sqrt(2.0 / np.pi).astype(np.float32) * (x + 0.044715 * (x * x * x))))


def _mod_kernel(c_ref, w_ref, b_ref, o_ref):
    c = c_ref[...]
    o_ref[...] = _dot(_silu(c), w_ref[...], precision=HI) + b_ref[...]


def _modulation(cond, w_ada, b_ada):
    rows = cond.shape[0]
    tn = 1536
    return pl.pallas_call(
        _mod_kernel,
        out_shape=jax.ShapeDtypeStruct((rows, 6 * D), F32),
        grid=(6 * D // tn,),
        in_specs=[pl.BlockSpec((rows, D), lambda i: (0, 0)),
                  pl.BlockSpec((D, tn), lambda i: (0, i)),
                  pl.BlockSpec((1, tn), lambda i: (0, i))],
        out_specs=pl.BlockSpec((rows, tn), lambda i: (0, i)),
        compiler_params=pltpu.CompilerParams(dimension_semantics=("arbitrary",),
                                             vmem_limit_bytes=VMEM_LIMIT),
    )(cond, w_ada, b_ada)


def _in_kernel(x_ref, xp_ref, xn_ref, ctx_ref, mod_ref, cmod_ref, g1_ref, w_ref, wab_ref, wabt_ref,
               lng_ref, lnb_ref, ws_ref, bst_ref, conv_ref, alog_ref, dtb_ref, alogt_ref, dtbt_ref,
               ya_ref, qkv_ref, z_ref, gb_ref, gbt_ref, ext_ref):
    j = pl.program_id(1)
    is_ctx = j == 0
    n_lat_blocks = pl.num_programs(1) - 1
    mod = mod_ref[0]
    cm = cmod_ref[...]
    sh = jnp.where(is_ctx, cm[:, 0:D], mod[:, 0:D])
    sc = jnp.where(is_ctx, cm[:, D:2 * D], mod[:, D:2 * D])
    scale = g1_ref[...] * (1.0 + sc)

    def normmod(xv):
        ms = jnp.mean(xv * xv, axis=-1, keepdims=True)
        return (xv * lax.rsqrt(ms + EPS) * scale + sh).astype(BF16)

    xmain = jnp.where(is_ctx, ctx_ref[0], x_ref[0])
    xe = normmod(jnp.concatenate([xp_ref[0], xmain, xn_ref[0]], axis=0))
    xb = normmod(xmain)

    pa = _dot(xe, w_ref[:, 0:2 * D_A])[HALO:HALO + TM_IN]
    ga = _gelu_tanh(pa)
    u = ga[:, :D_A]
    v = ga[:, D_A:]
    mu = jnp.mean(v, axis=-1, keepdims=True)
    vc = v - mu
    var = jnp.mean(vc * vc, axis=-1, keepdims=True)
    vn = (vc * lax.rsqrt(var + EPS) * lng_ref[...] + lnb_ref[...]).astype(BF16)
    bst = bst_ref[...]
    for n in range(TM_IN // CHUNK):
        rows = slice(n * CHUNK, (n + 1) * CHUNK)
        for h in range(HEADS):
            cols = slice(h * HD, (h + 1) * HD)
            s = _dot(ws_ref[h], vn[rows, cols]) + bst[:, h:h + 1]
            ya_ref[0, rows, cols] = (u[rows, cols] * s).astype(BF16)

    pq = _dot(xe, w_ref[:, 2 * D_A:2 * D_A + N_QKV])
    rid = lax.broadcasted_iota(jnp.int32, (TM_IN + 2 * HALO, 1), 0)
    prev_ok = j >= 2
    next_ok = jnp.logical_and(j >= 1, j < n_lat_blocks)
    valid = jnp.logical_or(jnp.logical_and(rid >= HALO, rid < HALO + TM_IN),
                           jnp.logical_or(jnp.logical_and(rid < HALO, prev_ok),
                                          jnp.logical_and(rid >= HALO + TM_IN, next_ok)))
    ext_ref[...] = jnp.where(valid, pq, 0.0)
    pad = (CONV_W - 1) // 2
    acc = None
    for t in range(CONV_W):
        term = conv_ref[t:t + 1, :] * ext_ref[pl.ds(HALO - pad + t, TM_IN), :]
        acc = term if acc is None else acc + term
    act = _silu(acc)
    for h in range(2 * HEADS):
        cols = slice(h * HD, (h + 1) * HD)
        t = act[:, cols]
        nrm = lax.rsqrt(jnp.sum(t * t, axis=-1, keepdims=True) + EPS)
        if h < HEADS:
            nrm = nrm * (HD ** -0.5)
        qkv_ref[0, :, cols] = (t * nrm).astype(BF16)
    qkv_ref[0, :, 2 * D_B:] = act[:, 2 * D_B:].astype(BF16)

    z_ref[0] = _dot(xe, w_ref[:, 2 * D_A + N_QKV:N_MAIN])[HALO:HALO + TM_IN].astype(BF16)

    r2 = lax.broadcasted_iota(jnp.int32, (TM_IN, TM_IN), 0)
    c2 = lax.broadcasted_iota(jnp.int32, (TM_IN, TM_IN), 1)
    same = (r2 // CHUNK) == (c2 // CHUNK)
    tri_l = jnp.where(jnp.logical_and(same, c2 <= r2), 1.0, 0.0).astype(F32)
    tri_u = jnp.where(jnp.logical_and(same, c2 >= r2), 1.0, 0.0).astype(F32)

    ab = _dot(xb, wab_ref[...])
    g_tok = -jnp.exp(alog_ref[...]) * _softplus(ab + dtb_ref[...])
    lane = lax.broadcasted_iota(jnp.int32, ab.shape, 1)
    gb = jnp.where(lane < HEADS, _dot(tri_l, g_tok, precision=HI),
                   jnp.where(lane < 2 * HEADS, _dot(tri_u, g_tok, precision=HI), _sigmoid(ab)))
    gb_ref[0] = gb[:, 0:N_AB]

    abt = _dot_nt(wabt_ref[...], xb)
    g_t = -jnp.exp(alogt_ref[...]) * _softplus(abt + dtbt_ref[...])
    row = lax.broadcasted_iota(jnp.int32, abt.shape, 0)
    gbt_ref[0] = jnp.where(row < HEADS, _dot(g_t, tri_u, precision=HI),
                           jnp.where(row < 2 * HEADS, _dot(g_t, tri_l, precision=HI), _sigmoid(abt)))


def _in_proj(x, ctx, mod_lat, mod_ctx, g1, w_main, w_ab, w_abt, lng, lnb, ws, bst, conv, alog, dtb, alogt, dtbt):
    B, L, _ = x.shape
    n_lat = L // TM_IN
    n_steps = n_lat + 1
    LC = L + TM_IN
    hb = TM_IN // HALO

    def full(shape):
        return pl.BlockSpec(shape, lambda b, j: (0,) * len(shape))

    in_specs = [
        pl.BlockSpec((1, TM_IN, D), lambda b, j: (b, jnp.maximum(j - 1, 0), 0)),
        pl.BlockSpec((1, HALO, D), lambda b, j: (b, jnp.clip((j - 1) * hb - 1, 0, L // HALO - 1), 0)),
        pl.BlockSpec((1, HALO, D), lambda b, j: (b, jnp.clip(j * hb, 0, L // HALO - 1), 0)),
        pl.BlockSpec((1, TM_IN, D), lambda b, j: (b, 0, 0)),
        pl.BlockSpec((1, 1, 6 * D), lambda b, j: (b, 0, 0)),
        full((1, 6 * D)), full((1, D)), full((D, N_MAIN)), full((D, 128)), full((N_AB, D)),
        full((1, D_A)), full((1, D_A)), full((HEADS, CHUNK, CHUNK)), full((CHUNK, HEADS)),
        full((CONV_W, N_QKV)), full((1, 128)), full((1, 128)), full((N_AB, 1)), full((N_AB, 1)),
    ]
    out_shape = [
        jax.ShapeDtypeStruct((B, L, D_A), BF16),
        jax.ShapeDtypeStruct((B, LC, N_QKV), BF16),
        jax.ShapeDtypeStruct((B, LC, D_B), BF16),
        jax.ShapeDtypeStruct((B, LC, N_AB), F32),
        jax.ShapeDtypeStruct((B, N_AB, LC), F32),
    ]
    out_specs = [
        pl.BlockSpec((1, TM_IN, D_A), lambda b, j: (b, jnp.maximum(j - 1, 0), 0)),
        pl.BlockSpec((1, TM_IN, N_QKV), lambda b, j: (b, j, 0)),
        pl.BlockSpec((1, TM_IN, D_B), lambda b, j: (b, j, 0)),
        pl.BlockSpec((1, TM_IN, N_AB), lambda b, j: (b, j, 0)),
        pl.BlockSpec((1, N_AB, TM_IN), lambda b, j: (b, 0, j)),
    ]
    return pl.pallas_call(
        _in_kernel,
        out_shape=out_shape,
        grid=(B, n_steps),
        in_specs=in_specs,
        out_specs=out_specs,
        scratch_shapes=[pltpu.VMEM((TM_IN + 2 * HALO, N_QKV), F32)],
        compiler_params=pltpu.CompilerParams(dimension_semantics=("arbitrary", "arbitrary"),
                                             vmem_limit_bytes=VMEM_LIMIT),
    )(x, x, x, ctx, mod_lat, mod_ctx, g1, w_main, w_ab, w_abt, lng, lnb, ws, bst, conv, alog, dtb, alogt, dtbt)


def _unit_tri_inverse(lmat, same_blk, eye):
    def mm(a, b):
        return _dot(a.astype(BF16), b.astype(BF16))

    dg = jnp.where(same_blk, lmat, 0.0)
    off = lmat - dg
    d2 = mm(dg, dg)
    d4 = mm(d2, d2)
    d8 = mm(d4, d4)
    p = eye - dg
    p = p + mm(p, d2)
    p = p + mm(p, d4)
    p = p + mm(p, d8)
    n1 = mm(p, off)
    n2 = mm(n1, n1)
    n4 = mm(n2, n2)
    r = p - mm(n1, p)
    r = r + mm(n2, r)
    r = r + mm(n4, r)
    return r


def _delta_kernel(qf_ref, qb_ref, zf_ref, zb_ref, gf_ref, gbk_ref, gtf_ref, gtb_ref, on_ref,
                  y_ref, s_ref, oacc_ref, *, n_ctx, n_lat):
    s = pl.program_id(1)

    @pl.when(s == 0)
    def _():
        s_ref[...] = jnp.zeros_like(s_ref)

    row = lax.broadcasted_iota(jnp.int32, (CHUNK, CHUNK), 0)
    col = lax.broadcasted_iota(jnp.int32, (CHUNK, CHUNK), 1)
    same_blk = (row // 16) == (col // 16)
    eye = jnp.where(row == col, 1.0, 0.0).astype(F32)
    onorm = on_ref[...]
    half = n_ctx + n_lat // 2

    for d in range(2):
        qkv_ref = qf_ref if d == 0 else qb_ref
        z_ref = zf_ref if d == 0 else zb_ref
        gcols = (gf_ref if d == 0 else gbk_ref)[0]
        gt = (gtf_ref if d == 0 else gtb_ref)[0]
        incl = (row >= col) if d == 0 else (row <= col)
        strict = (row > col) if d == 0 else (row < col)
        lat_chunk = (s - n_ctx) if d == 0 else (n_ctx + n_lat - 1 - s)
        off = pl.multiple_of(jnp.clip(lat_chunk, 0, n_lat - 1) * CHUNK, CHUNK)
        for h in range(HEADS):
            cols = slice(h * HD, (h + 1) * HD)
            q = qkv_ref[0, :, h * HD:(h + 1) * HD]
            k = qkv_ref[0, :, D_B + h * HD:D_B + (h + 1) * HD]
            v = qkv_ref[0, :, 2 * D_B + h * HD:2 * D_B + (h + 1) * HD]
            ci = d * HEADS + h
            gc = gcols[:, ci:ci + 1]
            beta = gcols[:, 2 * HEADS + ci:2 * HEADS + ci + 1]
            gr = gt[ci:ci + 1, :]
            glast = gr[:, CHUNK - 1:CHUNK] if d == 0 else gr[:, 0:1]

            gram = _dot_nt(jnp.concatenate([q, k], axis=0), k)
            decay = jnp.where(incl, jnp.exp(jnp.where(incl, gc - gr, 0.0)), 0.0)
            qk = gram[:CHUNK] * decay
            lmat = jnp.where(strict, gram[CHUNK:] * decay, 0.0) * beta
            tinv = _unit_tri_inverse(lmat, same_blk, eye)

            qf = q.astype(F32)
            kf = k.astype(F32)
            egc = jnp.exp(gc)
            rhs = jnp.concatenate([v.astype(F32) * beta, kf * (beta * egc)], axis=1).astype(BF16)
            uw = _dot(tinv.astype(BF16), rhs)
            qd = qf * egc
            kd = (kf * jnp.exp(glast - gc)).astype(BF16)

            st = s_ref[ci]
            a1 = _dot(jnp.concatenate([uw[:, HD:], qd], axis=0).astype(BF16), st.astype(BF16))
            vnew = (uw[:, :HD] - a1[:CHUNK]).astype(BF16)
            o = a1[CHUNK:] + _dot(qk.astype(BF16), vnew)
            s_ref[ci] = st * jnp.exp(glast) + _dot_tn(kd, vnew)

            @pl.when(jnp.logical_and(s >= n_ctx, s < half))
            def _():
                oacc_ref[pl.ds(off, CHUNK), cols] = o

            @pl.when(s >= half)
            def _():
                tot = oacc_ref[pl.ds(off, CHUNK), cols] + o
                ms = jnp.mean(tot * tot, axis=-1, keepdims=True)
                zz = z_ref[0, :, cols].astype(F32)
                y_ref[0, pl.ds(off, CHUNK), cols] = (tot * lax.rsqrt(ms + EPS) * onorm * _silu(zz)).astype(BF16)


def _delta(qkv, z, gb, gbt, onorm, L):
    B, LC, _ = qkv.shape
    n_all = LC // CHUNK
    n_lat = L // CHUNK
    n_ctx = n_all - n_lat

    def cf(s):
        return s

    def cb(s):
        return jnp.where(s < n_ctx, n_ctx - 1 - s, n_all + n_ctx - 1 - s)

    in_specs = [
        pl.BlockSpec((1, CHUNK, N_QKV), lambda b, s: (b, cf(s), 0)),
        pl.BlockSpec((1, CHUNK, N_QKV), lambda b, s: (b, cb(s), 0)),
        pl.BlockSpec((1, CHUNK, D_B), lambda b, s: (b, cf(s), 0)),
        pl.BlockSpec((1, CHUNK, D_B), lambda b, s: (b, cb(s), 0)),
        pl.BlockSpec((1, CHUNK, N_AB), lambda b, s: (b, cf(s), 0)),
        pl.BlockSpec((1, CHUNK, N_AB), lambda b, s: (b, cb(s), 0)),
        pl.BlockSpec((1, N_AB, CHUNK), lambda b, s: (b, 0, cf(s))),
        pl.BlockSpec((1, N_AB, CHUNK), lambda b, s: (b, 0, cb(s))),
        pl.BlockSpec((1, HD), lambda b, s: (0, 0)),
    ]
    return pl.pallas_call(
        functools.partial(_delta_kernel, n_ctx=n_ctx, n_lat=n_lat),
        out_shape=jax.ShapeDtypeStruct((B, L, D_B), BF16),
        grid=(B, n_all),
        in_specs=in_specs,
        out_specs=pl.BlockSpec((1, L, D_B), lambda b, s: (b, 0, 0)),
        scratch_shapes=[pltpu.VMEM((2 * HEADS, HD, HD), F32), pltpu.VMEM((L, D_B), F32)],
        compiler_params=pltpu.CompilerParams(dimension_semantics=("arbitrary", "arbitrary"),
                                             vmem_limit_bytes=VMEM_LIMIT),
    )(qkv, qkv, z, z, gb, gb, gbt, gbt, onorm)


def _out_kernel(x_ref, ya_ref, yb_ref, mod_ref, wa_ref, wb_ref, g2_ref, wrt_ref, brt_ref,
                x2_ref, h_ref, route_ref, cnt_ref, base_ref):
    i = pl.program_id(0)

    @pl.when(i == 0)
    def _():
        base_ref[...] = jnp.zeros_like(base_ref)

    mod = mod_ref[0]
    gt1 = mod[:, 2 * D:3 * D]
    sh2 = mod[:, 3 * D:4 * D]
    sc2 = mod[:, 4 * D:5 * D]
    mix = _dot(ya_ref[...], wa_ref[...]) + _dot(yb_ref[...], wb_ref[...])
    x2 = x_ref[...] + gt1 * mix
    x2_ref[...] = x2
    ms = jnp.mean(x2 * x2, axis=-1, keepdims=True)
    hv = x2 * lax.rsqrt(ms + EPS) * (g2_ref[...] * (1.0 + sc2)) + sh2
    h_ref[...] = hv

    lt = _dot_nt(wrt_ref[...], hv.astype(BF16)) + brt_ref[...]
    gl = [lt[r:r + 1, :] for r in range(N_GROUPS)]
    gmax = jnp.maximum(jnp.maximum(gl[0], gl[1]), jnp.maximum(gl[2], gl[3]))
    gsel = jnp.where(gl[0] == gmax, 0, jnp.where(gl[1] == gmax, 1, jnp.where(gl[2] == gmax, 2, 3)))
    p_g = 1.0 / (jnp.exp(gl[0] - gmax) + jnp.exp(gl[1] - gmax) + jnp.exp(gl[2] - gmax) + jnp.exp(gl[3] - gmax))
    el = []
    for e in range(EPG):
        r = [lt[N_GROUPS + g * EPG + e:N_GROUPS + g * EPG + e + 1, :] for g in range(N_GROUPS)]
        el.append(jnp.where(gsel == 0, r[0], jnp.where(gsel == 1, r[1], jnp.where(gsel == 2, r[2], r[3]))))
    m1 = jnp.maximum(jnp.maximum(el[0], el[1]), jnp.maximum(el[2], el[3]))
    i1 = jnp.where(el[0] == m1, 0, jnp.where(el[1] == m1, 1, jnp.where(el[2] == m1, 2, 3)))
    neg = jnp.float32(-jnp.inf)
    el2 = [jnp.where(i1 == e, neg, el[e]) for e in range(EPG)]
    m2 = jnp.maximum(jnp.maximum(el2[0], el2[1]), jnp.maximum(el2[2], el2[3]))
    i2 = jnp.where(jnp.logical_and(el2[0] == m2, i1 != 0), 0,
                   jnp.where(jnp.logical_and(el2[1] == m2, i1 != 1), 1,
                             jnp.where(jnp.logical_and(el2[2] == m2, i1 != 2), 2, 3)))
    t = jnp.exp(m2 - m1)
    w1 = p_g / (1.0 + t)
    w2 = p_g * t / (1.0 + t)
    first_low = i1 < i2
    ea = jnp.where(first_low, i1, i2)
    eb = jnp.where(first_low, i2, i1)
    w_a = jnp.where(first_low, w1, w2)
    w_b = jnp.where(first_low, w2, w1)
    pair = jnp.where(ea == 0, eb - 1, jnp.where(ea == 1, eb + 1, 5))
    bucket = gsel * N_PAIRS + pair

    tm = bucket.shape[1]
    rows = lax.broadcasted_iota(jnp.int32, (32, tm), 0)
    onehot = jnp.where(rows == bucket, 1.0, 0.0).astype(F32)
    r2 = lax.broadcasted_iota(jnp.int32, (tm, tm), 0)
    c2 = lax.broadcasted_iota(jnp.int32, (tm, tm), 1)
    tri = jnp.where(r2 <= c2, 1.0, 0.0).astype(BF16)
    prefix = _dot(onehot.astype(BF16), tri)
    base = base_ref[:, 0:1]
    rank = jnp.sum(onehot * (prefix - 1.0 + base), axis=0, keepdims=True)
    newbase = base + prefix[:, tm - 1:tm]
    base_ref[...] = jnp.broadcast_to(newbase, base_ref.shape)
    cnt_ref[...] = jnp.broadcast_to(newbase, cnt_ref.shape)
    route_ref[0] = jnp.concatenate([bucket.astype(F32), rank, w_a, w_b, jnp.zeros((4, tm), F32)], axis=0)


def _out_proj(x2d, ya, yb, mod3, w_oa, w_ob, g2, wrt, brt, L):
    T = x2d.shape[0]
    nb = T // TM_OUT
    per_b = L // TM_OUT
    out_shape = [
        jax.ShapeDtypeStruct((T, D), F32),
        jax.ShapeDtypeStruct((T, D), F32),
        jax.ShapeDtypeStruct((nb, 8, TM_OUT), F32),
        jax.ShapeDtypeStruct((32, 128), F32),
    ]
    return pl.pallas_call(
        _out_kernel,
        out_shape=out_shape,
        grid=(nb,),
        in_specs=[
            pl.BlockSpec((TM_OUT, D), lambda i: (i, 0)),
            pl.BlockSpec((TM_OUT, D_A), lambda i: (i, 0)),
            pl.BlockSpec((TM_OUT, D_B), lambda i: (i, 0)),
            pl.BlockSpec((1, 1, 6 * D), lambda i: (i // per_b, 0, 0)),
            pl.BlockSpec((D_A, D), lambda i: (0, 0)),
            pl.BlockSpec((D_B, D), lambda i: (0, 0)),
            pl.BlockSpec((1, D), lambda i: (0, 0)),
            pl.BlockSpec((32, D), lambda i: (0, 0)),
            pl.BlockSpec((32, 1), lambda i: (0, 0)),
        ],
        out_specs=[
            pl.BlockSpec((TM_OUT, D), lambda i: (i, 0)),
            pl.BlockSpec((TM_OUT, D), lambda i: (i, 0)),
            pl.BlockSpec((1, 8, TM_OUT), lambda i: (i, 0, 0)),
            pl.BlockSpec((32, 128), lambda i: (0, 0)),
        ],
        scratch_shapes=[pltpu.VMEM((32, 128), F32)],
        compiler_params=pltpu.CompilerParams(dimension_semantics=("arbitrary",),
                                             vmem_limit_bytes=VMEM_LIMIT),
    )(x2d, ya, yb, mod3, w_oa, w_ob, g2, wrt, brt)


def _row_gather_start(idx_ref, base, src_hbm, buf, slot, sem, n):
    def body(r, carry):
        idx = idx_ref[base + r]
        pltpu.make_async_copy(src_hbm.at[pl.ds(idx, 1)], buf.at[slot, pl.ds(r, 1)], sem.at[slot]).start()
        return carry
    lax.fori_loop(0, n, body, 0, unroll=8)


def _row_gather_wait(buf, slot, sem):
    pltpu.make_async_copy(buf.at[slot], buf.at[slot], sem.at[slot]).wait()


def _moe_kernel(ea_ref, eb_ref, nv_ref, tok_ref, h_hbm, wab_ref, wga_ref, wgb_ref, wda_ref, wdb_ref,
                o_ref, buf, sem):
    i = pl.program_id(0)
    n = pl.num_programs(0)
    slot = lax.rem(i, 2)

    @pl.when(i == 0)
    def _():
        _row_gather_start(tok_ref, 0, h_hbm, buf, 0, sem, BM_MOE)

    @pl.when(i + 1 < n)
    def _():
        _row_gather_start(tok_ref, (i + 1) * BM_MOE, h_hbm, buf, 1 - slot, sem, BM_MOE)

    _row_gather_wait(buf, slot, sem)

    @pl.when(nv_ref[i] > 0)
    def _():
        xb = buf[slot].astype(BF16)
        wab = wab_ref[...]
        ga = _dot(xb, wga_ref[0])
        act_a = (_silu(ga[:, :D_EXPERT]) * ga[:, D_EXPERT:] * wab[:, 0:1]).astype(BF16)
        gb = _dot(xb, wgb_ref[0])
        act_b = (_silu(gb[:, :D_EXPERT]) * gb[:, D_EXPERT:] * wab[:, 1:2]).astype(BF16)
        o_ref[...] = _dot(act_a, wda_ref[0]) + _dot(act_b, wdb_ref[0])

    @pl.when(nv_ref[i] <= 0)
    def _():
        o_ref[...] = jnp.zeros_like(o_ref)


def _moe(ea, eb, nvalid, tok, h, wab_sorted, w_gu, w_dn):
    nblk = ea.shape[0]
    S = nblk * BM_MOE
    grid_spec = pltpu.PrefetchScalarGridSpec(
        num_scalar_prefetch=4,
        grid=(nblk,),
        in_specs=[
            pl.BlockSpec(memory_space=pl.ANY),
            pl.BlockSpec((BM_MOE, 2), lambda i, ea, eb, nv, tok: (i, 0)),
            pl.BlockSpec((1, D, 2 * D_EXPERT), lambda i, ea, eb, nv, tok: (ea[i], 0, 0)),
            pl.BlockSpec((1, D, 2 * D_EXPERT), lambda i, ea, eb, nv, tok: (eb[i], 0, 0)),
            pl.BlockSpec((1, D_EXPERT, D), lambda i, ea, eb, nv, tok: (ea[i], 0, 0)),
            pl.BlockSpec((1, D_EXPERT, D), lambda i, ea, eb, nv, tok: (eb[i], 0, 0)),
        ],
        out_specs=pl.BlockSpec((BM_MOE, D), lambda i, ea, eb, nv, tok: (i, 0)),
        scratch_shapes=[pltpu.VMEM((2, BM_MOE, D), F32), pltpu.SemaphoreType.DMA((2,))],
    )
    return pl.pallas_call(
        _moe_kernel,
        out_shape=jax.ShapeDtypeStruct((S, D), F32),
        grid_spec=grid_spec,
        compiler_params=pltpu.CompilerParams(dimension_semantics=("arbitrary",),
                                             vmem_limit_bytes=VMEM_LIMIT),
    )(ea, eb, nvalid, tok, h, wab_sorted, w_gu, w_gu, w_dn, w_dn)


def _final_kernel(pos_ref, x2_ref, mod_ref, fg_ref, ms_hbm, o_ref, buf, sem):
    i = pl.program_id(0)
    n = pl.num_programs(0)
    slot = lax.rem(i, 2)

    @pl.when(i == 0)
    def _():
        _row_gather_start(pos_ref, 0, ms_hbm, buf, 0, sem, TM_FIN)

    @pl.when(i + 1 < n)
    def _():
        _row_gather_start(pos_ref, (i + 1) * TM_FIN, ms_hbm, buf, 1 - slot, sem, TM_FIN)

    _row_gather_wait(buf, slot, sem)
    gt2 = mod_ref[0][:, 5 * D:6 * D]
    y = x2_ref[...] + gt2 * buf[slot]
    ms = jnp.mean(y * y, axis=-1, keepdims=True)
    o_ref[...] = y * lax.rsqrt(ms + EPS) * fg_ref[...]


def _final(pos, x2, mod3, fg, ms, L):
    T = x2.shape[0]
    per_b = L // TM_FIN
    grid_spec = pltpu.PrefetchScalarGridSpec(
        num_scalar_prefetch=1,
        grid=(T // TM_FIN,),
        in_specs=[
            pl.BlockSpec((TM_FIN, D), lambda i, pos: (i, 0)),
            pl.BlockSpec((1, 1, 6 * D), lambda i, pos: (i // per_b, 0, 0)),
            pl.BlockSpec((1, D), lambda i, pos: (0, 0)),
            pl.BlockSpec(memory_space=pl.ANY),
        ],
        out_specs=pl.BlockSpec((TM_FIN, D), lambda i, pos: (i, 0)),
        scratch_shapes=[pltpu.VMEM((2, TM_FIN, D), F32), pltpu.SemaphoreType.DMA((2,))],
    )
    return pl.pallas_call(
        _final_kernel,
        out_shape=jax.ShapeDtypeStruct((T, D), F32),
        grid_spec=grid_spec,
        compiler_params=pltpu.CompilerParams(dimension_semantics=("arbitrary",),
                                             vmem_limit_bytes=VMEM_LIMIT),
    )(pos, x2, mod3, fg, ms)


def kernel(x, c, ctx, c_ctx, w_ada, b_ada, norm1_g, w_in, ln_a_g, ln_a_b, w_spatial, b_spatial, conv_qkv, a_log,
           dt_bias, onorm_g, w_out, norm2_g, w_group, b_group, w_router, b_router, w_gate_up, w_down, final_g):
    B, L, _ = x.shape
    T = B * L
    assert w_ada.shape[0] == 1 and ctx.shape[1] == TM_IN and L % TM_OUT == 0

    cond = jnp.concatenate([c, c_ctx[None, :], jnp.zeros((7, D), F32)], axis=0)
    mod = _modulation(cond, w_ada[0], b_ada[0][None, :])
    mod_lat = mod[:B].reshape(B, 1, 6 * D)
    mod_ctx = mod[B:B + 1]

    w_main = w_in[0][:, :N_MAIN].astype(BF16)
    w_ab = jnp.pad(w_in[0][:, N_MAIN:], ((0, 0), (0, 128 - N_AB))).astype(BF16)
    w_abt = w_in[0][:, N_MAIN:].T.astype(BF16)
    alog = a_log[0].reshape(1, 2 * HEADS)
    dtb = dt_bias[0].reshape(1, 2 * HEADS)
    alog_row = jnp.pad(alog, ((0, 0), (0, 128 - 2 * HEADS)))
    dtb_row = jnp.pad(dtb, ((0, 0), (0, 128 - 2 * HEADS)))
    alog_col = jnp.pad(alog, ((0, 0), (0, N_AB - 2 * HEADS))).T
    dtb_col = jnp.pad(dtb, ((0, 0), (0, N_AB - 2 * HEADS))).T

    ya, qkv, z, gb, gbt = _in_proj(
        x, ctx, mod_lat, mod_ctx, norm1_g, w_main, w_ab, w_abt, ln_a_g, ln_a_b,
        w_spatial[0].astype(BF16), b_spatial[0].T, conv_qkv[0], alog_row, dtb_row, alog_col, dtb_col)

    yb = _delta(qkv, z, gb, gbt, onorm_g, L)

    wrt = jnp.concatenate([w_group[0].T, w_router[0].T, jnp.zeros((32 - N_GROUPS - N_EXPERTS, D), F32)], axis=0)
    brt = jnp.concatenate([b_group[0], b_router[0], jnp.zeros((32 - N_GROUPS - N_EXPERTS,), F32)])[:, None]
    w_o = w_out[0].astype(BF16)
    x2, h, route, cnt = _out_proj(x.reshape(T, D), ya.reshape(T, D_A), yb.reshape(T, D_B), mod_lat,
                                  w_o[:D_A], w_o[D_A:], norm2_g, wrt.astype(BF16), brt, L)

    bucket = route[:, 0, :].reshape(T).astype(jnp.int32)
    rank = route[:, 1, :].reshape(T).astype(jnp.int32)
    wab = jnp.stack([route[:, 2, :].reshape(T), route[:, 3, :].reshape(T)], axis=1)
    counts = cnt[:N_BUCKETS, 0].astype(jnp.int32)
    nblk_b = (counts + BM_MOE - 1) // BM_MOE
    blk_end = jnp.cumsum(nblk_b)
    blk_start = blk_end - nblk_b
    pos = blk_start[bucket] * BM_MOE + rank
    n_blocks = T // BM_MOE + N_BUCKETS
    S = n_blocks * BM_MOE
    tok = jnp.zeros((S,), jnp.int32).at[pos].set(jnp.arange(T, dtype=jnp.int32))
    wab_sorted = jnp.zeros((S, 2), F32).at[pos].set(wab)
    blk = jnp.arange(n_blocks, dtype=jnp.int32)
    used = blk < blk_end[-1]
    bkt = jnp.searchsorted(blk_end, jnp.minimum(blk, blk_end[-1] - 1), side="right").astype(jnp.int32)
    nvalid = jnp.where(used, jnp.clip(counts[bkt] - (blk - blk_start[bkt]) * BM_MOE, 0, BM_MOE), 0).astype(jnp.int32)
    pa = jnp.asarray(PAIR_A, jnp.int32)
    pb = jnp.asarray(PAIR_B, jnp.int32)
    ea = (bkt // N_PAIRS) * EPG + pa[bkt % N_PAIRS]
    eb = (bkt // N_PAIRS) * EPG + pb[bkt % N_PAIRS]

    ms = _moe(ea, eb, nvalid, tok, h, wab_sorted, w_gate_up[0].astype(BF16), w_down[0].astype(BF16))
    out = _final(pos, x2, mod_lat, final_g[None, :], ms, L)
    return out.reshape(B, L, D)
```

```python
import functools

import jax
import jax.numpy as jnp
import numpy as np
from jax import lax
from jax.experimental import pallas as pl
from jax.experimental.pallas import tpu as pltpu

F32 = jnp.float32
BF16 = jnp.bfloat16
EPS = 1e-6

D = 1024
D_A = 512
D_B = 512
HEADS = 4
HD = 128
CHUNK = 128
CONV_W = 5
N_QKV = 3 * D_B
N_MAIN = 2 * D_A + 4 * D_B
N_AB = 16
N_GROUPS = 4
EPG = 4
N_EXPERTS = 16
D_EXPERT = 512
N_PAIRS = 6
N_BUCKETS = N_GROUPS * N_PAIRS
PAIR_A = (0, 0, 0, 1, 1, 2)
PAIR_B = (1, 2, 3, 2, 3, 3)

TM_IN = 256
HALO = 8
TM_OUT = 512
BM_MOE = 256
TM_FIN = 256
VMEM_LIMIT = 56 * 1024 * 1024

HI = lax.Precision.HIGHEST


def _dot(a, b, precision=None):
    return jnp.dot(a, b, preferred_element_type=F32, precision=precision)


def _dot_nt(a, b):
    return lax.dot_general(a, b, (((1,), (1,)), ((), ())), preferred_element_type=F32)


def _dot_tn(a, b):
    return lax.dot_general(a, b, (((0,), (0,)), ((), ())), preferred_element_type=F32)


def _sigmoid(x):
    return 1.0 / (1.0 + jnp.exp(-x))


def _silu(x):
    return x * _sigmoid(x)


def _softplus(x):
    return jnp.maximum(x, 0.0) + jnp.log(1.0 + jnp.exp(-jnp.abs(x)))


def _gelu_tanh(x):
    return 0.5 * x * (1.0 + jnp.tanh(np.sqrt(2.0 / np.pi).astype(np.float32) * (x + 0.044715 * (x * x * x))))


def _mod_kernel(c_ref, w_ref, b_ref, o_ref):
    c = c_ref[...]
    o_ref[...] = _dot(_silu(c), w_ref[...], precision=HI) + b_ref[...]


def _modulation(cond, w_ada, b_ada):
    rows = cond.shape[0]
    tn = 1536
    return pl.pallas_call(
        _mod_kernel,
        out_shape=jax.ShapeDtypeStruct((rows, 6 * D), F32),
        grid=(6 * D // tn,),
        in_specs=[pl.BlockSpec((rows, D), lambda i: (0, 0)),
                  pl.BlockSpec((D, tn), lambda i: (0, i)),
                  pl.BlockSpec((1, tn), lambda i: (0, i))],
        out_specs=pl.BlockSpec((rows, tn), lambda i: (0, i)),
        compiler_params=pltpu.CompilerParams(dimension_semantics=("arbitrary",),
                                             vmem_limit_bytes=VMEM_LIMIT),
    )(cond, w_ada, b_ada)


def _in_kernel(x_ref, xp_ref, xn_ref, ctx_ref, mod_ref, cmod_ref, g1_ref, w_ref, wab_ref, wabt_ref,
               lng_ref, lnb_ref, ws_ref, bst_ref, conv_ref, alog_ref, dtb_ref, alogt_ref, dtbt_ref,
               ya_ref, qkv_ref, z_ref, gb_ref, gbt_ref, ext_ref):
    j = pl.program_id(1)
    is_ctx = j == 0
    n_lat_blocks = pl.num_programs(1) - 1
    mod = mod_ref[0]
    cm = cmod_ref[...]
    sh = jnp.where(is_ctx, cm[:, 0:D], mod[:, 0:D])
    sc = jnp.where(is_ctx, cm[:, D:2 * D], mod[:, D:2 * D])
    scale = g1_ref[...] * (1.0 + sc)

    def normmod(xv):
        ms = jnp.mean(xv * xv, axis=-1, keepdims=True)
        return (xv * lax.rsqrt(ms + EPS) * scale + sh).astype(BF16)

    xmain = jnp.where(is_ctx, ctx_ref[0], x_ref[0])
    xe = normmod(jnp.concatenate([xp_ref[0], xmain, xn_ref[0]], axis=0))
    xb = normmod(xmain)

    pa = _dot(xe, w_ref[:, 0:2 * D_A])[HALO:HALO + TM_IN]
    ga = _gelu_tanh(pa)
    u = ga[:, :D_A]
    v = ga[:, D_A:]
    mu = jnp.mean(v, axis=-1, keepdims=True)
    vc = v - mu
    var = jnp.mean(vc * vc, axis=-1, keepdims=True)
    vn = (vc * lax.rsqrt(var + EPS) * lng_ref[...] + lnb_ref[...]).astype(BF16)
    bst = bst_ref[...]
    for n in range(TM_IN // CHUNK):
        rows = slice(n * CHUNK, (n + 1) * CHUNK)
        for h in range(HEADS):
            cols = slice(h * HD, (h + 1) * HD)
            s = _dot(ws_ref[h], vn[rows, cols]) + bst[:, h:h + 1]
            ya_ref[0, rows, cols] = (u[rows, cols] * s).astype(BF16)

    pq = _dot(xe, w_ref[:, 2 * D_A:2 * D_A + N_QKV])
    rid = lax.broadcasted_iota(jnp.int32, (TM_IN + 2 * HALO, 1), 0)
    prev_ok = j >= 2
    next_ok = jnp.logical_and(j >= 1, j < n_lat_blocks)
    valid = jnp.logical_or(jnp.logical_and(rid >= HALO, rid < HALO + TM_IN),
                           jnp.logical_or(jnp.logical_and(rid < HALO, prev_ok),
                                          jnp.logical_and(rid >= HALO + TM_IN, next_ok)))
    ext_ref[...] = jnp.where(valid, pq, 0.0)
    pad = (CONV_W - 1) // 2
    acc = None
    for t in range(CONV_W):
        term = conv_ref[t:t + 1, :] * ext_ref[pl.ds(HALO - pad + t, TM_IN), :]
        acc = term if acc is None else acc + term
    act = _silu(acc)
    for h in range(2 * HEADS):
        cols = slice(h * HD, (h + 1) * HD)
        t = act[:, cols]
        nrm = lax.rsqrt(jnp.sum(t * t, axis=-1, keepdims=True) + EPS)
        if h < HEADS:
            nrm = nrm * (HD ** -0.5)
        qkv_ref[0, :, cols] = (t * nrm).astype(BF16)
    qkv_ref[0, :, 2 * D_B:] = act[:, 2 * D_B:].astype(BF16)

    z_ref[0] = _dot(xe, w_ref[:, 2 * D_A + N_QKV:N_MAIN])[HALO:HALO + TM_IN].astype(BF16)

    r2 = lax.broadcasted_iota(jnp.int32, (TM_IN, TM_IN), 0)
    c2 = lax.broadcasted_iota(jnp.int32, (TM_IN, TM_IN), 1)
    same = (r2 // CHUNK) == (c2 // CHUNK)
    tri_l = jnp.where(jnp.logical_and(same, c2 <= r2), 1.0, 0.0).astype(F32)
    tri_u = jnp.where(jnp.logical_and(same, c2 >= r2), 1.0, 0.0).astype(F32)

    ab = _dot(xb, wab_ref[...])
    g_tok = -jnp.exp(alog_ref[...]) * _softplus(ab + dtb_ref[...])
    lane = lax.broadcasted_iota(jnp.int32, ab.shape, 1)
    gb = jnp.where(lane < HEADS, _dot(tri_l, g_tok, precision=HI),
                   jnp.where(lane < 2 * HEADS, _dot(tri_u, g_tok, precision=HI), _sigmoid(ab)))
    gb_ref[0] = gb[:, 0:N_AB]

    abt = _dot_nt(wabt_ref[...], xb)
    g_t = -jnp.exp(alogt_ref[...]) * _softplus(abt + dtbt_ref[...])
    row = lax.broadcasted_iota(jnp.int32, abt.shape, 0)
    gbt_ref[0] = jnp.where(row < HEADS, _dot(g_t, tri_u, precision=HI),
                           jnp.where(row < 2 * HEADS, _dot(g_t, tri_l, precision=HI), _sigmoid(abt)))


def _in_proj(x, ctx, mod_lat, mod_ctx, g1, w_main, w_ab, w_abt, lng, lnb, ws, bst, conv, alog, dtb, alogt, dtbt):
    B, L, _ = x.shape
    n_lat = L // TM_IN
    n_steps = n_lat + 1
    LC = L + TM_IN
    hb = TM_IN // HALO

    def full(shape):
        return pl.BlockSpec(shape, lambda b, j: (0,) * len(shape))

    in_specs = [
        pl.BlockSpec((1, TM_IN, D), lambda b, j: (b, jnp.maximum(j - 1, 0), 0)),
        pl.BlockSpec((1, HALO, D), lambda b, j: (b, jnp.clip((j - 1) * hb - 1, 0, L // HALO - 1), 0)),
        pl.BlockSpec((1, HALO, D), lambda b, j: (b, jnp.clip(j * hb, 0, L // HALO - 1), 0)),
        pl.BlockSpec((1, TM_IN, D), lambda b, j: (b, 0, 0)),
        pl.BlockSpec((1, 1, 6 * D), lambda b, j: (b, 0, 0)),
        full((1, 6 * D)), full((1, D)), full((D, N_MAIN)), full((D, 128)), full((N_AB, D)),
        full((1, D_A)), full((1, D_A)), full((HEADS, CHUNK, CHUNK)), full((CHUNK, HEADS)),
        full((CONV_W, N_QKV)), full((1, 128)), full((1, 128)), full((N_AB, 1)), full((N_AB, 1)),
    ]
    out_shape = [
        jax.ShapeDtypeStruct((B, L, D_A), BF16),
        jax.ShapeDtypeStruct((B, LC, N_QKV), BF16),
        jax.ShapeDtypeStruct((B, LC, D_B), BF16),
        jax.ShapeDtypeStruct((B, LC, N_AB), F32),
        jax.ShapeDtypeStruct((B, N_AB, LC), F32),
    ]
    out_specs = [
        pl.BlockSpec((1, TM_IN, D_A), lambda b, j: (b, jnp.maximum(j - 1, 0), 0)),
        pl.BlockSpec((1, TM_IN, N_QKV), lambda b, j: (b, j, 0)),
        pl.BlockSpec((1, TM_IN, D_B), lambda b, j: (b, j, 0)),
        pl.BlockSpec((1, TM_IN, N_AB), lambda b, j: (b, j, 0)),
        pl.BlockSpec((1, N_AB, TM_IN), lambda b, j: (b, 0, j)),
    ]
    return pl.pallas_call(
        _in_kernel,
        out_shape=out_shape,
        grid=(B, n_steps),
        in_specs=in_specs,
        out_specs=out_specs,
        scratch_shapes=[pltpu.VMEM((TM_IN + 2 * HALO, N_QKV), F32)],
        compiler_params=pltpu.CompilerParams(dimension_semantics=("arbitrary", "arbitrary"),
                                             vmem_limit_bytes=VMEM_LIMIT),
    )(x, x, x, ctx, mod_lat, mod_ctx, g1, w_main, w_ab, w_abt, lng, lnb, ws, bst, conv, alog, dtb, alogt, dtbt)


def _delta_kernel(qf_ref, qb_ref, zf_ref, zb_ref, gf_ref, gbk_ref, gtf_ref, gtb_ref, on_ref,
                  y_ref, s_ref, oacc_ref, *, n_ctx, n_lat):
    s = pl.program_id(1)

    @pl.when(s == 0)
    def _():
        s_ref[...] = jnp.zeros_like(s_ref)
        oacc_ref[...] = jnp.zeros_like(oacc_ref)

    row = lax.broadcasted_iota(jnp.int32, (CHUNK, CHUNK), 0)
    col = lax.broadcasted_iota(jnp.int32, (CHUNK, CHUNK), 1)
    low = row > col
    upp = row < col
    same_blk = (row // 16) == (col // 16)
    eye = jnp.where(row == col, 1.0, 0.0).astype(F32)
    onorm = on_ref[...]
    half = n_ctx + n_lat // 2
    second = s >= half
    gcols = (gf_ref[0], gbk_ref[0])
    gts = (gtf_ref[0], gtb_ref[0])
    qkv_refs = (qf_ref, qb_ref)
    z_refs = (zf_ref, zb_ref)
    hs = range(HEADS)

    def halves(xc, unit):
        fill = eye if unit else 0.0
        return jnp.where(low, xc, fill).astype(BF16), jnp.where(upp, xc, fill).astype(BF16)

    def as_lhs(hv):
        return jnp.concatenate(hv, axis=1)

    def as_rhs(*hvs):
        cols_ = [jnp.concatenate(hv, axis=0) for hv in hvs]
        return cols_[0] if len(cols_) == 1 else jnp.concatenate(cols_, axis=1)

    def load(d, h, part):
        return qkv_refs[d][0, :, part * D_B + h * HD:part * D_B + (h + 1) * HD]

    q = [[load(d, h, 0) for h in hs] for d in range(2)]
    k = [[load(d, h, 1) for h in hs] for d in range(2)]
    v = [[load(d, h, 2) for h in hs] for d in range(2)]
    gc = [[gcols[d][:, d * HEADS + h:d * HEADS + h + 1] for h in hs] for d in range(2)]
    beta = [[gcols[d][:, 2 * HEADS + d * HEADS + h:2 * HEADS + d * HEADS + h + 1] for h in hs] for d in range(2)]
    gr = [[gts[d][d * HEADS + h:d * HEADS + h + 1, :] for h in hs] for d in range(2)]
    glast = [[gr[0][h][:, CHUNK - 1:CHUNK] for h in hs], [gr[1][h][:, 0:1] for h in hs]]

    gram = [[_dot_nt(jnp.concatenate([q[d][h], k[d][h]], axis=0), k[d][h]) for h in hs] for d in range(2)]
    dec = [jnp.exp(jnp.where(low, gc[0][h] - gr[0][h], jnp.where(upp, gc[1][h] - gr[1][h], 0.0))) for h in hs]
    lc = [jnp.where(low, gram[0][h][CHUNK:] * beta[0][h], jnp.where(upp, gram[1][h][CHUNK:] * beta[1][h], 0.0))
          * dec[h] for h in hs]
    qk = [[jnp.where(upp, 0.0, gram[0][h][:CHUNK] * dec[h]).astype(BF16) for h in hs],
          [jnp.where(low, 0.0, gram[1][h][:CHUNK] * dec[h]).astype(BF16) for h in hs]]

    dg = [jnp.where(same_blk, lc[h], 0.0) for h in hs]
    ob = [lc[h] - dg[h] for h in hs]
    d1h = [halves(dg[h], False) for h in hs]
    d2 = [_dot(as_lhs(d1h[h]), as_rhs(d1h[h])) for h in hs]
    p0s = [-dg[h] for h in hs]
    d2h = [halves(d2[h], False) for h in hs]
    p0h = [halves(p0s[h], True) for h in hs]
    o2 = [_dot(as_lhs(d2h[h]), as_rhs(d2h[h], p0h[h])) for h in hs]
    p1s = [p0s[h] + o2[h][:, CHUNK:] for h in hs]
    d4h = [halves(o2[h][:, :CHUNK], False) for h in hs]
    p1h = [halves(p1s[h], True) for h in hs]
    o3 = [_dot(as_lhs(d4h[h]), as_rhs(d4h[h], p1h[h])) for h in hs]
    p2s = [p1s[h] + o3[h][:, CHUNK:] for h in hs]
    d8h = [halves(o3[h][:, :CHUNK], False) for h in hs]
    p2h = [halves(p2s[h], True) for h in hs]
    p3s = [p2s[h] + _dot(as_lhs(d8h[h]), as_rhs(p2h[h])) for h in hs]
    p3h = [halves(p3s[h], True) for h in hs]
    obh = [halves(ob[h], False) for h in hs]
    n1h = [halves(_dot(as_lhs(p3h[h]), as_rhs(obh[h])), False) for h in hs]
    o6 = [_dot(as_lhs(n1h[h]), as_rhs(n1h[h], p3h[h])) for h in hs]
    r0s = [p3s[h] - o6[h][:, CHUNK:] for h in hs]
    n2h = [halves(o6[h][:, :CHUNK], False) for h in hs]
    r0h = [halves(r0s[h], True) for h in hs]
    o7 = [_dot(as_lhs(n2h[h]), as_rhs(n2h[h], r0h[h])) for h in hs]
    r1s = [r0s[h] + o7[h][:, CHUNK:] for h in hs]
    n4h = [halves(o7[h][:, :CHUNK], False) for h in hs]
    r1h = [halves(r1s[h], True) for h in hs]
    tinv = [halves(r1s[h] + _dot(as_lhs(n4h[h]), as_rhs(r1h[h])), True) for h in hs]

    for d in range(2):
        lat_chunk = (s - n_ctx) if d == 0 else (n_ctx + n_lat - 1 - s)
        off = pl.multiple_of(jnp.clip(lat_chunk, 0, n_lat - 1) * CHUNK, CHUNK)
        egc = [jnp.exp(gc[d][h]) for h in hs]
        kf = [k[d][h].astype(F32) for h in hs]
        rhs = [jnp.concatenate([v[d][h].astype(F32) * beta[d][h], kf[h] * (beta[d][h] * egc[h])], axis=1).astype(BF16)
               for h in hs]
        uw = [_dot(tinv[h][d], rhs[h]) for h in hs]
        qd = [q[d][h].astype(F32) * egc[h] for h in hs]
        kd = [(kf[h] * jnp.exp(glast[d][h] - gc[d][h])).astype(BF16) for h in hs]
        st = [s_ref[d * HEADS + h] for h in hs]
        a1 = [_dot(jnp.concatenate([uw[h][:, HD:], qd[h]], axis=0).astype(BF16), st[h].astype(BF16)) for h in hs]
        vnew = [(uw[h][:, :HD] - a1[h][:CHUNK]).astype(BF16) for h in hs]
        o = [a1[h][CHUNK:] + _dot(qk[d][h], vnew[h]) for h in hs]
        for h in hs:
            s_ref[d * HEADS + h] = st[h] * jnp.exp(glast[d][h]) + _dot_tn(kd[h], vnew[h])
        for h in hs:
            cols = slice(h * HD, (h + 1) * HD)
            tot = jnp.where(second, oacc_ref[pl.ds(off, CHUNK), cols], 0.0) + o[h]
            oacc_ref[pl.ds(off, CHUNK), cols] = tot
            ms = jnp.mean(tot * tot, axis=-1, keepdims=True)
            zz = z_refs[d][0, :, cols].astype(F32)
            y_ref[0, pl.ds(off, CHUNK), cols] = (tot * lax.rsqrt(ms + EPS) * onorm * _silu(zz)).astype(BF16)


def _delta(qkv, z, gb, gbt, onorm, L):
    B, LC, _ = qkv.shape
    n_all = LC // CHUNK
    n_lat = L // CHUNK
    n_ctx = n_all - n_lat

    def cf(s):
        return s

    def cb(s):
        return jnp.where(s < n_ctx, n_ctx - 1 - s, n_all + n_ctx - 1 - s)

    in_specs = [
        pl.BlockSpec((1, CHUNK, N_QKV), lambda b, s: (b, cf(s), 0)),
        pl.BlockSpec((1, CHUNK, N_QKV), lambda b, s: (b, cb(s), 0)),
        pl.BlockSpec((1, CHUNK, D_B), lambda b, s: (b, cf(s), 0)),
        pl.BlockSpec((1, CHUNK, D_B), lambda b, s: (b, cb(s), 0)),
        pl.BlockSpec((1, CHUNK, N_AB), lambda b, s: (b, cf(s), 0)),
        pl.BlockSpec((1, CHUNK, N_AB), lambda b, s: (b, cb(s), 0)),
        pl.BlockSpec((1, N_AB, CHUNK), lambda b, s: (b, 0, cf(s))),
        pl.BlockSpec((1, N_AB, CHUNK), lambda b, s: (b, 0, cb(s))),
        pl.BlockSpec((1, HD), lambda b, s: (0, 0)),
    ]
    return pl.pallas_call(
        functools.partial(_delta_kernel, n_ctx=n_ctx, n_lat=n_lat),
        out_shape=jax.ShapeDtypeStruct((B, L, D_B), BF16),
        grid=(B, n_all),
        in_specs=in_specs,
        out_specs=pl.BlockSpec((1, L, D_B), lambda b, s: (b, 0, 0)),
        scratch_shapes=[pltpu.VMEM((2 * HEADS, HD, HD), F32), pltpu.VMEM((L, D_B), F32)],
        compiler_params=pltpu.CompilerParams(dimension_semantics=("arbitrary", "arbitrary"),
                                             vmem_limit_bytes=VMEM_LIMIT),
    )(qkv, qkv, z, z, gb, gb, gbt, gbt, onorm)


def _out_kernel(x_ref, ya_ref, yb_ref, mod_ref, wa_ref, wb_ref, g2_ref, wrt_ref, brt_ref,
                x2_ref, h_ref, route_ref, cnt_ref, base_ref):
    i = pl.program_id(0)

    @pl.when(i == 0)
    def _():
        base_ref[...] = jnp.zeros_like(base_ref)

    mod = mod_ref[0]
    gt1 = mod[:, 2 * D:3 * D]
    sh2 = mod[:, 3 * D:4 * D]
    sc2 = mod[:, 4 * D:5 * D]
    mix = _dot(ya_ref[...], wa_ref[...]) + _dot(yb_ref[...], wb_ref[...])
    x2 = x_ref[...] + gt1 * mix
    x2_ref[...] = x2
    ms = jnp.mean(x2 * x2, axis=-1, keepdims=True)
    hv = x2 * lax.rsqrt(ms + EPS) * (g2_ref[...] * (1.0 + sc2)) + sh2
    h_ref[...] = hv

    lt = _dot_nt(wrt_ref[...], hv.astype(BF16)) + brt_ref[...]
    gl = [lt[r:r + 1, :] for r in range(N_GROUPS)]
    gmax = jnp.maximum(jnp.maximum(gl[0], gl[1]), jnp.maximum(gl[2], gl[3]))
    gsel = jnp.where(gl[0] == gmax, 0, jnp.where(gl[1] == gmax, 1, jnp.where(gl[2] == gmax, 2, 3)))
    p_g = 1.0 / (jnp.exp(gl[0] - gmax) + jnp.exp(gl[1] - gmax) + jnp.exp(gl[2] - gmax) + jnp.exp(gl[3] - gmax))
    el = []
    for e in range(EPG):
        r = [lt[N_GROUPS + g * EPG + e:N_GROUPS + g * EPG + e + 1, :] for g in range(N_GROUPS)]
        el.append(jnp.where(gsel == 0, r[0], jnp.where(gsel == 1, r[1], jnp.where(gsel == 2, r[2], r[3]))))
    m1 = jnp.maximum(jnp.maximum(el[0], el[1]), jnp.maximum(el[2], el[3]))
    i1 = jnp.where(el[0] == m1, 0, jnp.where(el[1] == m1, 1, jnp.where(el[2] == m1, 2, 3)))
    neg = jnp.float32(-jnp.inf)
    el2 = [jnp.where(i1 == e, neg, el[e]) for e in range(EPG)]
    m2 = jnp.maximum(jnp.maximum(el2[0], el2[1]), jnp.maximum(el2[2], el2[3]))
    i2 = jnp.where(jnp.logical_and(el2[0] == m2, i1 != 0), 0,
                   jnp.where(jnp.logical_and(el2[1] == m2, i1 != 1), 1,
                             jnp.where(jnp.logical_and(el2[2] == m2, i1 != 2), 2, 3)))
    t = jnp.exp(m2 - m1)
    w1 = p_g / (1.0 + t)
    w2 = p_g * t / (1.0 + t)
    first_low = i1 < i2
    ea = jnp.where(first_low, i1, i2)
    eb = jnp.where(first_low, i2, i1)
    w_a = jnp.where(first_low, w1, w2)
    w_b = jnp.where(first_low, w2, w1)
    pair = jnp.where(ea == 0, eb - 1, jnp.where(ea == 1, eb + 1, 5))
    bucket = gsel * N_PAIRS + pair

    tm = bucket.shape[1]
    rows = lax.broadcasted_iota(jnp.int32, (32, tm), 0)
    onehot = jnp.where(rows == bucket, 1.0, 0.0).astype(F32)
    r2 = lax.broadcasted_iota(jnp.int32, (tm, tm), 0)
    c2 = lax.broadcasted_iota(jnp.int32, (tm, tm), 1)
    tri = jnp.where(r2 <= c2, 1.0, 0.0).astype(BF16)
    prefix = _dot(onehot.astype(BF16), tri)
    base = base_ref[:, 0:1]
    rank = jnp.sum(onehot * (prefix - 1.0 + base), axis=0, keepdims=True)
    newbase = base + prefix[:, tm - 1:tm]
    base_ref[...] = jnp.broadcast_to(newbase, base_ref.shape)
    cnt_ref[...] = jnp.broadcast_to(newbase, cnt_ref.shape)
    route_ref[0] = jnp.concatenate([bucket.astype(F32), rank, w_a, w_b, jnp.zeros((4, tm), F32)], axis=0)


def _out_proj(x2d, ya, yb, mod3, w_oa, w_ob, g2, wrt, brt, L):
    T = x2d.shape[0]
    nb = T // TM_OUT
    per_b = L // TM_OUT
    out_shape = [
        jax.ShapeDtypeStruct((T, D), F32),
        jax.ShapeDtypeStruct((T, D), F32),
        jax.ShapeDtypeStruct((nb, 8, TM_OUT), F32),
        jax.ShapeDtypeStruct((32, 128), F32),
    ]
    return pl.pallas_call(
        _out_kernel,
        out_shape=out_shape,
        grid=(nb,),
        in_specs=[
            pl.BlockSpec((TM_OUT, D), lambda i: (i, 0)),
            pl.BlockSpec((TM_OUT, D_A), lambda i: (i, 0)),
            pl.BlockSpec((TM_OUT, D_B), lambda i: (i, 0)),
            pl.BlockSpec((1, 1, 6 * D), lambda i: (i // per_b, 0, 0)),
            pl.BlockSpec((D_A, D), lambda i: (0, 0)),
            pl.BlockSpec((D_B, D), lambda i: (0, 0)),
            pl.BlockSpec((1, D), lambda i: (0, 0)),
            pl.BlockSpec((32, D), lambda i: (0, 0)),
            pl.BlockSpec((32, 1), lambda i: (0, 0)),
        ],
        out_specs=[
            pl.BlockSpec((TM_OUT, D), lambda i: (i, 0)),
            pl.BlockSpec((TM_OUT, D), lambda i: (i, 0)),
            pl.BlockSpec((1, 8, TM_OUT), lambda i: (i, 0, 0)),
            pl.BlockSpec((32, 128), lambda i: (0, 0)),
        ],
        scratch_shapes=[pltpu.VMEM((32, 128), F32)],
        compiler_params=pltpu.CompilerParams(dimension_semantics=("arbitrary",),
                                             vmem_limit_bytes=VMEM_LIMIT),
    )(x2d, ya, yb, mod3, w_oa, w_ob, g2, wrt, brt)


def _row_gather_start(idx_ref, base, src_hbm, buf, slot, sem, n):
    def body(r, carry):
        idx = idx_ref[base + r]
        pltpu.make_async_copy(src_hbm.at[pl.ds(idx, 1)], buf.at[slot, pl.ds(r, 1)], sem.at[slot]).start()
        return carry
    lax.fori_loop(0, n, body, 0, unroll=8)


def _row_gather_wait(buf, slot, sem):
    pltpu.make_async_copy(buf.at[slot], buf.at[slot], sem.at[slot]).wait()


def _moe_kernel(ea_ref, eb_ref, nv_ref, tok_ref, h_hbm, wab_ref, wga_ref, wgb_ref, wda_ref, wdb_ref,
                o_ref, buf, sem):
    i = pl.program_id(0)
    n = pl.num_programs(0)
    slot = lax.rem(i, 2)

    @pl.when(i == 0)
    def _():
        _row_gather_start(tok_ref, 0, h_hbm, buf, 0, sem, BM_MOE)

    @pl.when(i + 1 < n)
    def _():
        _row_gather_start(tok_ref, (i + 1) * BM_MOE, h_hbm, buf, 1 - slot, sem, BM_MOE)

    _row_gather_wait(buf, slot, sem)

    @pl.when(nv_ref[i] > 0)
    def _():
        xb = buf[slot].astype(BF16)
        wab = wab_ref[...]
        ga = _dot(xb, wga_ref[0])
        act_a = (_silu(ga[:, :D_EXPERT]) * ga[:, D_EXPERT:] * wab[:, 0:1]).astype(BF16)
        gb = _dot(xb, wgb_ref[0])
        act_b = (_silu(gb[:, :D_EXPERT]) * gb[:, D_EXPERT:] * wab[:, 1:2]).astype(BF16)
        o_ref[...] = _dot(act_a, wda_ref[0]) + _dot(act_b, wdb_ref[0])

    @pl.when(nv_ref[i] <= 0)
    def _():
        o_ref[...] = jnp.zeros_like(o_ref)


def _moe(ea, eb, nvalid, tok, h, wab_sorted, w_gu, w_dn):
    nblk = ea.shape[0]
    S = nblk * BM_MOE
    grid_spec = pltpu.PrefetchScalarGridSpec(
        num_scalar_prefetch=4,
        grid=(nblk,),
        in_specs=[
            pl.BlockSpec(memory_space=pl.ANY),
            pl.BlockSpec((BM_MOE, 2), lambda i, ea, eb, nv, tok: (i, 0)),
            pl.BlockSpec((1, D, 2 * D_EXPERT), lambda i, ea, eb, nv, tok: (ea[i], 0, 0)),
            pl.BlockSpec((1, D, 2 * D_EXPERT), lambda i, ea, eb, nv, tok: (eb[i], 0, 0)),
            pl.BlockSpec((1, D_EXPERT, D), lambda i, ea, eb, nv, tok: (ea[i], 0, 0)),
            pl.BlockSpec((1, D_EXPERT, D), lambda i, ea, eb, nv, tok: (eb[i], 0, 0)),
        ],
        out_specs=pl.BlockSpec((BM_MOE, D), lambda i, ea, eb, nv, tok: (i, 0)),
        scratch_shapes=[pltpu.VMEM((2, BM_MOE, D), F32), pltpu.SemaphoreType.DMA((2,))],
    )
    return pl.pallas_call(
        _moe_kernel,
        out_shape=jax.ShapeDtypeStruct((S, D), F32),
        grid_spec=grid_spec,
        compiler_params=pltpu.CompilerParams(dimension_semantics=("arbitrary",),
                                             vmem_limit_bytes=VMEM_LIMIT),
    )(ea, eb, nvalid, tok, h, wab_sorted, w_gu, w_gu, w_dn, w_dn)


def _final_kernel(pos_ref, x2_ref, mod_ref, fg_ref, ms_hbm, o_ref, buf, sem):
    i = pl.program_id(0)
    n = pl.num_programs(0)
    slot = lax.rem(i, 2)

    @pl.when(i == 0)
    def _():
        _row_gather_start(pos_ref, 0, ms_hbm, buf, 0, sem, TM_FIN)

    @pl.when(i + 1 < n)
    def _():
        _row_gather_start(pos_ref, (i + 1) * TM_FIN, ms_hbm, buf, 1 - slot, sem, TM_FIN)

    _row_gather_wait(buf, slot, sem)
    gt2 = mod_ref[0][:, 5 * D:6 * D]
    y = x2_ref[...] + gt2 * buf[slot]
    ms = jnp.mean(y * y, axis=-1, keepdims=True)
    o_ref[...] = y * lax.rsqrt(ms + EPS) * fg_ref[...]


def _final(pos, x2, mod3, fg, ms, L):
    T = x2.shape[0]
    per_b = L // TM_FIN
    grid_spec = pltpu.PrefetchScalarGridSpec(
        num_scalar_prefetch=1,
        grid=(T // TM_FIN,),
        in_specs=[
            pl.BlockSpec((TM_FIN, D), lambda i, pos: (i, 0)),
            pl.BlockSpec((1, 1, 6 * D), lambda i, pos: (i // per_b, 0, 0)),
            pl.BlockSpec((1, D), lambda i, pos: (0, 0)),
            pl.BlockSpec(memory_space=pl.ANY),
        ],
        out_specs=pl.BlockSpec((TM_FIN, D), lambda i, pos: (i, 0)),
        scratch_shapes=[pltpu.VMEM((2, TM_FIN, D), F32), pltpu.SemaphoreType.DMA((2,))],
    )
    return pl.pallas_call(
        _final_kernel,
        out_shape=jax.ShapeDtypeStruct((T, D), F32),
        grid_spec=grid_spec,
        compiler_params=pltpu.CompilerParams(dimension_semantics=("arbitrary",),
                                             vmem_limit_bytes=VMEM_LIMIT),
    )(pos, x2, mod3, fg, ms)


def kernel(x, c, ctx, c_ctx, w_ada, b_ada, norm1_g, w_in, ln_a_g, ln_a_b, w_spatial, b_spatial, conv_qkv, a_log,
           dt_bias, onorm_g, w_out, norm2_g, w_group, b_group, w_router, b_router, w_gate_up, w_down, final_g):
    B, L, _ = x.shape
    T = B * L
    assert w_ada.shape[0] == 1 and ctx.shape[1] == TM_IN and L % TM_OUT == 0

    cond = jnp.concatenate([c, c_ctx[None, :], jnp.zeros((7, D), F32)], axis=0)
    mod = _modulation(cond, w_ada[0], b_ada[0][None, :])
    mod_lat = mod[:B].reshape(B, 1, 6 * D)
    mod_ctx = mod[B:B + 1]

    w_main = w_in[0][:, :N_MAIN].astype(BF16)
    w_ab = jnp.pad(w_in[0][:, N_MAIN:], ((0, 0), (0, 128 - N_AB))).astype(BF16)
    w_abt = w_in[0][:, N_MAIN:].T.astype(BF16)
    alog = a_log[0].reshape(1, 2 * HEADS)
    dtb = dt_bias[0].reshape(1, 2 * HEADS)
    alog_row = jnp.pad(alog, ((0, 0), (0, 128 - 2 * HEADS)))
    dtb_row = jnp.pad(dtb, ((0, 0), (0, 128 - 2 * HEADS)))
    alog_col = jnp.pad(alog, ((0, 0), (0, N_AB - 2 * HEADS))).T
    dtb_col = jnp.pad(dtb, ((0, 0), (0, N_AB - 2 * HEADS))).T

    ya, qkv, z, gb, gbt = _in_proj(
        x, ctx, mod_lat, mod_ctx, norm1_g, w_main, w_ab, w_abt, ln_a_g, ln_a_b,
        w_spatial[0].astype(BF16), b_spatial[0].T, conv_qkv[0], alog_row, dtb_row, alog_col, dtb_col)

    yb = _delta(qkv, z, gb, gbt, onorm_g, L)

    wrt = jnp.concatenate([w_group[0].T, w_router[0].T, jnp.zeros((32 - N_GROUPS - N_EXPERTS, D), F32)], axis=0)
    brt = jnp.concatenate([b_group[0], b_router[0], jnp.zeros((32 - N_GROUPS - N_EXPERTS,), F32)])[:, None]
    w_o = w_out[0].astype(BF16)
    x2, h, route, cnt = _out_proj(x.reshape(T, D), ya.reshape(T, D_A), yb.reshape(T, D_B), mod_lat,
                                  w_o[:D_A], w_o[D_A:], norm2_g, wrt.astype(BF16), brt, L)

    bucket = route[:, 0, :].reshape(T).astype(jnp.int32)
    rank = route[:, 1, :].reshape(T).astype(jnp.int32)
    wab = jnp.stack([route[:, 2, :].reshape(T), route[:, 3, :].reshape(T)], axis=1)
    counts = cnt[:N_BUCKETS, 0].astype(jnp.int32)
    nblk_b = (counts + BM_MOE - 1) // BM_MOE
    blk_end = jnp.cumsum(nblk_b)
    blk_start = blk_end - nblk_b
    pos = blk_start[bucket] * BM_MOE + rank
    n_blocks = T // BM_MOE + N_BUCKETS
    S = n_blocks * BM_MOE
    tok = jnp.zeros((S,), jnp.int32).at[pos].set(jnp.arange(T, dtype=jnp.int32))
    wab_sorted = jnp.zeros((S, 2), F32).at[pos].set(wab)
    blk = jnp.arange(n_blocks, dtype=jnp.int32)
    used = blk < blk_end[-1]
    bkt = jnp.sum((jnp.minimum(blk, blk_end[-1] - 1)[:, None] >= blk_end[None, :]).astype(jnp.int32), axis=1)
    nvalid = jnp.where(used, jnp.clip(counts[bkt] - (blk - blk_start[bkt]) * BM_MOE, 0, BM_MOE), 0).astype(jnp.int32)
    pa = jnp.asarray(PAIR_A, jnp.int32)
    pb = jnp.asarray(PAIR_B, jnp.int32)
    ea = (bkt // N_PAIRS) * EPG + pa[bkt % N_PAIRS]
    eb = (bkt // N_PAIRS) * EPG + pb[bkt % N_PAIRS]

    ms = _moe(ea, eb, nvalid, tok, h, wab_sorted, w_gate_up[0].astype(BF16), w_down[0].astype(BF16))
    out = _final(pos, x2, mod_lat, final_g[None, :], ms, L)
    return out.reshape(B, L, D)
```

```python
import functools

import jax
import jax.numpy as jnp
import numpy as np
from jax import lax
from jax.experimental import pallas as pl
from jax.experimental.pallas import tpu as pltpu
from jax.experimental.pallas import tpu_sc as plsc

F32 = jnp.float32
BF16 = jnp.bfloat16
EPS = 1e-6

D = 1024
D_A = 512
D_B = 512
HEADS = 4
HD = 128
CHUNK = 128
CONV_W = 5
N_QKV = 3 * D_B
N_MAIN = 2 * D_A + 4 * D_B
N_AB = 16
N_GROUPS = 4
EPG = 4
N_EXPERTS = 16
D_EXPERT = 512
N_PAIRS = 6
N_BUCKETS = N_GROUPS * N_PAIRS
PAIR_A = (0, 0, 0, 1, 1, 2)
PAIR_B = (1, 2, 3, 2, 3, 3)

TM_IN = 256
HALO = 8
TM_OUT = 512
BM_MOE = 256
TM_FIN = 512
VMEM_LIMIT = 56 * 1024 * 1024
SC_CORES = 2
SC_SUBCORES = 16
SC_WINDOW = 32

HI = lax.Precision.HIGHEST


def _dot(a, b, precision=None):
    return jnp.dot(a, b, preferred_element_type=F32, precision=precision)


def _dot_nt(a, b):
    return lax.dot_general(a, b, (((1,), (1,)), ((), ())), preferred_element_type=F32)


def _dot_tn(a, b):
    return lax.dot_general(a, b, (((0,), (0,)), ((), ())), preferred_element_type=F32)


def _sigmoid(x):
    return 1.0 / (1.0 + jnp.exp(-x))


def _silu(x):
    return x * _sigmoid(x)


def _softplus(x):
    return jnp.maximum(x, 0.0) + jnp.log(1.0 + jnp.exp(-jnp.abs(x)))


def _gelu_tanh(x):
    return 0.5 * x * (1.0 + jnp.tanh(np.sqrt(2.0 / np.pi).astype(np.float32) * (x + 0.044715 * (x * x * x))))


def _mod_kernel(c_ref, w_ref, b_ref, o_ref):
    c = c_ref[...]
    o_ref[...] = _dot(_silu(c), w_ref[...], precision=HI) + b_ref[...]


def _modulation(cond, w_ada, b_ada):
    rows = cond.shape[0]
    tn = 1536
    return pl.pallas_call(
        _mod_kernel,
        out_shape=jax.ShapeDtypeStruct((rows, 6 * D), F32),
        grid=(6 * D // tn,),
        in_specs=[pl.BlockSpec((rows, D), lambda i: (0, 0)),
                  pl.BlockSpec((D, tn), lambda i: (0, i)),
                  pl.BlockSpec((1, tn), lambda i: (0, i))],
        out_specs=pl.BlockSpec((rows, tn), lambda i: (0, i)),
        compiler_params=pltpu.CompilerParams(dimension_semantics=("arbitrary",),
                                             vmem_limit_bytes=VMEM_LIMIT),
    )(cond, w_ada, b_ada)


def _in_kernel(x_ref, xp_ref, xn_ref, ctx_ref, mod_ref, cmod_ref, g1_ref, w_ref, wab_ref, wabt_ref,
               lng_ref, lnb_ref, ws_ref, bst_ref, conv_ref, alog_ref, dtb_ref, alogt_ref, dtbt_ref,
               ya_ref, qkv_ref, z_ref, gb_ref, gbt_ref, ext_ref):
    j = pl.program_id(1)
    is_ctx = j == 0
    n_lat_blocks = pl.num_programs(1) - 1
    mod = mod_ref[0]
    cm = cmod_ref[...]
    sh = jnp.where(is_ctx, cm[:, 0:D], mod[:, 0:D])
    sc = jnp.where(is_ctx, cm[:, D:2 * D], mod[:, D:2 * D])
    scale = g1_ref[...] * (1.0 + sc)

    def normmod(xv):
        ms = jnp.mean(xv * xv, axis=-1, keepdims=True)
        return (xv * lax.rsqrt(ms + EPS) * scale + sh).astype(BF16)

    xmain = jnp.where(is_ctx, ctx_ref[0], x_ref[0])
    xe = normmod(jnp.concatenate([xp_ref[0], xmain, xn_ref[0]], axis=0))
    xb = normmod(xmain)

    pa = _dot(xe, w_ref[:, 0:2 * D_A])[HALO:HALO + TM_IN]
    ga = _gelu_tanh(pa)
    u = ga[:, :D_A]
    v = ga[:, D_A:]
    mu = jnp.mean(v, axis=-1, keepdims=True)
    vc = v - mu
    var = jnp.mean(vc * vc, axis=-1, keepdims=True)
    vn = (vc * lax.rsqrt(var + EPS) * lng_ref[...] + lnb_ref[...]).astype(BF16)
    bst = bst_ref[...]
    for n in range(TM_IN // CHUNK):
        rows = slice(n * CHUNK, (n + 1) * CHUNK)
        for h in range(HEADS):
            cols = slice(h * HD, (h + 1) * HD)
            s = _dot(ws_ref[h], vn[rows, cols]) + bst[:, h:h + 1]
            ya_ref[0, rows, cols] = (u[rows, cols] * s).astype(BF16)

    pq = _dot(xe, w_ref[:, 2 * D_A:2 * D_A + N_QKV])
    rid = lax.broadcasted_iota(jnp.int32, (TM_IN + 2 * HALO, 1), 0)
    prev_ok = j >= 2
    next_ok = jnp.logical_and(j >= 1, j < n_lat_blocks)
    valid = jnp.logical_or(jnp.logical_and(rid >= HALO, rid < HALO + TM_IN),
                           jnp.logical_or(jnp.logical_and(rid < HALO, prev_ok),
                                          jnp.logical_and(rid >= HALO + TM_IN, next_ok)))
    ext_ref[...] = jnp.where(valid, pq, 0.0)
    pad = (CONV_W - 1) // 2
    acc = None
    for t in range(CONV_W):
        term = conv_ref[t:t + 1, :] * ext_ref[pl.ds(HALO - pad + t, TM_IN), :]
        acc = term if acc is None else acc + term
    act = _silu(acc)
    for h in range(2 * HEADS):
        cols = slice(h * HD, (h + 1) * HD)
        t = act[:, cols]
        nrm = lax.rsqrt(jnp.sum(t * t, axis=-1, keepdims=True) + EPS)
        if h < HEADS:
            nrm = nrm * (HD ** -0.5)
        qkv_ref[0, :, cols] = (t * nrm).astype(BF16)
    qkv_ref[0, :, 2 * D_B:] = act[:, 2 * D_B:].astype(BF16)

    z_ref[0] = _dot(xe, w_ref[:, 2 * D_A + N_QKV:N_MAIN])[HALO:HALO + TM_IN].astype(BF16)

    r2 = lax.broadcasted_iota(jnp.int32, (TM_IN, TM_IN), 0)
    c2 = lax.broadcasted_iota(jnp.int32, (TM_IN, TM_IN), 1)
    same = (r2 // CHUNK) == (c2 // CHUNK)
    tri_l = jnp.where(jnp.logical_and(same, c2 <= r2), 1.0, 0.0).astype(F32)
    tri_u = jnp.where(jnp.logical_and(same, c2 >= r2), 1.0, 0.0).astype(F32)

    ab = _dot(xb, wab_ref[...])
    g_tok = -jnp.exp(alog_ref[...]) * _softplus(ab + dtb_ref[...])
    lane = lax.broadcasted_iota(jnp.int32, ab.shape, 1)
    gb = jnp.where(lane < HEADS, _dot(tri_l, g_tok, precision=HI),
                   jnp.where(lane < 2 * HEADS, _dot(tri_u, g_tok, precision=HI), _sigmoid(ab)))
    gb_ref[0] = gb[:, 0:N_AB]

    abt = _dot_nt(wabt_ref[...], xb)
    g_t = -jnp.exp(alogt_ref[...]) * _softplus(abt + dtbt_ref[...])
    row = lax.broadcasted_iota(jnp.int32, abt.shape, 0)
    gbt_ref[0] = jnp.where(row < HEADS, _dot(g_t, tri_u, precision=HI),
                           jnp.where(row < 2 * HEADS, _dot(g_t, tri_l, precision=HI), _sigmoid(abt)))


def _in_proj(x, ctx, mod_lat, mod_ctx, g1, w_main, w_ab, w_abt, lng, lnb, ws, bst, conv, alog, dtb, alogt, dtbt):
    B, L, _ = x.shape
    n_lat = L // TM_IN
    n_steps = n_lat + 1
    LC = L + TM_IN
    hb = TM_IN // HALO

    def full(shape):
        return pl.BlockSpec(shape, lambda b, j: (0,) * len(shape))

    in_specs = [
        pl.BlockSpec((1, TM_IN, D), lambda b, j: (b, jnp.maximum(j - 1, 0), 0)),
        pl.BlockSpec((1, HALO, D), lambda b, j: (b, jnp.clip((j - 1) * hb - 1, 0, L // HALO - 1), 0)),
        pl.BlockSpec((1, HALO, D), lambda b, j: (b, jnp.clip(j * hb, 0, L // HALO - 1), 0)),
        pl.BlockSpec((1, TM_IN, D), lambda b, j: (b, 0, 0)),
        pl.BlockSpec((1, 1, 6 * D), lambda b, j: (b, 0, 0)),
        full((1, 6 * D)), full((1, D)), full((D, N_MAIN)), full((D, 128)), full((N_AB, D)),
        full((1, D_A)), full((1, D_A)), full((HEADS, CHUNK, CHUNK)), full((CHUNK, HEADS)),
        full((CONV_W, N_QKV)), full((1, 128)), full((1, 128)), full((N_AB, 1)), full((N_AB, 1)),
    ]
    out_shape = [
        jax.ShapeDtypeStruct((B, L, D_A), BF16),
        jax.ShapeDtypeStruct((B, LC, N_QKV), BF16),
        jax.ShapeDtypeStruct((B, LC, D_B), BF16),
        jax.ShapeDtypeStruct((B, LC, N_AB), F32),
        jax.ShapeDtypeStruct((B, N_AB, LC), F32),
    ]
    out_specs = [
        pl.BlockSpec((1, TM_IN, D_A), lambda b, j: (b, jnp.maximum(j - 1, 0), 0)),
        pl.BlockSpec((1, TM_IN, N_QKV), lambda b, j: (b, j, 0)),
        pl.BlockSpec((1, TM_IN, D_B), lambda b, j: (b, j, 0)),
        pl.BlockSpec((1, TM_IN, N_AB), lambda b, j: (b, j, 0)),
        pl.BlockSpec((1, N_AB, TM_IN), lambda b, j: (b, 0, j)),
    ]
    return pl.pallas_call(
        _in_kernel,
        out_shape=out_shape,
        grid=(B, n_steps),
        in_specs=in_specs,
        out_specs=out_specs,
        scratch_shapes=[pltpu.VMEM((TM_IN + 2 * HALO, N_QKV), F32)],
        compiler_params=pltpu.CompilerParams(dimension_semantics=("arbitrary", "arbitrary"),
                                             vmem_limit_bytes=VMEM_LIMIT),
    )(x, x, x, ctx, mod_lat, mod_ctx, g1, w_main, w_ab, w_abt, lng, lnb, ws, bst, conv, alog, dtb, alogt, dtbt)


def _delta_kernel(qf_ref, qb_ref, zf_ref, zb_ref, gf_ref, gbk_ref, gtf_ref, gtb_ref, on_ref,
                  y_ref, s_ref, oacc_ref, *, n_ctx, n_lat):
    s = pl.program_id(1)

    @pl.when(s == 0)
    def _():
        s_ref[...] = jnp.zeros_like(s_ref)
        oacc_ref[...] = jnp.zeros_like(oacc_ref)

    row = lax.broadcasted_iota(jnp.int32, (CHUNK, CHUNK), 0)
    col = lax.broadcasted_iota(jnp.int32, (CHUNK, CHUNK), 1)
    low = row > col
    upp = row < col
    same_blk = (row // 16) == (col // 16)
    eye = jnp.where(row == col, 1.0, 0.0).astype(F32)
    onorm = on_ref[...]
    half = n_ctx + n_lat // 2
    second = s >= half
    gcols = (gf_ref[0], gbk_ref[0])
    gts = (gtf_ref[0], gtb_ref[0])
    qkv_refs = (qf_ref, qb_ref)
    z_refs = (zf_ref, zb_ref)
    hs = range(HEADS)

    def halves(xc, unit):
        fill = eye if unit else 0.0
        return jnp.where(low, xc, fill).astype(BF16), jnp.where(upp, xc, fill).astype(BF16)

    def as_lhs(hv):
        return jnp.concatenate(hv, axis=1)

    def as_rhs(*hvs):
        cols_ = [jnp.concatenate(hv, axis=0) for hv in hvs]
        return cols_[0] if len(cols_) == 1 else jnp.concatenate(cols_, axis=1)

    def load(d, h, part):
        return qkv_refs[d][0, :, part * D_B + h * HD:part * D_B + (h + 1) * HD]

    q = [[load(d, h, 0) for h in hs] for d in range(2)]
    k = [[load(d, h, 1) for h in hs] for d in range(2)]
    v = [[load(d, h, 2) for h in hs] for d in range(2)]
    gc = [[gcols[d][:, d * HEADS + h:d * HEADS + h + 1] for h in hs] for d in range(2)]
    beta = [[gcols[d][:, 2 * HEADS + d * HEADS + h:2 * HEADS + d * HEADS + h + 1] for h in hs] for d in range(2)]
    gr = [[gts[d][d * HEADS + h:d * HEADS + h + 1, :] for h in hs] for d in range(2)]
    glast = [[gr[0][h][:, CHUNK - 1:CHUNK] for h in hs], [gr[1][h][:, 0:1] for h in hs]]

    gram = [[_dot_nt(jnp.concatenate([q[d][h], k[d][h]], axis=0), k[d][h]) for h in hs] for d in range(2)]
    dec = [jnp.exp(jnp.where(low, gc[0][h] - gr[0][h], jnp.where(upp, gc[1][h] - gr[1][h], 0.0))) for h in hs]
    lc = [jnp.where(low, gram[0][h][CHUNK:] * beta[0][h], jnp.where(upp, gram[1][h][CHUNK:] * beta[1][h], 0.0))
          * dec[h] for h in hs]
    qk = [[jnp.where(upp, 0.0, gram[0][h][:CHUNK] * dec[h]).astype(BF16) for h in hs],
          [jnp.where(low, 0.0, gram[1][h][:CHUNK] * dec[h]).astype(BF16) for h in hs]]

    dg = [jnp.where(same_blk, lc[h], 0.0) for h in hs]
    ob = [lc[h] - dg[h] for h in hs]
    d1h = [halves(dg[h], False) for h in hs]
    d2 = [_dot(as_lhs(d1h[h]), as_rhs(d1h[h])) for h in hs]
    p0s = [-dg[h] for h in hs]
    d2h = [halves(d2[h], False) for h in hs]
    p0h = [halves(p0s[h], True) for h in hs]
    o2 = [_dot(as_lhs(d2h[h]), as_rhs(d2h[h], p0h[h])) for h in hs]
    p1s = [p0s[h] + o2[h][:, CHUNK:] for h in hs]
    d4h = [halves(o2[h][:, :CHUNK], False) for h in hs]
    p1h = [halves(p1s[h], True) for h in hs]
    o3 = [_dot(as_lhs(d4h[h]), as_rhs(d4h[h], p1h[h])) for h in hs]
    p2s = [p1s[h] + o3[h][:, CHUNK:] for h in hs]
    d8h = [halves(o3[h][:, :CHUNK], False) for h in hs]
    p2h = [halves(p2s[h], True) for h in hs]
    p3s = [p2s[h] + _dot(as_lhs(d8h[h]), as_rhs(p2h[h])) for h in hs]
    p3h = [halves(p3s[h], True) for h in hs]
    obh = [halves(ob[h], False) for h in hs]
    n1h = [halves(_dot(as_lhs(p3h[h]), as_rhs(obh[h])), False) for h in hs]
    o6 = [_dot(as_lhs(n1h[h]), as_rhs(n1h[h], p3h[h])) for h in hs]
    r0s = [p3s[h] - o6[h][:, CHUNK:] for h in hs]
    n2h = [halves(o6[h][:, :CHUNK], False) for h in hs]
    r0h = [halves(r0s[h], True) for h in hs]
    o7 = [_dot(as_lhs(n2h[h]), as_rhs(n2h[h], r0h[h])) for h in hs]
    r1s = [r0s[h] + o7[h][:, CHUNK:] for h in hs]
    n4h = [halves(o7[h][:, :CHUNK], False) for h in hs]
    r1h = [halves(r1s[h], True) for h in hs]
    tinv = [halves(r1s[h] + _dot(as_lhs(n4h[h]), as_rhs(r1h[h])), True) for h in hs]

    for d in range(2):
        lat_chunk = (s - n_ctx) if d == 0 else (n_ctx + n_lat - 1 - s)
        off = pl.multiple_of(jnp.clip(lat_chunk, 0, n_lat - 1) * CHUNK, CHUNK)
        egc = [jnp.exp(gc[d][h]) for h in hs]
        kf = [k[d][h].astype(F32) for h in hs]
        rhs = [jnp.concatenate([v[d][h].astype(F32) * beta[d][h], kf[h] * (beta[d][h] * egc[h])], axis=1).astype(BF16)
               for h in hs]
        uw = [_dot(tinv[h][d], rhs[h]) for h in hs]
        qd = [q[d][h].astype(F32) * egc[h] for h in hs]
        kd = [(kf[h] * jnp.exp(glast[d][h] - gc[d][h])).astype(BF16) for h in hs]
        st = [s_ref[d * HEADS + h] for h in hs]
        a1 = [_dot(jnp.concatenate([uw[h][:, HD:], qd[h]], axis=0).astype(BF16), st[h].astype(BF16)) for h in hs]
        vnew = [(uw[h][:, :HD] - a1[h][:CHUNK]).astype(BF16) for h in hs]
        o = [a1[h][CHUNK:] + _dot(qk[d][h], vnew[h]) for h in hs]
        for h in hs:
            s_ref[d * HEADS + h] = st[h] * jnp.exp(glast[d][h]) + _dot_tn(kd[h], vnew[h])
        for h in hs:
            cols = slice(h * HD, (h + 1) * HD)
            tot = jnp.where(second, oacc_ref[pl.ds(off, CHUNK), cols], 0.0) + o[h]
            oacc_ref[pl.ds(off, CHUNK), cols] = tot
            ms = jnp.mean(tot * tot, axis=-1, keepdims=True)
            zz = z_refs[d][0, :, cols].astype(F32)
            y_ref[0, pl.ds(off, CHUNK), cols] = (tot * lax.rsqrt(ms + EPS) * onorm * _silu(zz)).astype(BF16)


def _delta(qkv, z, gb, gbt, onorm, L):
    B, LC, _ = qkv.shape
    n_all = LC // CHUNK
    n_lat = L // CHUNK
    n_ctx = n_all - n_lat

    def cf(s):
        return s

    def cb(s):
        return jnp.where(s < n_ctx, n_ctx - 1 - s, n_all + n_ctx - 1 - s)

    in_specs = [
        pl.BlockSpec((1, CHUNK, N_QKV), lambda b, s: (b, cf(s), 0)),
        pl.BlockSpec((1, CHUNK, N_QKV), lambda b, s: (b, cb(s), 0)),
        pl.BlockSpec((1, CHUNK, D_B), lambda b, s: (b, cf(s), 0)),
        pl.BlockSpec((1, CHUNK, D_B), lambda b, s: (b, cb(s), 0)),
        pl.BlockSpec((1, CHUNK, N_AB), lambda b, s: (b, cf(s), 0)),
        pl.BlockSpec((1, CHUNK, N_AB), lambda b, s: (b, cb(s), 0)),
        pl.BlockSpec((1, N_AB, CHUNK), lambda b, s: (b, 0, cf(s))),
        pl.BlockSpec((1, N_AB, CHUNK), lambda b, s: (b, 0, cb(s))),
        pl.BlockSpec((1, HD), lambda b, s: (0, 0)),
    ]
    return pl.pallas_call(
        functools.partial(_delta_kernel, n_ctx=n_ctx, n_lat=n_lat),
        out_shape=jax.ShapeDtypeStruct((B, L, D_B), BF16),
        grid=(B, n_all),
        in_specs=in_specs,
        out_specs=pl.BlockSpec((1, L, D_B), lambda b, s: (b, 0, 0)),
        scratch_shapes=[pltpu.VMEM((2 * HEADS, HD, HD), F32), pltpu.VMEM((L, D_B), F32)],
        compiler_params=pltpu.CompilerParams(dimension_semantics=("arbitrary", "arbitrary"),
                                             vmem_limit_bytes=VMEM_LIMIT),
    )(qkv, qkv, z, z, gb, gb, gbt, gbt, onorm)


def _out_kernel(x_ref, ya_ref, yb_ref, mod_ref, wa_ref, wb_ref, g2_ref, wrt_ref, brt_ref,
                x2_ref, h_ref, route_ref, cnt_ref, base_ref):
    i = pl.program_id(0)

    @pl.when(i == 0)
    def _():
        base_ref[...] = jnp.zeros_like(base_ref)

    mod = mod_ref[0]
    gt1 = mod[:, 2 * D:3 * D]
    sh2 = mod[:, 3 * D:4 * D]
    sc2 = mod[:, 4 * D:5 * D]
    mix = _dot(ya_ref[...], wa_ref[...]) + _dot(yb_ref[...], wb_ref[...])
    x2 = x_ref[...] + gt1 * mix
    x2_ref[...] = x2
    ms = jnp.mean(x2 * x2, axis=-1, keepdims=True)
    hv = x2 * lax.rsqrt(ms + EPS) * (g2_ref[...] * (1.0 + sc2)) + sh2
    h_ref[...] = hv

    lt = _dot_nt(wrt_ref[...], hv.astype(BF16)) + brt_ref[...]
    gl = [lt[r:r + 1, :] for r in range(N_GROUPS)]
    gmax = jnp.maximum(jnp.maximum(gl[0], gl[1]), jnp.maximum(gl[2], gl[3]))
    gsel = jnp.where(gl[0] == gmax, 0, jnp.where(gl[1] == gmax, 1, jnp.where(gl[2] == gmax, 2, 3)))
    p_g = 1.0 / (jnp.exp(gl[0] - gmax) + jnp.exp(gl[1] - gmax) + jnp.exp(gl[2] - gmax) + jnp.exp(gl[3] - gmax))
    el = []
    for e in range(EPG):
        r = [lt[N_GROUPS + g * EPG + e:N_GROUPS + g * EPG + e + 1, :] for g in range(N_GROUPS)]
        el.append(jnp.where(gsel == 0, r[0], jnp.where(gsel == 1, r[1], jnp.where(gsel == 2, r[2], r[3]))))
    m1 = jnp.maximum(jnp.maximum(el[0], el[1]), jnp.maximum(el[2], el[3]))
    i1 = jnp.where(el[0] == m1, 0, jnp.where(el[1] == m1, 1, jnp.where(el[2] == m1, 2, 3)))
    neg = jnp.float32(-jnp.inf)
    el2 = [jnp.where(i1 == e, neg, el[e]) for e in range(EPG)]
    m2 = jnp.maximum(jnp.maximum(el2[0], el2[1]), jnp.maximum(el2[2], el2[3]))
    i2 = jnp.where(jnp.logical_and(el2[0] == m2, i1 != 0), 0,
                   jnp.where(jnp.logical_and(el2[1] == m2, i1 != 1), 1,
                             jnp.where(jnp.logical_and(el2[2] == m2, i1 != 2), 2, 3)))
    t = jnp.exp(m2 - m1)
    w1 = p_g / (1.0 + t)
    w2 = p_g * t / (1.0 + t)
    first_low = i1 < i2
    ea = jnp.where(first_low, i1, i2)
    eb = jnp.where(first_low, i2, i1)
    w_a = jnp.where(first_low, w1, w2)
    w_b = jnp.where(first_low, w2, w1)
    pair = jnp.where(ea == 0, eb - 1, jnp.where(ea == 1, eb + 1, 5))
    bucket = gsel * N_PAIRS + pair

    tm = bucket.shape[1]
    rows = lax.broadcasted_iota(jnp.int32, (32, tm), 0)
    onehot = jnp.where(rows == bucket, 1.0, 0.0).astype(F32)
    r2 = lax.broadcasted_iota(jnp.int32, (tm, tm), 0)
    c2 = lax.broadcasted_iota(jnp.int32, (tm, tm), 1)
    tri = jnp.where(r2 <= c2, 1.0, 0.0).astype(BF16)
    prefix = _dot(onehot.astype(BF16), tri)
    base = base_ref[:, 0:1]
    rank = jnp.sum(onehot * (prefix - 1.0 + base), axis=0, keepdims=True)
    newbase = base + prefix[:, tm - 1:tm]
    base_ref[...] = jnp.broadcast_to(newbase, base_ref.shape)
    cnt_ref[...] = jnp.broadcast_to(newbase, cnt_ref.shape)
    route_ref[0] = jnp.concatenate([bucket.astype(F32), rank, w_a, w_b, jnp.zeros((4, tm), F32)], axis=0)


def _out_proj(x2d, ya, yb, mod3, w_oa, w_ob, g2, wrt, brt, L):
    T = x2d.shape[0]
    nb = T // TM_OUT
    per_b = L // TM_OUT
    out_shape = [
        jax.ShapeDtypeStruct((T, D), F32),
        jax.ShapeDtypeStruct((T, D), F32),
        jax.ShapeDtypeStruct((nb, 8, TM_OUT), F32),
        jax.ShapeDtypeStruct((32, 128), F32),
    ]
    return pl.pallas_call(
        _out_kernel,
        out_shape=out_shape,
        grid=(nb,),
        in_specs=[
            pl.BlockSpec((TM_OUT, D), lambda i: (i, 0)),
            pl.BlockSpec((TM_OUT, D_A), lambda i: (i, 0)),
            pl.BlockSpec((TM_OUT, D_B), lambda i: (i, 0)),
            pl.BlockSpec((1, 1, 6 * D), lambda i: (i // per_b, 0, 0)),
            pl.BlockSpec((D_A, D), lambda i: (0, 0)),
            pl.BlockSpec((D_B, D), lambda i: (0, 0)),
            pl.BlockSpec((1, D), lambda i: (0, 0)),
            pl.BlockSpec((32, D), lambda i: (0, 0)),
            pl.BlockSpec((32, 1), lambda i: (0, 0)),
        ],
        out_specs=[
            pl.BlockSpec((TM_OUT, D), lambda i: (i, 0)),
            pl.BlockSpec((TM_OUT, D), lambda i: (i, 0)),
            pl.BlockSpec((1, 8, TM_OUT), lambda i: (i, 0, 0)),
            pl.BlockSpec((32, 128), lambda i: (0, 0)),
        ],
        scratch_shapes=[pltpu.VMEM((32, 128), F32)],
        compiler_params=pltpu.CompilerParams(dimension_semantics=("arbitrary",),
                                             vmem_limit_bytes=VMEM_LIMIT),
    )(x2d, ya, yb, mod3, w_oa, w_ob, g2, wrt, brt)


def _sc_row_gather(table, idx):
    n_rows = idx.shape[0]
    width = table.shape[1]
    n_workers = SC_CORES * SC_SUBCORES
    per_worker = n_rows // n_workers
    assert per_worker * n_workers == n_rows and per_worker % SC_WINDOW == 0
    mesh = plsc.VectorSubcoreMesh(core_axis_name="c", subcore_axis_name="s")

    @functools.partial(
        pl.kernel, mesh=mesh,
        out_type=jax.ShapeDtypeStruct((n_rows, width), table.dtype),
        scratch_types=[pltpu.VMEM((SC_WINDOW,), jnp.int32),
                       pltpu.VMEM((SC_WINDOW, width), table.dtype),
                       pltpu.SemaphoreType.DMA],
    )
    def gather(table_hbm, idx_hbm, out_hbm, idx_v, rows_v, sem):
        wid = lax.axis_index("s") * SC_CORES + lax.axis_index("c")
        base = wid * per_worker

        @pl.loop(0, per_worker // SC_WINDOW)
        def _(j):
            off = pl.multiple_of(base + j * SC_WINDOW, 8)
            pltpu.sync_copy(idx_hbm.at[pl.ds(off, SC_WINDOW)], idx_v)
            pltpu.async_copy(table_hbm.at[idx_v], rows_v, sem).wait()
            pltpu.sync_copy(rows_v, out_hbm.at[pl.ds(off, SC_WINDOW)])

    return gather(table, idx)


def _moe_kernel(ea_ref, eb_ref, nv_ref, hs_ref, wab_ref, wga_ref, wgb_ref, wda_ref, wdb_ref, o_ref):
    i = pl.program_id(0)

    @pl.when(nv_ref[i] > 0)
    def _():
        xb = hs_ref[...].astype(BF16)
        wab = wab_ref[...]
        ga = _dot(xb, wga_ref[0])
        act_a = (_silu(ga[:, :D_EXPERT]) * ga[:, D_EXPERT:] * wab[:, 0:1]).astype(BF16)
        gb = _dot(xb, wgb_ref[0])
        act_b = (_silu(gb[:, :D_EXPERT]) * gb[:, D_EXPERT:] * wab[:, 1:2]).astype(BF16)
        o_ref[...] = _dot(act_a, wda_ref[0]) + _dot(act_b, wdb_ref[0])

    @pl.when(nv_ref[i] <= 0)
    def _():
        o_ref[...] = jnp.zeros_like(o_ref)


def _moe(ea, eb, nvalid, hs, wab_sorted, w_gu, w_dn):
    nblk = ea.shape[0]
    S = nblk * BM_MOE
    grid_spec = pltpu.PrefetchScalarGridSpec(
        num_scalar_prefetch=3,
        grid=(nblk,),
        in_specs=[
            pl.BlockSpec((BM_MOE, D), lambda i, ea, eb, nv: (i, 0)),
            pl.BlockSpec((BM_MOE, 2), lambda i, ea, eb, nv: (i, 0)),
            pl.BlockSpec((1, D, 2 * D_EXPERT), lambda i, ea, eb, nv: (ea[i], 0, 0)),
            pl.BlockSpec((1, D, 2 * D_EXPERT), lambda i, ea, eb, nv: (eb[i], 0, 0)),
            pl.BlockSpec((1, D_EXPERT, D), lambda i, ea, eb, nv: (ea[i], 0, 0)),
            pl.BlockSpec((1, D_EXPERT, D), lambda i, ea, eb, nv: (eb[i], 0, 0)),
        ],
        out_specs=pl.BlockSpec((BM_MOE, D), lambda i, ea, eb, nv: (i, 0)),
    )
    return pl.pallas_call(
        _moe_kernel,
        out_shape=jax.ShapeDtypeStruct((S, D), F32),
        grid_spec=grid_spec,
        compiler_params=pltpu.CompilerParams(dimension_semantics=("arbitrary",),
                                             vmem_limit_bytes=VMEM_LIMIT),
    )(ea, eb, nvalid, hs, wab_sorted, w_gu, w_gu, w_dn, w_dn)


def _final_kernel(x2_ref, m_ref, mod_ref, fg_ref, o_ref):
    gt2 = mod_ref[0][:, 5 * D:6 * D]
    y = x2_ref[...] + gt2 * m_ref[...]
    ms = jnp.mean(y * y, axis=-1, keepdims=True)
    o_ref[...] = y * lax.rsqrt(ms + EPS) * fg_ref[...]


def _final(x2, m, mod3, fg, L):
    T = x2.shape[0]
    per_b = L // TM_FIN
    return pl.pallas_call(
        _final_kernel,
        out_shape=jax.ShapeDtypeStruct((T, D), F32),
        grid=(T // TM_FIN,),
        in_specs=[
            pl.BlockSpec((TM_FIN, D), lambda i: (i, 0)),
            pl.BlockSpec((TM_FIN, D), lambda i: (i, 0)),
            pl.BlockSpec((1, 1, 6 * D), lambda i: (i // per_b, 0, 0)),
            pl.BlockSpec((1, D), lambda i: (0, 0)),
        ],
        out_specs=pl.BlockSpec((TM_FIN, D), lambda i: (i, 0)),
        compiler_params=pltpu.CompilerParams(dimension_semantics=("arbitrary",),
                                             vmem_limit_bytes=VMEM_LIMIT),
    )(x2, m, mod3, fg)


def kernel(x, c, ctx, c_ctx, w_ada, b_ada, norm1_g, w_in, ln_a_g, ln_a_b, w_spatial, b_spatial, conv_qkv, a_log,
           dt_bias, onorm_g, w_out, norm2_g, w_group, b_group, w_router, b_router, w_gate_up, w_down, final_g):
    B, L, _ = x.shape
    T = B * L
    assert w_ada.shape[0] == 1 and ctx.shape[1] == TM_IN and L % TM_OUT == 0

    cond = jnp.concatenate([c, c_ctx[None, :], jnp.zeros((7, D), F32)], axis=0)
    mod = _modulation(cond, w_ada[0], b_ada[0][None, :])
    mod_lat = mod[:B].reshape(B, 1, 6 * D)
    mod_ctx = mod[B:B + 1]

    w_main = w_in[0][:, :N_MAIN].astype(BF16)
    w_ab = jnp.pad(w_in[0][:, N_MAIN:], ((0, 0), (0, 128 - N_AB))).astype(BF16)
    w_abt = w_in[0][:, N_MAIN:].T.astype(BF16)
    alog = a_log[0].reshape(1, 2 * HEADS)
    dtb = dt_bias[0].reshape(1, 2 * HEADS)
    alog_row = jnp.pad(alog, ((0, 0), (0, 128 - 2 * HEADS)))
    dtb_row = jnp.pad(dtb, ((0, 0), (0, 128 - 2 * HEADS)))
    alog_col = jnp.pad(alog, ((0, 0), (0, N_AB - 2 * HEADS))).T
    dtb_col = jnp.pad(dtb, ((0, 0), (0, N_AB - 2 * HEADS))).T

    ya, qkv, z, gb, gbt = _in_proj(
        x, ctx, mod_lat, mod_ctx, norm1_g, w_main, w_ab, w_abt, ln_a_g, ln_a_b,
        w_spatial[0].astype(BF16), b_spatial[0].T, conv_qkv[0], alog_row, dtb_row, alog_col, dtb_col)

    yb = _delta(qkv, z, gb, gbt, onorm_g, L)

    wrt = jnp.concatenate([w_group[0].T, w_router[0].T, jnp.zeros((32 - N_GROUPS - N_EXPERTS, D), F32)], axis=0)
    brt = jnp.concatenate([b_group[0], b_router[0], jnp.zeros((32 - N_GROUPS - N_EXPERTS,), F32)])[:, None]
    w_o = w_out[0].astype(BF16)
    x2, h, route, cnt = _out_proj(x.reshape(T, D), ya.reshape(T, D_A), yb.reshape(T, D_B), mod_lat,
                                  w_o[:D_A], w_o[D_A:], norm2_g, wrt.astype(BF16), brt, L)

    bucket = route[:, 0, :].reshape(T).astype(jnp.int32)
    rank = route[:, 1, :].reshape(T).astype(jnp.int32)
    wab = jnp.stack([route[:, 2, :].reshape(T), route[:, 3, :].reshape(T)], axis=1)
    counts = cnt[:N_BUCKETS, 0].astype(jnp.int32)
    nblk_b = (counts + BM_MOE - 1) // BM_MOE
    blk_end = jnp.cumsum(nblk_b)
    blk_start = blk_end - nblk_b
    pos = blk_start[bucket] * BM_MOE + rank
    n_blocks = T // BM_MOE + N_BUCKETS
    S = n_blocks * BM_MOE
    tok = jnp.zeros((S,), jnp.int32).at[pos].set(jnp.arange(T, dtype=jnp.int32))
    wab_sorted = jnp.zeros((S, 2), F32).at[pos].set(wab)
    blk = jnp.arange(n_blocks, dtype=jnp.int32)
    used = blk < blk_end[-1]
    bkt = jnp.sum((jnp.minimum(blk, blk_end[-1] - 1)[:, None] >= blk_end[None, :]).astype(jnp.int32), axis=1)
    nvalid = jnp.where(used, jnp.clip(counts[bkt] - (blk - blk_start[bkt]) * BM_MOE, 0, BM_MOE), 0).astype(jnp.int32)
    pa = jnp.asarray(PAIR_A, jnp.int32)
    pb = jnp.asarray(PAIR_B, jnp.int32)
    ea = (bkt // N_PAIRS) * EPG + pa[bkt % N_PAIRS]
    eb = (bkt // N_PAIRS) * EPG + pb[bkt % N_PAIRS]

    hs = _sc_row_gather(h, tok)
    ms = _moe(ea, eb, nvalid, hs, wab_sorted, w_gate_up[0].astype(BF16), w_down[0].astype(BF16))
    out = _final(x2, _sc_row_gather(ms, pos), mod_lat, final_g[None, :], L)
    return out.reshape(B, L, D)
```

```python
import functools

import jax
import jax.numpy as jnp
import numpy as np
from jax import lax
from jax.experimental import pallas as pl
from jax.experimental.pallas import tpu as pltpu
from jax.experimental.pallas import tpu_sc as plsc

F32 = jnp.float32
BF16 = jnp.bfloat16
EPS = 1e-6

D = 1024
D_A = 512
D_B = 512
HEADS = 4
HD = 128
CHUNK = 128
CONV_W = 5
N_QKV = 3 * D_B
N_MAIN = 2 * D_A + 4 * D_B
N_AB = 16
N_GROUPS = 4
EPG = 4
N_EXPERTS = 16
D_EXPERT = 512
N_PAIRS = 6
N_BUCKETS = N_GROUPS * N_PAIRS
D_ROW = D + 128
LANE_WA = D + 2
PAIR_A = (0, 0, 0, 1, 1, 2)
PAIR_B = (1, 2, 3, 2, 3, 3)

TM_IN = 256
HALO = 8
TM_OUT = 512
BM_MOE = 256
TM_FIN = 512
VMEM_LIMIT = 56 * 1024 * 1024
SC_CORES = 2
SC_SUBCORES = 16
SC_WINDOW = 32

HI = lax.Precision.HIGHEST


def _dot(a, b, precision=None):
    return jnp.dot(a, b, preferred_element_type=F32, precision=precision)


def _dot_nt(a, b):
    return lax.dot_general(a, b, (((1,), (1,)), ((), ())), preferred_element_type=F32)


def _dot_tn(a, b):
    return lax.dot_general(a, b, (((0,), (0,)), ((), ())), preferred_element_type=F32)


def _sigmoid(x):
    return 1.0 / (1.0 + jnp.exp(-x))


def _silu(x):
    return x * _sigmoid(x)


def _softplus(x):
    return jnp.maximum(x, 0.0) + jnp.log(1.0 + jnp.exp(-jnp.abs(x)))


def _gelu_tanh(x):
    return 0.5 * x * (1.0 + jnp.tanh(np.sqrt(2.0 / np.pi).astype(np.float32) * (x + 0.044715 * (x * x * x))))


def _mod_kernel(c_ref, w_ref, b_ref, o_ref):
    c = c_ref[...]
    o_ref[...] = _dot(_silu(c), w_ref[...], precision=HI) + b_ref[...]


def _modulation(cond, w_ada, b_ada):
    rows = cond.shape[0]
    tn = 1536
    return pl.pallas_call(
        _mod_kernel,
        out_shape=jax.ShapeDtypeStruct((rows, 6 * D), F32),
        grid=(6 * D // tn,),
        in_specs=[pl.BlockSpec((rows, D), lambda i: (0, 0)),
                  pl.BlockSpec((D, tn), lambda i: (0, i)),
                  pl.BlockSpec((1, tn), lambda i: (0, i))],
        out_specs=pl.BlockSpec((rows, tn), lambda i: (0, i)),
        compiler_params=pltpu.CompilerParams(dimension_semantics=("arbitrary",),
                                             vmem_limit_bytes=VMEM_LIMIT),
    )(cond, w_ada, b_ada)


def _in_kernel(x_ref, xp_ref, xn_ref, ctx_ref, mod_ref, cmod_ref, g1_ref, w_ref, wab_ref, wabt_ref,
               lng_ref, lnb_ref, ws_ref, bst_ref, conv_ref, alog_ref, dtb_ref, alogt_ref, dtbt_ref,
               ya_ref, qkv_ref, z_ref, gb_ref, gbt_ref, ext_ref, wbf_ref):
    j = pl.program_id(1)

    @pl.when(jnp.logical_and(pl.program_id(0) == 0, j == 0))
    def _():
        wbf_ref[...] = w_ref[...].astype(BF16)

    is_ctx = j == 0
    n_lat_blocks = pl.num_programs(1) - 1
    mod = mod_ref[0]
    cm = cmod_ref[...]
    sh = jnp.where(is_ctx, cm[:, 0:D], mod[:, 0:D])
    sc = jnp.where(is_ctx, cm[:, D:2 * D], mod[:, D:2 * D])
    scale = g1_ref[...] * (1.0 + sc)

    xmain = jnp.where(is_ctx, ctx_ref[0], x_ref[0])
    xv = jnp.concatenate([xp_ref[0], xmain, xn_ref[0]], axis=0)
    xnorm = xv * lax.rsqrt(jnp.mean(xv * xv, axis=-1, keepdims=True) + EPS) * scale + sh
    xe = xnorm.astype(BF16)
    xb = xnorm[HALO:HALO + TM_IN].astype(BF16)

    pa = _dot(xe, wbf_ref[:, 0:2 * D_A])[HALO:HALO + TM_IN]
    ga = _gelu_tanh(pa)
    u = ga[:, :D_A]
    v = ga[:, D_A:]
    mu = jnp.mean(v, axis=-1, keepdims=True)
    vc = v - mu
    var = jnp.mean(vc * vc, axis=-1, keepdims=True)
    vn = (vc * lax.rsqrt(var + EPS) * lng_ref[...] + lnb_ref[...]).astype(BF16)
    bst = bst_ref[...]
    for n in range(TM_IN // CHUNK):
        rows = slice(n * CHUNK, (n + 1) * CHUNK)
        for h in range(HEADS):
            cols = slice(h * HD, (h + 1) * HD)
            s = _dot(ws_ref[h], vn[rows, cols]) + bst[:, h:h + 1]
            ya_ref[0, rows, cols] = (u[rows, cols] * s).astype(BF16)

    pq = _dot(xe, wbf_ref[:, 2 * D_A:2 * D_A + N_QKV])
    rid = lax.broadcasted_iota(jnp.int32, (TM_IN + 2 * HALO, 1), 0)
    prev_ok = j >= 2
    next_ok = jnp.logical_and(j >= 1, j < n_lat_blocks)
    valid = jnp.logical_or(jnp.logical_and(rid >= HALO, rid < HALO + TM_IN),
                           jnp.logical_or(jnp.logical_and(rid < HALO, prev_ok),
                                          jnp.logical_and(rid >= HALO + TM_IN, next_ok)))
    ext_ref[...] = jnp.where(valid, pq, 0.0)
    pad = (CONV_W - 1) // 2
    acc = None
    for t in range(CONV_W):
        term = conv_ref[t:t + 1, :] * ext_ref[pl.ds(HALO - pad + t, TM_IN), :]
        acc = term if acc is None else acc + term
    act = _silu(acc)
    for h in range(2 * HEADS):
        cols = slice(h * HD, (h + 1) * HD)
        t = act[:, cols]
        nrm = lax.rsqrt(jnp.sum(t * t, axis=-1, keepdims=True) + EPS)
        if h < HEADS:
            nrm = nrm * (HD ** -0.5)
        qkv_ref[0, :, cols] = (t * nrm).astype(BF16)
    qkv_ref[0, :, 2 * D_B:] = act[:, 2 * D_B:].astype(BF16)

    z_ref[0] = _dot(xe, wbf_ref[:, 2 * D_A + N_QKV:N_MAIN])[HALO:HALO + TM_IN].astype(BF16)

    r2 = lax.broadcasted_iota(jnp.int32, (TM_IN, TM_IN), 0)
    c2 = lax.broadcasted_iota(jnp.int32, (TM_IN, TM_IN), 1)
    same = (r2 // CHUNK) == (c2 // CHUNK)
    tri_l = jnp.where(jnp.logical_and(same, c2 <= r2), 1.0, 0.0).astype(F32)
    tri_u = jnp.where(jnp.logical_and(same, c2 >= r2), 1.0, 0.0).astype(F32)

    ab = _dot(xb, wab_ref[...])
    g_tok = -jnp.exp(alog_ref[...]) * _softplus(ab + dtb_ref[...])
    lane = lax.broadcasted_iota(jnp.int32, ab.shape, 1)
    gb = jnp.where(lane < HEADS, _dot(tri_l, g_tok, precision=HI),
                   jnp.where(lane < 2 * HEADS, _dot(tri_u, g_tok, precision=HI), _sigmoid(ab)))
    gb_ref[0] = gb[:, 0:N_AB]

    abt = _dot_nt(wabt_ref[...], xb)
    g_t = -jnp.exp(alogt_ref[...]) * _softplus(abt + dtbt_ref[...])
    row = lax.broadcasted_iota(jnp.int32, abt.shape, 0)
    gbt_ref[0] = jnp.where(row < HEADS, _dot(g_t, tri_u, precision=HI),
                           jnp.where(row < 2 * HEADS, _dot(g_t, tri_l, precision=HI), _sigmoid(abt)))


def _in_proj(x, ctx, mod_lat, mod_ctx, g1, w_main, w_ab, w_abt, lng, lnb, ws, bst, conv, alog, dtb, alogt, dtbt):
    B, L, _ = x.shape
    n_lat = L // TM_IN
    n_steps = n_lat + 1
    LC = L + TM_IN
    hb = TM_IN // HALO

    def full(shape):
        return pl.BlockSpec(shape, lambda b, j: (0,) * len(shape))

    in_specs = [
        pl.BlockSpec((1, TM_IN, D), lambda b, j: (b, jnp.maximum(j - 1, 0), 0)),
        pl.BlockSpec((1, HALO, D), lambda b, j: (b, jnp.clip((j - 1) * hb - 1, 0, L // HALO - 1), 0)),
        pl.BlockSpec((1, HALO, D), lambda b, j: (b, jnp.clip(j * hb, 0, L // HALO - 1), 0)),
        pl.BlockSpec((1, TM_IN, D), lambda b, j: (b, 0, 0)),
        pl.BlockSpec((1, 1, 6 * D), lambda b, j: (b, 0, 0)),
        full((1, 6 * D)), full((1, D)),
        pl.BlockSpec((D, N_MAIN), lambda b, j: (0, 0), pipeline_mode=pl.Buffered(1)),
        full((D, 128)), full((N_AB, D)),
        full((1, D_A)), full((1, D_A)), full((HEADS, CHUNK, CHUNK)), full((CHUNK, HEADS)),
        full((CONV_W, N_QKV)), full((1, 128)), full((1, 128)), full((N_AB, 1)), full((N_AB, 1)),
    ]
    out_shape = [
        jax.ShapeDtypeStruct((B, L, D_A), BF16),
        jax.ShapeDtypeStruct((B, LC, N_QKV), BF16),
        jax.ShapeDtypeStruct((B, LC, D_B), BF16),
        jax.ShapeDtypeStruct((B, LC, N_AB), F32),
        jax.ShapeDtypeStruct((B, N_AB, LC), F32),
    ]
    out_specs = [
        pl.BlockSpec((1, TM_IN, D_A), lambda b, j: (b, jnp.maximum(j - 1, 0), 0)),
        pl.BlockSpec((1, TM_IN, N_QKV), lambda b, j: (b, j, 0)),
        pl.BlockSpec((1, TM_IN, D_B), lambda b, j: (b, j, 0)),
        pl.BlockSpec((1, TM_IN, N_AB), lambda b, j: (b, j, 0)),
        pl.BlockSpec((1, N_AB, TM_IN), lambda b, j: (b, 0, j)),
    ]
    return pl.pallas_call(
        _in_kernel,
        out_shape=out_shape,
        grid=(B, n_steps),
        in_specs=in_specs,
        out_specs=out_specs,
        scratch_shapes=[pltpu.VMEM((TM_IN + 2 * HALO, N_QKV), F32), pltpu.VMEM((D, N_MAIN), BF16)],
        compiler_params=pltpu.CompilerParams(dimension_semantics=("arbitrary", "arbitrary"),
                                             vmem_limit_bytes=VMEM_LIMIT),
    )(x, x, x, ctx, mod_lat, mod_ctx, g1, w_main, w_ab, w_abt, lng, lnb, ws, bst, conv, alog, dtb, alogt, dtbt)


def _delta_kernel(qf_ref, qb_ref, zf_ref, zb_ref, gf_ref, gbk_ref, gtf_ref, gtb_ref, on_ref,
                  y_ref, s_ref, oacc_ref, *, n_ctx, n_lat):
    s = pl.program_id(1)

    @pl.when(s == 0)
    def _():
        s_ref[...] = jnp.zeros_like(s_ref)
        oacc_ref[...] = jnp.zeros_like(oacc_ref)

    row = lax.broadcasted_iota(jnp.int32, (CHUNK, CHUNK), 0)
    col = lax.broadcasted_iota(jnp.int32, (CHUNK, CHUNK), 1)
    low = row > col
    upp = row < col
    same_blk = (row // 16) == (col // 16)
    eye = jnp.where(row == col, 1.0, 0.0).astype(F32)
    onorm = on_ref[...]
    half = n_ctx + n_lat // 2
    second = s >= half
    gcols = (gf_ref[0], gbk_ref[0])
    gts = (gtf_ref[0], gtb_ref[0])
    qkv_refs = (qf_ref, qb_ref)
    z_refs = (zf_ref, zb_ref)
    hs = range(HEADS)

    def halves(xc, unit):
        fill = eye if unit else 0.0
        return jnp.where(low, xc, fill).astype(BF16), jnp.where(upp, xc, fill).astype(BF16)

    def as_lhs(hv):
        return jnp.concatenate(hv, axis=1)

    def as_rhs(*hvs):
        cols_ = [jnp.concatenate(hv, axis=0) for hv in hvs]
        return cols_[0] if len(cols_) == 1 else jnp.concatenate(cols_, axis=1)

    def load(d, h, part):
        return qkv_refs[d][0, :, part * D_B + h * HD:part * D_B + (h + 1) * HD]

    q = [[load(d, h, 0) for h in hs] for d in range(2)]
    k = [[load(d, h, 1) for h in hs] for d in range(2)]
    v = [[load(d, h, 2) for h in hs] for d in range(2)]
    gc = [[gcols[d][:, d * HEADS + h:d * HEADS + h + 1] for h in hs] for d in range(2)]
    beta = [[gcols[d][:, 2 * HEADS + d * HEADS + h:2 * HEADS + d * HEADS + h + 1] for h in hs] for d in range(2)]
    gr = [[gts[d][d * HEADS + h:d * HEADS + h + 1, :] for h in hs] for d in range(2)]
    glast = [[gr[0][h][:, CHUNK - 1:CHUNK] for h in hs], [gr[1][h][:, 0:1] for h in hs]]

    gram = [[_dot_nt(jnp.concatenate([q[d][h], k[d][h]], axis=0), k[d][h]) for h in hs] for d in range(2)]
    dec = [jnp.exp(jnp.where(low, gc[0][h] - gr[0][h], jnp.where(upp, gc[1][h] - gr[1][h], 0.0))) for h in hs]
    lc = [jnp.where(low, gram[0][h][CHUNK:] * beta[0][h], jnp.where(upp, gram[1][h][CHUNK:] * beta[1][h], 0.0))
          * dec[h] for h in hs]
    qk = [[jnp.where(upp, 0.0, gram[0][h][:CHUNK] * dec[h]).astype(BF16) for h in hs],
          [jnp.where(low, 0.0, gram[1][h][:CHUNK] * dec[h]).astype(BF16) for h in hs]]

    dg = [jnp.where(same_blk, lc[h], 0.0) for h in hs]
    ob = [lc[h] - dg[h] for h in hs]
    d1h = [halves(dg[h], False) for h in hs]
    d2 = [_dot(as_lhs(d1h[h]), as_rhs(d1h[h])) for h in hs]
    p0s = [-dg[h] for h in hs]
    d2h = [halves(d2[h], False) for h in hs]
    p0h = [halves(p0s[h], True) for h in hs]
    o2 = [_dot(as_lhs(d2h[h]), as_rhs(d2h[h], p0h[h])) for h in hs]
    p1s = [p0s[h] + o2[h][:, CHUNK:] for h in hs]
    d4h = [halves(o2[h][:, :CHUNK], False) for h in hs]
    p1h = [halves(p1s[h], True) for h in hs]
    o3 = [_dot(as_lhs(d4h[h]), as_rhs(d4h[h], p1h[h])) for h in hs]
    p2s = [p1s[h] + o3[h][:, CHUNK:] for h in hs]
    d8h = [halves(o3[h][:, :CHUNK], False) for h in hs]
    p2h = [halves(p2s[h], True) for h in hs]
    p3s = [p2s[h] + _dot(as_lhs(d8h[h]), as_rhs(p2h[h])) for h in hs]
    p3h = [halves(p3s[h], True) for h in hs]
    obh = [halves(ob[h], False) for h in hs]
    n1h = [halves(_dot(as_lhs(p3h[h]), as_rhs(obh[h])), False) for h in hs]
    o6 = [_dot(as_lhs(n1h[h]), as_rhs(n1h[h], p3h[h])) for h in hs]
    r0s = [p3s[h] - o6[h][:, CHUNK:] for h in hs]
    n2h = [halves(o6[h][:, :CHUNK], False) for h in hs]
    r0h = [halves(r0s[h], True) for h in hs]
    o7 = [_dot(as_lhs(n2h[h]), as_rhs(n2h[h], r0h[h])) for h in hs]
    r1s = [r0s[h] + o7[h][:, CHUNK:] for h in hs]
    n4h = [halves(o7[h][:, :CHUNK], False) for h in hs]
    r1h = [halves(r1s[h], True) for h in hs]
    tinv = [halves(r1s[h] + _dot(as_lhs(n4h[h]), as_rhs(r1h[h])), True) for h in hs]

    for d in range(2):
        lat_chunk = (s - n_ctx) if d == 0 else (n_ctx + n_lat - 1 - s)
        off = pl.multiple_of(jnp.clip(lat_chunk, 0, n_lat - 1) * CHUNK, CHUNK)
        egc = [jnp.exp(gc[d][h]) for h in hs]
        kf = [k[d][h].astype(F32) for h in hs]
        rhs = [jnp.concatenate([v[d][h].astype(F32) * beta[d][h], kf[h] * (beta[d][h] * egc[h])], axis=1).astype(BF16)
               for h in hs]
        uw = [_dot(tinv[h][d], rhs[h]) for h in hs]
        qd = [q[d][h].astype(F32) * egc[h] for h in hs]
        kd = [(kf[h] * jnp.exp(glast[d][h] - gc[d][h])).astype(BF16) for h in hs]
        st = [s_ref[d * HEADS + h] for h in hs]
        a1 = [_dot(jnp.concatenate([uw[h][:, HD:], qd[h]], axis=0).astype(BF16), st[h].astype(BF16)) for h in hs]
        vnew = [(uw[h][:, :HD] - a1[h][:CHUNK]).astype(BF16) for h in hs]
        o = [a1[h][CHUNK:] + _dot(qk[d][h], vnew[h]) for h in hs]
        for h in hs:
            s_ref[d * HEADS + h] = st[h] * jnp.exp(glast[d][h]) + _dot_tn(kd[h], vnew[h])
        for h in hs:
            cols = slice(h * HD, (h + 1) * HD)
            tot = jnp.where(second, oacc_ref[pl.ds(off, CHUNK), cols], 0.0) + o[h]
            oacc_ref[pl.ds(off, CHUNK), cols] = tot
            ms = jnp.mean(tot * tot, axis=-1, keepdims=True)
            zz = z_refs[d][0, :, cols].astype(F32)
            y_ref[0, pl.ds(off, CHUNK), cols] = (tot * lax.rsqrt(ms + EPS) * onorm * _silu(zz)).astype(BF16)


def _delta(qkv, z, gb, gbt, onorm, L):
    B, LC, _ = qkv.shape
    n_all = LC // CHUNK
    n_lat = L // CHUNK
    n_ctx = n_all - n_lat

    def cf(s):
        return s

    def cb(s):
        return jnp.where(s < n_ctx, n_ctx - 1 - s, n_all + n_ctx - 1 - s)

    in_specs = [
        pl.BlockSpec((1, CHUNK, N_QKV), lambda b, s: (b, cf(s), 0)),
        pl.BlockSpec((1, CHUNK, N_QKV), lambda b, s: (b, cb(s), 0)),
        pl.BlockSpec((1, CHUNK, D_B), lambda b, s: (b, cf(s), 0)),
        pl.BlockSpec((1, CHUNK, D_B), lambda b, s: (b, cb(s), 0)),
        pl.BlockSpec((1, CHUNK, N_AB), lambda b, s: (b, cf(s), 0)),
        pl.BlockSpec((1, CHUNK, N_AB), lambda b, s: (b, cb(s), 0)),
        pl.BlockSpec((1, N_AB, CHUNK), lambda b, s: (b, 0, cf(s))),
        pl.BlockSpec((1, N_AB, CHUNK), lambda b, s: (b, 0, cb(s))),
        pl.BlockSpec((1, HD), lambda b, s: (0, 0)),
    ]
    return pl.pallas_call(
        functools.partial(_delta_kernel, n_ctx=n_ctx, n_lat=n_lat),
        out_shape=jax.ShapeDtypeStruct((B, L, D_B), BF16),
        grid=(B, n_all),
        in_specs=in_specs,
        out_specs=pl.BlockSpec((1, L, D_B), lambda b, s: (b, 0, 0)),
        scratch_shapes=[pltpu.VMEM((2 * HEADS, HD, HD), F32), pltpu.VMEM((L, D_B), F32)],
        compiler_params=pltpu.CompilerParams(dimension_semantics=("arbitrary", "arbitrary"),
                                             vmem_limit_bytes=VMEM_LIMIT),
    )(qkv, qkv, z, z, gb, gb, gbt, gbt, onorm)


def _out_kernel(x_ref, ya_ref, yb_ref, mod_ref, wo_ref, g2_ref, wrt_ref, brt_ref,
                x2_ref, h_ref, route_ref, cnt_ref, base_ref, wbf_ref):
    i = pl.program_id(0)

    @pl.when(i == 0)
    def _():
        base_ref[...] = jnp.zeros_like(base_ref)
        wbf_ref[...] = wo_ref[...].astype(BF16)

    mod = mod_ref[0]
    gt1 = mod[:, 2 * D:3 * D]
    sh2 = mod[:, 3 * D:4 * D]
    sc2 = mod[:, 4 * D:5 * D]
    mix = _dot(ya_ref[...], wbf_ref[0:D_A, :]) + _dot(yb_ref[...], wbf_ref[D_A:, :])
    x2 = x_ref[...] + gt1 * mix
    x2_ref[...] = x2
    ms = jnp.mean(x2 * x2, axis=-1, keepdims=True)
    hv = x2 * lax.rsqrt(ms + EPS) * (g2_ref[...] * (1.0 + sc2)) + sh2
    h_ref[:, 0:D] = hv

    lt = _dot_nt(wrt_ref[...], hv.astype(BF16)) + brt_ref[...]
    gl = [lt[r:r + 1, :] for r in range(N_GROUPS)]
    gmax = jnp.maximum(jnp.maximum(gl[0], gl[1]), jnp.maximum(gl[2], gl[3]))
    gsel = jnp.where(gl[0] == gmax, 0, jnp.where(gl[1] == gmax, 1, jnp.where(gl[2] == gmax, 2, 3)))
    p_g = 1.0 / (jnp.exp(gl[0] - gmax) + jnp.exp(gl[1] - gmax) + jnp.exp(gl[2] - gmax) + jnp.exp(gl[3] - gmax))
    el = []
    for e in range(EPG):
        r = [lt[N_GROUPS + g * EPG + e:N_GROUPS + g * EPG + e + 1, :] for g in range(N_GROUPS)]
        el.append(jnp.where(gsel == 0, r[0], jnp.where(gsel == 1, r[1], jnp.where(gsel == 2, r[2], r[3]))))
    m1 = jnp.maximum(jnp.maximum(el[0], el[1]), jnp.maximum(el[2], el[3]))
    i1 = jnp.where(el[0] == m1, 0, jnp.where(el[1] == m1, 1, jnp.where(el[2] == m1, 2, 3)))
    neg = jnp.float32(-jnp.inf)
    el2 = [jnp.where(i1 == e, neg, el[e]) for e in range(EPG)]
    m2 = jnp.maximum(jnp.maximum(el2[0], el2[1]), jnp.maximum(el2[2], el2[3]))
    i2 = jnp.where(jnp.logical_and(el2[0] == m2, i1 != 0), 0,
                   jnp.where(jnp.logical_and(el2[1] == m2, i1 != 1), 1,
                             jnp.where(jnp.logical_and(el2[2] == m2, i1 != 2), 2, 3)))
    t = jnp.exp(m2 - m1)
    w1 = p_g / (1.0 + t)
    w2 = p_g * t / (1.0 + t)
    first_low = i1 < i2
    ea = jnp.where(first_low, i1, i2)
    eb = jnp.where(first_low, i2, i1)
    w_a = jnp.where(first_low, w1, w2)
    w_b = jnp.where(first_low, w2, w1)
    pair = jnp.where(ea == 0, eb - 1, jnp.where(ea == 1, eb + 1, 5))
    bucket = gsel * N_PAIRS + pair

    tm = bucket.shape[1]
    rows = lax.broadcasted_iota(jnp.int32, (32, tm), 0)
    onehot = jnp.where(rows == bucket, 1.0, 0.0).astype(F32)
    r2 = lax.broadcasted_iota(jnp.int32, (tm, tm), 0)
    c2 = lax.broadcasted_iota(jnp.int32, (tm, tm), 1)
    tri = jnp.where(r2 <= c2, 1.0, 0.0).astype(BF16)
    prefix = _dot(onehot.astype(BF16), tri)
    base = base_ref[:, 0:1]
    rank = jnp.sum(onehot * (prefix - 1.0 + base), axis=0, keepdims=True)
    newbase = base + prefix[:, tm - 1:tm]
    base_ref[...] = jnp.broadcast_to(newbase, base_ref.shape)
    cnt_ref[...] = jnp.broadcast_to(newbase, cnt_ref.shape)
    route = jnp.concatenate([bucket.astype(F32), rank, w_a, w_b, jnp.zeros((4, tm), F32)], axis=0)
    route_ref[0] = route
    h_ref[:, D:D_ROW] = jnp.transpose(jnp.concatenate([route, jnp.zeros((120, tm), F32)], axis=0))


def _out_proj(x2d, ya, yb, mod3, w_o, g2, wrt, brt, L):
    T = x2d.shape[0]
    nb = T // TM_OUT
    per_b = L // TM_OUT
    out_shape = [
        jax.ShapeDtypeStruct((T, D), F32),
        jax.ShapeDtypeStruct((T, D_ROW), F32),
        jax.ShapeDtypeStruct((nb, 8, TM_OUT), F32),
        jax.ShapeDtypeStruct((32, 128), F32),
    ]
    return pl.pallas_call(
        _out_kernel,
        out_shape=out_shape,
        grid=(nb,),
        in_specs=[
            pl.BlockSpec((TM_OUT, D), lambda i: (i, 0)),
            pl.BlockSpec((TM_OUT, D_A), lambda i: (i, 0)),
            pl.BlockSpec((TM_OUT, D_B), lambda i: (i, 0)),
            pl.BlockSpec((1, 1, 6 * D), lambda i: (i // per_b, 0, 0)),
            pl.BlockSpec((D, D), lambda i: (0, 0), pipeline_mode=pl.Buffered(1)),
            pl.BlockSpec((1, D), lambda i: (0, 0)),
            pl.BlockSpec((32, D), lambda i: (0, 0)),
            pl.BlockSpec((32, 1), lambda i: (0, 0)),
        ],
        out_specs=[
            pl.BlockSpec((TM_OUT, D), lambda i: (i, 0)),
            pl.BlockSpec((TM_OUT, D_ROW), lambda i: (i, 0)),
            pl.BlockSpec((1, 8, TM_OUT), lambda i: (i, 0, 0)),
            pl.BlockSpec((32, 128), lambda i: (0, 0)),
        ],
        scratch_shapes=[pltpu.VMEM((32, 128), F32), pltpu.VMEM((D, D), BF16)],
        compiler_params=pltpu.CompilerParams(dimension_semantics=("arbitrary",),
                                             vmem_limit_bytes=VMEM_LIMIT),
    )(x2d, ya, yb, mod3, w_o, g2, wrt, brt)


def _sc_row_gather(table, idx):
    n_rows = idx.shape[0]
    width = table.shape[1]
    n_workers = SC_CORES * SC_SUBCORES
    per_worker = n_rows // n_workers
    assert per_worker * n_workers == n_rows and per_worker % SC_WINDOW == 0
    mesh = plsc.VectorSubcoreMesh(core_axis_name="c", subcore_axis_name="s")

    @functools.partial(
        pl.kernel, mesh=mesh,
        out_type=jax.ShapeDtypeStruct((n_rows, width), table.dtype),
        scratch_types=[pltpu.VMEM((SC_WINDOW,), jnp.int32),
                       pltpu.VMEM((SC_WINDOW, width), table.dtype),
                       pltpu.SemaphoreType.DMA],
    )
    def gather(table_hbm, idx_hbm, out_hbm, idx_v, rows_v, sem):
        wid = lax.axis_index("s") * SC_CORES + lax.axis_index("c")
        base = wid * per_worker

        @pl.loop(0, per_worker // SC_WINDOW)
        def _(j):
            off = pl.multiple_of(base + j * SC_WINDOW, 8)
            pltpu.sync_copy(idx_hbm.at[pl.ds(off, SC_WINDOW)], idx_v)
            pltpu.async_copy(table_hbm.at[idx_v], rows_v, sem).wait()
            pltpu.sync_copy(rows_v, out_hbm.at[pl.ds(off, SC_WINDOW)])

    return gather(table, idx)


def _sc_row_scatter(rows, pos, n_out):
    n_rows, width = rows.shape
    n_workers = SC_CORES * SC_SUBCORES
    per_worker = n_rows // n_workers
    n_chunks = per_worker // SC_WINDOW
    assert per_worker * n_workers == n_rows and n_chunks * SC_WINDOW == per_worker
    mesh = plsc.VectorSubcoreMesh(core_axis_name="c", subcore_axis_name="s")

    @functools.partial(
        pl.kernel, mesh=mesh,
        out_type=jax.ShapeDtypeStruct((n_out, width), rows.dtype),
        scratch_types=[pltpu.VMEM((n_chunks, SC_WINDOW), jnp.int32),
                       pltpu.VMEM((SC_WINDOW, width), rows.dtype),
                       pltpu.SemaphoreType.DMA],
    )
    def scatter(rows_hbm, pos_hbm, out_hbm, idx_v, rows_v, sem):
        wid = lax.axis_index("s") * SC_CORES + lax.axis_index("c")
        base = wid * per_worker
        pltpu.sync_copy(pos_hbm.at[wid], idx_v)

        @pl.loop(0, n_chunks)
        def _(j):
            off = pl.multiple_of(base + j * SC_WINDOW, 8)
            pltpu.sync_copy(rows_hbm.at[pl.ds(off, SC_WINDOW)], rows_v)
            pltpu.async_copy(rows_v, out_hbm.at[idx_v.at[j]], sem).wait()

    return scatter(rows, pos.reshape(n_workers, n_chunks, SC_WINDOW))


def _moe_kernel(ea_ref, eb_ref, nv_ref, chg_ref, hs_ref, wga_ref, wgb_ref, wda_ref, wdb_ref, o_ref, wg_s, wd_s):
    i = pl.program_id(0)

    @pl.when(chg_ref[i] == 1)
    def _():
        wg_s[0] = wga_ref[0].astype(BF16)
        wg_s[1] = wgb_ref[0].astype(BF16)
        wd_s[0] = wda_ref[0].astype(BF16)
        wd_s[1] = wdb_ref[0].astype(BF16)

    @pl.when(nv_ref[i] > 0)
    def _():
        live = lax.broadcasted_iota(jnp.int32, (BM_MOE, 1), 0) < nv_ref[i]
        xb = jnp.where(live, hs_ref[:, 0:D], 0.0).astype(BF16)
        w_a = jnp.where(live, hs_ref[:, LANE_WA:LANE_WA + 1], 0.0)
        w_b = jnp.where(live, hs_ref[:, LANE_WA + 1:LANE_WA + 2], 0.0)
        ga = _dot(xb, wg_s[0])
        act_a = (_silu(ga[:, :D_EXPERT]) * ga[:, D_EXPERT:] * w_a).astype(BF16)
        gb = _dot(xb, wg_s[1])
        act_b = (_silu(gb[:, :D_EXPERT]) * gb[:, D_EXPERT:] * w_b).astype(BF16)
        o_ref[...] = _dot(act_a, wd_s[0]) + _dot(act_b, wd_s[1])

    @pl.when(nv_ref[i] <= 0)
    def _():
        o_ref[...] = jnp.zeros_like(o_ref)


def _moe(ea, eb, nvalid, chg, hs, w_gu, w_dn):
    nblk = ea.shape[0]
    S = nblk * BM_MOE
    grid_spec = pltpu.PrefetchScalarGridSpec(
        num_scalar_prefetch=4,
        grid=(nblk,),
        in_specs=[
            pl.BlockSpec((BM_MOE, D_ROW), lambda i, ea, eb, nv, chg: (i, 0)),
            pl.BlockSpec((1, D, 2 * D_EXPERT), lambda i, ea, eb, nv, chg: (ea[i], 0, 0)),
            pl.BlockSpec((1, D, 2 * D_EXPERT), lambda i, ea, eb, nv, chg: (eb[i], 0, 0)),
            pl.BlockSpec((1, D_EXPERT, D), lambda i, ea, eb, nv, chg: (ea[i], 0, 0)),
            pl.BlockSpec((1, D_EXPERT, D), lambda i, ea, eb, nv, chg: (eb[i], 0, 0)),
        ],
        out_specs=pl.BlockSpec((BM_MOE, D), lambda i, ea, eb, nv, chg: (i, 0)),
        scratch_shapes=[pltpu.VMEM((2, D, 2 * D_EXPERT), BF16), pltpu.VMEM((2, D_EXPERT, D), BF16)],
    )
    return pl.pallas_call(
        _moe_kernel,
        out_shape=jax.ShapeDtypeStruct((S, D), F32),
        grid_spec=grid_spec,
        compiler_params=pltpu.CompilerParams(dimension_semantics=("arbitrary",),
                                             vmem_limit_bytes=VMEM_LIMIT),
    )(ea, eb, nvalid, chg, hs, w_gu, w_gu, w_dn, w_dn)


def _final_kernel(x2_ref, m_ref, mod_ref, fg_ref, o_ref):
    gt2 = mod_ref[0][:, 5 * D:6 * D]
    y = x2_ref[...] + gt2 * m_ref[...]
    ms = jnp.mean(y * y, axis=-1, keepdims=True)
    o_ref[...] = y * lax.rsqrt(ms + EPS) * fg_ref[...]


def _final(x2, m, mod3, fg, L):
    T = x2.shape[0]
    per_b = L // TM_FIN
    return pl.pallas_call(
        _final_kernel,
        out_shape=jax.ShapeDtypeStruct((T, D), F32),
        grid=(T // TM_FIN,),
        in_specs=[
            pl.BlockSpec((TM_FIN, D), lambda i: (i, 0)),
            pl.BlockSpec((TM_FIN, D), lambda i: (i, 0)),
            pl.BlockSpec((1, 1, 6 * D), lambda i: (i // per_b, 0, 0)),
            pl.BlockSpec((1, D), lambda i: (0, 0)),
        ],
        out_specs=pl.BlockSpec((TM_FIN, D), lambda i: (i, 0)),
        compiler_params=pltpu.CompilerParams(dimension_semantics=("arbitrary",),
                                             vmem_limit_bytes=VMEM_LIMIT),
    )(x2, m, mod3, fg)


def kernel(x, c, ctx, c_ctx, w_ada, b_ada, norm1_g, w_in, ln_a_g, ln_a_b, w_spatial, b_spatial, conv_qkv, a_log,
           dt_bias, onorm_g, w_out, norm2_g, w_group, b_group, w_router, b_router, w_gate_up, w_down, final_g):
    B, L, _ = x.shape
    T = B * L
    assert w_ada.shape[0] == 1 and ctx.shape[1] == TM_IN and L % TM_OUT == 0

    cond = jnp.concatenate([c, c_ctx[None, :], jnp.zeros((7, D), F32)], axis=0)
    mod = _modulation(cond, w_ada[0], b_ada[0][None, :])
    mod_lat = mod[:B].reshape(B, 1, 6 * D)
    mod_ctx = mod[B:B + 1]

    w_ab = jnp.pad(w_in[0][:, N_MAIN:], ((0, 0), (0, 128 - N_AB))).astype(BF16)
    w_abt = w_in[0][:, N_MAIN:].T.astype(BF16)
    alog = a_log[0].reshape(1, 2 * HEADS)
    dtb = dt_bias[0].reshape(1, 2 * HEADS)
    alog_row = jnp.pad(alog, ((0, 0), (0, 128 - 2 * HEADS)))
    dtb_row = jnp.pad(dtb, ((0, 0), (0, 128 - 2 * HEADS)))
    alog_col = jnp.pad(alog, ((0, 0), (0, N_AB - 2 * HEADS))).T
    dtb_col = jnp.pad(dtb, ((0, 0), (0, N_AB - 2 * HEADS))).T

    ya, qkv, z, gb, gbt = _in_proj(
        x, ctx, mod_lat, mod_ctx, norm1_g, w_in[0], w_ab, w_abt, ln_a_g, ln_a_b,
        w_spatial[0].astype(BF16), b_spatial[0].T, conv_qkv[0], alog_row, dtb_row, alog_col, dtb_col)

    yb = _delta(qkv, z, gb, gbt, onorm_g, L)

    wrt = jnp.concatenate([w_group[0].T, w_router[0].T, jnp.zeros((32 - N_GROUPS - N_EXPERTS, D), F32)], axis=0)
    brt = jnp.concatenate([b_group[0], b_router[0], jnp.zeros((32 - N_GROUPS - N_EXPERTS,), F32)])[:, None]
    x2, h, route, cnt = _out_proj(x.reshape(T, D), ya.reshape(T, D_A), yb.reshape(T, D_B), mod_lat,
                                  w_out[0], norm2_g, wrt.astype(BF16), brt, L)

    bucket = route[:, 0, :].reshape(T).astype(jnp.int32)
    rank = route[:, 1, :].reshape(T).astype(jnp.int32)
    counts = cnt[:N_BUCKETS, 0].astype(jnp.int32)
    nblk_b = (counts + BM_MOE - 1) // BM_MOE
    blk_end = jnp.cumsum(nblk_b)
    blk_start = blk_end - nblk_b
    pos = blk_start[bucket] * BM_MOE + rank
    n_blocks = T // BM_MOE + N_BUCKETS
    S = n_blocks * BM_MOE
    blk = jnp.arange(n_blocks, dtype=jnp.int32)
    used = blk < blk_end[-1]
    bkt = jnp.sum((jnp.minimum(blk, blk_end[-1] - 1)[:, None] >= blk_end[None, :]).astype(jnp.int32), axis=1)
    nvalid = jnp.where(used, jnp.clip(counts[bkt] - (blk - blk_start[bkt]) * BM_MOE, 0, BM_MOE), 0).astype(jnp.int32)
    pa = jnp.asarray(PAIR_A, jnp.int32)
    pb = jnp.asarray(PAIR_B, jnp.int32)
    ea = (bkt // N_PAIRS) * EPG + pa[bkt % N_PAIRS]
    eb = (bkt // N_PAIRS) * EPG + pb[bkt % N_PAIRS]
    chg = jnp.concatenate([jnp.ones((1,), jnp.int32), (bkt[1:] != bkt[:-1]).astype(jnp.int32)])

    hs = _sc_row_scatter(h, pos, S)
    ms = _moe(ea, eb, nvalid, chg, hs, w_gate_up[0], w_down[0])
    out = _final(x2, _sc_row_gather(ms, pos), mod_lat, final_g[None, :], L)
    return out.reshape(B, L, D)
```

```python
import functools

import jax
import jax.numpy as jnp
import numpy as np
from jax import lax
from jax.experimental import pallas as pl
from jax.experimental.pallas import tpu as pltpu
from jax.experimental.pallas import tpu_sc as plsc

F32 = jnp.float32
BF16 = jnp.bfloat16
EPS = 1e-6

D = 1024
D_A = 512
D_B = 512
HEADS = 4
HD = 128
CHUNK = 128
CONV_W = 5
N_QKV = 3 * D_B
N_MAIN = 2 * D_A + 4 * D_B
N_AB = 16
N_GROUPS = 4
EPG = 4
N_EXPERTS = 16
D_EXPERT = 512
N_PAIRS = 6
N_BUCKETS = N_GROUPS * N_PAIRS
D_ROW = D + 128
LANE_WA = D + 2
PAIR_A = (0, 0, 0, 1, 1, 2)
PAIR_B = (1, 2, 3, 2, 3, 3)

TM_IN = 256
HALO = 8
NB_DELTA = 2
TM_OUT = 512
BM_MOE = 256
TM_FIN = 512
VMEM_LIMIT = 56 * 1024 * 1024
SC_CORES = 2
SC_SUBCORES = 16
SC_WINDOW = 32

HI = lax.Precision.HIGHEST

_CHUNK_TRIL = np.kron(np.eye(TM_IN // CHUNK, dtype=np.float32), np.tril(np.ones((CHUNK, CHUNK), np.float32)))


def _dot(a, b, precision=None):
    return jnp.dot(a, b, preferred_element_type=F32, precision=precision)


def _dot_nt(a, b):
    return lax.dot_general(a, b, (((1,), (1,)), ((), ())), preferred_element_type=F32)


def _dot_tn(a, b):
    return lax.dot_general(a, b, (((0,), (0,)), ((), ())), preferred_element_type=F32)


def _sigmoid(x):
    return 1.0 / (1.0 + jnp.exp(-x))


def _silu(x):
    return x * _sigmoid(x)


def _softplus(x):
    return jnp.maximum(x, 0.0) + jnp.log(1.0 + jnp.exp(-jnp.abs(x)))


def _gelu_tanh(x):
    return 0.5 * x * (1.0 + jnp.tanh(np.sqrt(2.0 / np.pi).astype(np.float32) * (x + 0.044715 * (x * x * x))))


def _mod_kernel(c_ref, w_ref, b_ref, o_ref):
    c = c_ref[...]
    o_ref[...] = _dot(_silu(c), w_ref[...], precision=HI) + b_ref[...]


def _modulation(cond, w_ada, b_ada):
    rows = cond.shape[0]
    tn = 1536
    return pl.pallas_call(
        _mod_kernel,
        out_shape=jax.ShapeDtypeStruct((rows, 6 * D), F32),
        grid=(6 * D // tn,),
        in_specs=[pl.BlockSpec((rows, D), lambda i: (0, 0)),
                  pl.BlockSpec((D, tn), lambda i: (0, i)),
                  pl.BlockSpec((1, tn), lambda i: (0, i))],
        out_specs=pl.BlockSpec((rows, tn), lambda i: (0, i)),
        compiler_params=pltpu.CompilerParams(dimension_semantics=("arbitrary",),
                                             vmem_limit_bytes=VMEM_LIMIT),
    )(cond, w_ada, b_ada)


def _in_kernel(x_ref, xp_ref, xn_ref, ctx_ref, mod_ref, cmod_ref, g1_ref, w_ref, wab_ref, wabt_ref,
               lng_ref, lnb_ref, ws_ref, bst_ref, conv_ref, alog_ref, dtb_ref, alogt_ref, dtbt_ref, tril_ref, triu_ref,
               ya_ref, qkv_ref, z_ref, gb_ref, gbt_ref, ext_ref, wbf_ref):
    j = pl.program_id(1)

    @pl.when(jnp.logical_and(pl.program_id(0) == 0, j == 0))
    def _():
        wbf_ref[...] = w_ref[...].astype(BF16)

    is_ctx = j == 0
    n_lat_blocks = pl.num_programs(1) - 1
    mod = mod_ref[0]
    cm = cmod_ref[...]
    sh = jnp.where(is_ctx, cm[:, 0:D], mod[:, 0:D])
    sc = jnp.where(is_ctx, cm[:, D:2 * D], mod[:, D:2 * D])
    scale = g1_ref[...] * (1.0 + sc)

    xmain = jnp.where(is_ctx, ctx_ref[0], x_ref[0])
    xv = jnp.concatenate([xp_ref[0], xmain, xn_ref[0]], axis=0)
    xnorm = xv * lax.rsqrt(jnp.mean(xv * xv, axis=-1, keepdims=True) + EPS) * scale + sh
    xe = xnorm.astype(BF16)
    xb = xnorm[HALO:HALO + TM_IN].astype(BF16)

    pa = _dot(xe, wbf_ref[:, 0:2 * D_A])[HALO:HALO + TM_IN]
    ga = _gelu_tanh(pa)
    u = ga[:, :D_A]
    v = ga[:, D_A:]
    mu = jnp.mean(v, axis=-1, keepdims=True)
    vc = v - mu
    var = jnp.mean(vc * vc, axis=-1, keepdims=True)
    vn = (vc * lax.rsqrt(var + EPS) * lng_ref[...] + lnb_ref[...]).astype(BF16)
    bst = bst_ref[...]
    for n in range(TM_IN // CHUNK):
        rows = slice(n * CHUNK, (n + 1) * CHUNK)
        for h in range(HEADS):
            cols = slice(h * HD, (h + 1) * HD)
            s = _dot(ws_ref[h], vn[rows, cols]) + bst[:, h:h + 1]
            ya_ref[0, rows, cols] = (u[rows, cols] * s).astype(BF16)

    pq = _dot(xe, wbf_ref[:, 2 * D_A:2 * D_A + N_QKV])
    rid = lax.broadcasted_iota(jnp.int32, (TM_IN + 2 * HALO, 1), 0)
    prev_ok = j >= 2
    next_ok = jnp.logical_and(j >= 1, j < n_lat_blocks)
    valid = jnp.logical_or(jnp.logical_and(rid >= HALO, rid < HALO + TM_IN),
                           jnp.logical_or(jnp.logical_and(rid < HALO, prev_ok),
                                          jnp.logical_and(rid >= HALO + TM_IN, next_ok)))
    ext_ref[...] = jnp.where(valid, pq, 0.0)
    pad = (CONV_W - 1) // 2
    acc = None
    for t in range(CONV_W):
        term = conv_ref[t:t + 1, :] * ext_ref[pl.ds(HALO - pad + t, TM_IN), :]
        acc = term if acc is None else acc + term
    act = _silu(acc)
    for h in range(2 * HEADS):
        cols = slice(h * HD, (h + 1) * HD)
        t = act[:, cols]
        nrm = lax.rsqrt(jnp.sum(t * t, axis=-1, keepdims=True) + EPS)
        if h < HEADS:
            nrm = nrm * (HD ** -0.5)
        qkv_ref[0, :, cols] = (t * nrm).astype(BF16)
    qkv_ref[0, :, 2 * D_B:] = act[:, 2 * D_B:].astype(BF16)

    z_ref[0] = _dot(xe, wbf_ref[:, 2 * D_A + N_QKV:N_MAIN])[HALO:HALO + TM_IN].astype(BF16)

    tri_l = tril_ref[...]
    tri_u = triu_ref[...]

    def split3(g):
        hi = g.astype(BF16)
        r1 = g - hi.astype(F32)
        mid = r1.astype(BF16)
        return hi, mid, (r1 - mid.astype(F32)).astype(BF16)

    ab = _dot(xb, wab_ref[...])
    g3 = split3(-jnp.exp(alog_ref[...]) * _softplus(ab + dtb_ref[...]))
    lane = lax.broadcasted_iota(jnp.int32, ab.shape, 1)
    gb = jnp.where(lane < HEADS, _dot(tri_l, g3[0]) + _dot(tri_l, g3[1]) + _dot(tri_l, g3[2]),
                   jnp.where(lane < 2 * HEADS, _dot(tri_u, g3[0]) + _dot(tri_u, g3[1]) + _dot(tri_u, g3[2]),
                             _sigmoid(ab)))
    gb_ref[0] = gb[:, 0:N_AB]

    abt = _dot_nt(wabt_ref[...], xb)
    t3 = split3(-jnp.exp(alogt_ref[...]) * _softplus(abt + dtbt_ref[...]))
    row = lax.broadcasted_iota(jnp.int32, abt.shape, 0)
    gbt_ref[0] = jnp.where(row < HEADS, _dot(t3[0], tri_u) + _dot(t3[1], tri_u) + _dot(t3[2], tri_u),
                           jnp.where(row < 2 * HEADS, _dot(t3[0], tri_l) + _dot(t3[1], tri_l) + _dot(t3[2], tri_l),
                                     _sigmoid(abt)))


def _in_proj(x, ctx, mod_lat, mod_ctx, g1, w_main, w_ab, w_abt, lng, lnb, ws, bst, conv, alog, dtb, alogt, dtbt):
    B, L, _ = x.shape
    n_lat = L // TM_IN
    n_steps = n_lat + 1
    LC = L + TM_IN
    hb = TM_IN // HALO

    def full(shape):
        return pl.BlockSpec(shape, lambda b, j: (0,) * len(shape))

    in_specs = [
        pl.BlockSpec((1, TM_IN, D), lambda b, j: (b, jnp.maximum(j - 1, 0), 0)),
        pl.BlockSpec((1, HALO, D), lambda b, j: (b, jnp.clip((j - 1) * hb - 1, 0, L // HALO - 1), 0)),
        pl.BlockSpec((1, HALO, D), lambda b, j: (b, jnp.clip(j * hb, 0, L // HALO - 1), 0)),
        pl.BlockSpec((1, TM_IN, D), lambda b, j: (b, 0, 0)),
        pl.BlockSpec((1, 1, 6 * D), lambda b, j: (b, 0, 0)),
        full((1, 6 * D)), full((1, D)),
        pl.BlockSpec((D, N_MAIN), lambda b, j: (0, 0), pipeline_mode=pl.Buffered(1)),
        full((D, 128)), full((N_AB, D)),
        full((1, D_A)), full((1, D_A)), full((HEADS, CHUNK, CHUNK)), full((CHUNK, HEADS)),
        full((CONV_W, N_QKV)), full((1, 128)), full((1, 128)), full((N_AB, 1)), full((N_AB, 1)),
        full((TM_IN, TM_IN)), full((TM_IN, TM_IN)),
    ]
    out_shape = [
        jax.ShapeDtypeStruct((B, L, D_A), BF16),
        jax.ShapeDtypeStruct((B, LC, N_QKV), BF16),
        jax.ShapeDtypeStruct((B, LC, D_B), BF16),
        jax.ShapeDtypeStruct((B, LC, N_AB), F32),
        jax.ShapeDtypeStruct((B, N_AB, LC), F32),
    ]
    out_specs = [
        pl.BlockSpec((1, TM_IN, D_A), lambda b, j: (b, jnp.maximum(j - 1, 0), 0)),
        pl.BlockSpec((1, TM_IN, N_QKV), lambda b, j: (b, j, 0)),
        pl.BlockSpec((1, TM_IN, D_B), lambda b, j: (b, j, 0)),
        pl.BlockSpec((1, TM_IN, N_AB), lambda b, j: (b, j, 0)),
        pl.BlockSpec((1, N_AB, TM_IN), lambda b, j: (b, 0, j)),
    ]
    return pl.pallas_call(
        _in_kernel,
        out_shape=out_shape,
        grid=(B, n_steps),
        in_specs=in_specs,
        out_specs=out_specs,
        scratch_shapes=[pltpu.VMEM((TM_IN + 2 * HALO, N_QKV), F32), pltpu.VMEM((D, N_MAIN), BF16)],
        compiler_params=pltpu.CompilerParams(dimension_semantics=("arbitrary", "arbitrary"),
                                             vmem_limit_bytes=VMEM_LIMIT),
    )(x, x, x, ctx, mod_lat, mod_ctx, g1, w_main, w_ab, w_abt, lng, lnb, ws, bst, conv, alog, dtb, alogt, dtbt,
      jnp.asarray(_CHUNK_TRIL, BF16), jnp.asarray(_CHUNK_TRIL.T, BF16))


def _delta_kernel(qf_ref, qb_ref, zf_ref, zb_ref, gf_ref, gbk_ref, gtf_ref, gtb_ref, on_ref,
                  y_ref, s_ref, oacc_ref, *, n_ctx, n_lat):
    s = pl.program_id(1)

    @pl.when(s == 0)
    def _():
        s_ref[...] = jnp.zeros_like(s_ref)
        oacc_ref[...] = jnp.zeros_like(oacc_ref)

    row = lax.broadcasted_iota(jnp.int32, (CHUNK, CHUNK), 0)
    col = lax.broadcasted_iota(jnp.int32, (CHUNK, CHUNK), 1)
    low = row > col
    upp = row < col
    same_blk = (row // 16) == (col // 16)
    eye = jnp.where(row == col, 1.0, 0.0).astype(BF16)
    zero = jnp.zeros((CHUNK, CHUNK), BF16)
    onorm = on_ref[...]
    half = n_ctx + n_lat // 2
    second = s >= half
    g_refs = (gf_ref, gbk_ref)
    gt_refs = (gtf_ref, gtb_ref)
    qkv_refs = (qf_ref, qb_ref)
    z_refs = (zf_ref, zb_ref)
    nb = qf_ref.shape[0]
    ps = range(nb * HEADS)

    def halves(xc, unit):
        xb = xc.astype(BF16)
        fill = eye if unit else zero
        return jnp.where(low, xb, fill), jnp.where(upp, xb, fill)

    def as_lhs(hv):
        return jnp.concatenate(hv, axis=1)

    def as_rhs(*hvs):
        cols_ = [jnp.concatenate(hv, axis=0) for hv in hvs]
        return cols_[0] if len(cols_) == 1 else jnp.concatenate(cols_, axis=1)

    def load(d, p, part):
        bb, h = divmod(p, HEADS)
        return qkv_refs[d][bb, :, part * D_B + h * HD:part * D_B + (h + 1) * HD]

    def gcol(d, p, base):
        bb, h = divmod(p, HEADS)
        c = base + d * HEADS + h
        return g_refs[d][bb, :, c:c + 1]

    def grow(d, p, base):
        bb, h = divmod(p, HEADS)
        r = base + d * HEADS + h
        return gt_refs[d][bb, r:r + 1, :]

    def lanes(col):
        return jnp.broadcast_to(col, (CHUNK, HD))

    q = [[load(d, p, 0) for p in ps] for d in range(2)]
    k = [[load(d, p, 1) for p in ps] for d in range(2)]
    v = [[load(d, p, 2) for p in ps] for d in range(2)]
    gcl = [[lanes(gcol(d, p, 0)) for p in ps] for d in range(2)]
    betal = [[lanes(gcol(d, p, 2 * HEADS)) for p in ps] for d in range(2)]
    gr = [[grow(d, p, 0) for p in ps] for d in range(2)]
    betar = [[grow(d, p, 2 * HEADS) for p in ps] for d in range(2)]
    glast = [[gr[0][p][:, CHUNK - 1:CHUNK] for p in ps], [gr[1][p][:, 0:1] for p in ps]]

    gram = [[_dot_nt(jnp.concatenate([q[d][p], k[d][p]], axis=0), k[d][p]) for p in ps] for d in range(2)]
    dec = [jnp.exp(jnp.where(low, gcl[0][p] - gr[0][p], jnp.where(upp, gcl[1][p] - gr[1][p], 0.0))) for p in ps]
    lc = [jnp.where(low, gram[0][p][CHUNK:] * betar[0][p], jnp.where(upp, gram[1][p][CHUNK:] * betar[1][p], 0.0))
          * dec[p] for p in ps]
    qk = [[jnp.where(upp, 0.0, gram[0][p][:CHUNK] * dec[p]).astype(BF16) for p in ps],
          [jnp.where(low, 0.0, gram[1][p][:CHUNK] * dec[p]).astype(BF16) for p in ps]]

    dg = [jnp.where(same_blk, lc[p], 0.0) for p in ps]
    ob = [lc[p] - dg[p] for p in ps]
    d1h = [halves(dg[p], False) for p in ps]
    d2 = [_dot(as_lhs(d1h[p]), as_rhs(d1h[p])) for p in ps]
    p0s = [-dg[p] for p in ps]
    d2h = [halves(d2[p], False) for p in ps]
    p0h = [halves(p0s[p], True) for p in ps]
    o2 = [_dot(as_lhs(d2h[p]), as_rhs(d2h[p], p0h[p])) for p in ps]
    p1s = [p0s[p] + o2[p][:, CHUNK:] for p in ps]
    d4h = [halves(o2[p][:, :CHUNK], False) for p in ps]
    p1h = [halves(p1s[p], True) for p in ps]
    o3 = [_dot(as_lhs(d4h[p]), as_rhs(d4h[p], p1h[p])) for p in ps]
    p2s = [p1s[p] + o3[p][:, CHUNK:] for p in ps]
    d8h = [halves(o3[p][:, :CHUNK], False) for p in ps]
    p2h = [halves(p2s[p], True) for p in ps]
    p3s = [p2s[p] + _dot(as_lhs(d8h[p]), as_rhs(p2h[p])) for p in ps]
    p3h = [halves(p3s[p], True) for p in ps]
    obh = [halves(ob[p], False) for p in ps]
    n1h = [halves(_dot(as_lhs(p3h[p]), as_rhs(obh[p])), False) for p in ps]
    o6 = [_dot(as_lhs(n1h[p]), as_rhs(n1h[p], p3h[p])) for p in ps]
    r0s = [p3s[p] - o6[p][:, CHUNK:] for p in ps]
    n2h = [halves(o6[p][:, :CHUNK], False) for p in ps]
    r0h = [halves(r0s[p], True) for p in ps]
    o7 = [_dot(as_lhs(n2h[p]), as_rhs(n2h[p], r0h[p])) for p in ps]
    r1s = [r0s[p] + o7[p][:, CHUNK:] for p in ps]
    n4h = [halves(o7[p][:, :CHUNK], False) for p in ps]
    r1h = [halves(r1s[p], True) for p in ps]
    tinv = [halves(r1s[p] + _dot(as_lhs(n4h[p]), as_rhs(r1h[p])), True) for p in ps]

    for d in range(2):
        lat_chunk = (s - n_ctx) if d == 0 else (n_ctx + n_lat - 1 - s)
        off = pl.multiple_of(jnp.clip(lat_chunk, 0, n_lat - 1) * CHUNK, CHUNK)
        sidx = [(p // HEADS * 2 + d) * HEADS + p % HEADS for p in ps]
        egc = [jnp.exp(gcl[d][p]) for p in ps]
        kf = [k[d][p].astype(F32) for p in ps]
        rhs = [jnp.concatenate([v[d][p], (kf[p] * egc[p]).astype(BF16)], axis=1) for p in ps]
        uw = [_dot(tinv[p][d], rhs[p]) for p in ps]
        qd = [q[d][p].astype(F32) * egc[p] for p in ps]
        kd = [(kf[p] * jnp.exp(glast[d][p] - gcl[d][p])).astype(BF16) for p in ps]
        st = [s_ref[sidx[p]] for p in ps]
        a1 = [_dot(jnp.concatenate([uw[p][:, HD:] * betal[d][p], qd[p]], axis=0).astype(BF16), st[p].astype(BF16))
              for p in ps]
        vnew = [(uw[p][:, :HD] * betal[d][p] - a1[p][:CHUNK]).astype(BF16) for p in ps]
        o = [a1[p][CHUNK:] + _dot(qk[d][p], vnew[p]) for p in ps]
        for p in ps:
            s_ref[sidx[p]] = st[p] * jnp.exp(glast[d][p]) + _dot_tn(kd[p], vnew[p])
        for p in ps:
            bb, h = divmod(p, HEADS)
            cols = slice(h * HD, (h + 1) * HD)
            tot = jnp.where(second, oacc_ref[bb, pl.ds(off, CHUNK), cols], 0.0) + o[p]
            oacc_ref[bb, pl.ds(off, CHUNK), cols] = tot
            ms = jnp.mean(tot * tot, axis=-1, keepdims=True)
            zz = z_refs[d][bb, :, cols].astype(F32)
            y_ref[bb, pl.ds(off, CHUNK), cols] = (tot * lax.rsqrt(ms + EPS) * onorm * _silu(zz)).astype(BF16)


def _delta(qkv, z, gb, gbt, onorm, L):
    B, LC, _ = qkv.shape
    n_all = LC // CHUNK
    n_lat = L // CHUNK
    n_ctx = n_all - n_lat

    def cf(s):
        return s

    def cb(s):
        return jnp.where(s < n_ctx, n_ctx - 1 - s, n_all + n_ctx - 1 - s)

    in_specs = [
        pl.BlockSpec((NB_DELTA, CHUNK, N_QKV), lambda b, s: (b, cf(s), 0)),
        pl.BlockSpec((NB_DELTA, CHUNK, N_QKV), lambda b, s: (b, cb(s), 0)),
        pl.BlockSpec((NB_DELTA, CHUNK, D_B), lambda b, s: (b, cf(s), 0)),
        pl.BlockSpec((NB_DELTA, CHUNK, D_B), lambda b, s: (b, cb(s), 0)),
        pl.BlockSpec((NB_DELTA, CHUNK, N_AB), lambda b, s: (b, cf(s), 0)),
        pl.BlockSpec((NB_DELTA, CHUNK, N_AB), lambda b, s: (b, cb(s), 0)),
        pl.BlockSpec((NB_DELTA, N_AB, CHUNK), lambda b, s: (b, 0, cf(s))),
        pl.BlockSpec((NB_DELTA, N_AB, CHUNK), lambda b, s: (b, 0, cb(s))),
        pl.BlockSpec((1, HD), lambda b, s: (0, 0)),
    ]
    return pl.pallas_call(
        functools.partial(_delta_kernel, n_ctx=n_ctx, n_lat=n_lat),
        out_shape=jax.ShapeDtypeStruct((B, L, D_B), BF16),
        grid=(B // NB_DELTA, n_all),
        in_specs=in_specs,
        out_specs=pl.BlockSpec((NB_DELTA, L, D_B), lambda b, s: (b, 0, 0)),
        scratch_shapes=[pltpu.VMEM((NB_DELTA * 2 * HEADS, HD, HD), F32), pltpu.VMEM((NB_DELTA, L, D_B), F32)],
        compiler_params=pltpu.CompilerParams(dimension_semantics=("arbitrary", "arbitrary"),
                                             vmem_limit_bytes=VMEM_LIMIT),
    )(qkv, qkv, z, z, gb, gb, gbt, gbt, onorm)


def _out_kernel(x_ref, ya_ref, yb_ref, mod_ref, wo_ref, g2_ref, wrt_ref, brt_ref,
                x2_ref, h_ref, route_ref, cnt_ref, base_ref, wbf_ref):
    i = pl.program_id(0)

    @pl.when(i == 0)
    def _():
        base_ref[...] = jnp.zeros_like(base_ref)
        wbf_ref[...] = wo_ref[...].astype(BF16)

    mod = mod_ref[0]
    gt1 = mod[:, 2 * D:3 * D]
    sh2 = mod[:, 3 * D:4 * D]
    sc2 = mod[:, 4 * D:5 * D]
    mix = _dot(ya_ref[...], wbf_ref[0:D_A, :]) + _dot(yb_ref[...], wbf_ref[D_A:, :])
    x2 = x_ref[...] + gt1 * mix
    x2_ref[...] = x2
    ms = jnp.mean(x2 * x2, axis=-1, keepdims=True)
    hv = x2 * lax.rsqrt(ms + EPS) * (g2_ref[...] * (1.0 + sc2)) + sh2
    h_ref[:, 0:D] = hv

    lt = _dot_nt(wrt_ref[...], hv.astype(BF16)) + brt_ref[...]
    gl = [lt[r:r + 1, :] for r in range(N_GROUPS)]
    gmax = jnp.maximum(jnp.maximum(gl[0], gl[1]), jnp.maximum(gl[2], gl[3]))
    gsel = jnp.where(gl[0] == gmax, 0, jnp.where(gl[1] == gmax, 1, jnp.where(gl[2] == gmax, 2, 3)))
    p_g = 1.0 / (jnp.exp(gl[0] - gmax) + jnp.exp(gl[1] - gmax) + jnp.exp(gl[2] - gmax) + jnp.exp(gl[3] - gmax))
    el = []
    for e in range(EPG):
        r = [lt[N_GROUPS + g * EPG + e:N_GROUPS + g * EPG + e + 1, :] for g in range(N_GROUPS)]
        el.append(jnp.where(gsel == 0, r[0], jnp.where(gsel == 1, r[1], jnp.where(gsel == 2, r[2], r[3]))))
    m1 = jnp.maximum(jnp.maximum(el[0], el[1]), jnp.maximum(el[2], el[3]))
    i1 = jnp.where(el[0] == m1, 0, jnp.where(el[1] == m1, 1, jnp.where(el[2] == m1, 2, 3)))
    neg = jnp.float32(-jnp.inf)
    el2 = [jnp.where(i1 == e, neg, el[e]) for e in range(EPG)]
    m2 = jnp.maximum(jnp.maximum(el2[0], el2[1]), jnp.maximum(el2[2], el2[3]))
    i2 = jnp.where(jnp.logical_and(el2[0] == m2, i1 != 0), 0,
                   jnp.where(jnp.logical_and(el2[1] == m2, i1 != 1), 1,
                             jnp.where(jnp.logical_and(el2[2] == m2, i1 != 2), 2, 3)))
    t = jnp.exp(m2 - m1)
    w1 = p_g / (1.0 + t)
    w2 = p_g * t / (1.0 + t)
    first_low = i1 < i2
    ea = jnp.where(first_low, i1, i2)
    eb = jnp.where(first_low, i2, i1)
    w_a = jnp.where(first_low, w1, w2)
    w_b = jnp.where(first_low, w2, w1)
    pair = jnp.where(ea == 0, eb - 1, jnp.where(ea == 1, eb + 1, 5))
    bucket = gsel * N_PAIRS + pair

    tm = bucket.shape[1]
    rows = lax.broadcasted_iota(jnp.int32, (32, tm), 0)
    onehot = jnp.where(rows == bucket, 1.0, 0.0).astype(F32)
    r2 = lax.broadcasted_iota(jnp.int32, (tm, tm), 0)
    c2 = lax.broadcasted_iota(jnp.int32, (tm, tm), 1)
    tri = jnp.where(r2 <= c2, 1.0, 0.0).astype(BF16)
    prefix = _dot(onehot.astype(BF16), tri)
    base = base_ref[:, 0:1]
    rank = jnp.sum(onehot * (prefix - 1.0 + base), axis=0, keepdims=True)
    newbase = base + prefix[:, tm - 1:tm]
    base_ref[...] = jnp.broadcast_to(newbase, base_ref.shape)
    cnt_ref[...] = jnp.broadcast_to(newbase, cnt_ref.shape)
    route = jnp.concatenate([bucket.astype(F32), rank, w_a, w_b, jnp.zeros((4, tm), F32)], axis=0)
    route_ref[0] = route
    h_ref[:, D:D_ROW] = jnp.transpose(jnp.concatenate([route, jnp.zeros((120, tm), F32)], axis=0))


def _out_proj(x2d, ya, yb, mod3, w_o, g2, wrt, brt, L):
    T = x2d.shape[0]
    nb = T // TM_OUT
    per_b = L // TM_OUT
    out_shape = [
        jax.ShapeDtypeStruct((T, D), F32),
        jax.ShapeDtypeStruct((T, D_ROW), F32),
        jax.ShapeDtypeStruct((nb, 8, TM_OUT), F32),
        jax.ShapeDtypeStruct((32, 128), F32),
    ]
    return pl.pallas_call(
        _out_kernel,
        out_shape=out_shape,
        grid=(nb,),
        in_specs=[
            pl.BlockSpec((TM_OUT, D), lambda i: (i, 0)),
            pl.BlockSpec((TM_OUT, D_A), lambda i: (i, 0)),
            pl.BlockSpec((TM_OUT, D_B), lambda i: (i, 0)),
            pl.BlockSpec((1, 1, 6 * D), lambda i: (i // per_b, 0, 0)),
            pl.BlockSpec((D, D), lambda i: (0, 0), pipeline_mode=pl.Buffered(1)),
            pl.BlockSpec((1, D), lambda i: (0, 0)),
            pl.BlockSpec((32, D), lambda i: (0, 0)),
            pl.BlockSpec((32, 1), lambda i: (0, 0)),
        ],
        out_specs=[
            pl.BlockSpec((TM_OUT, D), lambda i: (i, 0)),
            pl.BlockSpec((TM_OUT, D_ROW), lambda i: (i, 0)),
            pl.BlockSpec((1, 8, TM_OUT), lambda i: (i, 0, 0)),
            pl.BlockSpec((32, 128), lambda i: (0, 0)),
        ],
        scratch_shapes=[pltpu.VMEM((32, 128), F32), pltpu.VMEM((D, D), BF16)],
        compiler_params=pltpu.CompilerParams(dimension_semantics=("arbitrary",),
                                             vmem_limit_bytes=VMEM_LIMIT),
    )(x2d, ya, yb, mod3, w_o, g2, wrt, brt)


def _sc_row_gather(table, idx):
    n_rows = idx.shape[0]
    width = table.shape[1]
    n_workers = SC_CORES * SC_SUBCORES
    per_worker = n_rows // n_workers
    assert per_worker * n_workers == n_rows and per_worker % SC_WINDOW == 0
    mesh = plsc.VectorSubcoreMesh(core_axis_name="c", subcore_axis_name="s")

    @functools.partial(
        pl.kernel, mesh=mesh,
        out_type=jax.ShapeDtypeStruct((n_rows, width), table.dtype),
        scratch_types=[pltpu.VMEM((SC_WINDOW,), jnp.int32),
                       pltpu.VMEM((SC_WINDOW, width), table.dtype),
                       pltpu.SemaphoreType.DMA],
    )
    def gather(table_hbm, idx_hbm, out_hbm, idx_v, rows_v, sem):
        wid = lax.axis_index("s") * SC_CORES + lax.axis_index("c")
        base = wid * per_worker

        @pl.loop(0, per_worker // SC_WINDOW)
        def _(j):
            off = pl.multiple_of(base + j * SC_WINDOW, 8)
            pltpu.sync_copy(idx_hbm.at[pl.ds(off, SC_WINDOW)], idx_v)
            pltpu.async_copy(table_hbm.at[idx_v], rows_v, sem).wait()
            pltpu.sync_copy(rows_v, out_hbm.at[pl.ds(off, SC_WINDOW)])

    return gather(table, idx)


def _sc_row_scatter(rows, pos, n_out):
    n_rows, width = rows.shape
    n_workers = SC_CORES * SC_SUBCORES
    per_worker = n_rows // n_workers
    n_chunks = per_worker // SC_WINDOW
    assert per_worker * n_workers == n_rows and n_chunks * SC_WINDOW == per_worker
    mesh = plsc.VectorSubcoreMesh(core_axis_name="c", subcore_axis_name="s")

    @functools.partial(
        pl.kernel, mesh=mesh,
        out_type=jax.ShapeDtypeStruct((n_out, width), rows.dtype),
        scratch_types=[pltpu.VMEM((n_chunks, SC_WINDOW), jnp.int32),
                       pltpu.VMEM((SC_WINDOW, width), rows.dtype),
                       pltpu.SemaphoreType.DMA],
    )
    def scatter(rows_hbm, pos_hbm, out_hbm, idx_v, rows_v, sem):
        wid = lax.axis_index("s") * SC_CORES + lax.axis_index("c")
        base = wid * per_worker
        pltpu.sync_copy(pos_hbm.at[wid], idx_v)

        @pl.loop(0, n_chunks)
        def _(j):
            off = pl.multiple_of(base + j * SC_WINDOW, 8)
            pltpu.sync_copy(rows_hbm.at[pl.ds(off, SC_WINDOW)], rows_v)
            pltpu.async_copy(rows_v, out_hbm.at[idx_v.at[j]], sem).wait()

    return scatter(rows, pos.reshape(n_workers, n_chunks, SC_WINDOW))


def _moe_kernel(ea_ref, eb_ref, nv_ref, chg_ref, hs_ref, wga_ref, wgb_ref, wda_ref, wdb_ref, o_ref, wg_s, wd_s):
    i = pl.program_id(0)

    @pl.when(chg_ref[i] == 1)
    def _():
        wg_s[0] = wga_ref[0].astype(BF16)
        wg_s[1] = wgb_ref[0].astype(BF16)
        wd_s[0] = wda_ref[0].astype(BF16)
        wd_s[1] = wdb_ref[0].astype(BF16)

    @pl.when(nv_ref[i] > 0)
    def _():
        live = lax.broadcasted_iota(jnp.int32, (BM_MOE, 1), 0) < nv_ref[i]
        xb = jnp.where(live, hs_ref[:, 0:D], 0.0).astype(BF16)
        w_a = jnp.where(live, hs_ref[:, LANE_WA:LANE_WA + 1], 0.0)
        w_b = jnp.where(live, hs_ref[:, LANE_WA + 1:LANE_WA + 2], 0.0)
        ga = _dot(xb, wg_s[0])
        act_a = (_silu(ga[:, :D_EXPERT]) * ga[:, D_EXPERT:] * w_a).astype(BF16)
        gb = _dot(xb, wg_s[1])
        act_b = (_silu(gb[:, :D_EXPERT]) * gb[:, D_EXPERT:] * w_b).astype(BF16)
        o_ref[...] = _dot(act_a, wd_s[0]) + _dot(act_b, wd_s[1])

    @pl.when(nv_ref[i] <= 0)
    def _():
        o_ref[...] = jnp.zeros_like(o_ref)


def _moe(ea, eb, nvalid, chg, hs, w_gu, w_dn):
    nblk = ea.shape[0]
    S = nblk * BM_MOE
    grid_spec = pltpu.PrefetchScalarGridSpec(
        num_scalar_prefetch=4,
        grid=(nblk,),
        in_specs=[
            pl.BlockSpec((BM_MOE, D_ROW), lambda i, ea, eb, nv, chg: (i, 0)),
            pl.BlockSpec((1, D, 2 * D_EXPERT), lambda i, ea, eb, nv, chg: (ea[i], 0, 0)),
            pl.BlockSpec((1, D, 2 * D_EXPERT), lambda i, ea, eb, nv, chg: (eb[i], 0, 0)),
            pl.BlockSpec((1, D_EXPERT, D), lambda i, ea, eb, nv, chg: (ea[i], 0, 0)),
            pl.BlockSpec((1, D_EXPERT, D), lambda i, ea, eb, nv, chg: (eb[i], 0, 0)),
        ],
        out_specs=pl.BlockSpec((BM_MOE, D), lambda i, ea, eb, nv, chg: (i, 0)),
        scratch_shapes=[pltpu.VMEM((2, D, 2 * D_EXPERT), BF16), pltpu.VMEM((2, D_EXPERT, D), BF16)],
    )
    return pl.pallas_call(
        _moe_kernel,
        out_shape=jax.ShapeDtypeStruct((S, D), F32),
        grid_spec=grid_spec,
        compiler_params=pltpu.CompilerParams(dimension_semantics=("arbitrary",),
                                             vmem_limit_bytes=VMEM_LIMIT),
    )(ea, eb, nvalid, chg, hs, w_gu, w_gu, w_dn, w_dn)


def _final_kernel(x2_ref, m_ref, mod_ref, fg_ref, o_ref):
    gt2 = mod_ref[0][:, 5 * D:6 * D]
    y = x2_ref[...] + gt2 * m_ref[...]
    ms = jnp.mean(y * y, axis=-1, keepdims=True)
    o_ref[...] = y * lax.rsqrt(ms + EPS) * fg_ref[...]


def _final(x2, m, mod3, fg, L):
    T = x2.shape[0]
    per_b = L // TM_FIN
    return pl.pallas_call(
        _final_kernel,
        out_shape=jax.ShapeDtypeStruct((T, D), F32),
        grid=(T // TM_FIN,),
        in_specs=[
            pl.BlockSpec((TM_FIN, D), lambda i: (i, 0)),
            pl.BlockSpec((TM_FIN, D), lambda i: (i, 0)),
            pl.BlockSpec((1, 1, 6 * D), lambda i: (i // per_b, 0, 0)),
            pl.BlockSpec((1, D), lambda i: (0, 0)),
        ],
        out_specs=pl.BlockSpec((TM_FIN, D), lambda i: (i, 0)),
        compiler_params=pltpu.CompilerParams(dimension_semantics=("arbitrary",),
                                             vmem_limit_bytes=VMEM_LIMIT),
    )(x2, m, mod3, fg)


def kernel(x, c, ctx, c_ctx, w_ada, b_ada, norm1_g, w_in, ln_a_g, ln_a_b, w_spatial, b_spatial, conv_qkv, a_log,
           dt_bias, onorm_g, w_out, norm2_g, w_group, b_group, w_router, b_router, w_gate_up, w_down, final_g):
    B, L, _ = x.shape
    T = B * L
    assert w_ada.shape[0] == 1 and ctx.shape[1] == TM_IN and L % TM_OUT == 0

    cond = jnp.concatenate([c, c_ctx[None, :], jnp.zeros((7, D), F32)], axis=0)
    mod = _modulation(cond, w_ada[0], b_ada[0][None, :])
    mod_lat = mod[:B].reshape(B, 1, 6 * D)
    mod_ctx = mod[B:B + 1]

    w_ab = jnp.pad(w_in[0][:, N_MAIN:], ((0, 0), (0, 128 - N_AB))).astype(BF16)
    w_abt = w_in[0][:, N_MAIN:].T.astype(BF16)
    alog = a_log[0].reshape(1, 2 * HEADS)
    dtb = dt_bias[0].reshape(1, 2 * HEADS)
    alog_row = jnp.pad(alog, ((0, 0), (0, 128 - 2 * HEADS)))
    dtb_row = jnp.pad(dtb, ((0, 0), (0, 128 - 2 * HEADS)))
    alog_col = jnp.pad(alog, ((0, 0), (0, N_AB - 2 * HEADS))).T
    dtb_col = jnp.pad(dtb, ((0, 0), (0, N_AB - 2 * HEADS))).T

    ya, qkv, z, gb, gbt = _in_proj(
        x, ctx, mod_lat, mod_ctx, norm1_g, w_in[0], w_ab, w_abt, ln_a_g, ln_a_b,
        w_spatial[0].astype(BF16), b_spatial[0].T, conv_qkv[0], alog_row, dtb_row, alog_col, dtb_col)

    yb = _delta(qkv, z, gb, gbt, onorm_g, L)

    wrt = jnp.concatenate([w_group[0].T, w_router[0].T, jnp.zeros((32 - N_GROUPS - N_EXPERTS, D), F32)], axis=0)
    brt = jnp.concatenate([b_group[0], b_router[0], jnp.zeros((32 - N_GROUPS - N_EXPERTS,), F32)])[:, None]
    x2, h, route, cnt = _out_proj(x.reshape(T, D), ya.reshape(T, D_A), yb.reshape(T, D_B), mod_lat,
                                  w_out[0], norm2_g, wrt.astype(BF16), brt, L)

    bucket = route[:, 0, :].reshape(T).astype(jnp.int32)
    rank = route[:, 1, :].reshape(T).astype(jnp.int32)
    counts = cnt[:N_BUCKETS, 0].astype(jnp.int32)
    nblk_b = (counts + BM_MOE - 1) // BM_MOE
    blk_end = jnp.cumsum(nblk_b)
    blk_start = blk_end - nblk_b
    pos = blk_start[bucket] * BM_MOE + rank
    n_blocks = T // BM_MOE + N_BUCKETS
    S = n_blocks * BM_MOE
    blk = jnp.arange(n_blocks, dtype=jnp.int32)
    used = blk < blk_end[-1]
    bkt = jnp.sum((jnp.minimum(blk, blk_end[-1] - 1)[:, None] >= blk_end[None, :]).astype(jnp.int32), axis=1)
    nvalid = jnp.where(used, jnp.clip(counts[bkt] - (blk - blk_start[bkt]) * BM_MOE, 0, BM_MOE), 0).astype(jnp.int32)
    pa = jnp.asarray(PAIR_A, jnp.int32)
    pb = jnp.asarray(PAIR_B, jnp.int32)
    ea = (bkt // N_PAIRS) * EPG + pa[bkt % N_PAIRS]
    eb = (bkt // N_PAIRS) * EPG + pb[bkt % N_PAIRS]
    chg = jnp.concatenate([jnp.ones((1,), jnp.int32), (bkt[1:] != bkt[:-1]).astype(jnp.int32)])

    hs = _sc_row_scatter(h, pos, S)
    ms = _moe(ea, eb, nvalid, chg, hs, w_gate_up[0], w_down[0])
    out = _final(x2, _sc_row_gather(ms, pos), mod_lat, final_g[None, :], L)
    return out.reshape(B, L, D)
```

```python
import functools

import jax
import jax.numpy as jnp
import numpy as np
from jax import lax
from jax.experimental import pallas as pl
from jax.experimental.pallas import tpu as pltpu
from jax.experimental.pallas import tpu_sc as plsc

F32 = jnp.float32
BF16 = jnp.bfloat16
EPS = 1e-6

D = 1024
D_A = 512
D_B = 512
HEADS = 4
HD = 128
CHUNK = 128
CONV_W = 5
N_QKV = 3 * D_B
N_MAIN = 2 * D_A + 4 * D_B
N_AB = 16
N_GROUPS = 4
EPG = 4
N_EXPERTS = 16
D_EXPERT = 512
N_PAIRS = 6
N_BUCKETS = N_GROUPS * N_PAIRS
D_PACK = D // 2
D_ROW = D_PACK + 128
PAIR_A = (0, 0, 0, 1, 1, 2)
PAIR_B = (1, 2, 3, 2, 3, 3)

TM_IN = 256
HALO = 8
NB_IN = 2
NB_DELTA = 2
TM_OUT = 512
BM_MOE = 256
TM_FIN = 512
VMEM_LIMIT = 56 * 1024 * 1024
SC_CORES = 2
SC_SUBCORES = 16
SC_WINDOW = 32

HI = lax.Precision.HIGHEST

_CHUNK_TRIL = np.kron(np.eye(TM_IN // CHUNK, dtype=np.float32), np.tril(np.ones((CHUNK, CHUNK), np.float32)))


def _dot(a, b, precision=None):
    return jnp.dot(a, b, preferred_element_type=F32, precision=precision)


def _dot_nt(a, b):
    return lax.dot_general(a, b, (((1,), (1,)), ((), ())), preferred_element_type=F32)


def _dot_tn(a, b):
    return lax.dot_general(a, b, (((0,), (0,)), ((), ())), preferred_element_type=F32)


def _sigmoid(x):
    return 1.0 / (1.0 + jnp.exp(-x))


def _silu(x):
    return x * _sigmoid(x)


def _softplus(x):
    return jnp.maximum(x, 0.0) + jnp.log(1.0 + jnp.exp(-jnp.abs(x)))


def _pack_bf16_pairs(x):
    bits = lax.bitcast_convert_type(x.astype(BF16).astype(F32), jnp.uint32)
    return (bits[:, D_PACK:] & jnp.uint32(0xFFFF0000)) | (bits[:, :D_PACK] >> 16)


def _unpack_bf16_pairs(w):
    bits = w
    lo = lax.bitcast_convert_type(bits << 16, F32)
    hi = lax.bitcast_convert_type(bits & jnp.uint32(0xFFFF0000), F32)
    return jnp.concatenate([lo, hi], axis=1).astype(BF16)


def _gelu_tanh(x):
    return 0.5 * x * (1.0 + jnp.tanh(np.sqrt(2.0 / np.pi).astype(np.float32) * (x + 0.044715 * (x * x * x))))


def _mod_kernel(c_ref, w_ref, b_ref, o_ref):
    c = c_ref[...]
    o_ref[...] = _dot(_silu(c), w_ref[...], precision=HI) + b_ref[...]


def _modulation(cond, w_ada, b_ada):
    rows = cond.shape[0]
    tn = 1536
    return pl.pallas_call(
        _mod_kernel,
        out_shape=jax.ShapeDtypeStruct((rows, 6 * D), F32),
        grid=(6 * D // tn,),
        in_specs=[pl.BlockSpec((rows, D), lambda i: (0, 0)),
                  pl.BlockSpec((D, tn), lambda i: (0, i)),
                  pl.BlockSpec((1, tn), lambda i: (0, i))],
        out_specs=pl.BlockSpec((rows, tn), lambda i: (0, i)),
        compiler_params=pltpu.CompilerParams(dimension_semantics=("arbitrary",),
                                             vmem_limit_bytes=VMEM_LIMIT),
    )(cond, w_ada, b_ada)


def _in_kernel(x_ref, xp_ref, xn_ref, ctx_ref, mod_ref, cmod_ref, g1_ref, w_ref, wab_ref, wabt_ref,
               lng_ref, lnb_ref, ws_ref, bst_ref, conv_ref, alog_ref, dtb_ref, alogt_ref, dtbt_ref, tril_ref, triu_ref,
               ya_ref, qkv_ref, z_ref, gb_ref, gbt_ref, wbf_ref):
    j = pl.program_id(1)

    @pl.when(jnp.logical_and(pl.program_id(0) == 0, j == 0))
    def _():
        wbf_ref[...] = w_ref[...].astype(BF16)

    is_ctx = j == 0
    n_lat_blocks = pl.num_programs(1) - 1

    def one_batch_element(bb):
        mod = mod_ref[bb]
        cm = cmod_ref[...]
        sh = jnp.where(is_ctx, cm[:, 0:D], mod[:, 0:D])
        sc = jnp.where(is_ctx, cm[:, D:2 * D], mod[:, D:2 * D])
        scale = g1_ref[...] * (1.0 + sc)

        xmain = jnp.where(is_ctx, ctx_ref[bb], x_ref[bb])
        xv = jnp.concatenate([xp_ref[bb], xmain, xn_ref[bb]], axis=0)
        xnorm = xv * lax.rsqrt(jnp.mean(xv * xv, axis=-1, keepdims=True) + EPS) * scale + sh
        xe = xnorm.astype(BF16)
        xb = xnorm[HALO:HALO + TM_IN].astype(BF16)

        rid = lax.broadcasted_iota(jnp.int32, (TM_IN + 2 * HALO, 1), 0)
        prev_ok = j >= 2
        next_ok = jnp.logical_and(j >= 1, j < n_lat_blocks)
        valid = jnp.logical_or(jnp.logical_and(rid >= HALO, rid < HALO + TM_IN),
                               jnp.logical_or(jnp.logical_and(rid < HALO, prev_ok),
                                              jnp.logical_and(rid >= HALO + TM_IN, next_ok)))
        pad = (CONV_W - 1) // 2
        c_qkv = 2 * D_A

        def proj(c0, width):
            return _dot(xe, wbf_ref[:, c0:c0 + width])

        def conv_act(pq, c0):
            groups = (TM_IN + 2 * HALO) // 8
            x3 = jnp.where(valid, pq, 0.0).reshape(groups, 8, D_B)
            sub = lax.broadcasted_iota(jnp.int32, (1, 8, 1), 1)
            lo, hi = HALO // 8, HALO // 8 + TM_IN // 8
            acc = conv_ref[pad:pad + 1, c0:c0 + D_B] * x3[lo:hi]
            for t in range(CONV_W):
                s = t - pad
                if s == 0:
                    continue
                r = pltpu.roll(x3, (-s) % 8, axis=1)
                if s > 0:
                    sh = jnp.where(sub < 8 - s, r[lo:hi], r[lo + 1:hi + 1])
                else:
                    sh = jnp.where(sub >= -s, r[lo:hi], r[lo - 1:hi - 1])
                acc = acc + conv_ref[t:t + 1, c0:c0 + D_B] * sh
            return _silu(acc.reshape(TM_IN, D_B))

        def store_unit_heads(act, c0, gain):
            for h in range(HEADS):
                t = act[:, h * HD:(h + 1) * HD]
                nrm = lax.rsqrt(jnp.sum(t * t, axis=-1, keepdims=True) + EPS) * gain
                qkv_ref[bb, :, c0 + h * HD:c0 + (h + 1) * HD] = (t * nrm).astype(BF16)

        pq_q = proj(c_qkv, D_B)
        pq_k = proj(c_qkv + D_B, D_B)
        store_unit_heads(conv_act(pq_q, 0), 0, HD ** -0.5)
        pq_v = proj(c_qkv + 2 * D_B, D_B)
        store_unit_heads(conv_act(pq_k, D_B), D_B, 1.0)
        pa_u = proj(0, D_A)[HALO:HALO + TM_IN]
        qkv_ref[bb, :, 2 * D_B:] = conv_act(pq_v, 2 * D_B).astype(BF16)
        pa_v = proj(D_A, D_A)[HALO:HALO + TM_IN]
        u = _gelu_tanh(pa_u)
        pz = proj(c_qkv + N_QKV, D_B)
        v = _gelu_tanh(pa_v)
        mu = jnp.mean(v, axis=-1, keepdims=True)
        vc = v - mu
        var = jnp.mean(vc * vc, axis=-1, keepdims=True)
        vn = (vc * lax.rsqrt(var + EPS) * lng_ref[...] + lnb_ref[...]).astype(BF16)
        z_ref[bb] = pz[HALO:HALO + TM_IN].astype(BF16)

        bst = bst_ref[...]
        for n in range(TM_IN // CHUNK):
            rows = slice(n * CHUNK, (n + 1) * CHUNK)
            for h in range(HEADS):
                cols = slice(h * HD, (h + 1) * HD)
                s = _dot(ws_ref[h], vn[rows, cols]) + bst[:, h:h + 1]
                ya_ref[bb, rows, cols] = (u[rows, cols] * s).astype(BF16)

        tri_l = tril_ref[...]
        tri_u = triu_ref[...]

        def split3(g):
            hi = g.astype(BF16)
            r1 = g - hi.astype(F32)
            mid = r1.astype(BF16)
            return hi, mid, (r1 - mid.astype(F32)).astype(BF16)

        ab = _dot(xb, wab_ref[...])
        g3 = split3(-jnp.exp(alog_ref[...]) * _softplus(ab + dtb_ref[...]))
        lane = lax.broadcasted_iota(jnp.int32, ab.shape, 1)
        gb = jnp.where(lane < HEADS, _dot(tri_l, g3[0]) + _dot(tri_l, g3[1]) + _dot(tri_l, g3[2]),
                       jnp.where(lane < 2 * HEADS, _dot(tri_u, g3[0]) + _dot(tri_u, g3[1]) + _dot(tri_u, g3[2]),
                                 _sigmoid(ab)))
        gb_ref[bb] = gb[:, 0:N_AB]

        abt = _dot_nt(wabt_ref[...], xb)
        t3 = split3(-jnp.exp(alogt_ref[...]) * _softplus(abt + dtbt_ref[...]))
        row = lax.broadcasted_iota(jnp.int32, abt.shape, 0)
        gbt_ref[bb] = jnp.where(row < HEADS, _dot(t3[0], tri_u) + _dot(t3[1], tri_u) + _dot(t3[2], tri_u),
                               jnp.where(row < 2 * HEADS, _dot(t3[0], tri_l) + _dot(t3[1], tri_l) + _dot(t3[2], tri_l),
                                         _sigmoid(abt)))

    for bb in range(x_ref.shape[0]):
        one_batch_element(bb)


def _in_proj(x, ctx, mod_lat, mod_ctx, g1, w_main, w_ab, w_abt, lng, lnb, ws, bst, conv, alog, dtb, alogt, dtbt):
    B, L, _ = x.shape
    n_lat = L // TM_IN
    n_steps = n_lat + 1
    LC = L + TM_IN
    hb = TM_IN // HALO

    def full(shape):
        return pl.BlockSpec(shape, lambda b, j: (0,) * len(shape))

    in_specs = [
        pl.BlockSpec((NB_IN, TM_IN, D), lambda b, j: (b, jnp.maximum(j - 1, 0), 0)),
        pl.BlockSpec((NB_IN, HALO, D), lambda b, j: (b, jnp.clip((j - 1) * hb - 1, 0, L // HALO - 1), 0)),
        pl.BlockSpec((NB_IN, HALO, D), lambda b, j: (b, jnp.clip(j * hb, 0, L // HALO - 1), 0)),
        pl.BlockSpec((NB_IN, TM_IN, D), lambda b, j: (b, 0, 0)),
        pl.BlockSpec((NB_IN, 1, 6 * D), lambda b, j: (b, 0, 0)),
        full((1, 6 * D)), full((1, D)),
        pl.BlockSpec((D, N_MAIN), lambda b, j: (0, 0), pipeline_mode=pl.Buffered(1)),
        full((D, 128)), full((N_AB, D)),
        full((1, D_A)), full((1, D_A)), full((HEADS, CHUNK, CHUNK)), full((CHUNK, HEADS)),
        full((CONV_W, N_QKV)), full((1, 128)), full((1, 128)), full((N_AB, 1)), full((N_AB, 1)),
        full((TM_IN, TM_IN)), full((TM_IN, TM_IN)),
    ]
    out_shape = [
        jax.ShapeDtypeStruct((B, L, D_A), BF16),
        jax.ShapeDtypeStruct((B, LC, N_QKV), BF16),
        jax.ShapeDtypeStruct((B, LC, D_B), BF16),
        jax.ShapeDtypeStruct((B, LC, N_AB), F32),
        jax.ShapeDtypeStruct((B, N_AB, LC), F32),
    ]
    out_specs = [
        pl.BlockSpec((NB_IN, TM_IN, D_A), lambda b, j: (b, jnp.maximum(j - 1, 0), 0)),
        pl.BlockSpec((NB_IN, TM_IN, N_QKV), lambda b, j: (b, j, 0)),
        pl.BlockSpec((NB_IN, TM_IN, D_B), lambda b, j: (b, j, 0)),
        pl.BlockSpec((NB_IN, TM_IN, N_AB), lambda b, j: (b, j, 0)),
        pl.BlockSpec((NB_IN, N_AB, TM_IN), lambda b, j: (b, 0, j)),
    ]
    return pl.pallas_call(
        _in_kernel,
        out_shape=out_shape,
        grid=(B // NB_IN, n_steps),
        in_specs=in_specs,
        out_specs=out_specs,
        scratch_shapes=[pltpu.VMEM((D, N_MAIN), BF16)],
        compiler_params=pltpu.CompilerParams(dimension_semantics=("arbitrary", "arbitrary"),
                                             vmem_limit_bytes=VMEM_LIMIT),
    )(x, x, x, ctx, mod_lat, mod_ctx, g1, w_main, w_ab, w_abt, lng, lnb, ws, bst, conv, alog, dtb, alogt, dtbt,
      jnp.asarray(_CHUNK_TRIL, BF16), jnp.asarray(_CHUNK_TRIL.T, BF16))


def _delta_kernel(qf_ref, qb_ref, zf_ref, zb_ref, gf_ref, gbk_ref, gtf_ref, gtb_ref, on_ref,
                  y_ref, s_ref, oacc_ref, *, n_ctx, n_lat):
    s = pl.program_id(1)

    @pl.when(s == 0)
    def _():
        s_ref[...] = jnp.zeros_like(s_ref)
        oacc_ref[...] = jnp.zeros_like(oacc_ref)

    row = lax.broadcasted_iota(jnp.int32, (CHUNK, CHUNK), 0)
    col = lax.broadcasted_iota(jnp.int32, (CHUNK, CHUNK), 1)
    low = row > col
    upp = row < col
    same_blk = (row // 16) == (col // 16)
    eye = jnp.where(row == col, 1.0, 0.0).astype(BF16)
    zero = jnp.zeros((CHUNK, CHUNK), BF16)
    onorm = on_ref[...]
    half = n_ctx + n_lat // 2
    second = s >= half
    g_refs = (gf_ref, gbk_ref)
    gt_refs = (gtf_ref, gtb_ref)
    qkv_refs = (qf_ref, qb_ref)
    z_refs = (zf_ref, zb_ref)
    nb = qf_ref.shape[0]
    ps = range(nb * HEADS)

    def halves(xc, unit):
        xb = xc.astype(BF16)
        fill = eye if unit else zero
        return jnp.where(low, xb, fill), jnp.where(upp, xb, fill)

    def as_lhs(hv):
        return jnp.concatenate(hv, axis=1)

    def as_rhs(*hvs):
        cols_ = [jnp.concatenate(hv, axis=0) for hv in hvs]
        return cols_[0] if len(cols_) == 1 else jnp.concatenate(cols_, axis=1)

    def load(d, p, part):
        bb, h = divmod(p, HEADS)
        return qkv_refs[d][bb, :, part * D_B + h * HD:part * D_B + (h + 1) * HD]

    def gcol(d, p, base):
        bb, h = divmod(p, HEADS)
        c = base + d * HEADS + h
        return g_refs[d][bb, :, c:c + 1]

    def grow(d, p, base):
        bb, h = divmod(p, HEADS)
        r = base + d * HEADS + h
        return gt_refs[d][bb, r:r + 1, :]

    def lanes(col):
        return jnp.broadcast_to(col, (CHUNK, HD))

    q = [[load(d, p, 0) for p in ps] for d in range(2)]
    k = [[load(d, p, 1) for p in ps] for d in range(2)]
    v = [[load(d, p, 2) for p in ps] for d in range(2)]
    gcl = [[lanes(gcol(d, p, 0)) for p in ps] for d in range(2)]
    betal = [[lanes(gcol(d, p, 2 * HEADS)) for p in ps] for d in range(2)]
    gr = [[grow(d, p, 0) for p in ps] for d in range(2)]
    betar = [[grow(d, p, 2 * HEADS) for p in ps] for d in range(2)]
    glast = [[gr[0][p][:, CHUNK - 1:CHUNK] for p in ps], [gr[1][p][:, 0:1] for p in ps]]

    gram = [[_dot_nt(jnp.concatenate([q[d][p], k[d][p]], axis=0), k[d][p]) for p in ps] for d in range(2)]
    dec = [jnp.exp(jnp.where(low, gcl[0][p] - gr[0][p], jnp.where(upp, gcl[1][p] - gr[1][p], 0.0))) for p in ps]
    lc = [jnp.where(low, gram[0][p][CHUNK:] * betar[0][p], jnp.where(upp, gram[1][p][CHUNK:] * betar[1][p], 0.0))
          * dec[p] for p in ps]
    qk = [[jnp.where(upp, 0.0, gram[0][p][:CHUNK] * dec[p]).astype(BF16) for p in ps],
          [jnp.where(low, 0.0, gram[1][p][:CHUNK] * dec[p]).astype(BF16) for p in ps]]

    dg = [jnp.where(same_blk, lc[p], 0.0) for p in ps]
    ob = [lc[p] - dg[p] for p in ps]
    d1h = [halves(dg[p], False) for p in ps]
    d2 = [_dot(as_lhs(d1h[p]), as_rhs(d1h[p])) for p in ps]
    p0s = [-dg[p] for p in ps]
    d2h = [halves(d2[p], False) for p in ps]
    p0h = [halves(p0s[p], True) for p in ps]
    o2 = [_dot(as_lhs(d2h[p]), as_rhs(d2h[p], p0h[p])) for p in ps]
    p1s = [p0s[p] + o2[p][:, CHUNK:] for p in ps]
    d4h = [halves(o2[p][:, :CHUNK], False) for p in ps]
    p1h = [halves(p1s[p], True) for p in ps]
    o3 = [_dot(as_lhs(d4h[p]), as_rhs(d4h[p], p1h[p])) for p in ps]
    p2s = [p1s[p] + o3[p][:, CHUNK:] for p in ps]
    d8h = [halves(o3[p][:, :CHUNK], False) for p in ps]
    p2h = [halves(p2s[p], True) for p in ps]
    p3s = [p2s[p] + _dot(as_lhs(d8h[p]), as_rhs(p2h[p])) for p in ps]
    p3h = [halves(p3s[p], True) for p in ps]
    obh = [halves(ob[p], False) for p in ps]
    n1h = [halves(_dot(as_lhs(p3h[p]), as_rhs(obh[p])), False) for p in ps]
    o6 = [_dot(as_lhs(n1h[p]), as_rhs(n1h[p], p3h[p])) for p in ps]
    r0s = [p3s[p] - o6[p][:, CHUNK:] for p in ps]
    n2h = [halves(o6[p][:, :CHUNK], False) for p in ps]
    r0h = [halves(r0s[p], True) for p in ps]
    o7 = [_dot(as_lhs(n2h[p]), as_rhs(n2h[p], r0h[p])) for p in ps]
    r1s = [r0s[p] + o7[p][:, CHUNK:] for p in ps]
    n4h = [halves(o7[p][:, :CHUNK], False) for p in ps]
    r1h = [halves(r1s[p], True) for p in ps]
    tinv = [halves(r1s[p] + _dot(as_lhs(n4h[p]), as_rhs(r1h[p])), True) for p in ps]

    for d in range(2):
        lat_chunk = (s - n_ctx) if d == 0 else (n_ctx + n_lat - 1 - s)
        off = pl.multiple_of(jnp.clip(lat_chunk, 0, n_lat - 1) * CHUNK, CHUNK)
        sidx = [(p // HEADS * 2 + d) * HEADS + p % HEADS for p in ps]
        egc = [jnp.exp(gcl[d][p]) for p in ps]
        kf = [k[d][p].astype(F32) for p in ps]
        rhs = [jnp.concatenate([v[d][p], (kf[p] * egc[p]).astype(BF16)], axis=1) for p in ps]
        uw = [_dot(tinv[p][d], rhs[p]) for p in ps]
        qd = [q[d][p].astype(F32) * egc[p] for p in ps]
        kd = [(kf[p] * jnp.exp(glast[d][p] - gcl[d][p])).astype(BF16) for p in ps]
        st = [s_ref[sidx[p]] for p in ps]
        a1 = [_dot(jnp.concatenate([uw[p][:, HD:] * betal[d][p], qd[p]], axis=0).astype(BF16), st[p].astype(BF16))
              for p in ps]
        vnew = [(uw[p][:, :HD] * betal[d][p] - a1[p][:CHUNK]).astype(BF16) for p in ps]
        o = [a1[p][CHUNK:] + _dot(qk[d][p], vnew[p]) for p in ps]
        for p in ps:
            s_ref[sidx[p]] = st[p] * jnp.exp(glast[d][p]) + _dot_tn(kd[p], vnew[p])
        for p in ps:
            bb, h = divmod(p, HEADS)
            cols = slice(h * HD, (h + 1) * HD)
            tot = jnp.where(second, oacc_ref[bb, pl.ds(off, CHUNK), cols], 0.0) + o[p]
            oacc_ref[bb, pl.ds(off, CHUNK), cols] = tot
            ms = jnp.mean(tot * tot, axis=-1, keepdims=True)
            zz = z_refs[d][bb, :, cols].astype(F32)
            y_ref[bb, pl.ds(off, CHUNK), cols] = (tot * lax.rsqrt(ms + EPS) * onorm * _silu(zz)).astype(BF16)


def _delta(qkv, z, gb, gbt, onorm, L):
    B, LC, _ = qkv.shape
    n_all = LC // CHUNK
    n_lat = L // CHUNK
    n_ctx = n_all - n_lat

    def cf(s):
        return s

    def cb(s):
        return jnp.where(s < n_ctx, n_ctx - 1 - s, n_all + n_ctx - 1 - s)

    in_specs = [
        pl.BlockSpec((NB_DELTA, CHUNK, N_QKV), lambda b, s: (b, cf(s), 0)),
        pl.BlockSpec((NB_DELTA, CHUNK, N_QKV), lambda b, s: (b, cb(s), 0)),
        pl.BlockSpec((NB_DELTA, CHUNK, D_B), lambda b, s: (b, cf(s), 0)),
        pl.BlockSpec((NB_DELTA, CHUNK, D_B), lambda b, s: (b, cb(s), 0)),
        pl.BlockSpec((NB_DELTA, CHUNK, N_AB), lambda b, s: (b, cf(s), 0)),
        pl.BlockSpec((NB_DELTA, CHUNK, N_AB), lambda b, s: (b, cb(s), 0)),
        pl.BlockSpec((NB_DELTA, N_AB, CHUNK), lambda b, s: (b, 0, cf(s))),
        pl.BlockSpec((NB_DELTA, N_AB, CHUNK), lambda b, s: (b, 0, cb(s))),
        pl.BlockSpec((1, HD), lambda b, s: (0, 0)),
    ]
    return pl.pallas_call(
        functools.partial(_delta_kernel, n_ctx=n_ctx, n_lat=n_lat),
        out_shape=jax.ShapeDtypeStruct((B, L, D_B), BF16),
        grid=(B // NB_DELTA, n_all),
        in_specs=in_specs,
        out_specs=pl.BlockSpec((NB_DELTA, L, D_B), lambda b, s: (b, 0, 0)),
        scratch_shapes=[pltpu.VMEM((NB_DELTA * 2 * HEADS, HD, HD), F32), pltpu.VMEM((NB_DELTA, L, D_B), F32)],
        compiler_params=pltpu.CompilerParams(dimension_semantics=("arbitrary", "arbitrary"),
                                             vmem_limit_bytes=VMEM_LIMIT),
    )(qkv, qkv, z, z, gb, gb, gbt, gbt, onorm)


def _out_kernel(x_ref, ya_ref, yb_ref, mod_ref, wo_ref, g2_ref, wrt_ref, brt_ref,
                x2_ref, h_ref, route_ref, cnt_ref, base_ref, wbf_ref):
    i = pl.program_id(0)

    @pl.when(i == 0)
    def _():
        base_ref[...] = jnp.zeros_like(base_ref)
        wbf_ref[...] = wo_ref[...].astype(BF16)

    mod = mod_ref[0]
    gt1 = mod[:, 2 * D:3 * D]
    sh2 = mod[:, 3 * D:4 * D]
    sc2 = mod[:, 4 * D:5 * D]
    mix = _dot(ya_ref[...], wbf_ref[0:D_A, :]) + _dot(yb_ref[...], wbf_ref[D_A:, :])
    x2 = x_ref[...] + gt1 * mix
    x2_ref[...] = x2
    ms = jnp.mean(x2 * x2, axis=-1, keepdims=True)
    hv = x2 * lax.rsqrt(ms + EPS) * (g2_ref[...] * (1.0 + sc2)) + sh2
    hb = hv.astype(BF16)
    h_ref[:, 0:D_PACK] = _pack_bf16_pairs(hv)

    lt = _dot_nt(wrt_ref[...], hb) + brt_ref[...]
    gl = [lt[r:r + 1, :] for r in range(N_GROUPS)]
    gmax = jnp.maximum(jnp.maximum(gl[0], gl[1]), jnp.maximum(gl[2], gl[3]))
    gsel = jnp.where(gl[0] == gmax, 0, jnp.where(gl[1] == gmax, 1, jnp.where(gl[2] == gmax, 2, 3)))
    p_g = 1.0 / (jnp.exp(gl[0] - gmax) + jnp.exp(gl[1] - gmax) + jnp.exp(gl[2] - gmax) + jnp.exp(gl[3] - gmax))
    el = []
    for e in range(EPG):
        r = [lt[N_GROUPS + g * EPG + e:N_GROUPS + g * EPG + e + 1, :] for g in range(N_GROUPS)]
        el.append(jnp.where(gsel == 0, r[0], jnp.where(gsel == 1, r[1], jnp.where(gsel == 2, r[2], r[3]))))
    m1 = jnp.maximum(jnp.maximum(el[0], el[1]), jnp.maximum(el[2], el[3]))
    i1 = jnp.where(el[0] == m1, 0, jnp.where(el[1] == m1, 1, jnp.where(el[2] == m1, 2, 3)))
    neg = jnp.float32(-jnp.inf)
    el2 = [jnp.where(i1 == e, neg, el[e]) for e in range(EPG)]
    m2 = jnp.maximum(jnp.maximum(el2[0], el2[1]), jnp.maximum(el2[2], el2[3]))
    i2 = jnp.where(jnp.logical_and(el2[0] == m2, i1 != 0), 0,
                   jnp.where(jnp.logical_and(el2[1] == m2, i1 != 1), 1,
                             jnp.where(jnp.logical_and(el2[2] == m2, i1 != 2), 2, 3)))
    t = jnp.exp(m2 - m1)
    w1 = p_g / (1.0 + t)
    w2 = p_g * t / (1.0 + t)
    first_low = i1 < i2
    ea = jnp.where(first_low, i1, i2)
    eb = jnp.where(first_low, i2, i1)
    w_a = jnp.where(first_low, w1, w2)
    w_b = jnp.where(first_low, w2, w1)
    pair = jnp.where(ea == 0, eb - 1, jnp.where(ea == 1, eb + 1, 5))
    bucket = gsel * N_PAIRS + pair

    tm = bucket.shape[1]
    rows = lax.broadcasted_iota(jnp.int32, (32, tm), 0)
    onehot = jnp.where(rows == bucket, 1.0, 0.0).astype(F32)
    r2 = lax.broadcasted_iota(jnp.int32, (tm, tm), 0)
    c2 = lax.broadcasted_iota(jnp.int32, (tm, tm), 1)
    tri = jnp.where(r2 <= c2, 1.0, 0.0).astype(BF16)
    prefix = _dot(onehot.astype(BF16), tri)
    base = base_ref[:, 0:1]
    rank = jnp.sum(onehot * (prefix - 1.0 + base), axis=0, keepdims=True)
    newbase = base + prefix[:, tm - 1:tm]
    base_ref[...] = jnp.broadcast_to(newbase, base_ref.shape)
    cnt_ref[...] = jnp.broadcast_to(newbase, cnt_ref.shape)
    route = jnp.concatenate([bucket.astype(F32), rank, w_a, w_b, jnp.zeros((4, tm), F32)], axis=0)
    route_ref[0] = route
    route_t = jnp.transpose(jnp.concatenate([route, jnp.zeros((120, tm), F32)], axis=0))
    h_ref[:, D_PACK:D_ROW] = lax.bitcast_convert_type(route_t, jnp.uint32)


def _out_proj(x2d, ya, yb, mod3, w_o, g2, wrt, brt, L):
    T = x2d.shape[0]
    nb = T // TM_OUT
    per_b = L // TM_OUT
    out_shape = [
        jax.ShapeDtypeStruct((T, D), F32),
        jax.ShapeDtypeStruct((T, D_ROW), jnp.uint32),
        jax.ShapeDtypeStruct((nb, 8, TM_OUT), F32),
        jax.ShapeDtypeStruct((32, 128), F32),
    ]
    return pl.pallas_call(
        _out_kernel,
        out_shape=out_shape,
        grid=(nb,),
        in_specs=[
            pl.BlockSpec((TM_OUT, D), lambda i: (i, 0)),
            pl.BlockSpec((TM_OUT, D_A), lambda i: (i, 0)),
            pl.BlockSpec((TM_OUT, D_B), lambda i: (i, 0)),
            pl.BlockSpec((1, 1, 6 * D), lambda i: (i // per_b, 0, 0)),
            pl.BlockSpec((D, D), lambda i: (0, 0), pipeline_mode=pl.Buffered(1)),
            pl.BlockSpec((1, D), lambda i: (0, 0)),
            pl.BlockSpec((32, D), lambda i: (0, 0)),
            pl.BlockSpec((32, 1), lambda i: (0, 0)),
        ],
        out_specs=[
            pl.BlockSpec((TM_OUT, D), lambda i: (i, 0)),
            pl.BlockSpec((TM_OUT, D_ROW), lambda i: (i, 0)),
            pl.BlockSpec((1, 8, TM_OUT), lambda i: (i, 0, 0)),
            pl.BlockSpec((32, 128), lambda i: (0, 0)),
        ],
        scratch_shapes=[pltpu.VMEM((32, 128), F32), pltpu.VMEM((D, D), BF16)],
        compiler_params=pltpu.CompilerParams(dimension_semantics=("arbitrary",),
                                             vmem_limit_bytes=VMEM_LIMIT),
    )(x2d, ya, yb, mod3, w_o, g2, wrt, brt)


def _sc_row_gather(table, idx):
    n_rows = idx.shape[0]
    width = table.shape[1]
    n_workers = SC_CORES * SC_SUBCORES
    per_worker = n_rows // n_workers
    assert per_worker * n_workers == n_rows and per_worker % SC_WINDOW == 0
    mesh = plsc.VectorSubcoreMesh(core_axis_name="c", subcore_axis_name="s")

    @functools.partial(
        pl.kernel, mesh=mesh,
        out_type=jax.ShapeDtypeStruct((n_rows, width), table.dtype),
        scratch_types=[pltpu.VMEM((SC_WINDOW,), jnp.int32),
                       pltpu.VMEM((SC_WINDOW, width), table.dtype),
                       pltpu.SemaphoreType.DMA],
    )
    def gather(table_hbm, idx_hbm, out_hbm, idx_v, rows_v, sem):
        wid = lax.axis_index("s") * SC_CORES + lax.axis_index("c")
        base = wid * per_worker

        @pl.loop(0, per_worker // SC_WINDOW)
        def _(j):
            off = pl.multiple_of(base + j * SC_WINDOW, 8)
            pltpu.sync_copy(idx_hbm.at[pl.ds(off, SC_WINDOW)], idx_v)
            pltpu.async_copy(table_hbm.at[idx_v], rows_v, sem).wait()
            pltpu.sync_copy(rows_v, out_hbm.at[pl.ds(off, SC_WINDOW)])

    return gather(table, idx)


def _sc_row_scatter(rows, pos, n_out):
    n_rows, width = rows.shape
    n_workers = SC_CORES * SC_SUBCORES
    per_worker = n_rows // n_workers
    n_chunks = per_worker // SC_WINDOW
    assert per_worker * n_workers == n_rows and n_chunks * SC_WINDOW == per_worker
    mesh = plsc.VectorSubcoreMesh(core_axis_name="c", subcore_axis_name="s")

    @functools.partial(
        pl.kernel, mesh=mesh,
        out_type=jax.ShapeDtypeStruct((n_out, width), rows.dtype),
        scratch_types=[pltpu.VMEM((n_chunks, SC_WINDOW), jnp.int32),
                       pltpu.VMEM((SC_WINDOW, width), rows.dtype),
                       pltpu.SemaphoreType.DMA],
    )
    def scatter(rows_hbm, pos_hbm, out_hbm, idx_v, rows_v, sem):
        wid = lax.axis_index("s") * SC_CORES + lax.axis_index("c")
        base = wid * per_worker
        pltpu.sync_copy(pos_hbm.at[wid], idx_v)

        @pl.loop(0, n_chunks)
        def _(j):
            off = pl.multiple_of(base + j * SC_WINDOW, 8)
            pltpu.sync_copy(rows_hbm.at[pl.ds(off, SC_WINDOW)], rows_v)
            pltpu.async_copy(rows_v, out_hbm.at[idx_v.at[j]], sem).wait()

    return scatter(rows, pos.reshape(n_workers, n_chunks, SC_WINDOW))


def _moe_kernel(ea_ref, eb_ref, nv_ref, chg_ref, hs_ref, wga_ref, wgb_ref, wda_ref, wdb_ref, o_ref, wg_s, wd_s):
    i = pl.program_id(0)

    @pl.when(chg_ref[i] == 1)
    def _():
        wg_s[0] = wga_ref[0].astype(BF16)
        wg_s[1] = wgb_ref[0].astype(BF16)
        wd_s[0] = wda_ref[0].astype(BF16)
        wd_s[1] = wdb_ref[0].astype(BF16)

    @pl.when(nv_ref[i] > 0)
    def _():
        live = lax.broadcasted_iota(jnp.int32, (BM_MOE, 1), 0) < nv_ref[i]
        xb = _unpack_bf16_pairs(jnp.where(live, hs_ref[:, 0:D_PACK], jnp.uint32(0)))
        w_ab = lax.bitcast_convert_type(hs_ref[:, D_PACK:D_ROW], F32)
        w_a = jnp.where(live, w_ab[:, 2:3], 0.0)
        w_b = jnp.where(live, w_ab[:, 3:4], 0.0)
        ga = _dot(xb, wg_s[0])
        act_a = (_silu(ga[:, :D_EXPERT]) * ga[:, D_EXPERT:] * w_a).astype(BF16)
        gb = _dot(xb, wg_s[1])
        act_b = (_silu(gb[:, :D_EXPERT]) * gb[:, D_EXPERT:] * w_b).astype(BF16)
        o_ref[...] = _pack_bf16_pairs(_dot(act_a, wd_s[0]) + _dot(act_b, wd_s[1]))

    @pl.when(nv_ref[i] <= 0)
    def _():
        o_ref[...] = jnp.zeros_like(o_ref)


def _moe(ea, eb, nvalid, chg, hs, w_gu, w_dn):
    nblk = ea.shape[0]
    S = nblk * BM_MOE
    grid_spec = pltpu.PrefetchScalarGridSpec(
        num_scalar_prefetch=4,
        grid=(nblk,),
        in_specs=[
            pl.BlockSpec((BM_MOE, D_ROW), lambda i, ea, eb, nv, chg: (i, 0)),
            pl.BlockSpec((1, D, 2 * D_EXPERT), lambda i, ea, eb, nv, chg: (ea[i], 0, 0)),
            pl.BlockSpec((1, D, 2 * D_EXPERT), lambda i, ea, eb, nv, chg: (eb[i], 0, 0)),
            pl.BlockSpec((1, D_EXPERT, D), lambda i, ea, eb, nv, chg: (ea[i], 0, 0)),
            pl.BlockSpec((1, D_EXPERT, D), lambda i, ea, eb, nv, chg: (eb[i], 0, 0)),
        ],
        out_specs=pl.BlockSpec((BM_MOE, D_PACK), lambda i, ea, eb, nv, chg: (i, 0)),
        scratch_shapes=[pltpu.VMEM((2, D, 2 * D_EXPERT), BF16), pltpu.VMEM((2, D_EXPERT, D), BF16)],
    )
    return pl.pallas_call(
        _moe_kernel,
        out_shape=jax.ShapeDtypeStruct((S, D_PACK), jnp.uint32),
        grid_spec=grid_spec,
        compiler_params=pltpu.CompilerParams(dimension_semantics=("arbitrary",),
                                             vmem_limit_bytes=VMEM_LIMIT),
    )(ea, eb, nvalid, chg, hs, w_gu, w_gu, w_dn, w_dn)


def _final_kernel(x2_ref, m_ref, mod_ref, fg_ref, o_ref):
    gt2 = mod_ref[0][:, 5 * D:6 * D]
    y = x2_ref[...] + gt2 * _unpack_bf16_pairs(m_ref[...]).astype(F32)
    ms = jnp.mean(y * y, axis=-1, keepdims=True)
    o_ref[...] = y * lax.rsqrt(ms + EPS) * fg_ref[...]


def _final(x2, m, mod3, fg, L):
    T = x2.shape[0]
    per_b = L // TM_FIN
    return pl.pallas_call(
        _final_kernel,
        out_shape=jax.ShapeDtypeStruct((T, D), F32),
        grid=(T // TM_FIN,),
        in_specs=[
            pl.BlockSpec((TM_FIN, D), lambda i: (i, 0)),
            pl.BlockSpec((TM_FIN, D_PACK), lambda i: (i, 0)),
            pl.BlockSpec((1, 1, 6 * D), lambda i: (i // per_b, 0, 0)),
            pl.BlockSpec((1, D), lambda i: (0, 0)),
        ],
        out_specs=pl.BlockSpec((TM_FIN, D), lambda i: (i, 0)),
        compiler_params=pltpu.CompilerParams(dimension_semantics=("arbitrary",),
                                             vmem_limit_bytes=VMEM_LIMIT),
    )(x2, m, mod3, fg)


def kernel(x, c, ctx, c_ctx, w_ada, b_ada, norm1_g, w_in, ln_a_g, ln_a_b, w_spatial, b_spatial, conv_qkv, a_log,
           dt_bias, onorm_g, w_out, norm2_g, w_group, b_group, w_router, b_router, w_gate_up, w_down, final_g):
    B, L, _ = x.shape
    T = B * L
    assert w_ada.shape[0] == 1 and ctx.shape[1] == TM_IN and L % TM_OUT == 0

    cond = jnp.concatenate([c, c_ctx[None, :], jnp.zeros((7, D), F32)], axis=0)
    mod = _modulation(cond, w_ada[0], b_ada[0][None, :])
    mod_lat = mod[:B].reshape(B, 1, 6 * D)
    mod_ctx = mod[B:B + 1]

    w_ab = jnp.pad(w_in[0][:, N_MAIN:], ((0, 0), (0, 128 - N_AB))).astype(BF16)
    w_abt = w_in[0][:, N_MAIN:].T.astype(BF16)
    alog = a_log[0].reshape(1, 2 * HEADS)
    dtb = dt_bias[0].reshape(1, 2 * HEADS)
    alog_row = jnp.pad(alog, ((0, 0), (0, 128 - 2 * HEADS)))
    dtb_row = jnp.pad(dtb, ((0, 0), (0, 128 - 2 * HEADS)))
    alog_col = jnp.pad(alog, ((0, 0), (0, N_AB - 2 * HEADS))).T
    dtb_col = jnp.pad(dtb, ((0, 0), (0, N_AB - 2 * HEADS))).T

    ya, qkv, z, gb, gbt = _in_proj(
        x, ctx, mod_lat, mod_ctx, norm1_g, w_in[0], w_ab, w_abt, ln_a_g, ln_a_b,
        w_spatial[0].astype(BF16), b_spatial[0].T, conv_qkv[0], alog_row, dtb_row, alog_col, dtb_col)

    yb = _delta(qkv, z, gb, gbt, onorm_g, L)

    wrt = jnp.concatenate([w_group[0].T, w_router[0].T, jnp.zeros((32 - N_GROUPS - N_EXPERTS, D), F32)], axis=0)
    brt = jnp.concatenate([b_group[0], b_router[0], jnp.zeros((32 - N_GROUPS - N_EXPERTS,), F32)])[:, None]
    x2, h, route, cnt = _out_proj(x.reshape(T, D), ya.reshape(T, D_A), yb.reshape(T, D_B), mod_lat,
                                  w_out[0], norm2_g, wrt.astype(BF16), brt, L)

    bucket = route[:, 0, :].reshape(T).astype(jnp.int32)
    rank = route[:, 1, :].reshape(T).astype(jnp.int32)
    counts = cnt[:N_BUCKETS, 0].astype(jnp.int32)
    nblk_b = (counts + BM_MOE - 1) // BM_MOE
    blk_end = jnp.cumsum(nblk_b)
    blk_start = blk_end - nblk_b
    pos = blk_start[bucket] * BM_MOE + rank
    n_blocks = T // BM_MOE + N_BUCKETS
    S = n_blocks * BM_MOE
    blk = jnp.arange(n_blocks, dtype=jnp.int32)
    used = blk < blk_end[-1]
    bkt = jnp.sum((jnp.minimum(blk, blk_end[-1] - 1)[:, None] >= blk_end[None, :]).astype(jnp.int32), axis=1)
    nvalid = jnp.where(used, jnp.clip(counts[bkt] - (blk - blk_start[bkt]) * BM_MOE, 0, BM_MOE), 0).astype(jnp.int32)
    pa = jnp.asarray(PAIR_A, jnp.int32)
    pb = jnp.asarray(PAIR_B, jnp.int32)
    ea = (bkt // N_PAIRS) * EPG + pa[bkt % N_PAIRS]
    eb = (bkt // N_PAIRS) * EPG + pb[bkt % N_PAIRS]
    chg = jnp.concatenate([jnp.ones((1,), jnp.int32), (bkt[1:] != bkt[:-1]).astype(jnp.int32)])

    hs = _sc_row_scatter(h, pos, S)
    ms = _moe(ea, eb, nvalid, chg, hs, w_gate_up[0], w_down[0])
    out = _final(x2, _sc_row_gather(ms, pos), mod_lat, final_g[None, :], L)
    return out.reshape(B, L, D)
```

```python
import functools

import jax
import jax.numpy as jnp
import numpy as np
from jax import lax
from jax.experimental import pallas as pl
from jax.experimental.pallas import tpu as pltpu
from jax.experimental.pallas import tpu_sc as plsc

F32 = jnp.float32
BF16 = jnp.bfloat16
EPS = 1e-6

D = 1024
D_A = 512
D_B = 512
HEADS = 4
HD = 128
CHUNK = 128
CONV_W = 5
N_QKV = 3 * D_B
N_MAIN = 2 * D_A + 4 * D_B
N_AB = 16
N_GROUPS = 4
EPG = 4
N_EXPERTS = 16
D_EXPERT = 512
N_PAIRS = 6
N_BUCKETS = N_GROUPS * N_PAIRS
D_PACK = D // 2
D_ROW = D_PACK + 128
PAIR_A = (0, 0, 0, 1, 1, 3)
PAIR_B = (1, 2, 3, 3, 2, 2)

TM_IN = 256
HALO = 8
NB_IN = 2
NB_DELTA = 2
TM_OUT = 512
OUT_SPLIT = 2
BM_MOE = 256
TM_FIN = 512
VMEM_LIMIT = 56 * 1024 * 1024
SC_CORES = 2
SC_SUBCORES = 16
SC_WINDOW = 32

HI = lax.Precision.HIGHEST

_CHUNK_TRIL = np.kron(np.eye(TM_IN // CHUNK, dtype=np.float32), np.tril(np.ones((CHUNK, CHUNK), np.float32)))


def _dot(a, b, precision=None):
    return jnp.dot(a, b, preferred_element_type=F32, precision=precision)


def _dot_nt(a, b):
    return lax.dot_general(a, b, (((1,), (1,)), ((), ())), preferred_element_type=F32)


def _dot_tn(a, b):
    return lax.dot_general(a, b, (((0,), (0,)), ((), ())), preferred_element_type=F32)


def _sigmoid(x):
    return 1.0 / (1.0 + jnp.exp(-x))


def _silu(x):
    return x * _sigmoid(x)


def _softplus(x):
    return jnp.maximum(x, 0.0) + jnp.log(1.0 + jnp.exp(-jnp.abs(x)))


def _pack_bf16_pairs(x):
    bits = lax.bitcast_convert_type(x.astype(BF16).astype(F32), jnp.uint32)
    return (bits[:, D_PACK:] & jnp.uint32(0xFFFF0000)) | (bits[:, :D_PACK] >> 16)


def _unpack_bf16_pairs(w):
    bits = w
    lo = lax.bitcast_convert_type(bits << 16, F32)
    hi = lax.bitcast_convert_type(bits & jnp.uint32(0xFFFF0000), F32)
    return jnp.concatenate([lo, hi], axis=1).astype(BF16)


def _gelu_tanh(x):
    return 0.5 * x * (1.0 + jnp.tanh(np.sqrt(2.0 / np.pi).astype(np.float32) * (x + 0.044715 * (x * x * x))))


def _mod_kernel(c_ref, w_ref, b_ref, o_ref):
    c = c_ref[...]
    o_ref[...] = _dot(_silu(c), w_ref[...], precision=HI) + b_ref[...]


def _modulation(cond, w_ada, b_ada):
    rows = cond.shape[0]
    tn = 1536
    return pl.pallas_call(
        _mod_kernel,
        out_shape=jax.ShapeDtypeStruct((rows, 6 * D), F32),
        grid=(6 * D // tn,),
        in_specs=[pl.BlockSpec((rows, D), lambda i: (0, 0)),
                  pl.BlockSpec((D, tn), lambda i: (0, i)),
                  pl.BlockSpec((1, tn), lambda i: (0, i))],
        out_specs=pl.BlockSpec((rows, tn), lambda i: (0, i)),
        compiler_params=pltpu.CompilerParams(dimension_semantics=("arbitrary",),
                                             vmem_limit_bytes=VMEM_LIMIT),
    )(cond, w_ada, b_ada)


def _in_kernel(x_ref, xp_ref, xn_ref, ctx_ref, mod_ref, cmod_ref, g1_ref, w_ref, wab_ref, wabt_ref,
               lng_ref, lnb_ref, ws_ref, bst_ref, conv_ref, alog_ref, dtb_ref, alogt_ref, dtbt_ref, tril_ref, triu_ref,
               ya_ref, qkv_ref, z_ref, gb_ref, gbt_ref, wbf_ref):
    j = pl.program_id(1)

    @pl.when(jnp.logical_and(pl.program_id(0) == 0, j == 0))
    def _():
        wbf_ref[...] = w_ref[...].astype(BF16)

    is_ctx = j == 0
    n_lat_blocks = pl.num_programs(1) - 1

    def one_batch_element(bb):
        mod = mod_ref[bb]
        cm = cmod_ref[...]
        sh = jnp.where(is_ctx, cm[:, 0:D], mod[:, 0:D])
        sc = jnp.where(is_ctx, cm[:, D:2 * D], mod[:, D:2 * D])
        scale = g1_ref[...] * (1.0 + sc)

        xmain = jnp.where(is_ctx, ctx_ref[bb], x_ref[bb])
        xv = jnp.concatenate([xp_ref[bb], xmain, xn_ref[bb]], axis=0)
        xnorm = xv * lax.rsqrt(jnp.mean(xv * xv, axis=-1, keepdims=True) + EPS) * scale + sh
        xe = xnorm.astype(BF16)
        xb = xnorm[HALO:HALO + TM_IN].astype(BF16)

        rid = lax.broadcasted_iota(jnp.int32, (TM_IN + 2 * HALO, 1), 0)
        prev_ok = j >= 2
        next_ok = jnp.logical_and(j >= 1, j < n_lat_blocks)
        valid = jnp.logical_or(jnp.logical_and(rid >= HALO, rid < HALO + TM_IN),
                               jnp.logical_or(jnp.logical_and(rid < HALO, prev_ok),
                                              jnp.logical_and(rid >= HALO + TM_IN, next_ok)))
        pad = (CONV_W - 1) // 2
        c_qkv = 2 * D_A

        def proj(c0, width):
            return _dot(xe, wbf_ref[:, c0:c0 + width])

        def conv_act(pq, c0):
            groups = (TM_IN + 2 * HALO) // 8
            x3 = jnp.where(valid, pq, 0.0).reshape(groups, 8, D_B)
            sub = lax.broadcasted_iota(jnp.int32, (1, 8, 1), 1)
            lo, hi = HALO // 8, HALO // 8 + TM_IN // 8
            acc = conv_ref[pad:pad + 1, c0:c0 + D_B] * x3[lo:hi]
            for t in range(CONV_W):
                s = t - pad
                if s == 0:
                    continue
                r = pltpu.roll(x3, (-s) % 8, axis=1)
                if s > 0:
                    sh = jnp.where(sub < 8 - s, r[lo:hi], r[lo + 1:hi + 1])
                else:
                    sh = jnp.where(sub >= -s, r[lo:hi], r[lo - 1:hi - 1])
                acc = acc + conv_ref[t:t + 1, c0:c0 + D_B] * sh
            return _silu(acc.reshape(TM_IN, D_B))

        def store_unit_heads(act, c0, gain):
            for h in range(HEADS):
                t = act[:, h * HD:(h + 1) * HD]
                nrm = lax.rsqrt(jnp.sum(t * t, axis=-1, keepdims=True) + EPS) * gain
                qkv_ref[bb, :, c0 + h * HD:c0 + (h + 1) * HD] = (t * nrm).astype(BF16)

        pq_q = proj(c_qkv, D_B)
        pq_k = proj(c_qkv + D_B, D_B)
        store_unit_heads(conv_act(pq_q, 0), 0, HD ** -0.5)
        pq_v = proj(c_qkv + 2 * D_B, D_B)
        store_unit_heads(conv_act(pq_k, D_B), D_B, 1.0)
        pa_u = proj(0, D_A)[HALO:HALO + TM_IN]
        qkv_ref[bb, :, 2 * D_B:] = conv_act(pq_v, 2 * D_B).astype(BF16)
        pa_v = proj(D_A, D_A)[HALO:HALO + TM_IN]
        u = _gelu_tanh(pa_u)
        pz = proj(c_qkv + N_QKV, D_B)
        v = _gelu_tanh(pa_v)
        mu = jnp.mean(v, axis=-1, keepdims=True)
        vc = v - mu
        var = jnp.mean(vc * vc, axis=-1, keepdims=True)
        vn = (vc * lax.rsqrt(var + EPS) * lng_ref[...] + lnb_ref[...]).astype(BF16)
        z_ref[bb] = pz[HALO:HALO + TM_IN].astype(BF16)

        bst = bst_ref[...]
        for n in range(TM_IN // CHUNK):
            rows = slice(n * CHUNK, (n + 1) * CHUNK)
            for h in range(HEADS):
                cols = slice(h * HD, (h + 1) * HD)
                s = _dot(ws_ref[h], vn[rows, cols]) + bst[:, h:h + 1]
                ya_ref[bb, rows, cols] = (u[rows, cols] * s).astype(BF16)

        tri_l = tril_ref[...]
        tri_u = triu_ref[...]

        def split3(g):
            hi = g.astype(BF16)
            r1 = g - hi.astype(F32)
            mid = r1.astype(BF16)
            return hi, mid, (r1 - mid.astype(F32)).astype(BF16)

        ab = _dot(xb, wab_ref[...])
        g3 = split3(-jnp.exp(alog_ref[...]) * _softplus(ab + dtb_ref[...]))
        lane = lax.broadcasted_iota(jnp.int32, ab.shape, 1)
        gb = jnp.where(lane < HEADS, _dot(tri_l, g3[0]) + _dot(tri_l, g3[1]) + _dot(tri_l, g3[2]),
                       jnp.where(lane < 2 * HEADS, _dot(tri_u, g3[0]) + _dot(tri_u, g3[1]) + _dot(tri_u, g3[2]),
                                 _sigmoid(ab)))
        gb_ref[bb] = gb[:, 0:N_AB]

        abt = _dot_nt(wabt_ref[...], xb)
        t3 = split3(-jnp.exp(alogt_ref[...]) * _softplus(abt + dtbt_ref[...]))
        row = lax.broadcasted_iota(jnp.int32, abt.shape, 0)
        gbt_ref[bb] = jnp.where(row < HEADS, _dot(t3[0], tri_u) + _dot(t3[1], tri_u) + _dot(t3[2], tri_u),
                               jnp.where(row < 2 * HEADS, _dot(t3[0], tri_l) + _dot(t3[1], tri_l) + _dot(t3[2], tri_l),
                                         _sigmoid(abt)))

    for bb in range(x_ref.shape[0]):
        one_batch_element(bb)


def _in_proj(x, ctx, mod_lat, mod_ctx, g1, w_main, w_ab, w_abt, lng, lnb, ws, bst, conv, alog, dtb, alogt, dtbt):
    B, L, _ = x.shape
    n_lat = L // TM_IN
    n_steps = n_lat + 1
    LC = L + TM_IN
    hb = TM_IN // HALO

    def full(shape):
        return pl.BlockSpec(shape, lambda b, j: (0,) * len(shape))

    in_specs = [
        pl.BlockSpec((NB_IN, TM_IN, D), lambda b, j: (b, jnp.maximum(j - 1, 0), 0)),
        pl.BlockSpec((NB_IN, HALO, D), lambda b, j: (b, jnp.clip((j - 1) * hb - 1, 0, L // HALO - 1), 0)),
        pl.BlockSpec((NB_IN, HALO, D), lambda b, j: (b, jnp.clip(j * hb, 0, L // HALO - 1), 0)),
        pl.BlockSpec((NB_IN, TM_IN, D), lambda b, j: (b, 0, 0)),
        pl.BlockSpec((NB_IN, 1, 6 * D), lambda b, j: (b, 0, 0)),
        full((1, 6 * D)), full((1, D)),
        pl.BlockSpec((D, N_MAIN), lambda b, j: (0, 0), pipeline_mode=pl.Buffered(1)),
        full((D, 128)), full((N_AB, D)),
        full((1, D_A)), full((1, D_A)), full((HEADS, CHUNK, CHUNK)), full((CHUNK, HEADS)),
        full((CONV_W, N_QKV)), full((1, 128)), full((1, 128)), full((N_AB, 1)), full((N_AB, 1)),
        full((TM_IN, TM_IN)), full((TM_IN, TM_IN)),
    ]
    out_shape = [
        jax.ShapeDtypeStruct((B, L, D_A), BF16),
        jax.ShapeDtypeStruct((B, LC, N_QKV), BF16),
        jax.ShapeDtypeStruct((B, LC, D_B), BF16),
        jax.ShapeDtypeStruct((B, LC, N_AB), F32),
        jax.ShapeDtypeStruct((B, N_AB, LC), F32),
    ]
    out_specs = [
        pl.BlockSpec((NB_IN, TM_IN, D_A), lambda b, j: (b, jnp.maximum(j - 1, 0), 0)),
        pl.BlockSpec((NB_IN, TM_IN, N_QKV), lambda b, j: (b, j, 0)),
        pl.BlockSpec((NB_IN, TM_IN, D_B), lambda b, j: (b, j, 0)),
        pl.BlockSpec((NB_IN, TM_IN, N_AB), lambda b, j: (b, j, 0)),
        pl.BlockSpec((NB_IN, N_AB, TM_IN), lambda b, j: (b, 0, j)),
    ]
    return pl.pallas_call(
        _in_kernel,
        out_shape=out_shape,
        grid=(B // NB_IN, n_steps),
        in_specs=in_specs,
        out_specs=out_specs,
        scratch_shapes=[pltpu.VMEM((D, N_MAIN), BF16)],
        compiler_params=pltpu.CompilerParams(dimension_semantics=("arbitrary", "arbitrary"),
                                             vmem_limit_bytes=VMEM_LIMIT),
    )(x, x, x, ctx, mod_lat, mod_ctx, g1, w_main, w_ab, w_abt, lng, lnb, ws, bst, conv, alog, dtb, alogt, dtbt,
      jnp.asarray(_CHUNK_TRIL, BF16), jnp.asarray(_CHUNK_TRIL.T, BF16))


def _delta_kernel(qf_ref, qb_ref, zf_ref, zb_ref, gf_ref, gbk_ref, gtf_ref, gtb_ref, on_ref,
                  y_ref, s_ref, oacc_ref, *, n_ctx, n_lat):
    s = pl.program_id(1)

    @pl.when(s == 0)
    def _():
        s_ref[...] = jnp.zeros_like(s_ref)
        oacc_ref[...] = jnp.zeros_like(oacc_ref)

    row = lax.broadcasted_iota(jnp.int32, (CHUNK, CHUNK), 0)
    col = lax.broadcasted_iota(jnp.int32, (CHUNK, CHUNK), 1)
    low = row > col
    upp = row < col
    same_blk = (row // 16) == (col // 16)
    eye = jnp.where(row == col, 1.0, 0.0).astype(BF16)
    zero = jnp.zeros((CHUNK, CHUNK), BF16)
    onorm = on_ref[...]
    half = n_ctx + n_lat // 2
    second = s >= half
    g_refs = (gf_ref, gbk_ref)
    gt_refs = (gtf_ref, gtb_ref)
    qkv_refs = (qf_ref, qb_ref)
    z_refs = (zf_ref, zb_ref)
    nb = qf_ref.shape[0]
    ps = range(nb * HEADS)

    def halves(xc, unit):
        xb = xc.astype(BF16)
        fill = eye if unit else zero
        return jnp.where(low, xb, fill), jnp.where(upp, xb, fill)

    def as_lhs(hv):
        return jnp.concatenate(hv, axis=1)

    def as_rhs(*hvs):
        cols_ = [jnp.concatenate(hv, axis=0) for hv in hvs]
        return cols_[0] if len(cols_) == 1 else jnp.concatenate(cols_, axis=1)

    def load(d, p, part):
        bb, h = divmod(p, HEADS)
        return qkv_refs[d][bb, :, part * D_B + h * HD:part * D_B + (h + 1) * HD]

    def gcol(d, p, base):
        bb, h = divmod(p, HEADS)
        c = base + d * HEADS + h
        return g_refs[d][bb, :, c:c + 1]

    def grow(d, p, base):
        bb, h = divmod(p, HEADS)
        r = base + d * HEADS + h
        return gt_refs[d][bb, r:r + 1, :]

    def lanes(col):
        return jnp.broadcast_to(col, (CHUNK, HD))

    q = [[load(d, p, 0) for p in ps] for d in range(2)]
    k = [[load(d, p, 1) for p in ps] for d in range(2)]
    v = [[load(d, p, 2) for p in ps] for d in range(2)]
    gcl = [[lanes(gcol(d, p, 0)) for p in ps] for d in range(2)]
    betal = [[lanes(gcol(d, p, 2 * HEADS)) for p in ps] for d in range(2)]
    gr = [[grow(d, p, 0) for p in ps] for d in range(2)]
    betar = [[grow(d, p, 2 * HEADS) for p in ps] for d in range(2)]
    glast = [[gr[0][p][:, CHUNK - 1:CHUNK] for p in ps], [gr[1][p][:, 0:1] for p in ps]]

    gram = [[_dot_nt(jnp.concatenate([q[d][p], k[d][p]], axis=0), k[d][p]) for p in ps] for d in range(2)]
    dec = [jnp.exp(jnp.where(low, gcl[0][p] - gr[0][p], jnp.where(upp, gcl[1][p] - gr[1][p], 0.0))) for p in ps]
    lc = [jnp.where(low, gram[0][p][CHUNK:] * betar[0][p], jnp.where(upp, gram[1][p][CHUNK:] * betar[1][p], 0.0))
          * dec[p] for p in ps]
    qk = [[jnp.where(upp, 0.0, gram[0][p][:CHUNK] * dec[p]).astype(BF16) for p in ps],
          [jnp.where(low, 0.0, gram[1][p][:CHUNK] * dec[p]).astype(BF16) for p in ps]]

    dg = [jnp.where(same_blk, lc[p], 0.0) for p in ps]
    ob = [lc[p] - dg[p] for p in ps]
    d1h = [halves(dg[p], False) for p in ps]
    d2 = [_dot(as_lhs(d1h[p]), as_rhs(d1h[p])) for p in ps]
    p0s = [-dg[p] for p in ps]
    d2h = [halves(d2[p], False) for p in ps]
    p0h = [halves(p0s[p], True) for p in ps]
    o2 = [_dot(as_lhs(d2h[p]), as_rhs(d2h[p], p0h[p])) for p in ps]
    p1s = [p0s[p] + o2[p][:, CHUNK:] for p in ps]
    d4h = [halves(o2[p][:, :CHUNK], False) for p in ps]
    p1h = [halves(p1s[p], True) for p in ps]
    o3 = [_dot(as_lhs(d4h[p]), as_rhs(d4h[p], p1h[p])) for p in ps]
    p2s = [p1s[p] + o3[p][:, CHUNK:] for p in ps]
    d8h = [halves(o3[p][:, :CHUNK], False) for p in ps]
    p2h = [halves(p2s[p], True) for p in ps]
    p3s = [p2s[p] + _dot(as_lhs(d8h[p]), as_rhs(p2h[p])) for p in ps]
    p3h = [halves(p3s[p], True) for p in ps]
    obh = [halves(ob[p], False) for p in ps]
    n1h = [halves(_dot(as_lhs(p3h[p]), as_rhs(obh[p])), False) for p in ps]
    o6 = [_dot(as_lhs(n1h[p]), as_rhs(n1h[p], p3h[p])) for p in ps]
    r0s = [p3s[p] - o6[p][:, CHUNK:] for p in ps]
    n2h = [halves(o6[p][:, :CHUNK], False) for p in ps]
    r0h = [halves(r0s[p], True) for p in ps]
    o7 = [_dot(as_lhs(n2h[p]), as_rhs(n2h[p], r0h[p])) for p in ps]
    r1s = [r0s[p] + o7[p][:, CHUNK:] for p in ps]
    n4h = [halves(o7[p][:, :CHUNK], False) for p in ps]
    r1h = [halves(r1s[p], True) for p in ps]
    tinv = [halves(r1s[p] + _dot(as_lhs(n4h[p]), as_rhs(r1h[p])), True) for p in ps]

    offs = []
    for d in range(2):
        lat_chunk = (s - n_ctx) if d == 0 else (n_ctx + n_lat - 1 - s)
        off = pl.multiple_of(jnp.clip(lat_chunk, 0, n_lat - 1) * CHUNK, CHUNK)
        sidx = [(p // HEADS * 2 + d) * HEADS + p % HEADS for p in ps]
        egc = [jnp.exp(gcl[d][p]) for p in ps]
        kf = [k[d][p].astype(F32) for p in ps]
        rhs = [jnp.concatenate([v[d][p], (kf[p] * egc[p]).astype(BF16)], axis=1) for p in ps]
        uw = [_dot(tinv[p][d], rhs[p]) for p in ps]
        qd = [q[d][p].astype(F32) * egc[p] for p in ps]
        kd = [(kf[p] * jnp.exp(glast[d][p] - gcl[d][p])).astype(BF16) for p in ps]
        st = [s_ref[sidx[p]] for p in ps]
        a1 = [_dot(jnp.concatenate([uw[p][:, HD:] * betal[d][p], qd[p]], axis=0).astype(BF16), st[p].astype(BF16))
              for p in ps]
        vnew = [(uw[p][:, :HD] * betal[d][p] - a1[p][:CHUNK]).astype(BF16) for p in ps]
        o = [a1[p][CHUNK:] + _dot(qk[d][p], vnew[p]) for p in ps]
        for p in ps:
            s_ref[sidx[p]] = st[p] * jnp.exp(glast[d][p]) + _dot_tn(kd[p], vnew[p])
        for p in ps:
            bb, h = divmod(p, HEADS)
            cols = slice(h * HD, (h + 1) * HD)
            oacc_ref[bb, pl.ds(off, CHUNK), cols] = (
                jnp.where(second, oacc_ref[bb, pl.ds(off, CHUNK), cols], 0.0) + o[p])
        offs.append(off)

    @pl.when(second)
    def _():
        for d in range(2):
            for p in ps:
                bb, h = divmod(p, HEADS)
                cols = slice(h * HD, (h + 1) * HD)
                tot = oacc_ref[bb, pl.ds(offs[d], CHUNK), cols]
                ms = jnp.mean(tot * tot, axis=-1, keepdims=True)
                zz = z_refs[d][bb, :, cols].astype(F32)
                y_ref[bb, pl.ds(offs[d], CHUNK), cols] = (
                    tot * lax.rsqrt(ms + EPS) * onorm * _silu(zz)).astype(BF16)


def _delta(qkv, z, gb, gbt, onorm, L):
    B, LC, _ = qkv.shape
    n_all = LC // CHUNK
    n_lat = L // CHUNK
    n_ctx = n_all - n_lat

    def cf(s):
        return s

    def cb(s):
        return jnp.where(s < n_ctx, n_ctx - 1 - s, n_all + n_ctx - 1 - s)

    in_specs = [
        pl.BlockSpec((NB_DELTA, CHUNK, N_QKV), lambda b, s: (b, cf(s), 0)),
        pl.BlockSpec((NB_DELTA, CHUNK, N_QKV), lambda b, s: (b, cb(s), 0)),
        pl.BlockSpec((NB_DELTA, CHUNK, D_B), lambda b, s: (b, cf(s), 0)),
        pl.BlockSpec((NB_DELTA, CHUNK, D_B), lambda b, s: (b, cb(s), 0)),
        pl.BlockSpec((NB_DELTA, CHUNK, N_AB), lambda b, s: (b, cf(s), 0)),
        pl.BlockSpec((NB_DELTA, CHUNK, N_AB), lambda b, s: (b, cb(s), 0)),
        pl.BlockSpec((NB_DELTA, N_AB, CHUNK), lambda b, s: (b, 0, cf(s))),
        pl.BlockSpec((NB_DELTA, N_AB, CHUNK), lambda b, s: (b, 0, cb(s))),
        pl.BlockSpec((1, HD), lambda b, s: (0, 0)),
    ]
    return pl.pallas_call(
        functools.partial(_delta_kernel, n_ctx=n_ctx, n_lat=n_lat),
        out_shape=jax.ShapeDtypeStruct((B, L, D_B), BF16),
        grid=(B // NB_DELTA, n_all),
        in_specs=in_specs,
        out_specs=pl.BlockSpec((NB_DELTA, L, D_B), lambda b, s: (b, 0, 0)),
        scratch_shapes=[pltpu.VMEM((NB_DELTA * 2 * HEADS, HD, HD), F32), pltpu.VMEM((NB_DELTA, L, D_B), F32)],
        compiler_params=pltpu.CompilerParams(dimension_semantics=("arbitrary", "arbitrary"),
                                             vmem_limit_bytes=VMEM_LIMIT),
    )(qkv, qkv, z, z, gb, gb, gbt, gbt, onorm)


def _out_kernel(x_ref, ya_ref, yb_ref, mod_ref, wo_ref, g2_ref, wrt_ref, brt_ref,
                x2_ref, h_ref, route_ref, cnt_ref, base_ref, wbf_ref):
    i = pl.program_id(0)

    @pl.when(i == 0)
    def _():
        base_ref[...] = jnp.zeros_like(base_ref)
        wbf_ref[...] = wo_ref[...].astype(BF16)

    mod = mod_ref[0]
    gt1 = mod[:, 2 * D:3 * D]
    sh2 = mod[:, 3 * D:4 * D]
    sc2 = mod[:, 4 * D:5 * D]
    scale2 = g2_ref[...] * (1.0 + sc2)
    sub = TM_OUT // OUT_SPLIT
    hbs = []
    for r in range(OUT_SPLIT):
        rows = slice(r * sub, (r + 1) * sub)
        mix = _dot(ya_ref[rows, :], wbf_ref[0:D_A, :]) + _dot(yb_ref[rows, :], wbf_ref[D_A:, :])
        x2 = x_ref[rows, :] + gt1 * mix
        x2_ref[rows, :] = x2
        ms = jnp.mean(x2 * x2, axis=-1, keepdims=True)
        hv = x2 * lax.rsqrt(ms + EPS) * scale2 + sh2
        hbs.append(hv.astype(BF16))
        h_ref[rows, 0:D_PACK] = _pack_bf16_pairs(hv)
    hb = jnp.concatenate(hbs, axis=0)

    lt = _dot_nt(wrt_ref[...], hb) + brt_ref[...]
    gl = [lt[r:r + 1, :] for r in range(N_GROUPS)]
    gmax = jnp.maximum(jnp.maximum(gl[0], gl[1]), jnp.maximum(gl[2], gl[3]))
    gsel = jnp.where(gl[0] == gmax, 0, jnp.where(gl[1] == gmax, 1, jnp.where(gl[2] == gmax, 2, 3)))
    p_g = 1.0 / (jnp.exp(gl[0] - gmax) + jnp.exp(gl[1] - gmax) + jnp.exp(gl[2] - gmax) + jnp.exp(gl[3] - gmax))
    el = []
    for e in range(EPG):
        r = [lt[N_GROUPS + g * EPG + e:N_GROUPS + g * EPG + e + 1, :] for g in range(N_GROUPS)]
        el.append(jnp.where(gsel == 0, r[0], jnp.where(gsel == 1, r[1], jnp.where(gsel == 2, r[2], r[3]))))
    m1 = jnp.maximum(jnp.maximum(el[0], el[1]), jnp.maximum(el[2], el[3]))
    i1 = jnp.where(el[0] == m1, 0, jnp.where(el[1] == m1, 1, jnp.where(el[2] == m1, 2, 3)))
    neg = jnp.float32(-jnp.inf)
    el2 = [jnp.where(i1 == e, neg, el[e]) for e in range(EPG)]
    m2 = jnp.maximum(jnp.maximum(el2[0], el2[1]), jnp.maximum(el2[2], el2[3]))
    i2 = jnp.where(jnp.logical_and(el2[0] == m2, i1 != 0), 0,
                   jnp.where(jnp.logical_and(el2[1] == m2, i1 != 1), 1,
                             jnp.where(jnp.logical_and(el2[2] == m2, i1 != 2), 2, 3)))
    t = jnp.exp(m2 - m1)
    w1 = p_g / (1.0 + t)
    w2 = p_g * t / (1.0 + t)
    first_low = i1 < i2
    ea = jnp.where(first_low, i1, i2)
    eb = jnp.where(first_low, i2, i1)
    w_a = jnp.where(first_low, w1, w2)
    w_b = jnp.where(first_low, w2, w1)
    pair = jnp.where(ea == 0, eb - 1, jnp.where(ea == 1, jnp.where(eb == 3, 3, 4), 5))
    swap = pair == 5
    w_a, w_b = jnp.where(swap, w_b, w_a), jnp.where(swap, w_a, w_b)
    bucket = gsel * N_PAIRS + pair

    tm = bucket.shape[1]
    rows = lax.broadcasted_iota(jnp.int32, (32, tm), 0)
    onehot = jnp.where(rows == bucket, 1.0, 0.0).astype(F32)
    r2 = lax.broadcasted_iota(jnp.int32, (tm, tm), 0)
    c2 = lax.broadcasted_iota(jnp.int32, (tm, tm), 1)
    tri = jnp.where(r2 <= c2, 1.0, 0.0).astype(BF16)
    prefix = _dot(onehot.astype(BF16), tri)
    base = base_ref[:, 0:1]
    rank = jnp.sum(onehot * (prefix - 1.0 + base), axis=0, keepdims=True)
    newbase = base + prefix[:, tm - 1:tm]
    base_ref[...] = jnp.broadcast_to(newbase, base_ref.shape)
    cnt_ref[...] = jnp.broadcast_to(newbase, cnt_ref.shape)
    route = jnp.concatenate([bucket.astype(F32), rank, w_a, w_b, jnp.zeros((4, tm), F32)], axis=0)
    route_ref[0] = route
    route_t = jnp.transpose(jnp.concatenate([route, jnp.zeros((120, tm), F32)], axis=0))
    h_ref[:, D_PACK:D_ROW] = lax.bitcast_convert_type(route_t, jnp.uint32)


def _out_proj(x2d, ya, yb, mod3, w_o, g2, wrt, brt, L):
    T = x2d.shape[0]
    nb = T // TM_OUT
    per_b = L // TM_OUT
    out_shape = [
        jax.ShapeDtypeStruct((T, D), F32),
        jax.ShapeDtypeStruct((T, D_ROW), jnp.uint32),
        jax.ShapeDtypeStruct((nb, 8, TM_OUT), F32),
        jax.ShapeDtypeStruct((32, 128), F32),
    ]
    return pl.pallas_call(
        _out_kernel,
        out_shape=out_shape,
        grid=(nb,),
        in_specs=[
            pl.BlockSpec((TM_OUT, D), lambda i: (i, 0)),
            pl.BlockSpec((TM_OUT, D_A), lambda i: (i, 0)),
            pl.BlockSpec((TM_OUT, D_B), lambda i: (i, 0)),
            pl.BlockSpec((1, 1, 6 * D), lambda i: (i // per_b, 0, 0)),
            pl.BlockSpec((D, D), lambda i: (0, 0), pipeline_mode=pl.Buffered(1)),
            pl.BlockSpec((1, D), lambda i: (0, 0)),
            pl.BlockSpec((32, D), lambda i: (0, 0)),
            pl.BlockSpec((32, 1), lambda i: (0, 0)),
        ],
        out_specs=[
            pl.BlockSpec((TM_OUT, D), lambda i: (i, 0)),
            pl.BlockSpec((TM_OUT, D_ROW), lambda i: (i, 0)),
            pl.BlockSpec((1, 8, TM_OUT), lambda i: (i, 0, 0)),
            pl.BlockSpec((32, 128), lambda i: (0, 0)),
        ],
        scratch_shapes=[pltpu.VMEM((32, 128), F32), pltpu.VMEM((D, D), BF16)],
        compiler_params=pltpu.CompilerParams(dimension_semantics=("arbitrary",),
                                             vmem_limit_bytes=VMEM_LIMIT),
    )(x2d, ya, yb, mod3, w_o, g2, wrt, brt)


def _sc_row_gather(table, idx):
    n_rows = idx.shape[0]
    width = table.shape[1]
    n_workers = SC_CORES * SC_SUBCORES
    per_worker = n_rows // n_workers
    assert per_worker * n_workers == n_rows and per_worker % SC_WINDOW == 0
    mesh = plsc.VectorSubcoreMesh(core_axis_name="c", subcore_axis_name="s")

    @functools.partial(
        pl.kernel, mesh=mesh,
        out_type=jax.ShapeDtypeStruct((n_rows, width), table.dtype),
        scratch_types=[pltpu.VMEM((SC_WINDOW,), jnp.int32),
                       pltpu.VMEM((SC_WINDOW, width), table.dtype),
                       pltpu.SemaphoreType.DMA],
    )
    def gather(table_hbm, idx_hbm, out_hbm, idx_v, rows_v, sem):
        wid = lax.axis_index("s") * SC_CORES + lax.axis_index("c")
        base = wid * per_worker

        @pl.loop(0, per_worker // SC_WINDOW)
        def _(j):
            off = pl.multiple_of(base + j * SC_WINDOW, 8)
            pltpu.sync_copy(idx_hbm.at[pl.ds(off, SC_WINDOW)], idx_v)
            pltpu.async_copy(table_hbm.at[idx_v], rows_v, sem).wait()
            pltpu.sync_copy(rows_v, out_hbm.at[pl.ds(off, SC_WINDOW)])

    return gather(table, idx)


def _sc_row_scatter(rows, pos, n_out):
    n_rows, width = rows.shape
    n_workers = SC_CORES * SC_SUBCORES
    per_worker = n_rows // n_workers
    n_chunks = per_worker // SC_WINDOW
    assert per_worker * n_workers == n_rows and n_chunks * SC_WINDOW == per_worker
    mesh = plsc.VectorSubcoreMesh(core_axis_name="c", subcore_axis_name="s")

    @functools.partial(
        pl.kernel, mesh=mesh,
        out_type=jax.ShapeDtypeStruct((n_out, width), rows.dtype),
        scratch_types=[pltpu.VMEM((n_chunks, SC_WINDOW), jnp.int32),
                       pltpu.VMEM((SC_WINDOW, width), rows.dtype),
                       pltpu.SemaphoreType.DMA],
    )
    def scatter(rows_hbm, pos_hbm, out_hbm, idx_v, rows_v, sem):
        wid = lax.axis_index("s") * SC_CORES + lax.axis_index("c")
        base = wid * per_worker
        pltpu.sync_copy(pos_hbm.at[wid], idx_v)

        @pl.loop(0, n_chunks)
        def _(j):
            off = pl.multiple_of(base + j * SC_WINDOW, 8)
            pltpu.sync_copy(rows_hbm.at[pl.ds(off, SC_WINDOW)], rows_v)
            pltpu.async_copy(rows_v, out_hbm.at[idx_v.at[j]], sem).wait()

    return scatter(rows, pos.reshape(n_workers, n_chunks, SC_WINDOW))


def _moe_kernel(ea_ref, eb_ref, nv_ref, hs_ref, wga_ref, wgb_ref, wda_ref, wdb_ref, o_ref, wg_s, wd_s):
    i = pl.program_id(0)
    prev = jnp.maximum(i - 1, 0)

    @pl.when(jnp.logical_or(i == 0, ea_ref[i] != ea_ref[prev]))
    def _():
        wg_s[0] = wga_ref[0].astype(BF16)
        wd_s[0] = wda_ref[0].astype(BF16)

    @pl.when(jnp.logical_or(i == 0, eb_ref[i] != eb_ref[prev]))
    def _():
        wg_s[1] = wgb_ref[0].astype(BF16)
        wd_s[1] = wdb_ref[0].astype(BF16)

    @pl.when(nv_ref[i] > 0)
    def _():
        live = lax.broadcasted_iota(jnp.int32, (BM_MOE, 1), 0) < nv_ref[i]
        xb = _unpack_bf16_pairs(jnp.where(live, hs_ref[:, 0:D_PACK], jnp.uint32(0)))
        w_ab = lax.bitcast_convert_type(hs_ref[:, D_PACK:D_ROW], F32)
        w_a = jnp.where(live, w_ab[:, 2:3], 0.0)
        w_b = jnp.where(live, w_ab[:, 3:4], 0.0)
        ga = _dot(xb, wg_s[0])
        act_a = (_silu(ga[:, :D_EXPERT]) * ga[:, D_EXPERT:] * w_a).astype(BF16)
        gb = _dot(xb, wg_s[1])
        act_b = (_silu(gb[:, :D_EXPERT]) * gb[:, D_EXPERT:] * w_b).astype(BF16)
        o_ref[...] = _pack_bf16_pairs(_dot(act_a, wd_s[0]) + _dot(act_b, wd_s[1]))

    @pl.when(nv_ref[i] <= 0)
    def _():
        o_ref[...] = jnp.zeros_like(o_ref)


def _moe(ea, eb, nvalid, hs, w_gu, w_dn):
    nblk = ea.shape[0]
    S = nblk * BM_MOE
    grid_spec = pltpu.PrefetchScalarGridSpec(
        num_scalar_prefetch=3,
        grid=(nblk,),
        in_specs=[
            pl.BlockSpec((BM_MOE, D_ROW), lambda i, ea, eb, nv: (i, 0)),
            pl.BlockSpec((1, D, 2 * D_EXPERT), lambda i, ea, eb, nv: (ea[i], 0, 0)),
            pl.BlockSpec((1, D, 2 * D_EXPERT), lambda i, ea, eb, nv: (eb[i], 0, 0)),
            pl.BlockSpec((1, D_EXPERT, D), lambda i, ea, eb, nv: (ea[i], 0, 0)),
            pl.BlockSpec((1, D_EXPERT, D), lambda i, ea, eb, nv: (eb[i], 0, 0)),
        ],
        out_specs=pl.BlockSpec((BM_MOE, D_PACK), lambda i, ea, eb, nv: (i, 0)),
        scratch_shapes=[pltpu.VMEM((2, D, 2 * D_EXPERT), BF16), pltpu.VMEM((2, D_EXPERT, D), BF16)],
    )
    return pl.pallas_call(
        _moe_kernel,
        out_shape=jax.ShapeDtypeStruct((S, D_PACK), jnp.uint32),
        grid_spec=grid_spec,
        compiler_params=pltpu.CompilerParams(dimension_semantics=("arbitrary",),
                                             vmem_limit_bytes=VMEM_LIMIT),
    )(ea, eb, nvalid, hs, w_gu, w_gu, w_dn, w_dn)


def _final_kernel(x2_ref, m_ref, mod_ref, fg_ref, o_ref):
    gt2 = mod_ref[0][:, 5 * D:6 * D]
    y = x2_ref[...] + gt2 * _unpack_bf16_pairs(m_ref[...]).astype(F32)
    ms = jnp.mean(y * y, axis=-1, keepdims=True)
    o_ref[...] = y * lax.rsqrt(ms + EPS) * fg_ref[...]


def _final(x2, m, mod3, fg, L):
    T = x2.shape[0]
    per_b = L // TM_FIN
    return pl.pallas_call(
        _final_kernel,
        out_shape=jax.ShapeDtypeStruct((T, D), F32),
        grid=(T // TM_FIN,),
        in_specs=[
            pl.BlockSpec((TM_FIN, D), lambda i: (i, 0)),
            pl.BlockSpec((TM_FIN, D_PACK), lambda i: (i, 0)),
            pl.BlockSpec((1, 1, 6 * D), lambda i: (i // per_b, 0, 0)),
            pl.BlockSpec((1, D), lambda i: (0, 0)),
        ],
        out_specs=pl.BlockSpec((TM_FIN, D), lambda i: (i, 0)),
        compiler_params=pltpu.CompilerParams(dimension_semantics=("arbitrary",),
                                             vmem_limit_bytes=VMEM_LIMIT),
    )(x2, m, mod3, fg)


def kernel(x, c, ctx, c_ctx, w_ada, b_ada, norm1_g, w_in, ln_a_g, ln_a_b, w_spatial, b_spatial, conv_qkv, a_log,
           dt_bias, onorm_g, w_out, norm2_g, w_group, b_group, w_router, b_router, w_gate_up, w_down, final_g):
    B, L, _ = x.shape
    T = B * L
    assert w_ada.shape[0] == 1 and ctx.shape[1] == TM_IN and L % TM_OUT == 0

    cond = jnp.concatenate([c, c_ctx[None, :], jnp.zeros((7, D), F32)], axis=0)
    mod = _modulation(cond, w_ada[0], b_ada[0][None, :])
    mod_lat = mod[:B].reshape(B, 1, 6 * D)
    mod_ctx = mod[B:B + 1]

    w_ab = jnp.pad(w_in[0][:, N_MAIN:], ((0, 0), (0, 128 - N_AB))).astype(BF16)
    w_abt = w_in[0][:, N_MAIN:].T.astype(BF16)
    alog = a_log[0].reshape(1, 2 * HEADS)
    dtb = dt_bias[0].reshape(1, 2 * HEADS)
    alog_row = jnp.pad(alog, ((0, 0), (0, 128 - 2 * HEADS)))
    dtb_row = jnp.pad(dtb, ((0, 0), (0, 128 - 2 * HEADS)))
    alog_col = jnp.pad(alog, ((0, 0), (0, N_AB - 2 * HEADS))).T
    dtb_col = jnp.pad(dtb, ((0, 0), (0, N_AB - 2 * HEADS))).T

    ya, qkv, z, gb, gbt = _in_proj(
        x, ctx, mod_lat, mod_ctx, norm1_g, w_in[0], w_ab, w_abt, ln_a_g, ln_a_b,
        w_spatial[0].astype(BF16), b_spatial[0].T, conv_qkv[0], alog_row, dtb_row, alog_col, dtb_col)

    yb = _delta(qkv, z, gb, gbt, onorm_g, L)

    wrt = jnp.concatenate([w_group[0].T, w_router[0].T, jnp.zeros((32 - N_GROUPS - N_EXPERTS, D), F32)], axis=0)
    brt = jnp.concatenate([b_group[0], b_router[0], jnp.zeros((32 - N_GROUPS - N_EXPERTS,), F32)])[:, None]
    x2, h, route, cnt = _out_proj(x.reshape(T, D), ya.reshape(T, D_A), yb.reshape(T, D_B), mod_lat,
                                  w_out[0], norm2_g, wrt.astype(BF16), brt, L)

    bucket = route[:, 0, :].reshape(T).astype(jnp.int32)
    rank = route[:, 1, :].reshape(T).astype(jnp.int32)
    counts = cnt[:N_BUCKETS, 0].astype(jnp.int32)
    nblk_b = (counts + BM_MOE - 1) // BM_MOE
    blk_end = jnp.cumsum(nblk_b)
    blk_start = blk_end - nblk_b
    pos = blk_start[bucket] * BM_MOE + rank
    n_blocks = T // BM_MOE + N_BUCKETS
    S = n_blocks * BM_MOE
    blk = jnp.arange(n_blocks, dtype=jnp.int32)
    used = blk < blk_end[-1]
    bkt = jnp.sum((jnp.minimum(blk, blk_end[-1] - 1)[:, None] >= blk_end[None, :]).astype(jnp.int32), axis=1)
    nvalid = jnp.where(used, jnp.clip(counts[bkt] - (blk - blk_start[bkt]) * BM_MOE, 0, BM_MOE), 0).astype(jnp.int32)
    pa = jnp.asarray(PAIR_A, jnp.int32)
    pb = jnp.asarray(PAIR_B, jnp.int32)
    ea = (bkt // N_PAIRS) * EPG + pa[bkt % N_PAIRS]
    eb = (bkt // N_PAIRS) * EPG + pb[bkt % N_PAIRS]

    hs = _sc_row_scatter(h, pos, S)
    ms = _moe(ea, eb, nvalid, hs, w_gate_up[0], w_down[0])
    out = _final(x2, _sc_row_gather(ms, pos), mod_lat, final_g[None, :], L)
    return out.reshape(B, L, D)
```

```python
import functools

import jax
import jax.numpy as jnp
import numpy as np
from jax import lax
from jax.experimental import pallas as pl
from jax.experimental.pallas import tpu as pltpu
from jax.experimental.pallas import tpu_sc as plsc

F32 = jnp.float32
BF16 = jnp.bfloat16
EPS = 1e-6

D = 1024
D_A = 512
D_B = 512
HEADS = 4
HD = 128
CHUNK = 128
CONV_W = 5
N_QKV = 3 * D_B
N_MAIN = 2 * D_A + 4 * D_B
N_AB = 16
N_GROUPS = 4
EPG = 4
N_EXPERTS = 16
D_EXPERT = 512
N_PAIRS = 6
N_BUCKETS = N_GROUPS * N_PAIRS
D_PACK = D // 2
D_ROW = D_PACK + 128
PAIR_A = (0, 0, 0, 1, 1, 3)
PAIR_B = (1, 2, 3, 3, 2, 2)
_SLOT_A_EXPERT = np.array([g * EPG + PAIR_A[p] for g in range(N_GROUPS) for p in range(N_PAIRS)], np.int32)
_SLOT_B_EXPERT = np.array([g * EPG + PAIR_B[p] for g in range(N_GROUPS) for p in range(N_PAIRS)], np.int32)

TM_IN = 256
HALO = 8
NB_IN = 2
NB_DELTA = 2
TM_OUT = 512
OUT_SPLIT = 2
BM_MOE = 256
TM_FIN = 512
N_COMBINE_PARTS = 4
VMEM_LIMIT = 56 * 1024 * 1024
SC_CORES = 2
SC_SUBCORES = 16
SC_WINDOW = 32

HI = lax.Precision.HIGHEST

_CHUNK_TRIL = np.kron(np.eye(TM_IN // CHUNK, dtype=np.float32), np.tril(np.ones((CHUNK, CHUNK), np.float32)))


def _dot(a, b, precision=None):
    return jnp.dot(a, b, preferred_element_type=F32, precision=precision)


def _dot_nt(a, b):
    return lax.dot_general(a, b, (((1,), (1,)), ((), ())), preferred_element_type=F32)


def _dot_tn(a, b):
    return lax.dot_general(a, b, (((0,), (0,)), ((), ())), preferred_element_type=F32)


def _sigmoid(x):
    return 1.0 / (1.0 + jnp.exp(-x))


def _silu(x):
    return x * _sigmoid(x)


def _softplus(x):
    return jnp.maximum(x, 0.0) + jnp.log(1.0 + jnp.exp(-jnp.abs(x)))


def _pack_bf16_pairs(x):
    bits = lax.bitcast_convert_type(x.astype(BF16).astype(F32), jnp.uint32)
    return (bits[:, D_PACK:] & jnp.uint32(0xFFFF0000)) | (bits[:, :D_PACK] >> 16)


def _unpack_bf16_pairs(w):
    bits = w
    lo = lax.bitcast_convert_type(bits << 16, F32)
    hi = lax.bitcast_convert_type(bits & jnp.uint32(0xFFFF0000), F32)
    return jnp.concatenate([lo, hi], axis=1).astype(BF16)


def _gelu_tanh(x):
    return 0.5 * x * (1.0 + jnp.tanh(np.sqrt(2.0 / np.pi).astype(np.float32) * (x + 0.044715 * (x * x * x))))


def _mod_kernel(c_ref, w_ref, b_ref, o_ref):
    c = c_ref[...]
    o_ref[...] = _dot(_silu(c), w_ref[...], precision=HI) + b_ref[...]


def _modulation(cond, w_ada, b_ada):
    rows = cond.shape[0]
    tn = 1536
    return pl.pallas_call(
        _mod_kernel,
        out_shape=jax.ShapeDtypeStruct((rows, 6 * D), F32),
        grid=(6 * D // tn,),
        in_specs=[pl.BlockSpec((rows, D), lambda i: (0, 0)),
                  pl.BlockSpec((D, tn), lambda i: (0, i)),
                  pl.BlockSpec((1, tn), lambda i: (0, i))],
        out_specs=pl.BlockSpec((rows, tn), lambda i: (0, i)),
        compiler_params=pltpu.CompilerParams(dimension_semantics=("arbitrary",),
                                             vmem_limit_bytes=VMEM_LIMIT),
    )(cond, w_ada, b_ada)


def _in_kernel(x_ref, xp_ref, xn_ref, ctx_ref, mod_ref, cmod_ref, g1_ref, w_ref, wab_ref, wabt_ref,
               lng_ref, lnb_ref, ws_ref, bst_ref, conv_ref, alog_ref, dtb_ref, alogt_ref, dtbt_ref, tril_ref, triu_ref,
               ya_ref, qkv_ref, z_ref, gb_ref, gbt_ref, wbf_ref):
    j = pl.program_id(1)

    @pl.when(jnp.logical_and(pl.program_id(0) == 0, j == 0))
    def _():
        wbf_ref[...] = w_ref[0].astype(BF16)

    is_ctx = j == 0
    n_lat_blocks = pl.num_programs(1) - 1

    def one_batch_element(bb):
        mod = mod_ref[bb]
        cm = cmod_ref[...]
        sh = jnp.where(is_ctx, cm[:, 0:D], mod[:, 0:D])
        sc = jnp.where(is_ctx, cm[:, D:2 * D], mod[:, D:2 * D])
        scale = g1_ref[...] * (1.0 + sc)

        xmain = jnp.where(is_ctx, ctx_ref[bb], x_ref[bb])
        xv = jnp.concatenate([xp_ref[bb], xmain, xn_ref[bb]], axis=0)
        xnorm = xv * lax.rsqrt(jnp.mean(xv * xv, axis=-1, keepdims=True) + EPS) * scale + sh
        xe = xnorm.astype(BF16)
        xb = xnorm[HALO:HALO + TM_IN].astype(BF16)

        rid = lax.broadcasted_iota(jnp.int32, (TM_IN + 2 * HALO, 1), 0)
        prev_ok = j >= 2
        next_ok = jnp.logical_and(j >= 1, j < n_lat_blocks)
        valid = jnp.logical_or(jnp.logical_and(rid >= HALO, rid < HALO + TM_IN),
                               jnp.logical_or(jnp.logical_and(rid < HALO, prev_ok),
                                              jnp.logical_and(rid >= HALO + TM_IN, next_ok)))
        pad = (CONV_W - 1) // 2
        c_qkv = 2 * D_A

        def proj(c0, width):
            return _dot(xe, wbf_ref[:, c0:c0 + width])

        def conv_act(pq, c0):
            groups = (TM_IN + 2 * HALO) // 8
            x3 = jnp.where(valid, pq, 0.0).reshape(groups, 8, D_B)
            sub = lax.broadcasted_iota(jnp.int32, (1, 8, 1), 1)
            lo, hi = HALO // 8, HALO // 8 + TM_IN // 8
            acc = conv_ref[pad:pad + 1, c0:c0 + D_B] * x3[lo:hi]
            for t in range(CONV_W):
                s = t - pad
                if s == 0:
                    continue
                r = pltpu.roll(x3, (-s) % 8, axis=1)
                if s > 0:
                    sh = jnp.where(sub < 8 - s, r[lo:hi], r[lo + 1:hi + 1])
                else:
                    sh = jnp.where(sub >= -s, r[lo:hi], r[lo - 1:hi - 1])
                acc = acc + conv_ref[t:t + 1, c0:c0 + D_B] * sh
            return _silu(acc.reshape(TM_IN, D_B))

        def store_unit_heads(act, c0, gain):
            for h in range(HEADS):
                t = act[:, h * HD:(h + 1) * HD]
                nrm = lax.rsqrt(jnp.sum(t * t, axis=-1, keepdims=True) + EPS) * gain
                qkv_ref[bb, :, c0 + h * HD:c0 + (h + 1) * HD] = (t * nrm).astype(BF16)

        pq_q = proj(c_qkv, D_B)
        pq_k = proj(c_qkv + D_B, D_B)
        store_unit_heads(conv_act(pq_q, 0), 0, HD ** -0.5)
        pq_v = proj(c_qkv + 2 * D_B, D_B)
        store_unit_heads(conv_act(pq_k, D_B), D_B, 1.0)
        pa_u = proj(0, D_A)[HALO:HALO + TM_IN]
        qkv_ref[bb, :, 2 * D_B:] = conv_act(pq_v, 2 * D_B).astype(BF16)
        pa_v = proj(D_A, D_A)[HALO:HALO + TM_IN]
        u = _gelu_tanh(pa_u)
        pz = proj(c_qkv + N_QKV, D_B)
        v = _gelu_tanh(pa_v)
        mu = jnp.mean(v, axis=-1, keepdims=True)
        vc = v - mu
        var = jnp.mean(vc * vc, axis=-1, keepdims=True)
        vn = (vc * lax.rsqrt(var + EPS) * lng_ref[...] + lnb_ref[...]).astype(BF16)
        z_ref[bb] = pz[HALO:HALO + TM_IN].astype(BF16)

        bst = bst_ref[...]
        for n in range(TM_IN // CHUNK):
            rows = slice(n * CHUNK, (n + 1) * CHUNK)
            for h in range(HEADS):
                cols = slice(h * HD, (h + 1) * HD)
                s = _dot(ws_ref[h], vn[rows, cols]) + bst[:, h:h + 1]
                ya_ref[bb, rows, cols] = (u[rows, cols] * s).astype(BF16)

        tri_l = tril_ref[...]
        tri_u = triu_ref[...]

        def split3(g):
            hi = g.astype(BF16)
            r1 = g - hi.astype(F32)
            mid = r1.astype(BF16)
            return hi, mid, (r1 - mid.astype(F32)).astype(BF16)

        ab = _dot(xb, wab_ref[...])
        g3 = split3(-jnp.exp(alog_ref[...]) * _softplus(ab + dtb_ref[...]))
        lane = lax.broadcasted_iota(jnp.int32, ab.shape, 1)
        gb = jnp.where(lane < HEADS, _dot(tri_l, g3[0]) + _dot(tri_l, g3[1]) + _dot(tri_l, g3[2]),
                       jnp.where(lane < 2 * HEADS, _dot(tri_u, g3[0]) + _dot(tri_u, g3[1]) + _dot(tri_u, g3[2]),
                                 _sigmoid(ab)))
        gb_ref[bb] = gb[:, 0:N_AB]

        abt = _dot_nt(wabt_ref[...], xb)
        t3 = split3(-jnp.exp(alogt_ref[...]) * _softplus(abt + dtbt_ref[...]))
        row = lax.broadcasted_iota(jnp.int32, abt.shape, 0)
        gbt_ref[bb] = jnp.where(row < HEADS, _dot(t3[0], tri_u) + _dot(t3[1], tri_u) + _dot(t3[2], tri_u),
                               jnp.where(row < 2 * HEADS, _dot(t3[0], tri_l) + _dot(t3[1], tri_l) + _dot(t3[2], tri_l),
                                         _sigmoid(abt)))

    for bb in range(x_ref.shape[0]):
        one_batch_element(bb)


def _in_proj(x, ctx, mod_lat, mod_ctx, g1, w_main, w_ab, w_abt, lng, lnb, ws, bst, conv, alog, dtb, alogt, dtbt):
    B, L, _ = x.shape
    n_lat = L // TM_IN
    n_steps = n_lat + 1
    LC = L + TM_IN
    hb = TM_IN // HALO

    def full(shape):
        return pl.BlockSpec(shape, lambda b, j: (0,) * len(shape))

    in_specs = [
        pl.BlockSpec((NB_IN, TM_IN, D), lambda b, j: (b, jnp.maximum(j - 1, 0), 0)),
        pl.BlockSpec((NB_IN, HALO, D), lambda b, j: (b, jnp.clip((j - 1) * hb - 1, 0, L // HALO - 1), 0)),
        pl.BlockSpec((NB_IN, HALO, D), lambda b, j: (b, jnp.clip(j * hb, 0, L // HALO - 1), 0)),
        pl.BlockSpec((NB_IN, TM_IN, D), lambda b, j: (b, 0, 0)),
        pl.BlockSpec((NB_IN, 1, 6 * D), lambda b, j: (b, 0, 0)),
        full((1, 6 * D)), full((1, D)),
        pl.BlockSpec((1, D, N_MAIN), lambda b, j: (0, 0, 0), pipeline_mode=pl.Buffered(1)),
        full((D, 128)), full((N_AB, D)),
        full((1, D_A)), full((1, D_A)), full((HEADS, CHUNK, CHUNK)), full((CHUNK, HEADS)),
        full((CONV_W, N_QKV)), full((1, 128)), full((1, 128)), full((N_AB, 1)), full((N_AB, 1)),
        full((TM_IN, TM_IN)), full((TM_IN, TM_IN)),
    ]
    out_shape = [
        jax.ShapeDtypeStruct((B, L, D_A), BF16),
        jax.ShapeDtypeStruct((B, LC, N_QKV), BF16),
        jax.ShapeDtypeStruct((B, LC, D_B), BF16),
        jax.ShapeDtypeStruct((B, LC, N_AB), F32),
        jax.ShapeDtypeStruct((B, N_AB, LC), F32),
    ]
    out_specs = [
        pl.BlockSpec((NB_IN, TM_IN, D_A), lambda b, j: (b, jnp.maximum(j - 1, 0), 0)),
        pl.BlockSpec((NB_IN, TM_IN, N_QKV), lambda b, j: (b, j, 0)),
        pl.BlockSpec((NB_IN, TM_IN, D_B), lambda b, j: (b, j, 0)),
        pl.BlockSpec((NB_IN, TM_IN, N_AB), lambda b, j: (b, j, 0)),
        pl.BlockSpec((NB_IN, N_AB, TM_IN), lambda b, j: (b, 0, j)),
    ]
    return pl.pallas_call(
        _in_kernel,
        out_shape=out_shape,
        grid=(B // NB_IN, n_steps),
        in_specs=in_specs,
        out_specs=out_specs,
        scratch_shapes=[pltpu.VMEM((D, N_MAIN), BF16)],
        compiler_params=pltpu.CompilerParams(dimension_semantics=("arbitrary", "arbitrary"),
                                             vmem_limit_bytes=VMEM_LIMIT),
    )(x, x, x, ctx, mod_lat, mod_ctx, g1, w_main, w_ab, w_abt, lng, lnb, ws, bst, conv, alog, dtb, alogt, dtbt,
      jnp.asarray(_CHUNK_TRIL, BF16), jnp.asarray(_CHUNK_TRIL.T, BF16))


def _delta_kernel(qf_ref, qb_ref, zf_ref, zb_ref, gf_ref, gbk_ref, gtf_ref, gtb_ref, on_ref,
                  y_ref, s_ref, oacc_ref, *, n_ctx, n_lat):
    s = pl.program_id(1)

    @pl.when(s == 0)
    def _():
        s_ref[...] = jnp.zeros_like(s_ref)
        oacc_ref[...] = jnp.zeros_like(oacc_ref)

    row = lax.broadcasted_iota(jnp.int32, (CHUNK, CHUNK), 0)
    col = lax.broadcasted_iota(jnp.int32, (CHUNK, CHUNK), 1)
    low = row > col
    upp = row < col
    same_blk = (row // 16) == (col // 16)
    eye = jnp.where(row == col, 1.0, 0.0).astype(BF16)
    zero = jnp.zeros((CHUNK, CHUNK), BF16)
    onorm = on_ref[...]
    half = n_ctx + n_lat // 2
    second = s >= half
    g_refs = (gf_ref, gbk_ref)
    gt_refs = (gtf_ref, gtb_ref)
    qkv_refs = (qf_ref, qb_ref)
    z_refs = (zf_ref, zb_ref)
    nb = qf_ref.shape[0]
    ps = range(nb * HEADS)

    def halves(xc, unit):
        xb = xc.astype(BF16)
        fill = eye if unit else zero
        return jnp.where(low, xb, fill), jnp.where(upp, xb, fill)

    def as_lhs(hv):
        return jnp.concatenate(hv, axis=1)

    def as_rhs(*hvs):
        cols_ = [jnp.concatenate(hv, axis=0) for hv in hvs]
        return cols_[0] if len(cols_) == 1 else jnp.concatenate(cols_, axis=1)

    def load(d, p, part):
        bb, h = divmod(p, HEADS)
        return qkv_refs[d][bb, :, part * D_B + h * HD:part * D_B + (h + 1) * HD]

    def gcol(d, p, base):
        bb, h = divmod(p, HEADS)
        c = base + d * HEADS + h
        return g_refs[d][bb, :, c:c + 1]

    def grow(d, p, base):
        bb, h = divmod(p, HEADS)
        r = base + d * HEADS + h
        return gt_refs[d][bb, r:r + 1, :]

    def lanes(col):
        return jnp.broadcast_to(col, (CHUNK, HD))

    q = [[load(d, p, 0) for p in ps] for d in range(2)]
    k = [[load(d, p, 1) for p in ps] for d in range(2)]
    v = [[load(d, p, 2) for p in ps] for d in range(2)]
    gcl = [[lanes(gcol(d, p, 0)) for p in ps] for d in range(2)]
    betal = [[lanes(gcol(d, p, 2 * HEADS)) for p in ps] for d in range(2)]
    gr = [[grow(d, p, 0) for p in ps] for d in range(2)]
    betar = [[grow(d, p, 2 * HEADS) for p in ps] for d in range(2)]
    glast = [[gr[0][p][:, CHUNK - 1:CHUNK] for p in ps], [gr[1][p][:, 0:1] for p in ps]]

    gram = [[_dot_nt(jnp.concatenate([q[d][p], k[d][p]], axis=0), k[d][p]) for p in ps] for d in range(2)]
    dec = [jnp.exp(jnp.where(low, gcl[0][p] - gr[0][p], jnp.where(upp, gcl[1][p] - gr[1][p], 0.0))) for p in ps]
    lc = [jnp.where(low, gram[0][p][CHUNK:] * betar[0][p], jnp.where(upp, gram[1][p][CHUNK:] * betar[1][p], 0.0))
          * dec[p] for p in ps]
    qk = [[jnp.where(upp, 0.0, gram[0][p][:CHUNK] * dec[p]).astype(BF16) for p in ps],
          [jnp.where(low, 0.0, gram[1][p][:CHUNK] * dec[p]).astype(BF16) for p in ps]]

    dg = [jnp.where(same_blk, lc[p], 0.0) for p in ps]
    ob = [lc[p] - dg[p] for p in ps]
    d1h = [halves(dg[p], False) for p in ps]
    d2 = [_dot(as_lhs(d1h[p]), as_rhs(d1h[p])) for p in ps]
    p0s = [-dg[p] for p in ps]
    d2h = [halves(d2[p], False) for p in ps]
    p0h = [halves(p0s[p], True) for p in ps]
    o2 = [_dot(as_lhs(d2h[p]), as_rhs(d2h[p], p0h[p])) for p in ps]
    p1s = [p0s[p] + o2[p][:, CHUNK:] for p in ps]
    d4h = [halves(o2[p][:, :CHUNK], False) for p in ps]
    p1h = [halves(p1s[p], True) for p in ps]
    o3 = [_dot(as_lhs(d4h[p]), as_rhs(d4h[p], p1h[p])) for p in ps]
    p2s = [p1s[p] + o3[p][:, CHUNK:] for p in ps]
    d8h = [halves(o3[p][:, :CHUNK], False) for p in ps]
    p2h = [halves(p2s[p], True) for p in ps]
    p3s = [p2s[p] + _dot(as_lhs(d8h[p]), as_rhs(p2h[p])) for p in ps]
    p3h = [halves(p3s[p], True) for p in ps]
    obh = [halves(ob[p], False) for p in ps]
    n1h = [halves(_dot(as_lhs(p3h[p]), as_rhs(obh[p])), False) for p in ps]
    o6 = [_dot(as_lhs(n1h[p]), as_rhs(n1h[p], p3h[p])) for p in ps]
    r0s = [p3s[p] - o6[p][:, CHUNK:] for p in ps]
    n2h = [halves(o6[p][:, :CHUNK], False) for p in ps]
    r0h = [halves(r0s[p], True) for p in ps]
    o7 = [_dot(as_lhs(n2h[p]), as_rhs(n2h[p], r0h[p])) for p in ps]
    r1s = [r0s[p] + o7[p][:, CHUNK:] for p in ps]
    n4h = [halves(o7[p][:, :CHUNK], False) for p in ps]
    r1h = [halves(r1s[p], True) for p in ps]
    tinv = [halves(r1s[p] + _dot(as_lhs(n4h[p]), as_rhs(r1h[p])), True) for p in ps]

    offs = []
    for d in range(2):
        lat_chunk = (s - n_ctx) if d == 0 else (n_ctx + n_lat - 1 - s)
        off = pl.multiple_of(jnp.clip(lat_chunk, 0, n_lat - 1) * CHUNK, CHUNK)
        sidx = [(p // HEADS * 2 + d) * HEADS + p % HEADS for p in ps]
        egc = [jnp.exp(gcl[d][p]) for p in ps]
        kf = [k[d][p].astype(F32) for p in ps]
        rhs = [jnp.concatenate([v[d][p], (kf[p] * egc[p]).astype(BF16)], axis=1) for p in ps]
        uw = [_dot(tinv[p][d], rhs[p]) for p in ps]
        qd = [q[d][p].astype(F32) * egc[p] for p in ps]
        kd = [(kf[p] * jnp.exp(glast[d][p] - gcl[d][p])).astype(BF16) for p in ps]
        st = [s_ref[sidx[p]] for p in ps]
        a1 = [_dot(jnp.concatenate([uw[p][:, HD:] * betal[d][p], qd[p]], axis=0).astype(BF16), st[p].astype(BF16))
              for p in ps]
        vnew = [(uw[p][:, :HD] * betal[d][p] - a1[p][:CHUNK]).astype(BF16) for p in ps]
        o = [a1[p][CHUNK:] + _dot(qk[d][p], vnew[p]) for p in ps]
        for p in ps:
            s_ref[sidx[p]] = st[p] * jnp.exp(glast[d][p]) + _dot_tn(kd[p], vnew[p])
        for p in ps:
            bb, h = divmod(p, HEADS)
            cols = slice(h * HD, (h + 1) * HD)
            oacc_ref[bb, pl.ds(off, CHUNK), cols] = (
                jnp.where(second, oacc_ref[bb, pl.ds(off, CHUNK), cols], 0.0) + o[p])
        offs.append(off)

    @pl.when(second)
    def _():
        for d in range(2):
            for p in ps:
                bb, h = divmod(p, HEADS)
                cols = slice(h * HD, (h + 1) * HD)
                tot = oacc_ref[bb, pl.ds(offs[d], CHUNK), cols]
                ms = jnp.mean(tot * tot, axis=-1, keepdims=True)
                zz = z_refs[d][bb, :, cols].astype(F32)
                y_ref[bb, pl.ds(offs[d], CHUNK), cols] = (
                    tot * lax.rsqrt(ms + EPS) * onorm * _silu(zz)).astype(BF16)


def _delta(qkv, z, gb, gbt, onorm, L):
    B, LC, _ = qkv.shape
    n_all = LC // CHUNK
    n_lat = L // CHUNK
    n_ctx = n_all - n_lat

    def cf(s):
        return s

    def cb(s):
        return jnp.where(s < n_ctx, n_ctx - 1 - s, n_all + n_ctx - 1 - s)

    in_specs = [
        pl.BlockSpec((NB_DELTA, CHUNK, N_QKV), lambda b, s: (b, cf(s), 0)),
        pl.BlockSpec((NB_DELTA, CHUNK, N_QKV), lambda b, s: (b, cb(s), 0)),
        pl.BlockSpec((NB_DELTA, CHUNK, D_B), lambda b, s: (b, cf(s), 0)),
        pl.BlockSpec((NB_DELTA, CHUNK, D_B), lambda b, s: (b, cb(s), 0)),
        pl.BlockSpec((NB_DELTA, CHUNK, N_AB), lambda b, s: (b, cf(s), 0)),
        pl.BlockSpec((NB_DELTA, CHUNK, N_AB), lambda b, s: (b, cb(s), 0)),
        pl.BlockSpec((NB_DELTA, N_AB, CHUNK), lambda b, s: (b, 0, cf(s))),
        pl.BlockSpec((NB_DELTA, N_AB, CHUNK), lambda b, s: (b, 0, cb(s))),
        pl.BlockSpec((1, HD), lambda b, s: (0, 0)),
    ]
    return pl.pallas_call(
        functools.partial(_delta_kernel, n_ctx=n_ctx, n_lat=n_lat),
        out_shape=jax.ShapeDtypeStruct((B, L, D_B), BF16),
        grid=(B // NB_DELTA, n_all),
        in_specs=in_specs,
        out_specs=pl.BlockSpec((NB_DELTA, L, D_B), lambda b, s: (b, 0, 0)),
        scratch_shapes=[pltpu.VMEM((NB_DELTA * 2 * HEADS, HD, HD), F32), pltpu.VMEM((NB_DELTA, L, D_B), F32)],
        compiler_params=pltpu.CompilerParams(dimension_semantics=("arbitrary", "arbitrary"),
                                             vmem_limit_bytes=VMEM_LIMIT),
    )(qkv, qkv, z, z, gb, gb, gbt, gbt, onorm)


def _out_kernel(x_ref, ya_ref, yb_ref, mod_ref, wo_ref, g2_ref, wrt_ref, brt_ref,
                x2_ref, h_ref, route_ref, cnt_ref, base_ref, wbf_ref):
    i = pl.program_id(0)

    @pl.when(i == 0)
    def _():
        base_ref[...] = jnp.zeros_like(base_ref)
        wbf_ref[...] = wo_ref[...].astype(BF16)

    mod = mod_ref[0]
    gt1 = mod[:, 2 * D:3 * D]
    sh2 = mod[:, 3 * D:4 * D]
    sc2 = mod[:, 4 * D:5 * D]
    scale2 = g2_ref[...] * (1.0 + sc2)
    sub = TM_OUT // OUT_SPLIT
    hbs = []
    for r in range(OUT_SPLIT):
        rows = slice(r * sub, (r + 1) * sub)
        mix = _dot(ya_ref[rows, :], wbf_ref[0:D_A, :]) + _dot(yb_ref[rows, :], wbf_ref[D_A:, :])
        x2 = x_ref[rows, :] + gt1 * mix
        x2_ref[rows, :] = x2
        ms = jnp.mean(x2 * x2, axis=-1, keepdims=True)
        hv = x2 * lax.rsqrt(ms + EPS) * scale2 + sh2
        hbs.append(hv.astype(BF16))
        h_ref[rows, 0:D_PACK] = _pack_bf16_pairs(hv)
    hb = jnp.concatenate(hbs, axis=0)

    lt = _dot_nt(wrt_ref[...], hb) + brt_ref[...]
    gl = [lt[r:r + 1, :] for r in range(N_GROUPS)]
    gmax = jnp.maximum(jnp.maximum(gl[0], gl[1]), jnp.maximum(gl[2], gl[3]))
    gsel = jnp.where(gl[0] == gmax, 0, jnp.where(gl[1] == gmax, 1, jnp.where(gl[2] == gmax, 2, 3)))
    p_g = 1.0 / (jnp.exp(gl[0] - gmax) + jnp.exp(gl[1] - gmax) + jnp.exp(gl[2] - gmax) + jnp.exp(gl[3] - gmax))
    el = []
    for e in range(EPG):
        r = [lt[N_GROUPS + g * EPG + e:N_GROUPS + g * EPG + e + 1, :] for g in range(N_GROUPS)]
        el.append(jnp.where(gsel == 0, r[0], jnp.where(gsel == 1, r[1], jnp.where(gsel == 2, r[2], r[3]))))
    m1 = jnp.maximum(jnp.maximum(el[0], el[1]), jnp.maximum(el[2], el[3]))
    i1 = jnp.where(el[0] == m1, 0, jnp.where(el[1] == m1, 1, jnp.where(el[2] == m1, 2, 3)))
    neg = jnp.float32(-jnp.inf)
    el2 = [jnp.where(i1 == e, neg, el[e]) for e in range(EPG)]
    m2 = jnp.maximum(jnp.maximum(el2[0], el2[1]), jnp.maximum(el2[2], el2[3]))
    i2 = jnp.where(jnp.logical_and(el2[0] == m2, i1 != 0), 0,
                   jnp.where(jnp.logical_and(el2[1] == m2, i1 != 1), 1,
                             jnp.where(jnp.logical_and(el2[2] == m2, i1 != 2), 2, 3)))
    t = jnp.exp(m2 - m1)
    w1 = p_g / (1.0 + t)
    w2 = p_g * t / (1.0 + t)
    first_low = i1 < i2
    ea = jnp.where(first_low, i1, i2)
    eb = jnp.where(first_low, i2, i1)
    w_a = jnp.where(first_low, w1, w2)
    w_b = jnp.where(first_low, w2, w1)
    pair = jnp.where(ea == 0, eb - 1, jnp.where(ea == 1, jnp.where(eb == 3, 3, 4), 5))
    swap = pair == 5
    w_a, w_b = jnp.where(swap, w_b, w_a), jnp.where(swap, w_a, w_b)
    bucket = gsel * N_PAIRS + pair

    tm = bucket.shape[1]
    rows = lax.broadcasted_iota(jnp.int32, (32, tm), 0)
    onehot = jnp.where(rows == bucket, 1.0, 0.0).astype(F32)
    r2 = lax.broadcasted_iota(jnp.int32, (tm, tm), 0)
    c2 = lax.broadcasted_iota(jnp.int32, (tm, tm), 1)
    tri = jnp.where(r2 <= c2, 1.0, 0.0).astype(BF16)
    prefix = _dot(onehot.astype(BF16), tri)
    base = base_ref[:, 0:1]
    rank = jnp.sum(onehot * (prefix - 1.0 + base), axis=0, keepdims=True)
    newbase = base + prefix[:, tm - 1:tm]
    base_ref[...] = jnp.broadcast_to(newbase, base_ref.shape)
    cnt_ref[...] = jnp.broadcast_to(newbase, cnt_ref.shape)
    route = jnp.concatenate([bucket.astype(F32), rank, w_a, w_b, jnp.zeros((4, tm), F32)], axis=0)
    route_ref[0] = route
    route_t = jnp.transpose(jnp.concatenate([route, jnp.zeros((120, tm), F32)], axis=0))
    h_ref[:, D_PACK:D_ROW] = lax.bitcast_convert_type(route_t, jnp.uint32)


def _out_proj(x2d, ya, yb, mod3, w_o, g2, wrt, brt, L):
    T = x2d.shape[0]
    nb = T // TM_OUT
    per_b = L // TM_OUT
    out_shape = [
        jax.ShapeDtypeStruct((T, D), F32),
        jax.ShapeDtypeStruct((T, D_ROW), jnp.uint32),
        jax.ShapeDtypeStruct((nb, 8, TM_OUT), F32),
        jax.ShapeDtypeStruct((32, 128), F32),
    ]
    return pl.pallas_call(
        _out_kernel,
        out_shape=out_shape,
        grid=(nb,),
        in_specs=[
            pl.BlockSpec((TM_OUT, D), lambda i: (i, 0)),
            pl.BlockSpec((TM_OUT, D_A), lambda i: (i, 0)),
            pl.BlockSpec((TM_OUT, D_B), lambda i: (i, 0)),
            pl.BlockSpec((1, 1, 6 * D), lambda i: (i // per_b, 0, 0)),
            pl.BlockSpec((D, D), lambda i: (0, 0), pipeline_mode=pl.Buffered(1)),
            pl.BlockSpec((1, D), lambda i: (0, 0)),
            pl.BlockSpec((32, D), lambda i: (0, 0)),
            pl.BlockSpec((32, 1), lambda i: (0, 0)),
        ],
        out_specs=[
            pl.BlockSpec((TM_OUT, D), lambda i: (i, 0)),
            pl.BlockSpec((TM_OUT, D_ROW), lambda i: (i, 0)),
            pl.BlockSpec((1, 8, TM_OUT), lambda i: (i, 0, 0)),
            pl.BlockSpec((32, 128), lambda i: (0, 0)),
        ],
        scratch_shapes=[pltpu.VMEM((32, 128), F32), pltpu.VMEM((D, D), BF16)],
        compiler_params=pltpu.CompilerParams(dimension_semantics=("arbitrary",),
                                             vmem_limit_bytes=VMEM_LIMIT),
    )(x2d, ya, yb, mod3, w_o, g2, wrt, brt)


def _sc_row_gather(table, idx):
    n_rows = idx.shape[0]
    width = table.shape[1]
    n_workers = SC_CORES * SC_SUBCORES
    per_worker = n_rows // n_workers
    assert per_worker * n_workers == n_rows and per_worker % SC_WINDOW == 0
    mesh = plsc.VectorSubcoreMesh(core_axis_name="c", subcore_axis_name="s")

    @functools.partial(
        pl.kernel, mesh=mesh,
        out_type=jax.ShapeDtypeStruct((n_rows, width), table.dtype),
        scratch_types=[pltpu.VMEM((SC_WINDOW,), jnp.int32),
                       pltpu.VMEM((SC_WINDOW, width), table.dtype),
                       pltpu.SemaphoreType.DMA],
    )
    def gather(table_hbm, idx_hbm, out_hbm, idx_v, rows_v, sem):
        wid = lax.axis_index("s") * SC_CORES + lax.axis_index("c")
        base = wid * per_worker

        @pl.loop(0, per_worker // SC_WINDOW)
        def _(j):
            off = pl.multiple_of(base + j * SC_WINDOW, 8)
            pltpu.sync_copy(idx_hbm.at[pl.ds(off, SC_WINDOW)], idx_v)
            pltpu.async_copy(table_hbm.at[idx_v], rows_v, sem).wait()
            pltpu.sync_copy(rows_v, out_hbm.at[pl.ds(off, SC_WINDOW)])

    return gather(table, idx)


def _sc_row_scatter(rows, pos, n_out):
    n_rows, width = rows.shape
    n_workers = SC_CORES * SC_SUBCORES
    per_worker = n_rows // n_workers
    n_chunks = per_worker // SC_WINDOW
    assert per_worker * n_workers == n_rows and n_chunks * SC_WINDOW == per_worker
    mesh = plsc.VectorSubcoreMesh(core_axis_name="c", subcore_axis_name="s")

    @functools.partial(
        pl.kernel, mesh=mesh,
        out_type=jax.ShapeDtypeStruct((n_out, width), rows.dtype),
        scratch_types=[pltpu.VMEM((n_chunks, SC_WINDOW), jnp.int32),
                       pltpu.VMEM((SC_WINDOW, width), rows.dtype),
                       pltpu.SemaphoreType.DMA],
    )
    def scatter(rows_hbm, pos_hbm, out_hbm, idx_v, rows_v, sem):
        wid = lax.axis_index("s") * SC_CORES + lax.axis_index("c")
        base = wid * per_worker
        pltpu.sync_copy(pos_hbm.at[wid], idx_v)

        @pl.loop(0, n_chunks)
        def _(j):
            off = pl.multiple_of(base + j * SC_WINDOW, 8)
            pltpu.sync_copy(rows_hbm.at[pl.ds(off, SC_WINDOW)], rows_v)
            pltpu.async_copy(rows_v, out_hbm.at[idx_v.at[j]], sem).wait()

    return scatter(rows, pos.reshape(n_workers, n_chunks, SC_WINDOW))


def _moe_kernel(ea_ref, eb_ref, nv_ref, hs_ref, wga_ref, wgb_ref, wda_ref, wdb_ref, o_ref, wg_s, wd_s):
    i = pl.program_id(0)
    prev = jnp.maximum(i - 1, 0)

    @pl.when(jnp.logical_or(i == 0, ea_ref[i] != ea_ref[prev]))
    def _():
        wg_s[0] = wga_ref[0].astype(BF16)
        wd_s[0] = wda_ref[0].astype(BF16)

    @pl.when(jnp.logical_or(i == 0, eb_ref[i] != eb_ref[prev]))
    def _():
        wg_s[1] = wgb_ref[0].astype(BF16)
        wd_s[1] = wdb_ref[0].astype(BF16)

    @pl.when(nv_ref[i] > 0)
    def _():
        live = lax.broadcasted_iota(jnp.int32, (BM_MOE, 1), 0) < nv_ref[i]
        xb = _unpack_bf16_pairs(jnp.where(live, hs_ref[:, 0:D_PACK], jnp.uint32(0)))
        w_ab = lax.bitcast_convert_type(hs_ref[:, D_PACK:D_ROW], F32)
        w_a = jnp.where(live, w_ab[:, 2:3], 0.0)
        w_b = jnp.where(live, w_ab[:, 3:4], 0.0)
        ga = _dot(xb, wg_s[0])
        act_a = (_silu(ga[:, :D_EXPERT]) * ga[:, D_EXPERT:] * w_a).astype(BF16)
        gb = _dot(xb, wg_s[1])
        act_b = (_silu(gb[:, :D_EXPERT]) * gb[:, D_EXPERT:] * w_b).astype(BF16)
        o_ref[...] = _pack_bf16_pairs(_dot(act_a, wd_s[0]) + _dot(act_b, wd_s[1]))

    @pl.when(nv_ref[i] <= 0)
    def _():
        o_ref[...] = jnp.zeros_like(o_ref)


def _moe(ea, eb, nvalid, hs, w_gu, w_dn):
    nblk = ea.shape[0]
    S = nblk * BM_MOE
    grid_spec = pltpu.PrefetchScalarGridSpec(
        num_scalar_prefetch=3,
        grid=(nblk,),
        in_specs=[
            pl.BlockSpec((BM_MOE, D_ROW), lambda i, ea, eb, nv: (i, 0)),
            pl.BlockSpec((1, D, 2 * D_EXPERT), lambda i, ea, eb, nv: (ea[i], 0, 0)),
            pl.BlockSpec((1, D, 2 * D_EXPERT), lambda i, ea, eb, nv: (eb[i], 0, 0)),
            pl.BlockSpec((1, D_EXPERT, D), lambda i, ea, eb, nv: (ea[i], 0, 0)),
            pl.BlockSpec((1, D_EXPERT, D), lambda i, ea, eb, nv: (eb[i], 0, 0)),
        ],
        out_specs=pl.BlockSpec((BM_MOE, D_PACK), lambda i, ea, eb, nv: (i, 0)),
        scratch_shapes=[pltpu.VMEM((2, D, 2 * D_EXPERT), BF16), pltpu.VMEM((2, D_EXPERT, D), BF16)],
    )
    return pl.pallas_call(
        _moe_kernel,
        out_shape=jax.ShapeDtypeStruct((S, D_PACK), jnp.uint32),
        grid_spec=grid_spec,
        compiler_params=pltpu.CompilerParams(dimension_semantics=("arbitrary",),
                                             vmem_limit_bytes=VMEM_LIMIT),
    )(ea, eb, nvalid, hs, w_gu, w_gu, w_dn, w_dn)


def _final_kernel(x2_ref, m_ref, mod_ref, fg_ref, *rest):
    o_ref = rest[-1]
    gt2 = mod_ref[0][:, 5 * D:6 * D]
    y = x2_ref[...] + gt2 * _unpack_bf16_pairs(m_ref[...]).astype(F32)
    ms = jnp.mean(y * y, axis=-1, keepdims=True)
    o_ref[...] = y * lax.rsqrt(ms + EPS) * fg_ref[...]


def _final(x2, m_part, mod3, fg, L, part, prev_out):
    T = x2.shape[0]
    steps = m_part.shape[0] // TM_FIN
    off = part * steps
    per_b = L // TM_FIN
    in_specs = [
        pl.BlockSpec((TM_FIN, D), lambda i: (i + off, 0)),
        pl.BlockSpec((TM_FIN, D_PACK), lambda i: (i, 0)),
        pl.BlockSpec((1, 1, 6 * D), lambda i: ((i + off) // per_b, 0, 0)),
        pl.BlockSpec((1, D), lambda i: (0, 0)),
    ]
    args = [x2, m_part, mod3, fg]
    aliases = {}
    if prev_out is not None:
        in_specs.append(pl.BlockSpec(memory_space=pl.ANY))
        args.append(prev_out)
        aliases = {len(args) - 1: 0}
    return pl.pallas_call(
        _final_kernel,
        out_shape=jax.ShapeDtypeStruct((T, D), F32),
        grid=(steps,),
        in_specs=in_specs,
        out_specs=pl.BlockSpec((TM_FIN, D), lambda i: (i + off, 0)),
        input_output_aliases=aliases,
        compiler_params=pltpu.CompilerParams(dimension_semantics=("arbitrary",),
                                             vmem_limit_bytes=VMEM_LIMIT),
    )(*args)


def kernel(x, c, ctx, c_ctx, w_ada, b_ada, norm1_g, w_in, ln_a_g, ln_a_b, w_spatial, b_spatial, conv_qkv, a_log,
           dt_bias, onorm_g, w_out, norm2_g, w_group, b_group, w_router, b_router, w_gate_up, w_down, final_g):
    B, L, _ = x.shape
    T = B * L
    assert w_ada.shape[0] == 1 and ctx.shape[1] == TM_IN and L % TM_OUT == 0 and T % (N_COMBINE_PARTS * TM_FIN) == 0

    cond = jnp.concatenate([c, c_ctx[None, :], jnp.zeros((7, D), F32)], axis=0)
    mod = _modulation(cond, w_ada[0], b_ada[0][None, :])
    mod_lat = mod[:B].reshape(B, 1, 6 * D)
    mod_ctx = mod[B:B + 1]

    w_ab = jnp.pad(w_in[0][:, N_MAIN:], ((0, 0), (0, 128 - N_AB))).astype(BF16)
    w_abt = w_in[0][:, N_MAIN:].T.astype(BF16)
    alog = a_log[0].reshape(1, 2 * HEADS)
    dtb = dt_bias[0].reshape(1, 2 * HEADS)
    alog_row = jnp.pad(alog, ((0, 0), (0, 128 - 2 * HEADS)))
    dtb_row = jnp.pad(dtb, ((0, 0), (0, 128 - 2 * HEADS)))
    alog_col = jnp.pad(alog, ((0, 0), (0, N_AB - 2 * HEADS))).T
    dtb_col = jnp.pad(dtb, ((0, 0), (0, N_AB - 2 * HEADS))).T

    ya, qkv, z, gb, gbt = _in_proj(
        x, ctx, mod_lat, mod_ctx, norm1_g, w_in, w_ab, w_abt, ln_a_g, ln_a_b,
        w_spatial[0].astype(BF16), b_spatial[0].T, conv_qkv[0], alog_row, dtb_row, alog_col, dtb_col)

    yb = _delta(qkv, z, gb, gbt, onorm_g, L)

    wrt = jnp.concatenate([w_group[0].T, w_router[0].T, jnp.zeros((32 - N_GROUPS - N_EXPERTS, D), F32)], axis=0)
    brt = jnp.concatenate([b_group[0], b_router[0], jnp.zeros((32 - N_GROUPS - N_EXPERTS,), F32)])[:, None]
    x2, h, route, cnt = _out_proj(x.reshape(T, D), ya.reshape(T, D_A), yb.reshape(T, D_B), mod_lat,
                                  w_out[0], norm2_g, wrt.astype(BF16), brt, L)

    bucket = route[:, 0, :].reshape(T).astype(jnp.int32)
    rank = route[:, 1, :].reshape(T).astype(jnp.int32)
    counts = cnt[:N_BUCKETS, 0].astype(jnp.int32)
    nblk_b = (counts + BM_MOE - 1) // BM_MOE
    blk_end = jnp.cumsum(nblk_b)
    blk_start = blk_end - nblk_b
    kk = jnp.arange(N_BUCKETS, dtype=jnp.int32)

    def pick(idx, table):
        return jnp.sum(jnp.where(idx[:, None] == kk[None, :], table[None, :], 0), axis=1)

    pos = pick(bucket, blk_start) * BM_MOE + rank
    n_blocks = T // BM_MOE + N_BUCKETS
    S = n_blocks * BM_MOE
    blk = jnp.arange(n_blocks, dtype=jnp.int32)
    used = blk < blk_end[-1]
    bkt = jnp.sum((jnp.minimum(blk, blk_end[-1] - 1)[:, None] >= blk_end[None, :]).astype(jnp.int32), axis=1)
    nvalid = jnp.where(used, jnp.clip(pick(bkt, counts) - (blk - pick(bkt, blk_start)) * BM_MOE, 0, BM_MOE),
                       0).astype(jnp.int32)
    ea = pick(bkt, jnp.asarray(_SLOT_A_EXPERT))
    eb = pick(bkt, jnp.asarray(_SLOT_B_EXPERT))

    hs = _sc_row_scatter(h, pos, S)
    ms = _moe(ea, eb, nvalid, hs, w_gate_up[0], w_down[0])
    pos_parts = pos.reshape(N_COMBINE_PARTS, T // N_COMBINE_PARTS)
    m_parts = [_sc_row_gather(ms, pos_parts[q]) for q in range(N_COMBINE_PARTS)]
    out = None
    for q in range(N_COMBINE_PARTS):
        out = _final(x2, m_parts[q], mod_lat, final_g[None, :], L, q, out)
    return out.reshape(B, L, D)
```

```python
import functools

import jax
import jax.numpy as jnp
import numpy as np
from jax import lax
from jax.experimental import pallas as pl
from jax.experimental.pallas import tpu as pltpu
from jax.experimental.pallas import tpu_sc as plsc

F32 = jnp.float32
BF16 = jnp.bfloat16
EPS = 1e-6

D = 1024
D_A = 512
D_B = 512
HEADS = 4
HD = 128
CHUNK = 128
CONV_W = 5
N_QKV = 3 * D_B
N_MAIN = 2 * D_A + 4 * D_B
N_AB = 16
N_GROUPS = 4
EPG = 4
N_EXPERTS = 16
D_EXPERT = 512
N_PAIRS = 6
N_BUCKETS = N_GROUPS * N_PAIRS
D_PACK = D // 2
D_ROW = D_PACK + 128
PAIR_A = (0, 0, 0, 1, 1, 3)
PAIR_B = (1, 2, 3, 3, 2, 2)
_SLOT_A_EXPERT = np.array([g * EPG + PAIR_A[p] for g in range(N_GROUPS) for p in range(N_PAIRS)], np.int32)
_SLOT_B_EXPERT = np.array([g * EPG + PAIR_B[p] for g in range(N_GROUPS) for p in range(N_PAIRS)], np.int32)

TM_IN = 256
HALO = 8
NB_IN = 2
NB_DELTA = 2
TM_OUT = 512
OUT_SPLIT = 2
BM_MOE = 256
TM_FIN = 512
N_COMBINE_PARTS = 4
VMEM_LIMIT = 56 * 1024 * 1024
SC_CORES = 2
SC_SUBCORES = 16
SC_WINDOW = 32

HI = lax.Precision.HIGHEST

_CHUNK_TRIL = np.kron(np.eye(TM_IN // CHUNK, dtype=np.float32), np.tril(np.ones((CHUNK, CHUNK), np.float32)))


def _dot(a, b, precision=None):
    return jnp.dot(a, b, preferred_element_type=F32, precision=precision)


def _dot_nt(a, b):
    return lax.dot_general(a, b, (((1,), (1,)), ((), ())), preferred_element_type=F32)


def _dot_tn(a, b):
    return lax.dot_general(a, b, (((0,), (0,)), ((), ())), preferred_element_type=F32)


def _sigmoid(x):
    return 1.0 / (1.0 + jnp.exp(-x))


def _silu(x):
    return x * _sigmoid(x)


def _softplus(x):
    return jnp.maximum(x, 0.0) + jnp.log(1.0 + jnp.exp(-jnp.abs(x)))


def _pack_bf16_pairs(x):
    bits = lax.bitcast_convert_type(x.astype(BF16).astype(F32), jnp.uint32)
    return (bits[:, D_PACK:] & jnp.uint32(0xFFFF0000)) | (bits[:, :D_PACK] >> 16)


def _unpack_bf16_pairs(w):
    bits = w
    lo = lax.bitcast_convert_type(bits << 16, F32)
    hi = lax.bitcast_convert_type(bits & jnp.uint32(0xFFFF0000), F32)
    return jnp.concatenate([lo, hi], axis=1).astype(BF16)


def _gelu_tanh(x):
    return 0.5 * x * (1.0 + jnp.tanh(np.sqrt(2.0 / np.pi).astype(np.float32) * (x + 0.044715 * (x * x * x))))


def _mod_kernel(c_ref, w_ref, b_ref, o_ref):
    c = c_ref[...]
    o_ref[...] = _dot(_silu(c), w_ref[...], precision=HI) + b_ref[...]


def _modulation(cond, w_ada, b_ada):
    rows = cond.shape[0]
    tn = 1536
    return pl.pallas_call(
        _mod_kernel,
        out_shape=jax.ShapeDtypeStruct((rows, 6 * D), F32),
        grid=(6 * D // tn,),
        in_specs=[pl.BlockSpec((rows, D), lambda i: (0, 0)),
                  pl.BlockSpec((D, tn), lambda i: (0, i)),
                  pl.BlockSpec((1, tn), lambda i: (0, i))],
        out_specs=pl.BlockSpec((rows, tn), lambda i: (0, i)),
        compiler_params=pltpu.CompilerParams(dimension_semantics=("arbitrary",),
                                             vmem_limit_bytes=VMEM_LIMIT),
    )(cond, w_ada, b_ada)


def _in_kernel(x_ref, xp_ref, xn_ref, ctx_ref, mod_ref, cmod_ref, g1_ref, w_ref, wab_ref,
               lng_ref, lnb_ref, ws_ref, bst_ref, conv_ref, alog_ref, dtb_ref, alogt_ref, dtbt_ref, tril_ref, triu_ref,
               ya_ref, qkv_ref, z_ref, gb_ref, gbt_ref, wbf_ref, wabt_ref):
    j = pl.program_id(1)

    @pl.when(jnp.logical_and(pl.program_id(0) == 0, j == 0))
    def _():
        wbf_ref[...] = w_ref[0].astype(BF16)
        wabt_ref[...] = jnp.transpose(wab_ref[...])

    is_ctx = j == 0
    n_lat_blocks = pl.num_programs(1) - 1

    def one_batch_element(bb):
        mod = mod_ref[bb]
        cm = cmod_ref[...]
        sh = jnp.where(is_ctx, cm[:, 0:D], mod[:, 0:D])
        sc = jnp.where(is_ctx, cm[:, D:2 * D], mod[:, D:2 * D])
        scale = g1_ref[...] * (1.0 + sc)

        xmain = jnp.where(is_ctx, ctx_ref[bb], x_ref[bb])
        xv = jnp.concatenate([xp_ref[bb], xmain, xn_ref[bb]], axis=0)
        xnorm = xv * lax.rsqrt(jnp.mean(xv * xv, axis=-1, keepdims=True) + EPS) * scale + sh
        xe = xnorm.astype(BF16)
        xb = xnorm[HALO:HALO + TM_IN].astype(BF16)

        rid = lax.broadcasted_iota(jnp.int32, (TM_IN + 2 * HALO, 1), 0)
        prev_ok = j >= 2
        next_ok = jnp.logical_and(j >= 1, j < n_lat_blocks)
        valid = jnp.logical_or(jnp.logical_and(rid >= HALO, rid < HALO + TM_IN),
                               jnp.logical_or(jnp.logical_and(rid < HALO, prev_ok),
                                              jnp.logical_and(rid >= HALO + TM_IN, next_ok)))
        pad = (CONV_W - 1) // 2
        c_qkv = 2 * D_A

        def proj(c0, width):
            return _dot(xe, wbf_ref[:, c0:c0 + width])

        def conv_act(pq, c0):
            groups = (TM_IN + 2 * HALO) // 8
            x3 = jnp.where(valid, pq, 0.0).reshape(groups, 8, D_B)
            sub = lax.broadcasted_iota(jnp.int32, (1, 8, 1), 1)
            lo, hi = HALO // 8, HALO // 8 + TM_IN // 8
            acc = conv_ref[pad:pad + 1, c0:c0 + D_B] * x3[lo:hi]
            for t in range(CONV_W):
                s = t - pad
                if s == 0:
                    continue
                r = pltpu.roll(x3, (-s) % 8, axis=1)
                if s > 0:
                    sh = jnp.where(sub < 8 - s, r[lo:hi], r[lo + 1:hi + 1])
                else:
                    sh = jnp.where(sub >= -s, r[lo:hi], r[lo - 1:hi - 1])
                acc = acc + conv_ref[t:t + 1, c0:c0 + D_B] * sh
            return _silu(acc.reshape(TM_IN, D_B))

        def store_unit_heads(act, c0, gain):
            for h in range(HEADS):
                t = act[:, h * HD:(h + 1) * HD]
                nrm = lax.rsqrt(jnp.sum(t * t, axis=-1, keepdims=True) + EPS) * gain
                qkv_ref[bb, :, c0 + h * HD:c0 + (h + 1) * HD] = (t * nrm).astype(BF16)

        pq_q = proj(c_qkv, D_B)
        pq_k = proj(c_qkv + D_B, D_B)
        store_unit_heads(conv_act(pq_q, 0), 0, HD ** -0.5)
        pq_v = proj(c_qkv + 2 * D_B, D_B)
        store_unit_heads(conv_act(pq_k, D_B), D_B, 1.0)
        pa_u = proj(0, D_A)[HALO:HALO + TM_IN]
        qkv_ref[bb, :, 2 * D_B:] = conv_act(pq_v, 2 * D_B).astype(BF16)
        pa_v = proj(D_A, D_A)[HALO:HALO + TM_IN]
        u = _gelu_tanh(pa_u)
        pz = proj(c_qkv + N_QKV, D_B)
        v = _gelu_tanh(pa_v)
        mu = jnp.mean(v, axis=-1, keepdims=True)
        vc = v - mu
        var = jnp.mean(vc * vc, axis=-1, keepdims=True)
        vn = (vc * lax.rsqrt(var + EPS) * lng_ref[...] + lnb_ref[...]).astype(BF16)
        z_ref[bb] = pz[HALO:HALO + TM_IN].astype(BF16)

        bst = bst_ref[...]
        for n in range(TM_IN // CHUNK):
            rows = slice(n * CHUNK, (n + 1) * CHUNK)
            for h in range(HEADS):
                cols = slice(h * HD, (h + 1) * HD)
                s = _dot(ws_ref[h], vn[rows, cols]) + bst[:, h:h + 1]
                ya_ref[bb, rows, cols] = (u[rows, cols] * s).astype(BF16)

        tri_l = tril_ref[...]
        tri_u = triu_ref[...]

        def split3(g):
            hi = g.astype(BF16)
            r1 = g - hi.astype(F32)
            mid = r1.astype(BF16)
            return hi, mid, (r1 - mid.astype(F32)).astype(BF16)

        ab = _dot(xb, wab_ref[...])
        g3 = split3(-jnp.exp(alog_ref[...]) * _softplus(ab + dtb_ref[...]))
        lane = lax.broadcasted_iota(jnp.int32, ab.shape, 1)
        gb = jnp.where(lane < HEADS, _dot(tri_l, g3[0]) + _dot(tri_l, g3[1]) + _dot(tri_l, g3[2]),
                       jnp.where(lane < 2 * HEADS, _dot(tri_u, g3[0]) + _dot(tri_u, g3[1]) + _dot(tri_u, g3[2]),
                                 _sigmoid(ab)))
        gb_ref[bb] = gb[:, 0:N_AB]

        abt = _dot_nt(wabt_ref[0:N_AB, :], xb)
        t3 = split3(-jnp.exp(alogt_ref[...]) * _softplus(abt + dtbt_ref[...]))
        row = lax.broadcasted_iota(jnp.int32, abt.shape, 0)
        gbt_ref[bb] = jnp.where(row < HEADS, _dot(t3[0], tri_u) + _dot(t3[1], tri_u) + _dot(t3[2], tri_u),
                               jnp.where(row < 2 * HEADS, _dot(t3[0], tri_l) + _dot(t3[1], tri_l) + _dot(t3[2], tri_l),
                                         _sigmoid(abt)))

    for bb in range(x_ref.shape[0]):
        one_batch_element(bb)


def _in_proj(x, ctx, mod_lat, mod_ctx, g1, w_main, w_ab, lng, lnb, ws, bst, conv, alog, dtb, alogt, dtbt):
    B, L, _ = x.shape
    n_lat = L // TM_IN
    n_steps = n_lat + 1
    LC = L + TM_IN
    hb = TM_IN // HALO

    def full(shape):
        return pl.BlockSpec(shape, lambda b, j: (0,) * len(shape))

    in_specs = [
        pl.BlockSpec((NB_IN, TM_IN, D), lambda b, j: (b, jnp.maximum(j - 1, 0), 0)),
        pl.BlockSpec((NB_IN, HALO, D), lambda b, j: (b, jnp.clip((j - 1) * hb - 1, 0, L // HALO - 1), 0)),
        pl.BlockSpec((NB_IN, HALO, D), lambda b, j: (b, jnp.clip(j * hb, 0, L // HALO - 1), 0)),
        pl.BlockSpec((NB_IN, TM_IN, D), lambda b, j: (b, 0, 0)),
        pl.BlockSpec((NB_IN, 1, 6 * D), lambda b, j: (b, 0, 0)),
        full((1, 6 * D)), full((1, D)),
        pl.BlockSpec((1, D, N_MAIN), lambda b, j: (0, 0, 0), pipeline_mode=pl.Buffered(1)),
        full((D, 128)),
        full((1, D_A)), full((1, D_A)), full((HEADS, CHUNK, CHUNK)), full((CHUNK, HEADS)),
        full((CONV_W, N_QKV)), full((1, 128)), full((1, 128)), full((N_AB, 1)), full((N_AB, 1)),
        full((TM_IN, TM_IN)), full((TM_IN, TM_IN)),
    ]
    out_shape = [
        jax.ShapeDtypeStruct((B, L, D_A), BF16),
        jax.ShapeDtypeStruct((B, LC, N_QKV), BF16),
        jax.ShapeDtypeStruct((B, LC, D_B), BF16),
        jax.ShapeDtypeStruct((B, LC, N_AB), F32),
        jax.ShapeDtypeStruct((B, N_AB, LC), F32),
    ]
    out_specs = [
        pl.BlockSpec((NB_IN, TM_IN, D_A), lambda b, j: (b, jnp.maximum(j - 1, 0), 0)),
        pl.BlockSpec((NB_IN, TM_IN, N_QKV), lambda b, j: (b, j, 0)),
        pl.BlockSpec((NB_IN, TM_IN, D_B), lambda b, j: (b, j, 0)),
        pl.BlockSpec((NB_IN, TM_IN, N_AB), lambda b, j: (b, j, 0)),
        pl.BlockSpec((NB_IN, N_AB, TM_IN), lambda b, j: (b, 0, j)),
    ]
    return pl.pallas_call(
        _in_kernel,
        out_shape=out_shape,
        grid=(B // NB_IN, n_steps),
        in_specs=in_specs,
        out_specs=out_specs,
        scratch_shapes=[pltpu.VMEM((D, N_MAIN), BF16), pltpu.VMEM((128, D), BF16)],
        compiler_params=pltpu.CompilerParams(dimension_semantics=("arbitrary", "arbitrary"),
                                             vmem_limit_bytes=VMEM_LIMIT),
    )(x, x, x, ctx, mod_lat, mod_ctx, g1, w_main, w_ab, lng, lnb, ws, bst, conv, alog, dtb, alogt, dtbt,
      jnp.asarray(_CHUNK_TRIL, BF16), jnp.asarray(_CHUNK_TRIL.T, BF16))


def _delta_kernel(qf_ref, qb_ref, zf_ref, zb_ref, gf_ref, gbk_ref, gtf_ref, gtb_ref, on_ref,
                  y_ref, s_ref, oacc_ref, *, n_ctx, n_lat):
    s = pl.program_id(1)

    @pl.when(s == 0)
    def _():
        s_ref[...] = jnp.zeros_like(s_ref)
        oacc_ref[...] = jnp.zeros_like(oacc_ref)

    row = lax.broadcasted_iota(jnp.int32, (CHUNK, CHUNK), 0)
    col = lax.broadcasted_iota(jnp.int32, (CHUNK, CHUNK), 1)
    low = row > col
    upp = row < col
    same_blk = (row // 16) == (col // 16)
    eye = jnp.where(row == col, 1.0, 0.0).astype(BF16)
    zero = jnp.zeros((CHUNK, CHUNK), BF16)
    onorm = on_ref[...]
    half = n_ctx + n_lat // 2
    second = s >= half
    g_refs = (gf_ref, gbk_ref)
    gt_refs = (gtf_ref, gtb_ref)
    qkv_refs = (qf_ref, qb_ref)
    z_refs = (zf_ref, zb_ref)
    nb = qf_ref.shape[0]
    ps = range(nb * HEADS)

    def halves(xc, unit):
        xb = xc.astype(BF16)
        fill = eye if unit else zero
        return jnp.where(low, xb, fill), jnp.where(upp, xb, fill)

    def as_lhs(hv):
        return jnp.concatenate(hv, axis=1)

    def as_rhs(*hvs):
        cols_ = [jnp.concatenate(hv, axis=0) for hv in hvs]
        return cols_[0] if len(cols_) == 1 else jnp.concatenate(cols_, axis=1)

    def load(d, p, part):
        bb, h = divmod(p, HEADS)
        return qkv_refs[d][bb, :, part * D_B + h * HD:part * D_B + (h + 1) * HD]

    def gcol(d, p, base):
        bb, h = divmod(p, HEADS)
        c = base + d * HEADS + h
        return g_refs[d][bb, :, c:c + 1]

    def grow(d, p, base):
        bb, h = divmod(p, HEADS)
        r = base + d * HEADS + h
        return gt_refs[d][bb, r:r + 1, :]

    def lanes(col):
        return jnp.broadcast_to(col, (CHUNK, HD))

    q = [[load(d, p, 0) for p in ps] for d in range(2)]
    k = [[load(d, p, 1) for p in ps] for d in range(2)]
    v = [[load(d, p, 2) for p in ps] for d in range(2)]
    gcl = [[lanes(gcol(d, p, 0)) for p in ps] for d in range(2)]
    betal = [[lanes(gcol(d, p, 2 * HEADS)) for p in ps] for d in range(2)]
    gr = [[grow(d, p, 0) for p in ps] for d in range(2)]
    betar = [[grow(d, p, 2 * HEADS) for p in ps] for d in range(2)]
    glast = [[gr[0][p][:, CHUNK - 1:CHUNK] for p in ps], [gr[1][p][:, 0:1] for p in ps]]

    gram = [[_dot_nt(jnp.concatenate([q[d][p], k[d][p]], axis=0), k[d][p]) for p in ps] for d in range(2)]
    dec = [jnp.exp(jnp.where(low, gcl[0][p] - gr[0][p], jnp.where(upp, gcl[1][p] - gr[1][p], 0.0))) for p in ps]
    lc = [jnp.where(low, gram[0][p][CHUNK:] * betar[0][p], jnp.where(upp, gram[1][p][CHUNK:] * betar[1][p], 0.0))
          * dec[p] for p in ps]
    qk = [[jnp.where(upp, 0.0, gram[0][p][:CHUNK] * dec[p]).astype(BF16) for p in ps],
          [jnp.where(low, 0.0, gram[1][p][:CHUNK] * dec[p]).astype(BF16) for p in ps]]

    dg = [jnp.where(same_blk, lc[p], 0.0) for p in ps]
    ob = [lc[p] - dg[p] for p in ps]
    d1h = [halves(dg[p], False) for p in ps]
    d2 = [_dot(as_lhs(d1h[p]), as_rhs(d1h[p])) for p in ps]
    p0s = [-dg[p] for p in ps]
    d2h = [halves(d2[p], False) for p in ps]
    p0h = [halves(p0s[p], True) for p in ps]
    o2 = [_dot(as_lhs(d2h[p]), as_rhs(d2h[p], p0h[p])) for p in ps]
    p1s = [p0s[p] + o2[p][:, CHUNK:] for p in ps]
    d4h = [halves(o2[p][:, :CHUNK], False) for p in ps]
    p1h = [halves(p1s[p], True) for p in ps]
    o3 = [_dot(as_lhs(d4h[p]), as_rhs(d4h[p], p1h[p])) for p in ps]
    p2s = [p1s[p] + o3[p][:, CHUNK:] for p in ps]
    d8h = [halves(o3[p][:, :CHUNK], False) for p in ps]
    p2h = [halves(p2s[p], True) for p in ps]
    p3s = [p2s[p] + _dot(as_lhs(d8h[p]), as_rhs(p2h[p])) for p in ps]
    p3h = [halves(p3s[p], True) for p in ps]
    obh = [halves(ob[p], False) for p in ps]
    n1h = [halves(_dot(as_lhs(p3h[p]), as_rhs(obh[p])), False) for p in ps]
    o6 = [_dot(as_lhs(n1h[p]), as_rhs(n1h[p], p3h[p])) for p in ps]
    r0s = [p3s[p] - o6[p][:, CHUNK:] for p in ps]
    n2h = [halves(o6[p][:, :CHUNK], False) for p in ps]
    r0h = [halves(r0s[p], True) for p in ps]
    o7 = [_dot(as_lhs(n2h[p]), as_rhs(n2h[p], r0h[p])) for p in ps]
    r1s = [r0s[p] + o7[p][:, CHUNK:] for p in ps]
    n4h = [halves(o7[p][:, :CHUNK], False) for p in ps]
    r1h = [halves(r1s[p], True) for p in ps]
    tinv = [halves(r1s[p] + _dot(as_lhs(n4h[p]), as_rhs(r1h[p])), True) for p in ps]

    offs = []
    for d in range(2):
        lat_chunk = (s - n_ctx) if d == 0 else (n_ctx + n_lat - 1 - s)
        off = pl.multiple_of(jnp.clip(lat_chunk, 0, n_lat - 1) * CHUNK, CHUNK)
        sidx = [(p // HEADS * 2 + d) * HEADS + p % HEADS for p in ps]
        egc = [jnp.exp(gcl[d][p]) for p in ps]
        kf = [k[d][p].astype(F32) for p in ps]
        rhs = [jnp.concatenate([v[d][p], (kf[p] * egc[p]).astype(BF16)], axis=1) for p in ps]
        uw = [_dot(tinv[p][d], rhs[p]) for p in ps]
        qd = [q[d][p].astype(F32) * egc[p] for p in ps]
        kd = [(kf[p] * jnp.exp(glast[d][p] - gcl[d][p])).astype(BF16) for p in ps]
        st = [s_ref[sidx[p]] for p in ps]
        a1 = [_dot(jnp.concatenate([uw[p][:, HD:] * betal[d][p], qd[p]], axis=0).astype(BF16), st[p].astype(BF16))
              for p in ps]
        vnew = [(uw[p][:, :HD] * betal[d][p] - a1[p][:CHUNK]).astype(BF16) for p in ps]
        o = [a1[p][CHUNK:] + _dot(qk[d][p], vnew[p]) for p in ps]
        for p in ps:
            s_ref[sidx[p]] = st[p] * jnp.exp(glast[d][p]) + _dot_tn(kd[p], vnew[p])
        for p in ps:
            bb, h = divmod(p, HEADS)
            cols = slice(h * HD, (h + 1) * HD)
            oacc_ref[bb, pl.ds(off, CHUNK), cols] = (
                jnp.where(second, oacc_ref[bb, pl.ds(off, CHUNK), cols], 0.0) + o[p])
        offs.append(off)

    @pl.when(second)
    def _():
        for d in range(2):
            for p in ps:
                bb, h = divmod(p, HEADS)
                cols = slice(h * HD, (h + 1) * HD)
                tot = oacc_ref[bb, pl.ds(offs[d], CHUNK), cols]
                ms = jnp.mean(tot * tot, axis=-1, keepdims=True)
                zz = z_refs[d][bb, :, cols].astype(F32)
                y_ref[bb, pl.ds(offs[d], CHUNK), cols] = (
                    tot * lax.rsqrt(ms + EPS) * onorm * _silu(zz)).astype(BF16)


def _delta(qkv, z, gb, gbt, onorm, L):
    B, LC, _ = qkv.shape
    n_all = LC // CHUNK
    n_lat = L // CHUNK
    n_ctx = n_all - n_lat

    def cf(s):
        return s

    def cb(s):
        return jnp.where(s < n_ctx, n_ctx - 1 - s, n_all + n_ctx - 1 - s)

    in_specs = [
        pl.BlockSpec((NB_DELTA, CHUNK, N_QKV), lambda b, s: (b, cf(s), 0)),
        pl.BlockSpec((NB_DELTA, CHUNK, N_QKV), lambda b, s: (b, cb(s), 0)),
        pl.BlockSpec((NB_DELTA, CHUNK, D_B), lambda b, s: (b, cf(s), 0)),
        pl.BlockSpec((NB_DELTA, CHUNK, D_B), lambda b, s: (b, cb(s), 0)),
        pl.BlockSpec((NB_DELTA, CHUNK, N_AB), lambda b, s: (b, cf(s), 0)),
        pl.BlockSpec((NB_DELTA, CHUNK, N_AB), lambda b, s: (b, cb(s), 0)),
        pl.BlockSpec((NB_DELTA, N_AB, CHUNK), lambda b, s: (b, 0, cf(s))),
        pl.BlockSpec((NB_DELTA, N_AB, CHUNK), lambda b, s: (b, 0, cb(s))),
        pl.BlockSpec((1, HD), lambda b, s: (0, 0)),
    ]
    return pl.pallas_call(
        functools.partial(_delta_kernel, n_ctx=n_ctx, n_lat=n_lat),
        out_shape=jax.ShapeDtypeStruct((B, L, D_B), BF16),
        grid=(B // NB_DELTA, n_all),
        in_specs=in_specs,
        out_specs=pl.BlockSpec((NB_DELTA, L, D_B), lambda b, s: (b, 0, 0)),
        scratch_shapes=[pltpu.VMEM((NB_DELTA * 2 * HEADS, HD, HD), F32), pltpu.VMEM((NB_DELTA, L, D_B), F32)],
        compiler_params=pltpu.CompilerParams(dimension_semantics=("arbitrary", "arbitrary"),
                                             vmem_limit_bytes=VMEM_LIMIT),
    )(qkv, qkv, z, z, gb, gb, gbt, gbt, onorm)


def _out_kernel(x_ref, ya_ref, yb_ref, mod_ref, wo_ref, g2_ref, wrt_ref, brt_ref,
                x2_ref, h_ref, route_ref, cnt_ref, base_ref, wbf_ref):
    i = pl.program_id(0)

    @pl.when(i == 0)
    def _():
        base_ref[...] = jnp.zeros_like(base_ref)
        wbf_ref[...] = wo_ref[...].astype(BF16)

    mod = mod_ref[0]
    gt1 = mod[:, 2 * D:3 * D]
    sh2 = mod[:, 3 * D:4 * D]
    sc2 = mod[:, 4 * D:5 * D]
    scale2 = g2_ref[...] * (1.0 + sc2)
    sub = TM_OUT // OUT_SPLIT
    hbs = []
    for r in range(OUT_SPLIT):
        rows = slice(r * sub, (r + 1) * sub)
        mix = _dot(ya_ref[rows, :], wbf_ref[0:D_A, :]) + _dot(yb_ref[rows, :], wbf_ref[D_A:, :])
        x2 = x_ref[rows, :] + gt1 * mix
        x2_ref[rows, :] = x2
        ms = jnp.mean(x2 * x2, axis=-1, keepdims=True)
        hv = x2 * lax.rsqrt(ms + EPS) * scale2 + sh2
        hbs.append(hv.astype(BF16))
        h_ref[rows, 0:D_PACK] = _pack_bf16_pairs(hv)
    hb = jnp.concatenate(hbs, axis=0)

    lt = _dot_nt(wrt_ref[...], hb) + brt_ref[...]
    gl = [lt[r:r + 1, :] for r in range(N_GROUPS)]
    gmax = jnp.maximum(jnp.maximum(gl[0], gl[1]), jnp.maximum(gl[2], gl[3]))
    gsel = jnp.where(gl[0] == gmax, 0, jnp.where(gl[1] == gmax, 1, jnp.where(gl[2] == gmax, 2, 3)))
    p_g = 1.0 / (jnp.exp(gl[0] - gmax) + jnp.exp(gl[1] - gmax) + jnp.exp(gl[2] - gmax) + jnp.exp(gl[3] - gmax))
    el = []
    for e in range(EPG):
        r = [lt[N_GROUPS + g * EPG + e:N_GROUPS + g * EPG + e + 1, :] for g in range(N_GROUPS)]
        el.append(jnp.where(gsel == 0, r[0], jnp.where(gsel == 1, r[1], jnp.where(gsel == 2, r[2], r[3]))))
    m1 = jnp.maximum(jnp.maximum(el[0], el[1]), jnp.maximum(el[2], el[3]))
    i1 = jnp.where(el[0] == m1, 0, jnp.where(el[1] == m1, 1, jnp.where(el[2] == m1, 2, 3)))
    neg = jnp.float32(-jnp.inf)
    el2 = [jnp.where(i1 == e, neg, el[e]) for e in range(EPG)]
    m2 = jnp.maximum(jnp.maximum(el2[0], el2[1]), jnp.maximum(el2[2], el2[3]))
    i2 = jnp.where(jnp.logical_and(el2[0] == m2, i1 != 0), 0,
                   jnp.where(jnp.logical_and(el2[1] == m2, i1 != 1), 1,
                             jnp.where(jnp.logical_and(el2[2] == m2, i1 != 2), 2, 3)))
    t = jnp.exp(m2 - m1)
    w1 = p_g / (1.0 + t)
    w2 = p_g * t / (1.0 + t)
    first_low = i1 < i2
    ea = jnp.where(first_low, i1, i2)
    eb = jnp.where(first_low, i2, i1)
    w_a = jnp.where(first_low, w1, w2)
    w_b = jnp.where(first_low, w2, w1)
    pair = jnp.where(ea == 0, eb - 1, jnp.where(ea == 1, jnp.where(eb == 3, 3, 4), 5))
    swap = pair == 5
    w_a, w_b = jnp.where(swap, w_b, w_a), jnp.where(swap, w_a, w_b)
    bucket = gsel * N_PAIRS + pair

    tm = bucket.shape[1]
    rows = lax.broadcasted_iota(jnp.int32, (32, tm), 0)
    onehot = jnp.where(rows == bucket, 1.0, 0.0).astype(F32)
    r2 = lax.broadcasted_iota(jnp.int32, (tm, tm), 0)
    c2 = lax.broadcasted_iota(jnp.int32, (tm, tm), 1)
    tri = jnp.where(r2 <= c2, 1.0, 0.0).astype(BF16)
    prefix = _dot(onehot.astype(BF16), tri)
    base = base_ref[:, 0:1]
    rank = jnp.sum(onehot * (prefix - 1.0 + base), axis=0, keepdims=True)
    newbase = base + prefix[:, tm - 1:tm]
    base_ref[...] = jnp.broadcast_to(newbase, base_ref.shape)
    cnt_ref[...] = jnp.broadcast_to(newbase, cnt_ref.shape)
    route = jnp.concatenate([bucket.astype(F32), rank, w_a, w_b, jnp.zeros((4, tm), F32)], axis=0)
    route_ref[0] = route
    route_t = jnp.transpose(jnp.concatenate([route, jnp.zeros((120, tm), F32)], axis=0))
    h_ref[:, D_PACK:D_ROW] = lax.bitcast_convert_type(route_t, jnp.uint32)


def _out_proj(x2d, ya, yb, mod3, w_o, g2, wrt, brt, L):
    T = x2d.shape[0]
    nb = T // TM_OUT
    per_b = L // TM_OUT
    out_shape = [
        jax.ShapeDtypeStruct((T, D), F32),
        jax.ShapeDtypeStruct((T, D_ROW), jnp.uint32),
        jax.ShapeDtypeStruct((nb, 8, TM_OUT), F32),
        jax.ShapeDtypeStruct((32, 128), F32),
    ]
    return pl.pallas_call(
        _out_kernel,
        out_shape=out_shape,
        grid=(nb,),
        in_specs=[
            pl.BlockSpec((TM_OUT, D), lambda i: (i, 0)),
            pl.BlockSpec((TM_OUT, D_A), lambda i: (i, 0)),
            pl.BlockSpec((TM_OUT, D_B), lambda i: (i, 0)),
            pl.BlockSpec((1, 1, 6 * D), lambda i: (i // per_b, 0, 0)),
            pl.BlockSpec((D, D), lambda i: (0, 0), pipeline_mode=pl.Buffered(1)),
            pl.BlockSpec((1, D), lambda i: (0, 0)),
            pl.BlockSpec((32, D), lambda i: (0, 0)),
            pl.BlockSpec((32, 1), lambda i: (0, 0)),
        ],
        out_specs=[
            pl.BlockSpec((TM_OUT, D), lambda i: (i, 0)),
            pl.BlockSpec((TM_OUT, D_ROW), lambda i: (i, 0)),
            pl.BlockSpec((1, 8, TM_OUT), lambda i: (i, 0, 0)),
            pl.BlockSpec((32, 128), lambda i: (0, 0)),
        ],
        scratch_shapes=[pltpu.VMEM((32, 128), F32), pltpu.VMEM((D, D), BF16)],
        compiler_params=pltpu.CompilerParams(dimension_semantics=("arbitrary",),
                                             vmem_limit_bytes=VMEM_LIMIT),
    )(x2d, ya, yb, mod3, w_o, g2, wrt, brt)


def _sc_mesh_and_split(n_rows):
    n_workers = SC_CORES * SC_SUBCORES
    per_worker = n_rows // n_workers
    n_chunks = per_worker // SC_WINDOW
    assert per_worker * n_workers == n_rows and n_chunks * SC_WINDOW == per_worker and n_chunks % 2 == 0
    mesh = plsc.VectorSubcoreMesh(core_axis_name="c", subcore_axis_name="s")
    return mesh, n_workers, per_worker, n_chunks


def _sc_two_buffer_loop(n_chunks, first, second):
    first(0, 0).start()

    @pl.loop(0, n_chunks, step=2)
    def _(j):
        for b in range(2):
            jj = j + b
            first(jj, b).wait()

            @pl.when(jj + 1 < n_chunks)
            def _():
                @pl.when(jj >= 1)
                def _():
                    second(jj - 1, 1 - b).wait()
                first(jj + 1, 1 - b).start()

            second(jj, b).start()

    second(n_chunks - 2, 0).wait()
    second(n_chunks - 1, 1).wait()


def _sc_row_gather(table, idx):
    n_rows = idx.shape[0]
    width = table.shape[1]
    mesh, n_workers, per_worker, n_chunks = _sc_mesh_and_split(n_rows)

    @functools.partial(
        pl.kernel, mesh=mesh,
        out_type=jax.ShapeDtypeStruct((n_rows, width), table.dtype),
        scratch_types=[pltpu.VMEM((n_chunks, SC_WINDOW), jnp.int32),
                       pltpu.VMEM((2, SC_WINDOW, width), table.dtype),
                       pltpu.SemaphoreType.DMA((2,)),
                       pltpu.SemaphoreType.DMA((2,))],
    )
    def gather(table_hbm, idx_hbm, out_hbm, idx_v, rows_v, sem_in, sem_out):
        wid = lax.axis_index("s") * SC_CORES + lax.axis_index("c")
        base = wid * per_worker
        pltpu.sync_copy(idx_hbm.at[wid], idx_v)

        def fetch(j, b):
            return pltpu.make_async_copy(table_hbm.at[idx_v.at[j]], rows_v.at[b], sem_in.at[b])

        def put(j, b):
            off = pl.multiple_of(base + j * SC_WINDOW, 8)
            return pltpu.make_async_copy(rows_v.at[b], out_hbm.at[pl.ds(off, SC_WINDOW)], sem_out.at[b])

        _sc_two_buffer_loop(n_chunks, fetch, put)

    return gather(table, idx.reshape(n_workers, n_chunks, SC_WINDOW))


def _sc_row_scatter(rows, pos, n_out):
    n_rows, width = rows.shape
    mesh, n_workers, per_worker, n_chunks = _sc_mesh_and_split(n_rows)

    @functools.partial(
        pl.kernel, mesh=mesh,
        out_type=jax.ShapeDtypeStruct((n_out, width), rows.dtype),
        scratch_types=[pltpu.VMEM((n_chunks, SC_WINDOW), jnp.int32),
                       pltpu.VMEM((2, SC_WINDOW, width), rows.dtype),
                       pltpu.SemaphoreType.DMA((2,)),
                       pltpu.SemaphoreType.DMA((2,))],
    )
    def scatter(rows_hbm, pos_hbm, out_hbm, idx_v, rows_v, sem_in, sem_out):
        wid = lax.axis_index("s") * SC_CORES + lax.axis_index("c")
        base = wid * per_worker
        pltpu.sync_copy(pos_hbm.at[wid], idx_v)

        def fetch(j, b):
            off = pl.multiple_of(base + j * SC_WINDOW, 8)
            return pltpu.make_async_copy(rows_hbm.at[pl.ds(off, SC_WINDOW)], rows_v.at[b], sem_in.at[b])

        def put(j, b):
            return pltpu.make_async_copy(rows_v.at[b], out_hbm.at[idx_v.at[j]], sem_out.at[b])

        _sc_two_buffer_loop(n_chunks, fetch, put)

    return scatter(rows, pos.reshape(n_workers, n_chunks, SC_WINDOW))


def _moe_kernel(ea_ref, eb_ref, nv_ref, hs_ref, wga_ref, wgb_ref, wda_ref, wdb_ref, o_ref, wg_s, wd_s):
    i = pl.program_id(0)
    prev = jnp.maximum(i - 1, 0)

    @pl.when(jnp.logical_or(i == 0, ea_ref[i] != ea_ref[prev]))
    def _():
        wg_s[0] = wga_ref[0].astype(BF16)
        wd_s[0] = wda_ref[0].astype(BF16)

    @pl.when(jnp.logical_or(i == 0, eb_ref[i] != eb_ref[prev]))
    def _():
        wg_s[1] = wgb_ref[0].astype(BF16)
        wd_s[1] = wdb_ref[0].astype(BF16)

    @pl.when(nv_ref[i] > 0)
    def _():
        live = lax.broadcasted_iota(jnp.int32, (BM_MOE, 1), 0) < nv_ref[i]
        xb = _unpack_bf16_pairs(jnp.where(live, hs_ref[:, 0:D_PACK], jnp.uint32(0)))
        w_ab = lax.bitcast_convert_type(hs_ref[:, D_PACK:D_ROW], F32)
        w_a = jnp.where(live, w_ab[:, 2:3], 0.0)
        w_b = jnp.where(live, w_ab[:, 3:4], 0.0)
        ga = _dot(xb, wg_s[0])
        act_a = (_silu(ga[:, :D_EXPERT]) * ga[:, D_EXPERT:] * w_a).astype(BF16)
        gb = _dot(xb, wg_s[1])
        act_b = (_silu(gb[:, :D_EXPERT]) * gb[:, D_EXPERT:] * w_b).astype(BF16)
        o_ref[...] = _pack_bf16_pairs(_dot(act_a, wd_s[0]) + _dot(act_b, wd_s[1]))

    @pl.when(nv_ref[i] <= 0)
    def _():
        o_ref[...] = jnp.zeros_like(o_ref)


def _moe(ea, eb, nvalid, hs, w_gu, w_dn):
    nblk = ea.shape[0]
    S = nblk * BM_MOE
    grid_spec = pltpu.PrefetchScalarGridSpec(
        num_scalar_prefetch=3,
        grid=(nblk,),
        in_specs=[
            pl.BlockSpec((BM_MOE, D_ROW), lambda i, ea, eb, nv: (i, 0)),
            pl.BlockSpec((1, D, 2 * D_EXPERT), lambda i, ea, eb, nv: (ea[i], 0, 0)),
            pl.BlockSpec((1, D, 2 * D_EXPERT), lambda i, ea, eb, nv: (eb[i], 0, 0)),
            pl.BlockSpec((1, D_EXPERT, D), lambda i, ea, eb, nv: (ea[i], 0, 0)),
            pl.BlockSpec((1, D_EXPERT, D), lambda i, ea, eb, nv: (eb[i], 0, 0)),
        ],
        out_specs=pl.BlockSpec((BM_MOE, D_PACK), lambda i, ea, eb, nv: (i, 0)),
        scratch_shapes=[pltpu.VMEM((2, D, 2 * D_EXPERT), BF16), pltpu.VMEM((2, D_EXPERT, D), BF16)],
    )
    return pl.pallas_call(
        _moe_kernel,
        out_shape=jax.ShapeDtypeStruct((S, D_PACK), jnp.uint32),
        grid_spec=grid_spec,
        compiler_params=pltpu.CompilerParams(dimension_semantics=("arbitrary",),
                                             vmem_limit_bytes=VMEM_LIMIT),
    )(ea, eb, nvalid, hs, w_gu, w_gu, w_dn, w_dn)


def _final_kernel(x2_ref, m_ref, mod_ref, fg_ref, *rest):
    o_ref = rest[-1]
    gt2 = mod_ref[0][:, 5 * D:6 * D]
    y = x2_ref[...] + gt2 * _unpack_bf16_pairs(m_ref[...]).astype(F32)
    ms = jnp.mean(y * y, axis=-1, keepdims=True)
    o_ref[...] = y * lax.rsqrt(ms + EPS) * fg_ref[...]


def _final(x2, m_part, mod3, fg, L, part, prev_out):
    T = x2.shape[0]
    steps = m_part.shape[0] // TM_FIN
    off = part * steps
    per_b = L // TM_FIN
    in_specs = [
        pl.BlockSpec((TM_FIN, D), lambda i: (i + off, 0)),
        pl.BlockSpec((TM_FIN, D_PACK), lambda i: (i, 0)),
        pl.BlockSpec((1, 1, 6 * D), lambda i: ((i + off) // per_b, 0, 0)),
        pl.BlockSpec((1, D), lambda i: (0, 0)),
    ]
    args = [x2, m_part, mod3, fg]
    aliases = {}
    if prev_out is not None:
        in_specs.append(pl.BlockSpec(memory_space=pl.ANY))
        args.append(prev_out)
        aliases = {len(args) - 1: 0}
    return pl.pallas_call(
        _final_kernel,
        out_shape=jax.ShapeDtypeStruct((T, D), F32),
        grid=(steps,),
        in_specs=in_specs,
        out_specs=pl.BlockSpec((TM_FIN, D), lambda i: (i + off, 0)),
        input_output_aliases=aliases,
        compiler_params=pltpu.CompilerParams(dimension_semantics=("arbitrary",),
                                             vmem_limit_bytes=VMEM_LIMIT),
    )(*args)


def kernel(x, c, ctx, c_ctx, w_ada, b_ada, norm1_g, w_in, ln_a_g, ln_a_b, w_spatial, b_spatial, conv_qkv, a_log,
           dt_bias, onorm_g, w_out, norm2_g, w_group, b_group, w_router, b_router, w_gate_up, w_down, final_g):
    B, L, _ = x.shape
    T = B * L
    assert w_ada.shape[0] == 1 and ctx.shape[1] == TM_IN and L % TM_OUT == 0 and T % (N_COMBINE_PARTS * TM_FIN) == 0

    cond = jnp.concatenate([c, c_ctx[None, :], jnp.zeros((7, D), F32)], axis=0)
    mod = _modulation(cond, w_ada[0], b_ada[0][None, :])
    mod_lat = mod[:B].reshape(B, 1, 6 * D)
    mod_ctx = mod[B:B + 1]

    w_ab = jnp.pad(w_in[0][:, N_MAIN:], ((0, 0), (0, 128 - N_AB))).astype(BF16)
    alog = a_log[0].reshape(1, 2 * HEADS)
    dtb = dt_bias[0].reshape(1, 2 * HEADS)
    alog_row = jnp.pad(alog, ((0, 0), (0, 128 - 2 * HEADS)))
    dtb_row = jnp.pad(dtb, ((0, 0), (0, 128 - 2 * HEADS)))
    alog_col = jnp.pad(alog, ((0, 0), (0, N_AB - 2 * HEADS))).T
    dtb_col = jnp.pad(dtb, ((0, 0), (0, N_AB - 2 * HEADS))).T

    ya, qkv, z, gb, gbt = _in_proj(
        x, ctx, mod_lat, mod_ctx, norm1_g, w_in, w_ab, ln_a_g, ln_a_b,
        w_spatial[0].astype(BF16), b_spatial[0].T, conv_qkv[0], alog_row, dtb_row, alog_col, dtb_col)

    yb = _delta(qkv, z, gb, gbt, onorm_g, L)

    wrt = jnp.concatenate([w_group[0].T, w_router[0].T, jnp.zeros((32 - N_GROUPS - N_EXPERTS, D), F32)], axis=0)
    brt = jnp.concatenate([b_group[0], b_router[0], jnp.zeros((32 - N_GROUPS - N_EXPERTS,), F32)])[:, None]
    x2, h, route, cnt = _out_proj(x.reshape(T, D), ya.reshape(T, D_A), yb.reshape(T, D_B), mod_lat,
                                  w_out[0], norm2_g, wrt.astype(BF16), brt, L)

    bucket = route[:, 0, :].reshape(T).astype(jnp.int32)
    rank = route[:, 1, :].reshape(T).astype(jnp.int32)
    counts = cnt[:N_BUCKETS, 0].astype(jnp.int32)
    nblk_b = (counts + BM_MOE - 1) // BM_MOE
    blk_end = jnp.cumsum(nblk_b)
    blk_start = blk_end - nblk_b
    kk = jnp.arange(N_BUCKETS, dtype=jnp.int32)

    def pick(idx, table):
        return jnp.sum(jnp.where(idx[:, None] == kk[None, :], table[None, :], 0), axis=1)

    pos = pick(bucket, blk_start) * BM_MOE + rank
    n_blocks = T // BM_MOE + N_BUCKETS
    S = n_blocks * BM_MOE
    blk = jnp.arange(n_blocks, dtype=jnp.int32)
    used = blk < blk_end[-1]
    bkt = jnp.sum((jnp.minimum(blk, blk_end[-1] - 1)[:, None] >= blk_end[None, :]).astype(jnp.int32), axis=1)
    nvalid = jnp.where(used, jnp.clip(pick(bkt, counts) - (blk - pick(bkt, blk_start)) * BM_MOE, 0, BM_MOE),
                       0).astype(jnp.int32)
    ea = pick(bkt, jnp.asarray(_SLOT_A_EXPERT))
    eb = pick(bkt, jnp.asarray(_SLOT_B_EXPERT))

    hs = _sc_row_scatter(h, pos, S)
    ms = _moe(ea, eb, nvalid, hs, w_gate_up[0], w_down[0])
    pos_parts = pos.reshape(N_COMBINE_PARTS, T // N_COMBINE_PARTS)
    m_parts = [_sc_row_gather(ms, pos_parts[q]) for q in range(N_COMBINE_PARTS)]
    out = None
    for q in range(N_COMBINE_PARTS):
        out = _final(x2, m_parts[q], mod_lat, final_g[None, :], L, q, out)
    return out.reshape(B, L, D)
```

```python
import functools

import jax
import jax.numpy as jnp
import numpy as np
from jax import lax
from jax.experimental import pallas as pl
from jax.experimental.pallas import tpu as pltpu
from jax.experimental.pallas import tpu_sc as plsc

F32 = jnp.float32
BF16 = jnp.bfloat16
EPS = 1e-6

D = 1024
D_A = 512
D_B = 512
HEADS = 4
HD = 128
CHUNK = 128
CONV_W = 5
N_QKV = 3 * D_B
N_MAIN = 2 * D_A + 4 * D_B
N_AB = 16
N_GROUPS = 4
EPG = 4
N_EXPERTS = 16
D_EXPERT = 512
N_PAIRS = 6
N_BUCKETS = N_GROUPS * N_PAIRS
D_PACK = D // 2
D_ROW = D_PACK + 128
PAIR_A = (0, 0, 0, 1, 1, 3)
PAIR_B = (1, 2, 3, 3, 2, 2)
_SLOT_A_EXPERT = np.array([g * EPG + PAIR_A[p] for g in range(N_GROUPS) for p in range(N_PAIRS)], np.int32)
_SLOT_B_EXPERT = np.array([g * EPG + PAIR_B[p] for g in range(N_GROUPS) for p in range(N_PAIRS)], np.int32)

TM_IN = 256
HALO = 8
NB_IN = 2
NB_DELTA = 2
TM_OUT = 512
OUT_SPLIT = 2
BM_MOE = 256
TM_FIN = 512
N_COMBINE_PARTS = 4
VMEM_LIMIT = 56 * 1024 * 1024
SC_CORES = 2
SC_SUBCORES = 16
SC_WINDOW = 32

HI = lax.Precision.HIGHEST

_CHUNK_TRIL = np.kron(np.eye(TM_IN // CHUNK, dtype=np.float32), np.tril(np.ones((CHUNK, CHUNK), np.float32)))


def _dot(a, b, precision=None):
    return jnp.dot(a, b, preferred_element_type=F32, precision=precision)


def _dot_nt(a, b):
    return lax.dot_general(a, b, (((1,), (1,)), ((), ())), preferred_element_type=F32)


def _dot_tn(a, b):
    return lax.dot_general(a, b, (((0,), (0,)), ((), ())), preferred_element_type=F32)


def _sigmoid(x):
    return 0.5 + 0.5 * jnp.tanh(0.5 * x)


def _silu(x):
    h = 0.5 * x
    return h + h * jnp.tanh(h)


def _softplus(x):
    return jnp.maximum(x, 0.0) + jnp.log(1.0 + jnp.exp(-jnp.abs(x)))


def _pack_bf16_pairs(x):
    bits = lax.bitcast_convert_type(x.astype(BF16).astype(F32), jnp.uint32)
    return (bits[:, D_PACK:] & jnp.uint32(0xFFFF0000)) | (bits[:, :D_PACK] >> 16)


def _unpack_bf16_pairs(w):
    bits = w
    lo = lax.bitcast_convert_type(bits << 16, F32)
    hi = lax.bitcast_convert_type(bits & jnp.uint32(0xFFFF0000), F32)
    return jnp.concatenate([lo, hi], axis=1).astype(BF16)


def _gelu_tanh(x):
    return 0.5 * x * (1.0 + jnp.tanh(np.sqrt(2.0 / np.pi).astype(np.float32) * (x + 0.044715 * (x * x * x))))


def _mod_kernel(c_ref, w_ref, b_ref, o_ref):
    c = c_ref[...]
    o_ref[...] = _dot(_silu(c), w_ref[...], precision=HI) + b_ref[...]


def _modulation(cond, w_ada, b_ada):
    rows = cond.shape[0]
    tn = 1536
    return pl.pallas_call(
        _mod_kernel,
        out_shape=jax.ShapeDtypeStruct((rows, 6 * D), F32),
        grid=(6 * D // tn,),
        in_specs=[pl.BlockSpec((rows, D), lambda i: (0, 0)),
                  pl.BlockSpec((D, tn), lambda i: (0, i)),
                  pl.BlockSpec((1, tn), lambda i: (0, i))],
        out_specs=pl.BlockSpec((rows, tn), lambda i: (0, i)),
        compiler_params=pltpu.CompilerParams(dimension_semantics=("arbitrary",),
                                             vmem_limit_bytes=VMEM_LIMIT),
    )(cond, w_ada, b_ada)


def _in_kernel(x_ref, xp_ref, xn_ref, ctx_ref, mod_ref, cmod_ref, g1_ref, wt_ref,
               lng_ref, lnb_ref, ws_ref, bst_ref, conv_ref, alog_ref, dtb_ref, alogt_ref, dtbt_ref, tril_ref, triu_ref,
               ya_ref, qkv_ref, z_ref, gb_ref, gbt_ref, wbf_ref, wab_ref, wabt_ref):
    j = pl.program_id(1)

    @pl.when(jnp.logical_and(pl.program_id(0) == 0, j == 0))
    def _():
        for c0 in range(0, N_MAIN, 512):
            wbf_ref[:, c0:c0 + 512] = jnp.transpose(wt_ref[0, c0:c0 + 512, :]).astype(BF16)
        tail = jnp.concatenate([wt_ref[0, N_MAIN:N_MAIN + N_AB, :], jnp.zeros((128 - N_AB, D), F32)], axis=0)
        wabt_ref[...] = tail.astype(BF16)
        wab_ref[...] = jnp.transpose(tail).astype(BF16)

    is_ctx = j == 0
    n_lat_blocks = pl.num_programs(1) - 1

    def one_batch_element(bb):
        mod = mod_ref[bb]
        cm = cmod_ref[...]
        sh = jnp.where(is_ctx, cm[:, 0:D], mod[:, 0:D])
        sc = jnp.where(is_ctx, cm[:, D:2 * D], mod[:, D:2 * D])
        scale = g1_ref[...] * (1.0 + sc)

        xmain = jnp.where(is_ctx, ctx_ref[bb], x_ref[bb])
        xv = jnp.concatenate([xp_ref[bb], xmain, xn_ref[bb]], axis=0)
        xnorm = xv * lax.rsqrt(jnp.mean(xv * xv, axis=-1, keepdims=True) + EPS) * scale + sh
        xe = xnorm.astype(BF16)
        xb = xnorm[HALO:HALO + TM_IN].astype(BF16)

        rid = lax.broadcasted_iota(jnp.int32, (TM_IN + 2 * HALO, 1), 0)
        prev_ok = j >= 2
        next_ok = jnp.logical_and(j >= 1, j < n_lat_blocks)
        valid = jnp.logical_or(jnp.logical_and(rid >= HALO, rid < HALO + TM_IN),
                               jnp.logical_or(jnp.logical_and(rid < HALO, prev_ok),
                                              jnp.logical_and(rid >= HALO + TM_IN, next_ok)))
        pad = (CONV_W - 1) // 2
        c_qkv = 2 * D_A

        def proj(c0, width):
            return _dot(xe, wbf_ref[:, c0:c0 + width])

        def conv_act(pq, c0):
            groups = (TM_IN + 2 * HALO) // 8
            x3 = jnp.where(valid, pq, 0.0).reshape(groups, 8, D_B)
            sub = lax.broadcasted_iota(jnp.int32, (1, 8, 1), 1)
            lo, hi = HALO // 8, HALO // 8 + TM_IN // 8
            acc = conv_ref[pad:pad + 1, c0:c0 + D_B] * x3[lo:hi]
            for t in range(CONV_W):
                s = t - pad
                if s == 0:
                    continue
                r = pltpu.roll(x3, (-s) % 8, axis=1)
                if s > 0:
                    sh = jnp.where(sub < 8 - s, r[lo:hi], r[lo + 1:hi + 1])
                else:
                    sh = jnp.where(sub >= -s, r[lo:hi], r[lo - 1:hi - 1])
                acc = acc + conv_ref[t:t + 1, c0:c0 + D_B] * sh
            return _silu(acc.reshape(TM_IN, D_B))

        def store_unit_heads(act, c0, gain):
            for h in range(HEADS):
                t = act[:, h * HD:(h + 1) * HD]
                nrm = lax.rsqrt(jnp.sum(t * t, axis=-1, keepdims=True) + EPS) * gain
                qkv_ref[bb, :, c0 + h * HD:c0 + (h + 1) * HD] = (t * nrm).astype(BF16)

        pq_q = proj(c_qkv, D_B)
        pq_k = proj(c_qkv + D_B, D_B)
        store_unit_heads(conv_act(pq_q, 0), 0, HD ** -0.5)
        pq_v = proj(c_qkv + 2 * D_B, D_B)
        store_unit_heads(conv_act(pq_k, D_B), D_B, 1.0)
        pa_u = proj(0, D_A)[HALO:HALO + TM_IN]
        qkv_ref[bb, :, 2 * D_B:] = conv_act(pq_v, 2 * D_B).astype(BF16)
        pa_v = proj(D_A, D_A)[HALO:HALO + TM_IN]
        u = _gelu_tanh(pa_u)
        pz = proj(c_qkv + N_QKV, D_B)
        v = _gelu_tanh(pa_v)
        mu = jnp.mean(v, axis=-1, keepdims=True)
        vc = v - mu
        var = jnp.mean(vc * vc, axis=-1, keepdims=True)
        vn = (vc * lax.rsqrt(var + EPS) * lng_ref[...] + lnb_ref[...]).astype(BF16)
        z_ref[bb] = pz[HALO:HALO + TM_IN].astype(BF16)

        bst = bst_ref[...]
        for n in range(TM_IN // CHUNK):
            rows = slice(n * CHUNK, (n + 1) * CHUNK)
            for h in range(HEADS):
                cols = slice(h * HD, (h + 1) * HD)
                s = _dot(ws_ref[h], vn[rows, cols]) + bst[:, h:h + 1]
                ya_ref[bb, rows, cols] = (u[rows, cols] * s).astype(BF16)

        tri_l = tril_ref[...]
        tri_u = triu_ref[...]

        def split3(g):
            hi = g.astype(BF16)
            r1 = g - hi.astype(F32)
            mid = r1.astype(BF16)
            return hi, mid, (r1 - mid.astype(F32)).astype(BF16)

        ab = _dot(xb, wab_ref[...])
        g3 = split3(-jnp.exp(alog_ref[...]) * _softplus(ab + dtb_ref[...]))
        lane = lax.broadcasted_iota(jnp.int32, ab.shape, 1)
        gb = jnp.where(lane < HEADS, _dot(tri_l, g3[0]) + _dot(tri_l, g3[1]) + _dot(tri_l, g3[2]),
                       jnp.where(lane < 2 * HEADS, _dot(tri_u, g3[0]) + _dot(tri_u, g3[1]) + _dot(tri_u, g3[2]),
                                 _sigmoid(ab)))
        gb_ref[bb] = gb[:, 0:N_AB]

        abt = _dot_nt(wabt_ref[0:N_AB, :], xb)
        t3 = split3(-jnp.exp(alogt_ref[...]) * _softplus(abt + dtbt_ref[...]))
        row = lax.broadcasted_iota(jnp.int32, abt.shape, 0)
        gbt_ref[bb] = jnp.where(row < HEADS, _dot(t3[0], tri_u) + _dot(t3[1], tri_u) + _dot(t3[2], tri_u),
                               jnp.where(row < 2 * HEADS, _dot(t3[0], tri_l) + _dot(t3[1], tri_l) + _dot(t3[2], tri_l),
                                         _sigmoid(abt)))

    for bb in range(x_ref.shape[0]):
        one_batch_element(bb)


def _in_proj(x, ctx, mod_lat, mod_ctx, g1, w_in_t, lng, lnb, ws, bst, conv, alog, dtb, alogt, dtbt):
    B, L, _ = x.shape
    n_lat = L // TM_IN
    n_steps = n_lat + 1
    LC = L + TM_IN
    hb = TM_IN // HALO

    def full(shape):
        return pl.BlockSpec(shape, lambda b, j: (0,) * len(shape))

    in_specs = [
        pl.BlockSpec((NB_IN, TM_IN, D), lambda b, j: (b, jnp.maximum(j - 1, 0), 0)),
        pl.BlockSpec((NB_IN, HALO, D), lambda b, j: (b, jnp.clip((j - 1) * hb - 1, 0, L // HALO - 1), 0)),
        pl.BlockSpec((NB_IN, HALO, D), lambda b, j: (b, jnp.clip(j * hb, 0, L // HALO - 1), 0)),
        pl.BlockSpec((NB_IN, TM_IN, D), lambda b, j: (b, 0, 0)),
        pl.BlockSpec((NB_IN, 1, 6 * D), lambda b, j: (b, 0, 0)),
        full((1, 6 * D)), full((1, D)),
        pl.BlockSpec((1, N_MAIN + N_AB, D), lambda b, j: (0, 0, 0), pipeline_mode=pl.Buffered(1)),
        full((1, D_A)), full((1, D_A)), full((HEADS, CHUNK, CHUNK)), full((CHUNK, HEADS)),
        full((CONV_W, N_QKV)), full((1, 128)), full((1, 128)), full((N_AB, 1)), full((N_AB, 1)),
        full((TM_IN, TM_IN)), full((TM_IN, TM_IN)),
    ]
    out_shape = [
        jax.ShapeDtypeStruct((B, L, D_A), BF16),
        jax.ShapeDtypeStruct((B, LC, N_QKV), BF16),
        jax.ShapeDtypeStruct((B, LC, D_B), BF16),
        jax.ShapeDtypeStruct((B, LC, N_AB), F32),
        jax.ShapeDtypeStruct((B, N_AB, LC), F32),
    ]
    out_specs = [
        pl.BlockSpec((NB_IN, TM_IN, D_A), lambda b, j: (b, jnp.maximum(j - 1, 0), 0)),
        pl.BlockSpec((NB_IN, TM_IN, N_QKV), lambda b, j: (b, j, 0)),
        pl.BlockSpec((NB_IN, TM_IN, D_B), lambda b, j: (b, j, 0)),
        pl.BlockSpec((NB_IN, TM_IN, N_AB), lambda b, j: (b, j, 0)),
        pl.BlockSpec((NB_IN, N_AB, TM_IN), lambda b, j: (b, 0, j)),
    ]
    return pl.pallas_call(
        _in_kernel,
        out_shape=out_shape,
        grid=(B // NB_IN, n_steps),
        in_specs=in_specs,
        out_specs=out_specs,
        scratch_shapes=[pltpu.VMEM((D, N_MAIN), BF16), pltpu.VMEM((D, 128), BF16), pltpu.VMEM((128, D), BF16)],
        compiler_params=pltpu.CompilerParams(dimension_semantics=("arbitrary", "arbitrary"),
                                             vmem_limit_bytes=VMEM_LIMIT),
    )(x, x, x, ctx, mod_lat, mod_ctx, g1, w_in_t, lng, lnb, ws, bst, conv, alog, dtb, alogt, dtbt,
      jnp.asarray(_CHUNK_TRIL, BF16), jnp.asarray(_CHUNK_TRIL.T, BF16))


def _delta_kernel(qf_ref, qb_ref, zf_ref, zb_ref, gf_ref, gbk_ref, gtf_ref, gtb_ref, on_ref,
                  y_ref, s_ref, oacc_ref, *, n_ctx, n_lat):
    s = pl.program_id(1)

    @pl.when(s == 0)
    def _():
        s_ref[...] = jnp.zeros_like(s_ref)
        oacc_ref[...] = jnp.zeros_like(oacc_ref)

    row = lax.broadcasted_iota(jnp.int32, (CHUNK, CHUNK), 0)
    col = lax.broadcasted_iota(jnp.int32, (CHUNK, CHUNK), 1)
    low = row > col
    upp = row < col
    same_blk = (row // 16) == (col // 16)
    eye = jnp.where(row == col, 1.0, 0.0).astype(BF16)
    zero = jnp.zeros((CHUNK, CHUNK), BF16)
    onorm = on_ref[...]
    half = n_ctx + n_lat // 2
    second = s >= half
    g_refs = (gf_ref, gbk_ref)
    gt_refs = (gtf_ref, gtb_ref)
    qkv_refs = (qf_ref, qb_ref)
    z_refs = (zf_ref, zb_ref)
    nb = qf_ref.shape[0]
    ps = range(nb * HEADS)

    def halves(xc, unit):
        xb = xc.astype(BF16)
        fill = eye if unit else zero
        return jnp.where(low, xb, fill), jnp.where(upp, xb, fill)

    def as_lhs(hv):
        return jnp.concatenate(hv, axis=1)

    def as_rhs(*hvs):
        cols_ = [jnp.concatenate(hv, axis=0) for hv in hvs]
        return cols_[0] if len(cols_) == 1 else jnp.concatenate(cols_, axis=1)

    def load(d, p, part):
        bb, h = divmod(p, HEADS)
        return qkv_refs[d][bb, :, part * D_B + h * HD:part * D_B + (h + 1) * HD]

    def gcol(d, p, base):
        bb, h = divmod(p, HEADS)
        c = base + d * HEADS + h
        return g_refs[d][bb, :, c:c + 1]

    def grow(d, p, base):
        bb, h = divmod(p, HEADS)
        r = base + d * HEADS + h
        return gt_refs[d][bb, r:r + 1, :]

    def lanes(col):
        return jnp.broadcast_to(col, (CHUNK, HD))

    q = [[load(d, p, 0) for p in ps] for d in range(2)]
    k = [[load(d, p, 1) for p in ps] for d in range(2)]
    v = [[load(d, p, 2) for p in ps] for d in range(2)]
    gcl = [[lanes(gcol(d, p, 0)) for p in ps] for d in range(2)]
    betal = [[lanes(gcol(d, p, 2 * HEADS)) for p in ps] for d in range(2)]
    gr = [[grow(d, p, 0) for p in ps] for d in range(2)]
    betar = [[grow(d, p, 2 * HEADS) for p in ps] for d in range(2)]
    glast = [[gr[0][p][:, CHUNK - 1:CHUNK] for p in ps], [gr[1][p][:, 0:1] for p in ps]]

    gram = [[_dot_nt(jnp.concatenate([q[d][p], k[d][p]], axis=0), k[d][p]) for p in ps] for d in range(2)]
    dec = [jnp.exp(jnp.where(low, gcl[0][p] - gr[0][p], jnp.where(upp, gcl[1][p] - gr[1][p], 0.0))) for p in ps]
    lc = [jnp.where(low, gram[0][p][CHUNK:] * betar[0][p], jnp.where(upp, gram[1][p][CHUNK:] * betar[1][p], 0.0))
          * dec[p] for p in ps]
    qk = [[jnp.where(upp, 0.0, gram[0][p][:CHUNK] * dec[p]).astype(BF16) for p in ps],
          [jnp.where(low, 0.0, gram[1][p][:CHUNK] * dec[p]).astype(BF16) for p in ps]]

    dg = [jnp.where(same_blk, lc[p], 0.0) for p in ps]
    ob = [lc[p] - dg[p] for p in ps]
    d1h = [halves(dg[p], False) for p in ps]
    d2 = [_dot(as_lhs(d1h[p]), as_rhs(d1h[p])) for p in ps]
    p0s = [-dg[p] for p in ps]
    d2h = [halves(d2[p], False) for p in ps]
    p0h = [halves(p0s[p], True) for p in ps]
    o2 = [_dot(as_lhs(d2h[p]), as_rhs(d2h[p], p0h[p])) for p in ps]
    p1s = [p0s[p] + o2[p][:, CHUNK:] for p in ps]
    d4h = [halves(o2[p][:, :CHUNK], False) for p in ps]
    p1h = [halves(p1s[p], True) for p in ps]
    o3 = [_dot(as_lhs(d4h[p]), as_rhs(d4h[p], p1h[p])) for p in ps]
    p2s = [p1s[p] + o3[p][:, CHUNK:] for p in ps]
    d8h = [halves(o3[p][:, :CHUNK], False) for p in ps]
    p2h = [halves(p2s[p], True) for p in ps]
    p3s = [p2s[p] + _dot(as_lhs(d8h[p]), as_rhs(p2h[p])) for p in ps]
    p3h = [halves(p3s[p], True) for p in ps]
    obh = [halves(ob[p], False) for p in ps]
    n1h = [halves(_dot(as_lhs(p3h[p]), as_rhs(obh[p])), False) for p in ps]
    o6 = [_dot(as_lhs(n1h[p]), as_rhs(n1h[p], p3h[p])) for p in ps]
    r0s = [p3s[p] - o6[p][:, CHUNK:] for p in ps]
    n2h = [halves(o6[p][:, :CHUNK], False) for p in ps]
    r0h = [halves(r0s[p], True) for p in ps]
    o7 = [_dot(as_lhs(n2h[p]), as_rhs(n2h[p], r0h[p])) for p in ps]
    r1s = [r0s[p] + o7[p][:, CHUNK:] for p in ps]
    n4h = [halves(o7[p][:, :CHUNK], False) for p in ps]
    r1h = [halves(r1s[p], True) for p in ps]
    tinv = [halves(r1s[p] + _dot(as_lhs(n4h[p]), as_rhs(r1h[p])), True) for p in ps]

    offs = []
    for d in range(2):
        lat_chunk = (s - n_ctx) if d == 0 else (n_ctx + n_lat - 1 - s)
        off = pl.multiple_of(jnp.clip(lat_chunk, 0, n_lat - 1) * CHUNK, CHUNK)
        sidx = [(p // HEADS * 2 + d) * HEADS + p % HEADS for p in ps]
        egc = [jnp.exp(gcl[d][p]) for p in ps]
        kf = [k[d][p].astype(F32) for p in ps]
        rhs = [jnp.concatenate([v[d][p], (kf[p] * egc[p]).astype(BF16)], axis=1) for p in ps]
        uw = [_dot(tinv[p][d], rhs[p]) for p in ps]
        qd = [q[d][p].astype(F32) * egc[p] for p in ps]
        kd = [(kf[p] * jnp.exp(glast[d][p] - gcl[d][p])).astype(BF16) for p in ps]
        st = [s_ref[sidx[p]] for p in ps]
        a1 = [_dot(jnp.concatenate([uw[p][:, HD:] * betal[d][p], qd[p]], axis=0).astype(BF16), st[p].astype(BF16))
              for p in ps]
        vnew = [(uw[p][:, :HD] * betal[d][p] - a1[p][:CHUNK]).astype(BF16) for p in ps]
        o = [a1[p][CHUNK:] + _dot(qk[d][p], vnew[p]) for p in ps]
        for p in ps:
            s_ref[sidx[p]] = st[p] * jnp.exp(glast[d][p]) + _dot_tn(kd[p], vnew[p])
        for p in ps:
            bb, h = divmod(p, HEADS)
            cols = slice(h * HD, (h + 1) * HD)
            oacc_ref[bb, pl.ds(off, CHUNK), cols] = (
                jnp.where(second, oacc_ref[bb, pl.ds(off, CHUNK), cols], 0.0) + o[p])
        offs.append(off)

    @pl.when(second)
    def _():
        for d in range(2):
            for p in ps:
                bb, h = divmod(p, HEADS)
                cols = slice(h * HD, (h + 1) * HD)
                tot = oacc_ref[bb, pl.ds(offs[d], CHUNK), cols]
                ms = jnp.mean(tot * tot, axis=-1, keepdims=True)
                zz = z_refs[d][bb, :, cols].astype(F32)
                y_ref[bb, pl.ds(offs[d], CHUNK), cols] = (
                    tot * lax.rsqrt(ms + EPS) * onorm * _silu(zz)).astype(BF16)


def _delta(qkv, z, gb, gbt, onorm, L):
    B, LC, _ = qkv.shape
    n_all = LC // CHUNK
    n_lat = L // CHUNK
    n_ctx = n_all - n_lat

    def cf(s):
        return s

    def cb(s):
        return jnp.where(s < n_ctx, n_ctx - 1 - s, n_all + n_ctx - 1 - s)

    in_specs = [
        pl.BlockSpec((NB_DELTA, CHUNK, N_QKV), lambda b, s: (b, cf(s), 0)),
        pl.BlockSpec((NB_DELTA, CHUNK, N_QKV), lambda b, s: (b, cb(s), 0)),
        pl.BlockSpec((NB_DELTA, CHUNK, D_B), lambda b, s: (b, cf(s), 0)),
        pl.BlockSpec((NB_DELTA, CHUNK, D_B), lambda b, s: (b, cb(s), 0)),
        pl.BlockSpec((NB_DELTA, CHUNK, N_AB), lambda b, s: (b, cf(s), 0)),
        pl.BlockSpec((NB_DELTA, CHUNK, N_AB), lambda b, s: (b, cb(s), 0)),
        pl.BlockSpec((NB_DELTA, N_AB, CHUNK), lambda b, s: (b, 0, cf(s))),
        pl.BlockSpec((NB_DELTA, N_AB, CHUNK), lambda b, s: (b, 0, cb(s))),
        pl.BlockSpec((1, HD), lambda b, s: (0, 0)),
    ]
    return pl.pallas_call(
        functools.partial(_delta_kernel, n_ctx=n_ctx, n_lat=n_lat),
        out_shape=jax.ShapeDtypeStruct((B, L, D_B), BF16),
        grid=(B // NB_DELTA, n_all),
        in_specs=in_specs,
        out_specs=pl.BlockSpec((NB_DELTA, L, D_B), lambda b, s: (b, 0, 0)),
        scratch_shapes=[pltpu.VMEM((NB_DELTA * 2 * HEADS, HD, HD), F32), pltpu.VMEM((NB_DELTA, L, D_B), F32)],
        compiler_params=pltpu.CompilerParams(dimension_semantics=("arbitrary", "arbitrary"),
                                             vmem_limit_bytes=VMEM_LIMIT),
    )(qkv, qkv, z, z, gb, gb, gbt, gbt, onorm)


def _out_kernel(x_ref, ya_ref, yb_ref, mod_ref, wo_ref, g2_ref, wrt_ref, brt_ref,
                x2_ref, h_ref, route_ref, cnt_ref, base_ref, wbf_ref):
    i = pl.program_id(0)

    @pl.when(i == 0)
    def _():
        base_ref[...] = jnp.zeros_like(base_ref)
        wbf_ref[...] = wo_ref[...].astype(BF16)

    mod = mod_ref[0]
    gt1 = mod[:, 2 * D:3 * D]
    sh2 = mod[:, 3 * D:4 * D]
    sc2 = mod[:, 4 * D:5 * D]
    scale2 = g2_ref[...] * (1.0 + sc2)
    sub = TM_OUT // OUT_SPLIT
    hbs = []
    for r in range(OUT_SPLIT):
        rows = slice(r * sub, (r + 1) * sub)
        mix = _dot(ya_ref[rows, :], wbf_ref[0:D_A, :]) + _dot(yb_ref[rows, :], wbf_ref[D_A:, :])
        x2 = x_ref[rows, :] + gt1 * mix
        x2_ref[rows, :] = x2
        ms = jnp.mean(x2 * x2, axis=-1, keepdims=True)
        hv = x2 * lax.rsqrt(ms + EPS) * scale2 + sh2
        hbs.append(hv.astype(BF16))
        h_ref[rows, 0:D_PACK] = _pack_bf16_pairs(hv)
    hb = jnp.concatenate(hbs, axis=0)

    lt = _dot_nt(wrt_ref[...], hb) + brt_ref[...]
    gl = [lt[r:r + 1, :] for r in range(N_GROUPS)]
    gmax = jnp.maximum(jnp.maximum(gl[0], gl[1]), jnp.maximum(gl[2], gl[3]))
    gsel = jnp.where(gl[0] == gmax, 0, jnp.where(gl[1] == gmax, 1, jnp.where(gl[2] == gmax, 2, 3)))
    p_g = 1.0 / (jnp.exp(gl[0] - gmax) + jnp.exp(gl[1] - gmax) + jnp.exp(gl[2] - gmax) + jnp.exp(gl[3] - gmax))
    el = []
    for e in range(EPG):
        r = [lt[N_GROUPS + g * EPG + e:N_GROUPS + g * EPG + e + 1, :] for g in range(N_GROUPS)]
        el.append(jnp.where(gsel == 0, r[0], jnp.where(gsel == 1, r[1], jnp.where(gsel == 2, r[2], r[3]))))
    m1 = jnp.maximum(jnp.maximum(el[0], el[1]), jnp.maximum(el[2], el[3]))
    i1 = jnp.where(el[0] == m1, 0, jnp.where(el[1] == m1, 1, jnp.where(el[2] == m1, 2, 3)))
    neg = jnp.float32(-jnp.inf)
    el2 = [jnp.where(i1 == e, neg, el[e]) for e in range(EPG)]
    m2 = jnp.maximum(jnp.maximum(el2[0], el2[1]), jnp.maximum(el2[2], el2[3]))
    i2 = jnp.where(jnp.logical_and(el2[0] == m2, i1 != 0), 0,
                   jnp.where(jnp.logical_and(el2[1] == m2, i1 != 1), 1,
                             jnp.where(jnp.logical_and(el2[2] == m2, i1 != 2), 2, 3)))
    t = jnp.exp(m2 - m1)
    w1 = p_g / (1.0 + t)
    w2 = p_g * t / (1.0 + t)
    first_low = i1 < i2
    ea = jnp.where(first_low, i1, i2)
    eb = jnp.where(first_low, i2, i1)
    w_a = jnp.where(first_low, w1, w2)
    w_b = jnp.where(first_low, w2, w1)
    pair = jnp.where(ea == 0, eb - 1, jnp.where(ea == 1, jnp.where(eb == 3, 3, 4), 5))
    swap = pair == 5
    w_a, w_b = jnp.where(swap, w_b, w_a), jnp.where(swap, w_a, w_b)
    bucket = gsel * N_PAIRS + pair

    tm = bucket.shape[1]
    rows = lax.broadcasted_iota(jnp.int32, (32, tm), 0)
    onehot = jnp.where(rows == bucket, 1.0, 0.0).astype(F32)
    r2 = lax.broadcasted_iota(jnp.int32, (tm, tm), 0)
    c2 = lax.broadcasted_iota(jnp.int32, (tm, tm), 1)
    tri = jnp.where(r2 <= c2, 1.0, 0.0).astype(BF16)
    prefix = _dot(onehot.astype(BF16), tri)
    base = base_ref[:, 0:1]
    rank = jnp.sum(onehot * (prefix - 1.0 + base), axis=0, keepdims=True)
    newbase = base + prefix[:, tm - 1:tm]
    base_ref[...] = jnp.broadcast_to(newbase, base_ref.shape)
    cnt_ref[...] = jnp.broadcast_to(newbase, cnt_ref.shape)
    route = jnp.concatenate([bucket.astype(F32), rank, w_a, w_b, jnp.zeros((4, tm), F32)], axis=0)
    route_ref[0] = route
    route_t = jnp.transpose(jnp.concatenate([route, jnp.zeros((120, tm), F32)], axis=0))
    h_ref[:, D_PACK:D_ROW] = lax.bitcast_convert_type(route_t, jnp.uint32)


def _out_proj(x2d, ya, yb, mod3, w_o, g2, wrt, brt, L):
    T = x2d.shape[0]
    nb = T // TM_OUT
    per_b = L // TM_OUT
    out_shape = [
        jax.ShapeDtypeStruct((T, D), F32),
        jax.ShapeDtypeStruct((T, D_ROW), jnp.uint32),
        jax.ShapeDtypeStruct((nb, 8, TM_OUT), F32),
        jax.ShapeDtypeStruct((32, 128), F32),
    ]
    return pl.pallas_call(
        _out_kernel,
        out_shape=out_shape,
        grid=(nb,),
        in_specs=[
            pl.BlockSpec((TM_OUT, D), lambda i: (i, 0)),
            pl.BlockSpec((TM_OUT, D_A), lambda i: (i, 0)),
            pl.BlockSpec((TM_OUT, D_B), lambda i: (i, 0)),
            pl.BlockSpec((1, 1, 6 * D), lambda i: (i // per_b, 0, 0)),
            pl.BlockSpec((D, D), lambda i: (0, 0), pipeline_mode=pl.Buffered(1)),
            pl.BlockSpec((1, D), lambda i: (0, 0)),
            pl.BlockSpec((32, D), lambda i: (0, 0)),
            pl.BlockSpec((32, 1), lambda i: (0, 0)),
        ],
        out_specs=[
            pl.BlockSpec((TM_OUT, D), lambda i: (i, 0)),
            pl.BlockSpec((TM_OUT, D_ROW), lambda i: (i, 0)),
            pl.BlockSpec((1, 8, TM_OUT), lambda i: (i, 0, 0)),
            pl.BlockSpec((32, 128), lambda i: (0, 0)),
        ],
        scratch_shapes=[pltpu.VMEM((32, 128), F32), pltpu.VMEM((D, D), BF16)],
        compiler_params=pltpu.CompilerParams(dimension_semantics=("arbitrary",),
                                             vmem_limit_bytes=VMEM_LIMIT),
    )(x2d, ya, yb, mod3, w_o, g2, wrt, brt)


def _sc_mesh_and_split(n_rows):
    n_workers = SC_CORES * SC_SUBCORES
    per_worker = n_rows // n_workers
    n_chunks = per_worker // SC_WINDOW
    assert per_worker * n_workers == n_rows and n_chunks * SC_WINDOW == per_worker and n_chunks % 2 == 0
    mesh = plsc.VectorSubcoreMesh(core_axis_name="c", subcore_axis_name="s")
    return mesh, n_workers, per_worker, n_chunks


def _sc_two_buffer_loop(n_chunks, first, second):
    first(0, 0).start()

    @pl.loop(0, n_chunks, step=2)
    def _(j):
        for b in range(2):
            jj = j + b
            first(jj, b).wait()

            @pl.when(jj + 1 < n_chunks)
            def _():
                @pl.when(jj >= 1)
                def _():
                    second(jj - 1, 1 - b).wait()
                first(jj + 1, 1 - b).start()

            second(jj, b).start()

    second(n_chunks - 2, 0).wait()
    second(n_chunks - 1, 1).wait()


def _sc_row_gather(table, idx):
    n_rows = idx.shape[0]
    width = table.shape[1]
    mesh, n_workers, per_worker, n_chunks = _sc_mesh_and_split(n_rows)

    @functools.partial(
        pl.kernel, mesh=mesh,
        out_type=jax.ShapeDtypeStruct((n_rows, width), table.dtype),
        scratch_types=[pltpu.VMEM((n_chunks, SC_WINDOW), jnp.int32),
                       pltpu.VMEM((2, SC_WINDOW, width), table.dtype),
                       pltpu.SemaphoreType.DMA((2,)),
                       pltpu.SemaphoreType.DMA((2,))],
    )
    def gather(table_hbm, idx_hbm, out_hbm, idx_v, rows_v, sem_in, sem_out):
        wid = lax.axis_index("s") * SC_CORES + lax.axis_index("c")
        base = wid * per_worker
        pltpu.sync_copy(idx_hbm.at[wid], idx_v)

        def fetch(j, b):
            return pltpu.make_async_copy(table_hbm.at[idx_v.at[j]], rows_v.at[b], sem_in.at[b])

        def put(j, b):
            off = pl.multiple_of(base + j * SC_WINDOW, 8)
            return pltpu.make_async_copy(rows_v.at[b], out_hbm.at[pl.ds(off, SC_WINDOW)], sem_out.at[b])

        _sc_two_buffer_loop(n_chunks, fetch, put)

    return gather(table, idx.reshape(n_workers, n_chunks, SC_WINDOW))


def _sc_row_scatter(rows, pos, n_out):
    n_rows, width = rows.shape
    mesh, n_workers, per_worker, n_chunks = _sc_mesh_and_split(n_rows)

    @functools.partial(
        pl.kernel, mesh=mesh,
        out_type=jax.ShapeDtypeStruct((n_out, width), rows.dtype),
        scratch_types=[pltpu.VMEM((n_chunks, SC_WINDOW), jnp.int32),
                       pltpu.VMEM((2, SC_WINDOW, width), rows.dtype),
                       pltpu.SemaphoreType.DMA((2,)),
                       pltpu.SemaphoreType.DMA((2,))],
    )
    def scatter(rows_hbm, pos_hbm, out_hbm, idx_v, rows_v, sem_in, sem_out):
        wid = lax.axis_index("s") * SC_CORES + lax.axis_index("c")
        base = wid * per_worker
        pltpu.sync_copy(pos_hbm.at[wid], idx_v)

        def fetch(j, b):
            off = pl.multiple_of(base + j * SC_WINDOW, 8)
            return pltpu.make_async_copy(rows_hbm.at[pl.ds(off, SC_WINDOW)], rows_v.at[b], sem_in.at[b])

        def put(j, b):
            return pltpu.make_async_copy(rows_v.at[b], out_hbm.at[idx_v.at[j]], sem_out.at[b])

        _sc_two_buffer_loop(n_chunks, fetch, put)

    return scatter(rows, pos.reshape(n_workers, n_chunks, SC_WINDOW))


def _moe_kernel(ea_ref, eb_ref, nv_ref, hs_ref, wga_ref, wgb_ref, wda_ref, wdb_ref, o_ref, wg_s, wd_s):
    i = pl.program_id(0)
    prev = jnp.maximum(i - 1, 0)

    @pl.when(jnp.logical_or(i == 0, ea_ref[i] != ea_ref[prev]))
    def _():
        wg_s[0] = wga_ref[0].astype(BF16)
        wd_s[0] = wda_ref[0].astype(BF16)

    @pl.when(jnp.logical_or(i == 0, eb_ref[i] != eb_ref[prev]))
    def _():
        wg_s[1] = wgb_ref[0].astype(BF16)
        wd_s[1] = wdb_ref[0].astype(BF16)

    @pl.when(nv_ref[i] > 0)
    def _():
        live = lax.broadcasted_iota(jnp.int32, (BM_MOE, 1), 0) < nv_ref[i]
        xb = _unpack_bf16_pairs(jnp.where(live, hs_ref[:, 0:D_PACK], jnp.uint32(0)))
        w_ab = lax.bitcast_convert_type(hs_ref[:, D_PACK:D_ROW], F32)
        w_a = jnp.where(live, w_ab[:, 2:3], 0.0)
        w_b = jnp.where(live, w_ab[:, 3:4], 0.0)
        ga = _dot(xb, wg_s[0])
        act_a = (_silu(ga[:, :D_EXPERT]) * ga[:, D_EXPERT:] * w_a).astype(BF16)
        gb = _dot(xb, wg_s[1])
        act_b = (_silu(gb[:, :D_EXPERT]) * gb[:, D_EXPERT:] * w_b).astype(BF16)
        o_ref[...] = _pack_bf16_pairs(_dot(act_a, wd_s[0]) + _dot(act_b, wd_s[1]))

    @pl.when(nv_ref[i] <= 0)
    def _():
        o_ref[...] = jnp.zeros_like(o_ref)


def _moe(ea, eb, nvalid, hs, w_gu, w_dn):
    nblk = ea.shape[0]
    S = nblk * BM_MOE
    grid_spec = pltpu.PrefetchScalarGridSpec(
        num_scalar_prefetch=3,
        grid=(nblk,),
        in_specs=[
            pl.BlockSpec((BM_MOE, D_ROW), lambda i, ea, eb, nv: (i, 0)),
            pl.BlockSpec((1, D, 2 * D_EXPERT), lambda i, ea, eb, nv: (ea[i], 0, 0)),
            pl.BlockSpec((1, D, 2 * D_EXPERT), lambda i, ea, eb, nv: (eb[i], 0, 0)),
            pl.BlockSpec((1, D_EXPERT, D), lambda i, ea, eb, nv: (ea[i], 0, 0)),
            pl.BlockSpec((1, D_EXPERT, D), lambda i, ea, eb, nv: (eb[i], 0, 0)),
        ],
        out_specs=pl.BlockSpec((BM_MOE, D_PACK), lambda i, ea, eb, nv: (i, 0)),
        scratch_shapes=[pltpu.VMEM((2, D, 2 * D_EXPERT), BF16), pltpu.VMEM((2, D_EXPERT, D), BF16)],
    )
    return pl.pallas_call(
        _moe_kernel,
        out_shape=jax.ShapeDtypeStruct((S, D_PACK), jnp.uint32),
        grid_spec=grid_spec,
        compiler_params=pltpu.CompilerParams(dimension_semantics=("arbitrary",),
                                             vmem_limit_bytes=VMEM_LIMIT),
    )(ea, eb, nvalid, hs, w_gu, w_gu, w_dn, w_dn)


def _final_kernel(x2_ref, m_ref, mod_ref, fg_ref, *rest):
    o_ref = rest[-1]
    gt2 = mod_ref[0][:, 5 * D:6 * D]
    y = x2_ref[...] + gt2 * _unpack_bf16_pairs(m_ref[...]).astype(F32)
    ms = jnp.mean(y * y, axis=-1, keepdims=True)
    o_ref[...] = y * lax.rsqrt(ms + EPS) * fg_ref[...]


def _final(x2, m_part, mod3, fg, L, part, prev_out):
    T = x2.shape[0]
    steps = m_part.shape[0] // TM_FIN
    off = part * steps
    per_b = L // TM_FIN
    in_specs = [
        pl.BlockSpec((TM_FIN, D), lambda i: (i + off, 0)),
        pl.BlockSpec((TM_FIN, D_PACK), lambda i: (i, 0)),
        pl.BlockSpec((1, 1, 6 * D), lambda i: ((i + off) // per_b, 0, 0)),
        pl.BlockSpec((1, D), lambda i: (0, 0)),
    ]
    args = [x2, m_part, mod3, fg]
    aliases = {}
    if prev_out is not None:
        in_specs.append(pl.BlockSpec(memory_space=pl.ANY))
        args.append(prev_out)
        aliases = {len(args) - 1: 0}
    return pl.pallas_call(
        _final_kernel,
        out_shape=jax.ShapeDtypeStruct((T, D), F32),
        grid=(steps,),
        in_specs=in_specs,
        out_specs=pl.BlockSpec((TM_FIN, D), lambda i: (i + off, 0)),
        input_output_aliases=aliases,
        compiler_params=pltpu.CompilerParams(dimension_semantics=("arbitrary",),
                                             vmem_limit_bytes=VMEM_LIMIT),
    )(*args)


def kernel(x, c, ctx, c_ctx, w_ada, b_ada, norm1_g, w_in, ln_a_g, ln_a_b, w_spatial, b_spatial, conv_qkv, a_log,
           dt_bias, onorm_g, w_out, norm2_g, w_group, b_group, w_router, b_router, w_gate_up, w_down, final_g):
    B, L, _ = x.shape
    T = B * L
    assert w_ada.shape[0] == 1 and ctx.shape[1] == TM_IN and L % TM_OUT == 0 and T % (N_COMBINE_PARTS * TM_FIN) == 0

    cond = jnp.concatenate([c, c_ctx[None, :], jnp.zeros((7, D), F32)], axis=0)
    mod = _modulation(cond, w_ada[0], b_ada[0][None, :])
    mod_lat = mod[:B].reshape(B, 1, 6 * D)
    mod_ctx = mod[B:B + 1]

    alog = a_log[0].reshape(1, 2 * HEADS)
    dtb = dt_bias[0].reshape(1, 2 * HEADS)
    alog_row = jnp.pad(alog, ((0, 0), (0, 128 - 2 * HEADS)))
    dtb_row = jnp.pad(dtb, ((0, 0), (0, 128 - 2 * HEADS)))
    alog_col = jnp.pad(alog, ((0, 0), (0, N_AB - 2 * HEADS))).T
    dtb_col = jnp.pad(dtb, ((0, 0), (0, N_AB - 2 * HEADS))).T

    ya, qkv, z, gb, gbt = _in_proj(
        x, ctx, mod_lat, mod_ctx, norm1_g, jnp.swapaxes(w_in, 1, 2), ln_a_g, ln_a_b,
        w_spatial[0].astype(BF16), b_spatial[0].T, conv_qkv[0], alog_row, dtb_row, alog_col, dtb_col)

    yb = _delta(qkv, z, gb, gbt, onorm_g, L)

    wrt = jnp.concatenate([w_group[0].T, w_router[0].T, jnp.zeros((32 - N_GROUPS - N_EXPERTS, D), F32)], axis=0)
    brt = jnp.concatenate([b_group[0], b_router[0], jnp.zeros((32 - N_GROUPS - N_EXPERTS,), F32)])[:, None]
    x2, h, route, cnt = _out_proj(x.reshape(T, D), ya.reshape(T, D_A), yb.reshape(T, D_B), mod_lat,
                                  w_out[0], norm2_g, wrt.astype(BF16), brt, L)

    bucket = route[:, 0, :].reshape(T).astype(jnp.int32)
    rank = route[:, 1, :].reshape(T).astype(jnp.int32)
    counts = cnt[:N_BUCKETS, 0].astype(jnp.int32)
    nblk_b = (counts + BM_MOE - 1) // BM_MOE
    blk_end = jnp.cumsum(nblk_b)
    blk_start = blk_end - nblk_b
    kk = jnp.arange(N_BUCKETS, dtype=jnp.int32)

    def pick(idx, table):
        return jnp.sum(jnp.where(idx[:, None] == kk[None, :], table[None, :], 0), axis=1)

    pos = pick(bucket, blk_start) * BM_MOE + rank
    n_blocks = T // BM_MOE + N_BUCKETS
    S = n_blocks * BM_MOE
    blk = jnp.arange(n_blocks, dtype=jnp.int32)
    used = blk < blk_end[-1]
    bkt = jnp.sum((jnp.minimum(blk, blk_end[-1] - 1)[:, None] >= blk_end[None, :]).astype(jnp.int32), axis=1)
    nvalid = jnp.where(used, jnp.clip(pick(bkt, counts) - (blk - pick(bkt, blk_start)) * BM_MOE, 0, BM_MOE),
                       0).astype(jnp.int32)
    ea = pick(bkt, jnp.asarray(_SLOT_A_EXPERT))
    eb = pick(bkt, jnp.asarray(_SLOT_B_EXPERT))

    hs = _sc_row_scatter(h, pos, S)
    ms = _moe(ea, eb, nvalid, hs, w_gate_up[0], w_down[0])
    pos_parts = pos.reshape(N_COMBINE_PARTS, T // N_COMBINE_PARTS)
    m_parts = [_sc_row_gather(ms, pos_parts[q]) for q in range(N_COMBINE_PARTS)]
    out = None
    for q in range(N_COMBINE_PARTS):
        out = _final(x2, m_parts[q], mod_lat, final_g[None, :], L, q, out)
    return out.reshape(B, L, D)
```

```python
import functools

import jax
import jax.numpy as jnp
import numpy as np
from jax import lax
from jax.experimental import pallas as pl
from jax.experimental.pallas import tpu as pltpu
from jax.experimental.pallas import tpu_sc as plsc

F32 = jnp.float32
BF16 = jnp.bfloat16
EPS = 1e-6

D = 1024
D_A = 512
D_B = 512
HEADS = 4
HD = 128
CHUNK = 128
CONV_W = 5
N_QKV = 3 * D_B
N_MAIN = 2 * D_A + 4 * D_B
N_AB = 16
N_GROUPS = 4
EPG = 4
N_EXPERTS = 16
D_EXPERT = 512
N_PAIRS = 6
N_BUCKETS = N_GROUPS * N_PAIRS
D_PACK = D // 2
D_ROW = D_PACK + 128
PAIR_A = (0, 0, 0, 1, 1, 3)
PAIR_B = (1, 2, 3, 3, 2, 2)
_SLOT_A_EXPERT = np.array([g * EPG + PAIR_A[p] for g in range(N_GROUPS) for p in range(N_PAIRS)], np.int32)
_SLOT_B_EXPERT = np.array([g * EPG + PAIR_B[p] for g in range(N_GROUPS) for p in range(N_PAIRS)], np.int32)

TM_IN = 256
HALO = 8
NB_IN = 2
NB_DELTA = 2
TM_OUT = 1024
OUT_SPLIT = 4
BM_MOE = 256
TM_FIN = 512
N_COMBINE_PARTS = 4
VMEM_LIMIT = 56 * 1024 * 1024
SC_CORES = 2
SC_SUBCORES = 16
SC_WINDOW = 32

HI = lax.Precision.HIGHEST

_CHUNK_TRIL = np.kron(np.eye(TM_IN // CHUNK, dtype=np.float32), np.tril(np.ones((CHUNK, CHUNK), np.float32)))


def _dot(a, b, precision=None):
    return jnp.dot(a, b, preferred_element_type=F32, precision=precision)


def _dot_nt(a, b):
    return lax.dot_general(a, b, (((1,), (1,)), ((), ())), preferred_element_type=F32)


def _dot_tn(a, b):
    return lax.dot_general(a, b, (((0,), (0,)), ((), ())), preferred_element_type=F32)


def _sigmoid(x):
    return 0.5 + 0.5 * jnp.tanh(0.5 * x)


def _silu(x):
    h = 0.5 * x
    return h + h * jnp.tanh(h)


def _softplus(x):
    return jnp.maximum(x, 0.0) + jnp.log(1.0 + jnp.exp(-jnp.abs(x)))


def _pack_bf16_pairs(x):
    bits = lax.bitcast_convert_type(x.astype(BF16).astype(F32), jnp.uint32)
    return (bits[:, D_PACK:] & jnp.uint32(0xFFFF0000)) | (bits[:, :D_PACK] >> 16)


def _unpack_bf16_pairs(w):
    bits = w
    lo = lax.bitcast_convert_type(bits << 16, F32)
    hi = lax.bitcast_convert_type(bits & jnp.uint32(0xFFFF0000), F32)
    return jnp.concatenate([lo, hi], axis=1).astype(BF16)


def _gelu_tanh(x):
    return 0.5 * x * (1.0 + jnp.tanh(np.sqrt(2.0 / np.pi).astype(np.float32) * (x + 0.044715 * (x * x * x))))


def _mod_kernel(c_ref, w_ref, b_ref, o_ref):
    c = c_ref[...]
    o_ref[...] = _dot(_silu(c), w_ref[...], precision=HI) + b_ref[...]


def _modulation(cond, w_ada, b_ada):
    rows = cond.shape[0]
    tn = 1536
    return pl.pallas_call(
        _mod_kernel,
        out_shape=jax.ShapeDtypeStruct((rows, 6 * D), F32),
        grid=(6 * D // tn,),
        in_specs=[pl.BlockSpec((rows, D), lambda i: (0, 0)),
                  pl.BlockSpec((D, tn), lambda i: (0, i)),
                  pl.BlockSpec((1, tn), lambda i: (0, i))],
        out_specs=pl.BlockSpec((rows, tn), lambda i: (0, i)),
        compiler_params=pltpu.CompilerParams(dimension_semantics=("arbitrary",),
                                             vmem_limit_bytes=VMEM_LIMIT),
    )(cond, w_ada, b_ada)


def _in_kernel(x_ref, xp_ref, xn_ref, ctx_ref, mod_ref, cmod_ref, g1_ref, wt_ref,
               lng_ref, lnb_ref, ws_ref, bst_ref, conv_ref, alog_ref, dtb_ref, alogt_ref, dtbt_ref, tril_ref, triu_ref,
               ya_ref, qkv_ref, z_ref, gb_ref, gbt_ref, wbf_ref, wab_ref, wabt_ref):
    j = pl.program_id(1)

    @pl.when(jnp.logical_and(pl.program_id(0) == 0, j == 0))
    def _():
        for c0 in range(0, N_MAIN, 512):
            wbf_ref[:, c0:c0 + 512] = jnp.transpose(wt_ref[0, c0:c0 + 512, :]).astype(BF16)
        tail = jnp.concatenate([wt_ref[0, N_MAIN:N_MAIN + N_AB, :], jnp.zeros((128 - N_AB, D), F32)], axis=0)
        wabt_ref[...] = tail.astype(BF16)
        wab_ref[...] = jnp.transpose(tail).astype(BF16)

    is_ctx = j == 0
    n_lat_blocks = pl.num_programs(1) - 1

    def one_batch_element(bb):
        mod = mod_ref[bb]
        cm = cmod_ref[...]
        sh = jnp.where(is_ctx, cm[:, 0:D], mod[:, 0:D])
        sc = jnp.where(is_ctx, cm[:, D:2 * D], mod[:, D:2 * D])
        scale = g1_ref[...] * (1.0 + sc)

        xmain = jnp.where(is_ctx, ctx_ref[bb], x_ref[bb])
        xv = jnp.concatenate([xp_ref[bb], xmain, xn_ref[bb]], axis=0)
        xnorm = xv * lax.rsqrt(jnp.mean(xv * xv, axis=-1, keepdims=True) + EPS) * scale + sh
        xe = xnorm.astype(BF16)
        xb = xnorm[HALO:HALO + TM_IN].astype(BF16)

        rid = lax.broadcasted_iota(jnp.int32, (TM_IN + 2 * HALO, 1), 0)
        prev_ok = j >= 2
        next_ok = jnp.logical_and(j >= 1, j < n_lat_blocks)
        valid = jnp.logical_or(jnp.logical_and(rid >= HALO, rid < HALO + TM_IN),
                               jnp.logical_or(jnp.logical_and(rid < HALO, prev_ok),
                                              jnp.logical_and(rid >= HALO + TM_IN, next_ok)))
        pad = (CONV_W - 1) // 2
        c_qkv = 2 * D_A

        def proj(c0, width, halo=True):
            return _dot(xe if halo else xb, wbf_ref[:, c0:c0 + width])

        def conv_act(pq, c0):
            groups = (TM_IN + 2 * HALO) // 8
            x3 = jnp.where(valid, pq, 0.0).reshape(groups, 8, D_B)
            sub = lax.broadcasted_iota(jnp.int32, (1, 8, 1), 1)
            lo, hi = HALO // 8, HALO // 8 + TM_IN // 8
            acc = conv_ref[pad:pad + 1, c0:c0 + D_B] * x3[lo:hi]
            for t in range(CONV_W):
                s = t - pad
                if s == 0:
                    continue
                r = pltpu.roll(x3, (-s) % 8, axis=1)
                if s > 0:
                    sh = jnp.where(sub < 8 - s, r[lo:hi], r[lo + 1:hi + 1])
                else:
                    sh = jnp.where(sub >= -s, r[lo:hi], r[lo - 1:hi - 1])
                acc = acc + conv_ref[t:t + 1, c0:c0 + D_B] * sh
            return _silu(acc.reshape(TM_IN, D_B))

        def store_unit_heads(act, c0, gain):
            for h in range(HEADS):
                t = act[:, h * HD:(h + 1) * HD]
                nrm = lax.rsqrt(jnp.sum(t * t, axis=-1, keepdims=True) + EPS) * gain
                qkv_ref[bb, :, c0 + h * HD:c0 + (h + 1) * HD] = (t * nrm).astype(BF16)

        pq_q = proj(c_qkv, D_B)
        pq_k = proj(c_qkv + D_B, D_B)
        store_unit_heads(conv_act(pq_q, 0), 0, HD ** -0.5)
        pq_v = proj(c_qkv + 2 * D_B, D_B)
        store_unit_heads(conv_act(pq_k, D_B), D_B, 1.0)
        pa_u = proj(0, D_A, halo=False)
        qkv_ref[bb, :, 2 * D_B:] = conv_act(pq_v, 2 * D_B).astype(BF16)
        pa_v = proj(D_A, D_A, halo=False)
        u = _gelu_tanh(pa_u)
        pz = proj(c_qkv + N_QKV, D_B, halo=False)
        v = _gelu_tanh(pa_v)
        mu = jnp.mean(v, axis=-1, keepdims=True)
        vc = v - mu
        var = jnp.mean(vc * vc, axis=-1, keepdims=True)
        vn = (vc * lax.rsqrt(var + EPS) * lng_ref[...] + lnb_ref[...]).astype(BF16)
        z_ref[bb] = pz.astype(BF16)

        bst = bst_ref[...]
        for n in range(TM_IN // CHUNK):
            rows = slice(n * CHUNK, (n + 1) * CHUNK)
            for h in range(HEADS):
                cols = slice(h * HD, (h + 1) * HD)
                s = _dot(ws_ref[h], vn[rows, cols]) + bst[:, h:h + 1]
                ya_ref[bb, rows, cols] = (u[rows, cols] * s).astype(BF16)

        tri_l = tril_ref[...]
        tri_u = triu_ref[...]

        def split3(g):
            hi = g.astype(BF16)
            r1 = g - hi.astype(F32)
            mid = r1.astype(BF16)
            return hi, mid, (r1 - mid.astype(F32)).astype(BF16)

        ab = _dot(xb, wab_ref[...])
        g3 = split3(-jnp.exp(alog_ref[...]) * _softplus(ab + dtb_ref[...]))
        lane = lax.broadcasted_iota(jnp.int32, ab.shape, 1)
        gb = jnp.where(lane < HEADS, _dot(tri_l, g3[0]) + _dot(tri_l, g3[1]) + _dot(tri_l, g3[2]),
                       jnp.where(lane < 2 * HEADS, _dot(tri_u, g3[0]) + _dot(tri_u, g3[1]) + _dot(tri_u, g3[2]),
                                 _sigmoid(ab)))
        gb_ref[bb] = gb[:, 0:N_AB]

        abt = _dot_nt(wabt_ref[0:N_AB, :], xb)
        t3 = split3(-jnp.exp(alogt_ref[...]) * _softplus(abt + dtbt_ref[...]))
        row = lax.broadcasted_iota(jnp.int32, abt.shape, 0)
        gbt_ref[bb] = jnp.where(row < HEADS, _dot(t3[0], tri_u) + _dot(t3[1], tri_u) + _dot(t3[2], tri_u),
                               jnp.where(row < 2 * HEADS, _dot(t3[0], tri_l) + _dot(t3[1], tri_l) + _dot(t3[2], tri_l),
                                         _sigmoid(abt)))

    for bb in range(x_ref.shape[0]):
        one_batch_element(bb)


def _in_proj(x, ctx, mod_lat, mod_ctx, g1, w_in_t, lng, lnb, ws, bst, conv, alog, dtb, alogt, dtbt):
    B, L, _ = x.shape
    n_lat = L // TM_IN
    n_steps = n_lat + 1
    LC = L + TM_IN
    hb = TM_IN // HALO

    def full(shape):
        return pl.BlockSpec(shape, lambda b, j: (0,) * len(shape))

    in_specs = [
        pl.BlockSpec((NB_IN, TM_IN, D), lambda b, j: (b, jnp.maximum(j - 1, 0), 0)),
        pl.BlockSpec((NB_IN, HALO, D), lambda b, j: (b, jnp.clip((j - 1) * hb - 1, 0, L // HALO - 1), 0)),
        pl.BlockSpec((NB_IN, HALO, D), lambda b, j: (b, jnp.clip(j * hb, 0, L // HALO - 1), 0)),
        pl.BlockSpec((NB_IN, TM_IN, D), lambda b, j: (b, 0, 0)),
        pl.BlockSpec((NB_IN, 1, 6 * D), lambda b, j: (b, 0, 0)),
        full((1, 6 * D)), full((1, D)),
        pl.BlockSpec((1, N_MAIN + N_AB, D), lambda b, j: (0, 0, 0), pipeline_mode=pl.Buffered(1)),
        full((1, D_A)), full((1, D_A)), full((HEADS, CHUNK, CHUNK)), full((CHUNK, HEADS)),
        full((CONV_W, N_QKV)), full((1, 128)), full((1, 128)), full((N_AB, 1)), full((N_AB, 1)),
        full((TM_IN, TM_IN)), full((TM_IN, TM_IN)),
    ]
    out_shape = [
        jax.ShapeDtypeStruct((B, L, D_A), BF16),
        jax.ShapeDtypeStruct((B, LC, N_QKV), BF16),
        jax.ShapeDtypeStruct((B, LC, D_B), BF16),
        jax.ShapeDtypeStruct((B, LC, N_AB), F32),
        jax.ShapeDtypeStruct((B, N_AB, LC), F32),
    ]
    out_specs = [
        pl.BlockSpec((NB_IN, TM_IN, D_A), lambda b, j: (b, jnp.maximum(j - 1, 0), 0)),
        pl.BlockSpec((NB_IN, TM_IN, N_QKV), lambda b, j: (b, j, 0)),
        pl.BlockSpec((NB_IN, TM_IN, D_B), lambda b, j: (b, j, 0)),
        pl.BlockSpec((NB_IN, TM_IN, N_AB), lambda b, j: (b, j, 0)),
        pl.BlockSpec((NB_IN, N_AB, TM_IN), lambda b, j: (b, 0, j)),
    ]
    return pl.pallas_call(
        _in_kernel,
        out_shape=out_shape,
        grid=(B // NB_IN, n_steps),
        in_specs=in_specs,
        out_specs=out_specs,
        scratch_shapes=[pltpu.VMEM((D, N_MAIN), BF16), pltpu.VMEM((D, 128), BF16), pltpu.VMEM((128, D), BF16)],
        compiler_params=pltpu.CompilerParams(dimension_semantics=("arbitrary", "arbitrary"),
                                             vmem_limit_bytes=VMEM_LIMIT),
    )(x, x, x, ctx, mod_lat, mod_ctx, g1, w_in_t, lng, lnb, ws, bst, conv, alog, dtb, alogt, dtbt,
      jnp.asarray(_CHUNK_TRIL, BF16), jnp.asarray(_CHUNK_TRIL.T, BF16))


def _delta_kernel(qf_ref, qb_ref, zf_ref, zb_ref, gf_ref, gbk_ref, gtf_ref, gtb_ref, on_ref,
                  y_ref, s_ref, oacc_ref, *, n_ctx, n_lat):
    s = pl.program_id(1)

    @pl.when(s == 0)
    def _():
        s_ref[...] = jnp.zeros_like(s_ref)
        oacc_ref[...] = jnp.zeros_like(oacc_ref)

    row = lax.broadcasted_iota(jnp.int32, (CHUNK, CHUNK), 0)
    col = lax.broadcasted_iota(jnp.int32, (CHUNK, CHUNK), 1)
    low = row > col
    upp = row < col
    same_blk = (row // 16) == (col // 16)
    eye = jnp.where(row == col, 1.0, 0.0).astype(BF16)
    zero = jnp.zeros((CHUNK, CHUNK), BF16)
    onorm = on_ref[...]
    half = n_ctx + n_lat // 2
    second = s >= half
    g_refs = (gf_ref, gbk_ref)
    gt_refs = (gtf_ref, gtb_ref)
    qkv_refs = (qf_ref, qb_ref)
    z_refs = (zf_ref, zb_ref)
    nb = qf_ref.shape[0]
    ps = range(nb * HEADS)

    def halves(xc, unit):
        xb = xc.astype(BF16)
        fill = eye if unit else zero
        return jnp.where(low, xb, fill), jnp.where(upp, xb, fill)

    def as_lhs(hv):
        return jnp.concatenate(hv, axis=1)

    def as_rhs(*hvs):
        cols_ = [jnp.concatenate(hv, axis=0) for hv in hvs]
        return cols_[0] if len(cols_) == 1 else jnp.concatenate(cols_, axis=1)

    def load(d, p, part):
        bb, h = divmod(p, HEADS)
        return qkv_refs[d][bb, :, part * D_B + h * HD:part * D_B + (h + 1) * HD]

    def gcol(d, p, base):
        bb, h = divmod(p, HEADS)
        c = base + d * HEADS + h
        return g_refs[d][bb, :, c:c + 1]

    def grow(d, p, base):
        bb, h = divmod(p, HEADS)
        r = base + d * HEADS + h
        return gt_refs[d][bb, r:r + 1, :]

    def lanes(col):
        return jnp.broadcast_to(col, (CHUNK, HD))

    q = [[load(d, p, 0) for p in ps] for d in range(2)]
    k = [[load(d, p, 1) for p in ps] for d in range(2)]
    v = [[load(d, p, 2) for p in ps] for d in range(2)]
    gcl = [[lanes(gcol(d, p, 0)) for p in ps] for d in range(2)]
    betal = [[lanes(gcol(d, p, 2 * HEADS)) for p in ps] for d in range(2)]
    gr = [[grow(d, p, 0) for p in ps] for d in range(2)]
    betar = [[grow(d, p, 2 * HEADS) for p in ps] for d in range(2)]
    glast = [[gr[0][p][:, CHUNK - 1:CHUNK] for p in ps], [gr[1][p][:, 0:1] for p in ps]]

    gram = [[_dot_nt(jnp.concatenate([q[d][p], k[d][p]], axis=0), k[d][p]) for p in ps] for d in range(2)]
    dec = [jnp.exp(jnp.where(low, gcl[0][p] - gr[0][p], jnp.where(upp, gcl[1][p] - gr[1][p], 0.0))) for p in ps]
    lc = [jnp.where(low, gram[0][p][CHUNK:] * betar[0][p], jnp.where(upp, gram[1][p][CHUNK:] * betar[1][p], 0.0))
          * dec[p] for p in ps]
    qk = [[jnp.where(upp, 0.0, gram[0][p][:CHUNK] * dec[p]).astype(BF16) for p in ps],
          [jnp.where(low, 0.0, gram[1][p][:CHUNK] * dec[p]).astype(BF16) for p in ps]]

    dg = [jnp.where(same_blk, lc[p], 0.0) for p in ps]
    ob = [lc[p] - dg[p] for p in ps]
    d1h = [halves(dg[p], False) for p in ps]
    d2 = [_dot(as_lhs(d1h[p]), as_rhs(d1h[p])) for p in ps]
    p0s = [-dg[p] for p in ps]
    d2h = [halves(d2[p], False) for p in ps]
    p0h = [halves(p0s[p], True) for p in ps]
    o2 = [_dot(as_lhs(d2h[p]), as_rhs(d2h[p], p0h[p])) for p in ps]
    p1s = [p0s[p] + o2[p][:, CHUNK:] for p in ps]
    d4h = [halves(o2[p][:, :CHUNK], False) for p in ps]
    p1h = [halves(p1s[p], True) for p in ps]
    o3 = [_dot(as_lhs(d4h[p]), as_rhs(d4h[p], p1h[p])) for p in ps]
    p2s = [p1s[p] + o3[p][:, CHUNK:] for p in ps]
    d8h = [halves(o3[p][:, :CHUNK], False) for p in ps]
    p2h = [halves(p2s[p], True) for p in ps]
    p3s = [p2s[p] + _dot(as_lhs(d8h[p]), as_rhs(p2h[p])) for p in ps]
    p3h = [halves(p3s[p], True) for p in ps]
    obh = [halves(ob[p], False) for p in ps]
    n1h = [halves(_dot(as_lhs(p3h[p]), as_rhs(obh[p])), False) for p in ps]
    o6 = [_dot(as_lhs(n1h[p]), as_rhs(n1h[p], p3h[p])) for p in ps]
    r0s = [p3s[p] - o6[p][:, CHUNK:] for p in ps]
    n2h = [halves(o6[p][:, :CHUNK], False) for p in ps]
    r0h = [halves(r0s[p], True) for p in ps]
    o7 = [_dot(as_lhs(n2h[p]), as_rhs(n2h[p], r0h[p])) for p in ps]
    r1s = [r0s[p] + o7[p][:, CHUNK:] for p in ps]
    n4h = [halves(o7[p][:, :CHUNK], False) for p in ps]
    r1h = [halves(r1s[p], True) for p in ps]
    tinv = [halves(r1s[p] + _dot(as_lhs(n4h[p]), as_rhs(r1h[p])), True) for p in ps]

    offs = []
    for d in range(2):
        lat_chunk = (s - n_ctx) if d == 0 else (n_ctx + n_lat - 1 - s)
        off = pl.multiple_of(jnp.clip(lat_chunk, 0, n_lat - 1) * CHUNK, CHUNK)
        sidx = [(p // HEADS * 2 + d) * HEADS + p % HEADS for p in ps]
        egc = [jnp.exp(gcl[d][p]) for p in ps]
        kf = [k[d][p].astype(F32) for p in ps]
        rhs = [jnp.concatenate([v[d][p], (kf[p] * egc[p]).astype(BF16)], axis=1) for p in ps]
        uw = [_dot(tinv[p][d], rhs[p]) for p in ps]
        qd = [q[d][p].astype(F32) * egc[p] for p in ps]
        kd = [(kf[p] * jnp.exp(glast[d][p] - gcl[d][p])).astype(BF16) for p in ps]
        st = [s_ref[sidx[p]] for p in ps]
        a1 = [_dot(jnp.concatenate([uw[p][:, HD:] * betal[d][p], qd[p]], axis=0).astype(BF16), st[p].astype(BF16))
              for p in ps]
        vnew = [(uw[p][:, :HD] * betal[d][p] - a1[p][:CHUNK]).astype(BF16) for p in ps]
        o = [a1[p][CHUNK:] + _dot(qk[d][p], vnew[p]) for p in ps]
        for p in ps:
            s_ref[sidx[p]] = st[p] * jnp.exp(glast[d][p]) + _dot_tn(kd[p], vnew[p])
        for p in ps:
            bb, h = divmod(p, HEADS)
            cols = slice(h * HD, (h + 1) * HD)
            oacc_ref[bb, pl.ds(off, CHUNK), cols] = (
                jnp.where(second, oacc_ref[bb, pl.ds(off, CHUNK), cols], 0.0) + o[p])
        offs.append(off)

    @pl.when(second)
    def _():
        for d in range(2):
            for p in ps:
                bb, h = divmod(p, HEADS)
                cols = slice(h * HD, (h + 1) * HD)
                tot = oacc_ref[bb, pl.ds(offs[d], CHUNK), cols]
                ms = jnp.mean(tot * tot, axis=-1, keepdims=True)
                zz = z_refs[d][bb, :, cols].astype(F32)
                y_ref[bb, pl.ds(offs[d], CHUNK), cols] = (
                    tot * lax.rsqrt(ms + EPS) * onorm * _silu(zz)).astype(BF16)


def _delta(qkv, z, gb, gbt, onorm, L):
    B, LC, _ = qkv.shape
    n_all = LC // CHUNK
    n_lat = L // CHUNK
    n_ctx = n_all - n_lat

    def cf(s):
        return s

    def cb(s):
        return jnp.where(s < n_ctx, n_ctx - 1 - s, n_all + n_ctx - 1 - s)

    in_specs = [
        pl.BlockSpec((NB_DELTA, CHUNK, N_QKV), lambda b, s: (b, cf(s), 0)),
        pl.BlockSpec((NB_DELTA, CHUNK, N_QKV), lambda b, s: (b, cb(s), 0)),
        pl.BlockSpec((NB_DELTA, CHUNK, D_B), lambda b, s: (b, cf(s), 0)),
        pl.BlockSpec((NB_DELTA, CHUNK, D_B), lambda b, s: (b, cb(s), 0)),
        pl.BlockSpec((NB_DELTA, CHUNK, N_AB), lambda b, s: (b, cf(s), 0)),
        pl.BlockSpec((NB_DELTA, CHUNK, N_AB), lambda b, s: (b, cb(s), 0)),
        pl.BlockSpec((NB_DELTA, N_AB, CHUNK), lambda b, s: (b, 0, cf(s))),
        pl.BlockSpec((NB_DELTA, N_AB, CHUNK), lambda b, s: (b, 0, cb(s))),
        pl.BlockSpec((1, HD), lambda b, s: (0, 0)),
    ]
    return pl.pallas_call(
        functools.partial(_delta_kernel, n_ctx=n_ctx, n_lat=n_lat),
        out_shape=jax.ShapeDtypeStruct((B, L, D_B), BF16),
        grid=(B // NB_DELTA, n_all),
        in_specs=in_specs,
        out_specs=pl.BlockSpec((NB_DELTA, L, D_B), lambda b, s: (b, 0, 0)),
        scratch_shapes=[pltpu.VMEM((NB_DELTA * 2 * HEADS, HD, HD), F32), pltpu.VMEM((NB_DELTA, L, D_B), F32)],
        compiler_params=pltpu.CompilerParams(dimension_semantics=("arbitrary", "arbitrary"),
                                             vmem_limit_bytes=VMEM_LIMIT),
    )(qkv, qkv, z, z, gb, gb, gbt, gbt, onorm)


def _out_kernel(x_ref, ya_ref, yb_ref, mod_ref, wo_ref, g2_ref, wrt_ref, brt_ref, tri_ref,
                x2_ref, h_ref, route_ref, cnt_ref, base_ref, wbf_ref):
    i = pl.program_id(0)

    @pl.when(i == 0)
    def _():
        base_ref[...] = jnp.zeros_like(base_ref)
        wbf_ref[...] = wo_ref[...].astype(BF16)

    mod = mod_ref[0]
    gt1 = mod[:, 2 * D:3 * D]
    sh2 = mod[:, 3 * D:4 * D]
    sc2 = mod[:, 4 * D:5 * D]
    scale2 = g2_ref[...] * (1.0 + sc2)
    sub = TM_OUT // OUT_SPLIT
    hbs = []
    for r in range(OUT_SPLIT):
        rows = slice(r * sub, (r + 1) * sub)
        mix = _dot(ya_ref[rows, :], wbf_ref[0:D_A, :]) + _dot(yb_ref[rows, :], wbf_ref[D_A:, :])
        x2 = x_ref[rows, :] + gt1 * mix
        x2_ref[rows, :] = x2
        ms = jnp.mean(x2 * x2, axis=-1, keepdims=True)
        hv = x2 * lax.rsqrt(ms + EPS) * scale2 + sh2
        hbs.append(hv.astype(BF16))
        h_ref[rows, 0:D_PACK] = _pack_bf16_pairs(hv)
    hb = jnp.concatenate(hbs, axis=0)

    lt = _dot_nt(wrt_ref[...], hb) + brt_ref[...]
    gl = [lt[r:r + 1, :] for r in range(N_GROUPS)]
    gmax = jnp.maximum(jnp.maximum(gl[0], gl[1]), jnp.maximum(gl[2], gl[3]))
    gsel = jnp.where(gl[0] == gmax, 0, jnp.where(gl[1] == gmax, 1, jnp.where(gl[2] == gmax, 2, 3)))
    p_g = 1.0 / (jnp.exp(gl[0] - gmax) + jnp.exp(gl[1] - gmax) + jnp.exp(gl[2] - gmax) + jnp.exp(gl[3] - gmax))
    el = []
    for e in range(EPG):
        r = [lt[N_GROUPS + g * EPG + e:N_GROUPS + g * EPG + e + 1, :] for g in range(N_GROUPS)]
        el.append(jnp.where(gsel == 0, r[0], jnp.where(gsel == 1, r[1], jnp.where(gsel == 2, r[2], r[3]))))
    m1 = jnp.maximum(jnp.maximum(el[0], el[1]), jnp.maximum(el[2], el[3]))
    i1 = jnp.where(el[0] == m1, 0, jnp.where(el[1] == m1, 1, jnp.where(el[2] == m1, 2, 3)))
    neg = jnp.float32(-jnp.inf)
    el2 = [jnp.where(i1 == e, neg, el[e]) for e in range(EPG)]
    m2 = jnp.maximum(jnp.maximum(el2[0], el2[1]), jnp.maximum(el2[2], el2[3]))
    i2 = jnp.where(jnp.logical_and(el2[0] == m2, i1 != 0), 0,
                   jnp.where(jnp.logical_and(el2[1] == m2, i1 != 1), 1,
                             jnp.where(jnp.logical_and(el2[2] == m2, i1 != 2), 2, 3)))
    t = jnp.exp(m2 - m1)
    w1 = p_g / (1.0 + t)
    w2 = p_g * t / (1.0 + t)
    first_low = i1 < i2
    ea = jnp.where(first_low, i1, i2)
    eb = jnp.where(first_low, i2, i1)
    w_a = jnp.where(first_low, w1, w2)
    w_b = jnp.where(first_low, w2, w1)
    pair = jnp.where(ea == 0, eb - 1, jnp.where(ea == 1, jnp.where(eb == 3, 3, 4), 5))
    swap = pair == 5
    w_a, w_b = jnp.where(swap, w_b, w_a), jnp.where(swap, w_a, w_b)
    bucket = gsel * N_PAIRS + pair

    tm = bucket.shape[1]
    rows = lax.broadcasted_iota(jnp.int32, (32, tm), 0)
    onehot = jnp.where(rows == bucket, 1.0, 0.0).astype(F32)
    prefix = _dot(onehot.astype(BF16), tri_ref[...])
    base = base_ref[:, 0:1]
    rank = jnp.sum(onehot * (prefix - 1.0 + base), axis=0, keepdims=True)
    newbase = base + prefix[:, tm - 1:tm]
    base_ref[...] = jnp.broadcast_to(newbase, base_ref.shape)
    cnt_ref[...] = jnp.broadcast_to(newbase, cnt_ref.shape)
    route = jnp.concatenate([bucket.astype(F32), rank, w_a, w_b, jnp.zeros((4, tm), F32)], axis=0)
    route_ref[0] = route
    route_t = jnp.transpose(jnp.concatenate([route, jnp.zeros((120, tm), F32)], axis=0))
    h_ref[:, D_PACK:D_ROW] = lax.bitcast_convert_type(route_t, jnp.uint32)


def _out_proj(x2d, ya, yb, mod3, w_o, g2, wrt, brt, L):
    T = x2d.shape[0]
    nb = T // TM_OUT
    per_b = L // TM_OUT
    out_shape = [
        jax.ShapeDtypeStruct((T, D), F32),
        jax.ShapeDtypeStruct((T, D_ROW), jnp.uint32),
        jax.ShapeDtypeStruct((nb, 8, TM_OUT), F32),
        jax.ShapeDtypeStruct((32, 128), F32),
    ]
    return pl.pallas_call(
        _out_kernel,
        out_shape=out_shape,
        grid=(nb,),
        in_specs=[
            pl.BlockSpec((TM_OUT, D), lambda i: (i, 0)),
            pl.BlockSpec((TM_OUT, D_A), lambda i: (i, 0)),
            pl.BlockSpec((TM_OUT, D_B), lambda i: (i, 0)),
            pl.BlockSpec((1, 1, 6 * D), lambda i: (i // per_b, 0, 0)),
            pl.BlockSpec((D, D), lambda i: (0, 0), pipeline_mode=pl.Buffered(1)),
            pl.BlockSpec((1, D), lambda i: (0, 0)),
            pl.BlockSpec((32, D), lambda i: (0, 0)),
            pl.BlockSpec((32, 1), lambda i: (0, 0)),
            pl.BlockSpec((TM_OUT, TM_OUT), lambda i: (0, 0), pipeline_mode=pl.Buffered(1)),
        ],
        out_specs=[
            pl.BlockSpec((TM_OUT, D), lambda i: (i, 0)),
            pl.BlockSpec((TM_OUT, D_ROW), lambda i: (i, 0)),
            pl.BlockSpec((1, 8, TM_OUT), lambda i: (i, 0, 0)),
            pl.BlockSpec((32, 128), lambda i: (0, 0)),
        ],
        scratch_shapes=[pltpu.VMEM((32, 128), F32), pltpu.VMEM((D, D), BF16)],
        compiler_params=pltpu.CompilerParams(dimension_semantics=("arbitrary",),
                                             vmem_limit_bytes=VMEM_LIMIT),
    )(x2d, ya, yb, mod3, w_o, g2, wrt, brt, jnp.asarray(np.triu(np.ones((TM_OUT, TM_OUT), np.float32)), BF16))


def _sc_mesh_and_split(n_rows):
    n_workers = SC_CORES * SC_SUBCORES
    per_worker = n_rows // n_workers
    n_chunks = per_worker // SC_WINDOW
    assert per_worker * n_workers == n_rows and n_chunks * SC_WINDOW == per_worker and n_chunks % 2 == 0
    mesh = plsc.VectorSubcoreMesh(core_axis_name="c", subcore_axis_name="s")
    return mesh, n_workers, per_worker, n_chunks


def _sc_two_buffer_loop(n_chunks, first, second):
    first(0, 0).start()

    @pl.loop(0, n_chunks, step=2)
    def _(j):
        for b in range(2):
            jj = j + b
            first(jj, b).wait()

            @pl.when(jj + 1 < n_chunks)
            def _():
                @pl.when(jj >= 1)
                def _():
                    second(jj - 1, 1 - b).wait()
                first(jj + 1, 1 - b).start()

            second(jj, b).start()

    second(n_chunks - 2, 0).wait()
    second(n_chunks - 1, 1).wait()


def _sc_row_gather(table, idx):
    n_rows = idx.shape[0]
    width = table.shape[1]
    mesh, n_workers, per_worker, n_chunks = _sc_mesh_and_split(n_rows)

    @functools.partial(
        pl.kernel, mesh=mesh,
        out_type=jax.ShapeDtypeStruct((n_rows, width), table.dtype),
        scratch_types=[pltpu.VMEM((n_chunks, SC_WINDOW), jnp.int32),
                       pltpu.VMEM((2, SC_WINDOW, width), table.dtype),
                       pltpu.SemaphoreType.DMA((2,)),
                       pltpu.SemaphoreType.DMA((2,))],
    )
    def gather(table_hbm, idx_hbm, out_hbm, idx_v, rows_v, sem_in, sem_out):
        wid = lax.axis_index("s") * SC_CORES + lax.axis_index("c")
        base = wid * per_worker
        pltpu.sync_copy(idx_hbm.at[wid], idx_v)

        def fetch(j, b):
            return pltpu.make_async_copy(table_hbm.at[idx_v.at[j]], rows_v.at[b], sem_in.at[b])

        def put(j, b):
            off = pl.multiple_of(base + j * SC_WINDOW, 8)
            return pltpu.make_async_copy(rows_v.at[b], out_hbm.at[pl.ds(off, SC_WINDOW)], sem_out.at[b])

        _sc_two_buffer_loop(n_chunks, fetch, put)

    return gather(table, idx.reshape(n_workers, n_chunks, SC_WINDOW))


def _sc_row_scatter(rows, pos, n_out):
    n_rows, width = rows.shape
    mesh, n_workers, per_worker, n_chunks = _sc_mesh_and_split(n_rows)

    @functools.partial(
        pl.kernel, mesh=mesh,
        out_type=jax.ShapeDtypeStruct((n_out, width), rows.dtype),
        scratch_types=[pltpu.VMEM((n_chunks, SC_WINDOW), jnp.int32),
                       pltpu.VMEM((2, SC_WINDOW, width), rows.dtype),
                       pltpu.SemaphoreType.DMA((2,)),
                       pltpu.SemaphoreType.DMA((2,))],
    )
    def scatter(rows_hbm, pos_hbm, out_hbm, idx_v, rows_v, sem_in, sem_out):
        wid = lax.axis_index("s") * SC_CORES + lax.axis_index("c")
        base = wid * per_worker
        pltpu.sync_copy(pos_hbm.at[wid], idx_v)

        def fetch(j, b):
            off = pl.multiple_of(base + j * SC_WINDOW, 8)
            return pltpu.make_async_copy(rows_hbm.at[pl.ds(off, SC_WINDOW)], rows_v.at[b], sem_in.at[b])

        def put(j, b):
            return pltpu.make_async_copy(rows_v.at[b], out_hbm.at[idx_v.at[j]], sem_out.at[b])

        _sc_two_buffer_loop(n_chunks, fetch, put)

    return scatter(rows, pos.reshape(n_workers, n_chunks, SC_WINDOW))


def _moe_kernel(ea_ref, eb_ref, nv_ref, hs_ref, wga_ref, wgb_ref, wda_ref, wdb_ref, o_ref, wg_s, wd_s):
    i = pl.program_id(0)
    prev = jnp.maximum(i - 1, 0)

    @pl.when(jnp.logical_or(i == 0, ea_ref[i] != ea_ref[prev]))
    def _():
        wg_s[0] = wga_ref[0].astype(BF16)
        wd_s[0] = wda_ref[0].astype(BF16)

    @pl.when(jnp.logical_or(i == 0, eb_ref[i] != eb_ref[prev]))
    def _():
        wg_s[1] = wgb_ref[0].astype(BF16)
        wd_s[1] = wdb_ref[0].astype(BF16)

    @pl.when(nv_ref[i] > 0)
    def _():
        live = lax.broadcasted_iota(jnp.int32, (BM_MOE, 1), 0) < nv_ref[i]
        xb = _unpack_bf16_pairs(jnp.where(live, hs_ref[:, 0:D_PACK], jnp.uint32(0)))
        w_ab = lax.bitcast_convert_type(hs_ref[:, D_PACK:D_ROW], F32)
        w_a = jnp.where(live, w_ab[:, 2:3], 0.0)
        w_b = jnp.where(live, w_ab[:, 3:4], 0.0)
        ga = _dot(xb, wg_s[0])
        act_a = (_silu(ga[:, :D_EXPERT]) * ga[:, D_EXPERT:] * w_a).astype(BF16)
        gb = _dot(xb, wg_s[1])
        act_b = (_silu(gb[:, :D_EXPERT]) * gb[:, D_EXPERT:] * w_b).astype(BF16)
        o_ref[...] = _pack_bf16_pairs(_dot(act_a, wd_s[0]) + _dot(act_b, wd_s[1]))

    @pl.when(nv_ref[i] <= 0)
    def _():
        o_ref[...] = jnp.zeros_like(o_ref)


def _moe(ea, eb, nvalid, hs, w_gu, w_dn):
    nblk = ea.shape[0]
    S = nblk * BM_MOE
    grid_spec = pltpu.PrefetchScalarGridSpec(
        num_scalar_prefetch=3,
        grid=(nblk,),
        in_specs=[
            pl.BlockSpec((BM_MOE, D_ROW), lambda i, ea, eb, nv: (i, 0)),
            pl.BlockSpec((1, D, 2 * D_EXPERT), lambda i, ea, eb, nv: (ea[i], 0, 0)),
            pl.BlockSpec((1, D, 2 * D_EXPERT), lambda i, ea, eb, nv: (eb[i], 0, 0)),
            pl.BlockSpec((1, D_EXPERT, D), lambda i, ea, eb, nv: (ea[i], 0, 0)),
            pl.BlockSpec((1, D_EXPERT, D), lambda i, ea, eb, nv: (eb[i], 0, 0)),
        ],
        out_specs=pl.BlockSpec((BM_MOE, D_PACK), lambda i, ea, eb, nv: (i, 0)),
        scratch_shapes=[pltpu.VMEM((2, D, 2 * D_EXPERT), BF16), pltpu.VMEM((2, D_EXPERT, D), BF16)],
    )
    return pl.pallas_call(
        _moe_kernel,
        out_shape=jax.ShapeDtypeStruct((S, D_PACK), jnp.uint32),
        grid_spec=grid_spec,
        compiler_params=pltpu.CompilerParams(dimension_semantics=("arbitrary",),
                                             vmem_limit_bytes=VMEM_LIMIT),
    )(ea, eb, nvalid, hs, w_gu, w_gu, w_dn, w_dn)


def _final_kernel(x2_ref, m_ref, mod_ref, fg_ref, *rest):
    o_ref = rest[-1]
    gt2 = mod_ref[0][:, 5 * D:6 * D]
    y = x2_ref[...] + gt2 * _unpack_bf16_pairs(m_ref[...]).astype(F32)
    ms = jnp.mean(y * y, axis=-1, keepdims=True)
    o_ref[...] = y * lax.rsqrt(ms + EPS) * fg_ref[...]


def _final(x2, m_part, mod3, fg, L, part, prev_out):
    T = x2.shape[0]
    steps = m_part.shape[0] // TM_FIN
    off = part * steps
    per_b = L // TM_FIN
    in_specs = [
        pl.BlockSpec((TM_FIN, D), lambda i: (i + off, 0)),
        pl.BlockSpec((TM_FIN, D_PACK), lambda i: (i, 0)),
        pl.BlockSpec((1, 1, 6 * D), lambda i: ((i + off) // per_b, 0, 0)),
        pl.BlockSpec((1, D), lambda i: (0, 0)),
    ]
    args = [x2, m_part, mod3, fg]
    aliases = {}
    if prev_out is not None:
        in_specs.append(pl.BlockSpec(memory_space=pl.ANY))
        args.append(prev_out)
        aliases = {len(args) - 1: 0}
    return pl.pallas_call(
        _final_kernel,
        out_shape=jax.ShapeDtypeStruct((T, D), F32),
        grid=(steps,),
        in_specs=in_specs,
        out_specs=pl.BlockSpec((TM_FIN, D), lambda i: (i + off, 0)),
        input_output_aliases=aliases,
        compiler_params=pltpu.CompilerParams(dimension_semantics=("arbitrary",),
                                             vmem_limit_bytes=VMEM_LIMIT),
    )(*args)


def kernel(x, c, ctx, c_ctx, w_ada, b_ada, norm1_g, w_in, ln_a_g, ln_a_b, w_spatial, b_spatial, conv_qkv, a_log,
           dt_bias, onorm_g, w_out, norm2_g, w_group, b_group, w_router, b_router, w_gate_up, w_down, final_g):
    B, L, _ = x.shape
    T = B * L
    assert w_ada.shape[0] == 1 and ctx.shape[1] == TM_IN and L % TM_OUT == 0 and T % (N_COMBINE_PARTS * TM_FIN) == 0

    cond = jnp.concatenate([c, c_ctx[None, :], jnp.zeros((7, D), F32)], axis=0)
    mod = _modulation(cond, w_ada[0], b_ada[0][None, :])
    mod_lat = mod[:B].reshape(B, 1, 6 * D)
    mod_ctx = mod[B:B + 1]

    alog = a_log[0].reshape(1, 2 * HEADS)
    dtb = dt_bias[0].reshape(1, 2 * HEADS)
    alog_row = jnp.pad(alog, ((0, 0), (0, 128 - 2 * HEADS)))
    dtb_row = jnp.pad(dtb, ((0, 0), (0, 128 - 2 * HEADS)))
    alog_col = jnp.pad(alog, ((0, 0), (0, N_AB - 2 * HEADS))).T
    dtb_col = jnp.pad(dtb, ((0, 0), (0, N_AB - 2 * HEADS))).T

    ya, qkv, z, gb, gbt = _in_proj(
        x, ctx, mod_lat, mod_ctx, norm1_g, jnp.swapaxes(w_in, 1, 2), ln_a_g, ln_a_b,
        w_spatial[0].astype(BF16), b_spatial[0].T, conv_qkv[0], alog_row, dtb_row, alog_col, dtb_col)

    yb = _delta(qkv, z, gb, gbt, onorm_g, L)

    wrt = jnp.concatenate([w_group[0].T, w_router[0].T, jnp.zeros((32 - N_GROUPS - N_EXPERTS, D), F32)], axis=0)
    brt = jnp.concatenate([b_group[0], b_router[0], jnp.zeros((32 - N_GROUPS - N_EXPERTS,), F32)])[:, None]
    x2, h, route, cnt = _out_proj(x.reshape(T, D), ya.reshape(T, D_A), yb.reshape(T, D_B), mod_lat,
                                  w_out[0], norm2_g, wrt.astype(BF16), brt, L)

    bucket = route[:, 0, :].reshape(T).astype(jnp.int32)
    rank = route[:, 1, :].reshape(T).astype(jnp.int32)
    counts = cnt[:N_BUCKETS, 0].astype(jnp.int32)
    nblk_b = (counts + BM_MOE - 1) // BM_MOE
    blk_end = jnp.cumsum(nblk_b)
    blk_start = blk_end - nblk_b
    kk = jnp.arange(N_BUCKETS, dtype=jnp.int32)

    def pick(idx, table):
        return jnp.sum(jnp.where(idx[:, None] == kk[None, :], table[None, :], 0), axis=1)

    pos = pick(bucket, blk_start) * BM_MOE + rank
    n_blocks = T // BM_MOE + N_BUCKETS
    S = n_blocks * BM_MOE
    blk = jnp.arange(n_blocks, dtype=jnp.int32)
    used = blk < blk_end[-1]
    bkt = jnp.sum((jnp.minimum(blk, blk_end[-1] - 1)[:, None] >= blk_end[None, :]).astype(jnp.int32), axis=1)
    nvalid = jnp.where(used, jnp.clip(pick(bkt, counts) - (blk - pick(bkt, blk_start)) * BM_MOE, 0, BM_MOE),
                       0).astype(jnp.int32)
    ea = pick(bkt, jnp.asarray(_SLOT_A_EXPERT))
    eb = pick(bkt, jnp.asarray(_SLOT_B_EXPERT))

    hs = _sc_row_scatter(h, pos, S)
    ms = _moe(ea, eb, nvalid, hs, w_gate_up[0], w_down[0])
    pos_parts = pos.reshape(N_COMBINE_PARTS, T // N_COMBINE_PARTS)
    m_parts = [_sc_row_gather(ms, pos_parts[q]) for q in range(N_COMBINE_PARTS)]
    out = None
    for q in range(N_COMBINE_PARTS):
        out = _final(x2, m_parts[q], mod_lat, final_g[None, :], L, q, out)
    return out.reshape(B, L, D)
```

```python
import functools

import jax
import jax.numpy as jnp
import numpy as np
from jax import lax
from jax.experimental import pallas as pl
from jax.experimental.pallas import tpu as pltpu
from jax.experimental.pallas import tpu_sc as plsc

F32 = jnp.float32
BF16 = jnp.bfloat16
EPS = 1e-6

D = 1024
D_A = 512
D_B = 512
HEADS = 4
HD = 128
CHUNK = 128
CONV_W = 5
N_QKV = 3 * D_B
N_MAIN = 2 * D_A + 4 * D_B
N_AB = 16
N_GROUPS = 4
EPG = 4
N_EXPERTS = 16
D_EXPERT = 512
N_PAIRS = 6
N_BUCKETS = N_GROUPS * N_PAIRS
D_PACK = D // 2
D_ROW = D_PACK + 128
PAIR_A = (0, 0, 0, 1, 1, 3)
PAIR_B = (1, 2, 3, 3, 2, 2)
_SLOT_A_EXPERT = np.array([g * EPG + PAIR_A[p] for g in range(N_GROUPS) for p in range(N_PAIRS)], np.int32)
_SLOT_B_EXPERT = np.array([g * EPG + PAIR_B[p] for g in range(N_GROUPS) for p in range(N_PAIRS)], np.int32)

TM_IN = 256
HALO = 8
NB_IN = 2
NB_DELTA = 2
TM_OUT = 1024
OUT_SPLIT = 4
BM_MOE = 256
TM_FIN = 1024
N_COMBINE_PARTS = 4
VMEM_LIMIT = 56 * 1024 * 1024
SC_CORES = 2
SC_SUBCORES = 16
SC_WINDOW = 32

HI = lax.Precision.HIGHEST

_CHUNK_TRIL = np.kron(np.eye(TM_IN // CHUNK, dtype=np.float32), np.tril(np.ones((CHUNK, CHUNK), np.float32)))


def _dot(a, b, precision=None):
    return jnp.dot(a, b, preferred_element_type=F32, precision=precision)


def _dot_nt(a, b):
    return lax.dot_general(a, b, (((1,), (1,)), ((), ())), preferred_element_type=F32)


def _dot_tn(a, b):
    return lax.dot_general(a, b, (((0,), (0,)), ((), ())), preferred_element_type=F32)


def _sigmoid(x):
    return 0.5 + 0.5 * jnp.tanh(0.5 * x)


def _silu(x):
    h = 0.5 * x
    return h + h * jnp.tanh(h)


def _softplus(x):
    return jnp.maximum(x, 0.0) + jnp.log(1.0 + jnp.exp(-jnp.abs(x)))


def _pack_bf16_pairs(x):
    bits = lax.bitcast_convert_type(x.astype(BF16).astype(F32), jnp.uint32)
    return (bits[:, D_PACK:] & jnp.uint32(0xFFFF0000)) | (bits[:, :D_PACK] >> 16)


def _unpack_bf16_pairs(w):
    bits = w
    lo = lax.bitcast_convert_type(bits << 16, F32)
    hi = lax.bitcast_convert_type(bits & jnp.uint32(0xFFFF0000), F32)
    return jnp.concatenate([lo, hi], axis=1).astype(BF16)


def _gelu_tanh(x):
    return 0.5 * x * (1.0 + jnp.tanh(np.sqrt(2.0 / np.pi).astype(np.float32) * (x + 0.044715 * (x * x * x))))


def _mod_kernel(c_ref, w_ref, b_ref, o_ref):
    c = c_ref[...]
    o_ref[...] = _dot(_silu(c), w_ref[...], precision=HI) + b_ref[...]


def _modulation(cond, w_ada, b_ada):
    rows = cond.shape[0]
    tn = 1536
    return pl.pallas_call(
        _mod_kernel,
        out_shape=jax.ShapeDtypeStruct((rows, 6 * D), F32),
        grid=(6 * D // tn,),
        in_specs=[pl.BlockSpec((rows, D), lambda i: (0, 0)),
                  pl.BlockSpec((D, tn), lambda i: (0, i)),
                  pl.BlockSpec((1, tn), lambda i: (0, i))],
        out_specs=pl.BlockSpec((rows, tn), lambda i: (0, i)),
        compiler_params=pltpu.CompilerParams(dimension_semantics=("arbitrary",),
                                             vmem_limit_bytes=VMEM_LIMIT),
    )(cond, w_ada, b_ada)


def _in_kernel(x_ref, xp_ref, xn_ref, ctx_ref, mod_ref, cmod_ref, g1_ref, wt_ref,
               lng_ref, lnb_ref, ws_ref, bst_ref, conv_ref, alog_ref, dtb_ref, alogt_ref, dtbt_ref, tril_ref, triu_ref,
               ya_ref, qkv_ref, z_ref, gb_ref, gbt_ref, wbf_ref, wab_ref, wabt_ref):
    j = pl.program_id(1)

    @pl.when(jnp.logical_and(pl.program_id(0) == 0, j == 0))
    def _():
        for c0 in range(0, N_MAIN, 512):
            wbf_ref[:, c0:c0 + 512] = jnp.transpose(wt_ref[0, c0:c0 + 512, :]).astype(BF16)
        tail = jnp.concatenate([wt_ref[0, N_MAIN:N_MAIN + N_AB, :], jnp.zeros((128 - N_AB, D), F32)], axis=0)
        wabt_ref[...] = tail.astype(BF16)
        wab_ref[...] = jnp.transpose(tail).astype(BF16)

    is_ctx = j == 0
    n_lat_blocks = pl.num_programs(1) - 1

    def one_batch_element(bb):
        mod = mod_ref[bb]
        cm = cmod_ref[...]
        sh = jnp.where(is_ctx, cm[:, 0:D], mod[:, 0:D])
        sc = jnp.where(is_ctx, cm[:, D:2 * D], mod[:, D:2 * D])
        scale = g1_ref[...] * (1.0 + sc)

        xmain = jnp.where(is_ctx, ctx_ref[bb], x_ref[bb])
        xv = jnp.concatenate([xp_ref[bb], xmain, xn_ref[bb]], axis=0)
        xnorm = xv * lax.rsqrt(jnp.mean(xv * xv, axis=-1, keepdims=True) + EPS) * scale + sh
        xe = xnorm.astype(BF16)
        xb = xnorm[HALO:HALO + TM_IN].astype(BF16)

        rid = lax.broadcasted_iota(jnp.int32, (TM_IN + 2 * HALO, 1), 0)
        prev_ok = j >= 2
        next_ok = jnp.logical_and(j >= 1, j < n_lat_blocks)
        valid = jnp.logical_or(jnp.logical_and(rid >= HALO, rid < HALO + TM_IN),
                               jnp.logical_or(jnp.logical_and(rid < HALO, prev_ok),
                                              jnp.logical_and(rid >= HALO + TM_IN, next_ok)))
        pad = (CONV_W - 1) // 2
        c_qkv = 2 * D_A

        def proj(c0, width, halo=True):
            return _dot(xe if halo else xb, wbf_ref[:, c0:c0 + width])

        def conv_act(pq, c0):
            groups = (TM_IN + 2 * HALO) // 8
            x3 = jnp.where(valid, pq, 0.0).reshape(groups, 8, D_B)
            sub = lax.broadcasted_iota(jnp.int32, (1, 8, 1), 1)
            lo, hi = HALO // 8, HALO // 8 + TM_IN // 8
            acc = conv_ref[pad:pad + 1, c0:c0 + D_B] * x3[lo:hi]
            for t in range(CONV_W):
                s = t - pad
                if s == 0:
                    continue
                r = pltpu.roll(x3, (-s) % 8, axis=1)
                if s > 0:
                    sh = jnp.where(sub < 8 - s, r[lo:hi], r[lo + 1:hi + 1])
                else:
                    sh = jnp.where(sub >= -s, r[lo:hi], r[lo - 1:hi - 1])
                acc = acc + conv_ref[t:t + 1, c0:c0 + D_B] * sh
            return _silu(acc.reshape(TM_IN, D_B))

        def store_unit_heads(act, c0, gain):
            for h in range(HEADS):
                t = act[:, h * HD:(h + 1) * HD]
                nrm = lax.rsqrt(jnp.sum(t * t, axis=-1, keepdims=True) + EPS) * gain
                qkv_ref[bb, :, c0 + h * HD:c0 + (h + 1) * HD] = (t * nrm).astype(BF16)

        pq_q = proj(c_qkv, D_B)
        pq_k = proj(c_qkv + D_B, D_B)
        store_unit_heads(conv_act(pq_q, 0), 0, HD ** -0.5)
        pq_v = proj(c_qkv + 2 * D_B, D_B)
        store_unit_heads(conv_act(pq_k, D_B), D_B, 1.0)
        pa_u = proj(0, D_A, halo=False)
        qkv_ref[bb, :, 2 * D_B:] = conv_act(pq_v, 2 * D_B).astype(BF16)
        pa_v = proj(D_A, D_A, halo=False)
        u = _gelu_tanh(pa_u)
        pz = proj(c_qkv + N_QKV, D_B, halo=False)
        v = _gelu_tanh(pa_v)
        mu = jnp.mean(v, axis=-1, keepdims=True)
        vc = v - mu
        var = jnp.mean(vc * vc, axis=-1, keepdims=True)
        vn = (vc * lax.rsqrt(var + EPS) * lng_ref[...] + lnb_ref[...]).astype(BF16)
        z_ref[bb] = pz.astype(BF16)

        bst = bst_ref[...]
        for n in range(TM_IN // CHUNK):
            rows = slice(n * CHUNK, (n + 1) * CHUNK)
            for h in range(HEADS):
                cols = slice(h * HD, (h + 1) * HD)
                s = _dot(ws_ref[h], vn[rows, cols]) + bst[:, h:h + 1]
                ya_ref[bb, rows, cols] = (u[rows, cols] * s).astype(BF16)

        tri_l = tril_ref[...]
        tri_u = triu_ref[...]

        def split3(g):
            hi = g.astype(BF16)
            r1 = g - hi.astype(F32)
            mid = r1.astype(BF16)
            return hi, mid, (r1 - mid.astype(F32)).astype(BF16)

        ab = _dot(xb, wab_ref[...])
        g3 = split3(-jnp.exp(alog_ref[...]) * _softplus(ab + dtb_ref[...]))
        lane = lax.broadcasted_iota(jnp.int32, ab.shape, 1)
        gb = jnp.where(lane < HEADS, _dot(tri_l, g3[0]) + _dot(tri_l, g3[1]) + _dot(tri_l, g3[2]),
                       jnp.where(lane < 2 * HEADS, _dot(tri_u, g3[0]) + _dot(tri_u, g3[1]) + _dot(tri_u, g3[2]),
                                 _sigmoid(ab)))
        gb_ref[bb] = gb[:, 0:N_AB]

        abt = _dot_nt(wabt_ref[0:N_AB, :], xb)
        t3 = split3(-jnp.exp(alogt_ref[...]) * _softplus(abt + dtbt_ref[...]))
        row = lax.broadcasted_iota(jnp.int32, abt.shape, 0)
        gbt_ref[bb] = jnp.where(row < HEADS, _dot(t3[0], tri_u) + _dot(t3[1], tri_u) + _dot(t3[2], tri_u),
                               jnp.where(row < 2 * HEADS, _dot(t3[0], tri_l) + _dot(t3[1], tri_l) + _dot(t3[2], tri_l),
                                         _sigmoid(abt)))

    for bb in range(x_ref.shape[0]):
        one_batch_element(bb)


def _in_proj(x, ctx, mod_lat, mod_ctx, g1, w_in_t, lng, lnb, ws, bst, conv, alog, dtb, alogt, dtbt):
    B, L, _ = x.shape
    n_lat = L // TM_IN
    n_steps = n_lat + 1
    LC = L + TM_IN
    hb = TM_IN // HALO

    def full(shape):
        return pl.BlockSpec(shape, lambda b, j: (0,) * len(shape))

    in_specs = [
        pl.BlockSpec((NB_IN, TM_IN, D), lambda b, j: (b, jnp.maximum(j - 1, 0), 0)),
        pl.BlockSpec((NB_IN, HALO, D), lambda b, j: (b, jnp.clip((j - 1) * hb - 1, 0, L // HALO - 1), 0)),
        pl.BlockSpec((NB_IN, HALO, D), lambda b, j: (b, jnp.clip(j * hb, 0, L // HALO - 1), 0)),
        pl.BlockSpec((NB_IN, TM_IN, D), lambda b, j: (b, 0, 0)),
        pl.BlockSpec((NB_IN, 1, 6 * D), lambda b, j: (b, 0, 0)),
        full((1, 6 * D)), full((1, D)),
        pl.BlockSpec((1, N_MAIN + N_AB, D), lambda b, j: (0, 0, 0), pipeline_mode=pl.Buffered(1)),
        full((1, D_A)), full((1, D_A)), full((HEADS, CHUNK, CHUNK)), full((CHUNK, HEADS)),
        full((CONV_W, N_QKV)), full((1, 128)), full((1, 128)), full((N_AB, 1)), full((N_AB, 1)),
        full((TM_IN, TM_IN)), full((TM_IN, TM_IN)),
    ]
    out_shape = [
        jax.ShapeDtypeStruct((B, L, D_A), BF16),
        jax.ShapeDtypeStruct((B, LC, N_QKV), BF16),
        jax.ShapeDtypeStruct((B, LC, D_B), BF16),
        jax.ShapeDtypeStruct((B, LC, N_AB), F32),
        jax.ShapeDtypeStruct((B, N_AB, LC), F32),
    ]
    out_specs = [
        pl.BlockSpec((NB_IN, TM_IN, D_A), lambda b, j: (b, jnp.maximum(j - 1, 0), 0)),
        pl.BlockSpec((NB_IN, TM_IN, N_QKV), lambda b, j: (b, j, 0)),
        pl.BlockSpec((NB_IN, TM_IN, D_B), lambda b, j: (b, j, 0)),
        pl.BlockSpec((NB_IN, TM_IN, N_AB), lambda b, j: (b, j, 0)),
        pl.BlockSpec((NB_IN, N_AB, TM_IN), lambda b, j: (b, 0, j)),
    ]
    return pl.pallas_call(
        _in_kernel,
        out_shape=out_shape,
        grid=(B // NB_IN, n_steps),
        in_specs=in_specs,
        out_specs=out_specs,
        scratch_shapes=[pltpu.VMEM((D, N_MAIN), BF16), pltpu.VMEM((D, 128), BF16), pltpu.VMEM((128, D), BF16)],
        compiler_params=pltpu.CompilerParams(dimension_semantics=("arbitrary", "arbitrary"),
                                             vmem_limit_bytes=VMEM_LIMIT),
    )(x, x, x, ctx, mod_lat, mod_ctx, g1, w_in_t, lng, lnb, ws, bst, conv, alog, dtb, alogt, dtbt,
      jnp.asarray(_CHUNK_TRIL, BF16), jnp.asarray(_CHUNK_TRIL.T, BF16))


def _delta_kernel(qf_ref, qb_ref, zf_ref, zb_ref, gf_ref, gbk_ref, gtf_ref, gtb_ref, on_ref,
                  y_ref, s_ref, oacc_ref, *, n_ctx, n_lat):
    s = pl.program_id(1)

    @pl.when(s == 0)
    def _():
        s_ref[...] = jnp.zeros_like(s_ref)
        oacc_ref[...] = jnp.zeros_like(oacc_ref)

    row = lax.broadcasted_iota(jnp.int32, (CHUNK, CHUNK), 0)
    col = lax.broadcasted_iota(jnp.int32, (CHUNK, CHUNK), 1)
    low = row > col
    upp = row < col
    same_blk = (row // 16) == (col // 16)
    eye = jnp.where(row == col, 1.0, 0.0).astype(BF16)
    zero = jnp.zeros((CHUNK, CHUNK), BF16)
    onorm = on_ref[...]
    half = n_ctx + n_lat // 2
    second = s >= half
    g_refs = (gf_ref, gbk_ref)
    gt_refs = (gtf_ref, gtb_ref)
    qkv_refs = (qf_ref, qb_ref)
    z_refs = (zf_ref, zb_ref)
    nb = qf_ref.shape[0]
    ps = range(nb * HEADS)

    def halves(xc, unit):
        xb = xc.astype(BF16)
        fill = eye if unit else zero
        return jnp.where(low, xb, fill), jnp.where(upp, xb, fill)

    def as_lhs(hv):
        return jnp.concatenate(hv, axis=1)

    def as_rhs(*hvs):
        cols_ = [jnp.concatenate(hv, axis=0) for hv in hvs]
        return cols_[0] if len(cols_) == 1 else jnp.concatenate(cols_, axis=1)

    def load(d, p, part):
        bb, h = divmod(p, HEADS)
        return qkv_refs[d][bb, :, part * D_B + h * HD:part * D_B + (h + 1) * HD]

    def gcol(d, p, base):
        bb, h = divmod(p, HEADS)
        c = base + d * HEADS + h
        return g_refs[d][bb, :, c:c + 1]

    def grow(d, p, base):
        bb, h = divmod(p, HEADS)
        r = base + d * HEADS + h
        return gt_refs[d][bb, r:r + 1, :]

    def lanes(col):
        return jnp.broadcast_to(col, (CHUNK, HD))

    q = [[load(d, p, 0) for p in ps] for d in range(2)]
    k = [[load(d, p, 1) for p in ps] for d in range(2)]
    v = [[load(d, p, 2) for p in ps] for d in range(2)]
    gcl = [[lanes(gcol(d, p, 0)) for p in ps] for d in range(2)]
    betal = [[lanes(gcol(d, p, 2 * HEADS)) for p in ps] for d in range(2)]
    gr = [[grow(d, p, 0) for p in ps] for d in range(2)]
    betar = [[grow(d, p, 2 * HEADS) for p in ps] for d in range(2)]
    glast = [[gr[0][p][:, CHUNK - 1:CHUNK] for p in ps], [gr[1][p][:, 0:1] for p in ps]]

    gram = [[_dot_nt(jnp.concatenate([q[d][p], k[d][p]], axis=0), k[d][p]) for p in ps] for d in range(2)]
    dec = [jnp.exp(jnp.where(low, gcl[0][p] - gr[0][p], jnp.where(upp, gcl[1][p] - gr[1][p], 0.0))) for p in ps]
    lc = [jnp.where(low, gram[0][p][CHUNK:] * betar[0][p], jnp.where(upp, gram[1][p][CHUNK:] * betar[1][p], 0.0))
          * dec[p] for p in ps]
    qk = [[jnp.where(upp, 0.0, gram[0][p][:CHUNK] * dec[p]).astype(BF16) for p in ps],
          [jnp.where(low, 0.0, gram[1][p][:CHUNK] * dec[p]).astype(BF16) for p in ps]]

    dg = [jnp.where(same_blk, lc[p], 0.0) for p in ps]
    ob = [lc[p] - dg[p] for p in ps]
    d1h = [halves(dg[p], False) for p in ps]
    d2 = [_dot(as_lhs(d1h[p]), as_rhs(d1h[p])) for p in ps]
    p0s = [-dg[p] for p in ps]
    d2h = [halves(d2[p], False) for p in ps]
    p0h = [halves(p0s[p], True) for p in ps]
    o2 = [_dot(as_lhs(d2h[p]), as_rhs(d2h[p], p0h[p])) for p in ps]
    p1s = [p0s[p] + o2[p][:, CHUNK:] for p in ps]
    d4h = [halves(o2[p][:, :CHUNK], False) for p in ps]
    p1h = [halves(p1s[p], True) for p in ps]
    o3 = [_dot(as_lhs(d4h[p]), as_rhs(d4h[p], p1h[p])) for p in ps]
    p2s = [p1s[p] + o3[p][:, CHUNK:] for p in ps]
    d8h = [halves(o3[p][:, :CHUNK], False) for p in ps]
    p2h = [halves(p2s[p], True) for p in ps]
    p3s = [p2s[p] + _dot(as_lhs(d8h[p]), as_rhs(p2h[p])) for p in ps]
    p3h = [halves(p3s[p], True) for p in ps]
    obh = [halves(ob[p], False) for p in ps]
    n1h = [halves(_dot(as_lhs(p3h[p]), as_rhs(obh[p])), False) for p in ps]
    o6 = [_dot(as_lhs(n1h[p]), as_rhs(n1h[p], p3h[p])) for p in ps]
    r0s = [p3s[p] - o6[p][:, CHUNK:] for p in ps]
    n2h = [halves(o6[p][:, :CHUNK], False) for p in ps]
    r0h = [halves(r0s[p], True) for p in ps]
    o7 = [_dot(as_lhs(n2h[p]), as_rhs(n2h[p], r0h[p])) for p in ps]
    r1s = [r0s[p] + o7[p][:, CHUNK:] for p in ps]
    n4h = [halves(o7[p][:, :CHUNK], False) for p in ps]
    r1h = [halves(r1s[p], True) for p in ps]
    tinv = [halves(r1s[p] + _dot(as_lhs(n4h[p]), as_rhs(r1h[p])), True) for p in ps]

    offs = []
    for d in range(2):
        lat_chunk = (s - n_ctx) if d == 0 else (n_ctx + n_lat - 1 - s)
        off = pl.multiple_of(jnp.clip(lat_chunk, 0, n_lat - 1) * CHUNK, CHUNK)
        sidx = [(p // HEADS * 2 + d) * HEADS + p % HEADS for p in ps]
        egc = [jnp.exp(gcl[d][p]) for p in ps]
        kf = [k[d][p].astype(F32) for p in ps]
        rhs = [jnp.concatenate([v[d][p], (kf[p] * egc[p]).astype(BF16)], axis=1) for p in ps]
        uw = [_dot(tinv[p][d], rhs[p]) for p in ps]
        qd = [q[d][p].astype(F32) * egc[p] for p in ps]
        kd = [(kf[p] * jnp.exp(glast[d][p] - gcl[d][p])).astype(BF16) for p in ps]
        st = [s_ref[sidx[p]] for p in ps]
        a1 = [_dot(jnp.concatenate([uw[p][:, HD:] * betal[d][p], qd[p]], axis=0).astype(BF16), st[p].astype(BF16))
              for p in ps]
        vnew = [(uw[p][:, :HD] * betal[d][p] - a1[p][:CHUNK]).astype(BF16) for p in ps]
        o = [a1[p][CHUNK:] + _dot(qk[d][p], vnew[p]) for p in ps]
        for p in ps:
            s_ref[sidx[p]] = st[p] * jnp.exp(glast[d][p]) + _dot_tn(kd[p], vnew[p])
        for p in ps:
            bb, h = divmod(p, HEADS)
            cols = slice(h * HD, (h + 1) * HD)
            oacc_ref[bb, pl.ds(off, CHUNK), cols] = (
                jnp.where(second, oacc_ref[bb, pl.ds(off, CHUNK), cols], 0.0) + o[p])
        offs.append(off)

    @pl.when(second)
    def _():
        for d in range(2):
            for p in ps:
                bb, h = divmod(p, HEADS)
                cols = slice(h * HD, (h + 1) * HD)
                tot = oacc_ref[bb, pl.ds(offs[d], CHUNK), cols]
                ms = jnp.mean(tot * tot, axis=-1, keepdims=True)
                zz = z_refs[d][bb, :, cols].astype(F32)
                y_ref[bb, pl.ds(offs[d], CHUNK), cols] = (
                    tot * lax.rsqrt(ms + EPS) * onorm * _silu(zz)).astype(BF16)


def _delta(qkv, z, gb, gbt, onorm, L):
    B, LC, _ = qkv.shape
    n_all = LC // CHUNK
    n_lat = L // CHUNK
    n_ctx = n_all - n_lat

    def cf(s):
        return s

    def cb(s):
        return jnp.where(s < n_ctx, n_ctx - 1 - s, n_all + n_ctx - 1 - s)

    in_specs = [
        pl.BlockSpec((NB_DELTA, CHUNK, N_QKV), lambda b, s: (b, cf(s), 0)),
        pl.BlockSpec((NB_DELTA, CHUNK, N_QKV), lambda b, s: (b, cb(s), 0)),
        pl.BlockSpec((NB_DELTA, CHUNK, D_B), lambda b, s: (b, cf(s), 0)),
        pl.BlockSpec((NB_DELTA, CHUNK, D_B), lambda b, s: (b, cb(s), 0)),
        pl.BlockSpec((NB_DELTA, CHUNK, N_AB), lambda b, s: (b, cf(s), 0)),
        pl.BlockSpec((NB_DELTA, CHUNK, N_AB), lambda b, s: (b, cb(s), 0)),
        pl.BlockSpec((NB_DELTA, N_AB, CHUNK), lambda b, s: (b, 0, cf(s))),
        pl.BlockSpec((NB_DELTA, N_AB, CHUNK), lambda b, s: (b, 0, cb(s))),
        pl.BlockSpec((1, HD), lambda b, s: (0, 0)),
    ]
    return pl.pallas_call(
        functools.partial(_delta_kernel, n_ctx=n_ctx, n_lat=n_lat),
        out_shape=jax.ShapeDtypeStruct((B, L, D_B), BF16),
        grid=(B // NB_DELTA, n_all),
        in_specs=in_specs,
        out_specs=pl.BlockSpec((NB_DELTA, L, D_B), lambda b, s: (b, 0, 0)),
        scratch_shapes=[pltpu.VMEM((NB_DELTA * 2 * HEADS, HD, HD), F32), pltpu.VMEM((NB_DELTA, L, D_B), F32)],
        compiler_params=pltpu.CompilerParams(dimension_semantics=("arbitrary", "arbitrary"),
                                             vmem_limit_bytes=VMEM_LIMIT),
    )(qkv, qkv, z, z, gb, gb, gbt, gbt, onorm)


def _out_kernel(x_ref, ya_ref, yb_ref, mod_ref, wo_ref, g2_ref, wrt_ref, brt_ref, tri_ref,
                x2_ref, h_ref, route_ref, cnt_ref, base_ref, wbf_ref):
    i = pl.program_id(0)

    @pl.when(i == 0)
    def _():
        base_ref[...] = jnp.zeros_like(base_ref)
        wbf_ref[...] = wo_ref[...].astype(BF16)

    mod = mod_ref[0]
    gt1 = mod[:, 2 * D:3 * D]
    sh2 = mod[:, 3 * D:4 * D]
    sc2 = mod[:, 4 * D:5 * D]
    scale2 = g2_ref[...] * (1.0 + sc2)
    sub = TM_OUT // OUT_SPLIT
    hbs = []
    for r in range(OUT_SPLIT):
        rows = slice(r * sub, (r + 1) * sub)
        mix = _dot(ya_ref[rows, :], wbf_ref[0:D_A, :]) + _dot(yb_ref[rows, :], wbf_ref[D_A:, :])
        x2 = x_ref[rows, :] + gt1 * mix
        x2_ref[rows, :] = x2
        ms = jnp.mean(x2 * x2, axis=-1, keepdims=True)
        hv = x2 * lax.rsqrt(ms + EPS) * scale2 + sh2
        hbs.append(hv.astype(BF16))
        h_ref[rows, 0:D_PACK] = _pack_bf16_pairs(hv)
    hb = jnp.concatenate(hbs, axis=0)

    lt = _dot_nt(wrt_ref[...], hb) + brt_ref[...]
    gl = [lt[r:r + 1, :] for r in range(N_GROUPS)]
    gmax = jnp.maximum(jnp.maximum(gl[0], gl[1]), jnp.maximum(gl[2], gl[3]))
    gsel = jnp.where(gl[0] == gmax, 0, jnp.where(gl[1] == gmax, 1, jnp.where(gl[2] == gmax, 2, 3)))
    p_g = 1.0 / (jnp.exp(gl[0] - gmax) + jnp.exp(gl[1] - gmax) + jnp.exp(gl[2] - gmax) + jnp.exp(gl[3] - gmax))
    el = []
    for e in range(EPG):
        r = [lt[N_GROUPS + g * EPG + e:N_GROUPS + g * EPG + e + 1, :] for g in range(N_GROUPS)]
        el.append(jnp.where(gsel == 0, r[0], jnp.where(gsel == 1, r[1], jnp.where(gsel == 2, r[2], r[3]))))
    m1 = jnp.maximum(jnp.maximum(el[0], el[1]), jnp.maximum(el[2], el[3]))
    i1 = jnp.where(el[0] == m1, 0, jnp.where(el[1] == m1, 1, jnp.where(el[2] == m1, 2, 3)))
    neg = jnp.float32(-jnp.inf)
    el2 = [jnp.where(i1 == e, neg, el[e]) for e in range(EPG)]
    m2 = jnp.maximum(jnp.maximum(el2[0], el2[1]), jnp.maximum(el2[2], el2[3]))
    i2 = jnp.where(jnp.logical_and(el2[0] == m2, i1 != 0), 0,
                   jnp.where(jnp.logical_and(el2[1] == m2, i1 != 1), 1,
                             jnp.where(jnp.logical_and(el2[2] == m2, i1 != 2), 2, 3)))
    t = jnp.exp(m2 - m1)
    w1 = p_g / (1.0 + t)
    w2 = p_g * t / (1.0 + t)
    first_low = i1 < i2
    ea = jnp.where(first_low, i1, i2)
    eb = jnp.where(first_low, i2, i1)
    w_a = jnp.where(first_low, w1, w2)
    w_b = jnp.where(first_low, w2, w1)
    pair = jnp.where(ea == 0, eb - 1, jnp.where(ea == 1, jnp.where(eb == 3, 3, 4), 5))
    swap = pair == 5
    w_a, w_b = jnp.where(swap, w_b, w_a), jnp.where(swap, w_a, w_b)
    bucket = gsel * N_PAIRS + pair

    tm = bucket.shape[1]
    rows = lax.broadcasted_iota(jnp.int32, (32, tm), 0)
    onehot = jnp.where(rows == bucket, 1.0, 0.0).astype(F32)
    prefix = _dot(onehot.astype(BF16), tri_ref[...])
    base = base_ref[:, 0:1]
    rank = jnp.sum(onehot * (prefix - 1.0 + base), axis=0, keepdims=True)
    newbase = base + prefix[:, tm - 1:tm]
    base_ref[...] = jnp.broadcast_to(newbase, base_ref.shape)
    cnt_ref[...] = jnp.broadcast_to(newbase, cnt_ref.shape)
    route = jnp.concatenate([bucket.astype(F32), rank, w_a, w_b, jnp.zeros((4, tm), F32)], axis=0)
    route_ref[0] = route
    route_t = jnp.transpose(jnp.concatenate([route, jnp.zeros((120, tm), F32)], axis=0))
    h_ref[:, D_PACK:D_ROW] = lax.bitcast_convert_type(route_t, jnp.uint32)


def _out_proj(x2d, ya, yb, mod3, w_o, g2, wrt, brt, L):
    T = x2d.shape[0]
    nb = T // TM_OUT
    per_b = L // TM_OUT
    out_shape = [
        jax.ShapeDtypeStruct((T, D), F32),
        jax.ShapeDtypeStruct((T, D_ROW), jnp.uint32),
        jax.ShapeDtypeStruct((nb, 8, TM_OUT), F32),
        jax.ShapeDtypeStruct((32, 128), F32),
    ]
    return pl.pallas_call(
        _out_kernel,
        out_shape=out_shape,
        grid=(nb,),
        in_specs=[
            pl.BlockSpec((TM_OUT, D), lambda i: (i, 0)),
            pl.BlockSpec((TM_OUT, D_A), lambda i: (i, 0)),
            pl.BlockSpec((TM_OUT, D_B), lambda i: (i, 0)),
            pl.BlockSpec((1, 1, 6 * D), lambda i: (i // per_b, 0, 0)),
            pl.BlockSpec((D, D), lambda i: (0, 0), pipeline_mode=pl.Buffered(1)),
            pl.BlockSpec((1, D), lambda i: (0, 0)),
            pl.BlockSpec((32, D), lambda i: (0, 0)),
            pl.BlockSpec((32, 1), lambda i: (0, 0)),
            pl.BlockSpec((TM_OUT, TM_OUT), lambda i: (0, 0), pipeline_mode=pl.Buffered(1)),
        ],
        out_specs=[
            pl.BlockSpec((TM_OUT, D), lambda i: (i, 0)),
            pl.BlockSpec((TM_OUT, D_ROW), lambda i: (i, 0)),
            pl.BlockSpec((1, 8, TM_OUT), lambda i: (i, 0, 0)),
            pl.BlockSpec((32, 128), lambda i: (0, 0)),
        ],
        scratch_shapes=[pltpu.VMEM((32, 128), F32), pltpu.VMEM((D, D), BF16)],
        compiler_params=pltpu.CompilerParams(dimension_semantics=("arbitrary",),
                                             vmem_limit_bytes=VMEM_LIMIT),
    )(x2d, ya, yb, mod3, w_o, g2, wrt, brt, jnp.asarray(np.triu(np.ones((TM_OUT, TM_OUT), np.float32)), BF16))


def _sc_mesh_and_split(n_rows):
    n_workers = SC_CORES * SC_SUBCORES
    per_worker = n_rows // n_workers
    n_chunks = per_worker // SC_WINDOW
    assert per_worker * n_workers == n_rows and n_chunks * SC_WINDOW == per_worker and n_chunks % 2 == 0
    mesh = plsc.VectorSubcoreMesh(core_axis_name="c", subcore_axis_name="s")
    return mesh, n_workers, per_worker, n_chunks


def _sc_two_buffer_loop(n_chunks, first, second):
    first(0, 0).start()

    @pl.loop(0, n_chunks, step=2)
    def _(j):
        for b in range(2):
            jj = j + b
            first(jj, b).wait()

            @pl.when(jj + 1 < n_chunks)
            def _():
                @pl.when(jj >= 1)
                def _():
                    second(jj - 1, 1 - b).wait()
                first(jj + 1, 1 - b).start()

            second(jj, b).start()

    second(n_chunks - 2, 0).wait()
    second(n_chunks - 1, 1).wait()


def _sc_row_gather(table, idx):
    n_rows = idx.shape[0]
    width = table.shape[1]
    mesh, n_workers, per_worker, n_chunks = _sc_mesh_and_split(n_rows)

    @functools.partial(
        pl.kernel, mesh=mesh,
        out_type=jax.ShapeDtypeStruct((n_rows, width), table.dtype),
        scratch_types=[pltpu.VMEM((n_chunks, SC_WINDOW), jnp.int32),
                       pltpu.VMEM((2, SC_WINDOW, width), table.dtype),
                       pltpu.SemaphoreType.DMA((2,)),
                       pltpu.SemaphoreType.DMA((2,))],
    )
    def gather(table_hbm, idx_hbm, out_hbm, idx_v, rows_v, sem_in, sem_out):
        wid = lax.axis_index("s") * SC_CORES + lax.axis_index("c")
        base = wid * per_worker
        pltpu.sync_copy(idx_hbm.at[wid], idx_v)

        def fetch(j, b):
            return pltpu.make_async_copy(table_hbm.at[idx_v.at[j]], rows_v.at[b], sem_in.at[b])

        def put(j, b):
            off = pl.multiple_of(base + j * SC_WINDOW, 8)
            return pltpu.make_async_copy(rows_v.at[b], out_hbm.at[pl.ds(off, SC_WINDOW)], sem_out.at[b])

        _sc_two_buffer_loop(n_chunks, fetch, put)

    return gather(table, idx.reshape(n_workers, n_chunks, SC_WINDOW))


def _sc_row_scatter(rows, pos, n_out):
    n_rows, width = rows.shape
    mesh, n_workers, per_worker, n_chunks = _sc_mesh_and_split(n_rows)

    @functools.partial(
        pl.kernel, mesh=mesh,
        out_type=jax.ShapeDtypeStruct((n_out, width), rows.dtype),
        scratch_types=[pltpu.VMEM((n_chunks, SC_WINDOW), jnp.int32),
                       pltpu.VMEM((2, SC_WINDOW, width), rows.dtype),
                       pltpu.SemaphoreType.DMA((2,)),
                       pltpu.SemaphoreType.DMA((2,))],
    )
    def scatter(rows_hbm, pos_hbm, out_hbm, idx_v, rows_v, sem_in, sem_out):
        wid = lax.axis_index("s") * SC_CORES + lax.axis_index("c")
        base = wid * per_worker
        pltpu.sync_copy(pos_hbm.at[wid], idx_v)

        def fetch(j, b):
            off = pl.multiple_of(base + j * SC_WINDOW, 8)
            return pltpu.make_async_copy(rows_hbm.at[pl.ds(off, SC_WINDOW)], rows_v.at[b], sem_in.at[b])

        def put(j, b):
            return pltpu.make_async_copy(rows_v.at[b], out_hbm.at[idx_v.at[j]], sem_out.at[b])

        _sc_two_buffer_loop(n_chunks, fetch, put)

    return scatter(rows, pos.reshape(n_workers, n_chunks, SC_WINDOW))


def _moe_kernel(ea_ref, eb_ref, nv_ref, hs_ref, wga_ref, wgb_ref, wda_ref, wdb_ref, o_ref, wg_s, wd_s):
    i = pl.program_id(0)
    prev = jnp.maximum(i - 1, 0)

    @pl.when(jnp.logical_or(i == 0, ea_ref[i] != ea_ref[prev]))
    def _():
        wg_s[0] = wga_ref[0].astype(BF16)
        wd_s[0] = wda_ref[0].astype(BF16)

    @pl.when(jnp.logical_or(i == 0, eb_ref[i] != eb_ref[prev]))
    def _():
        wg_s[1] = wgb_ref[0].astype(BF16)
        wd_s[1] = wdb_ref[0].astype(BF16)

    @pl.when(nv_ref[i] > 0)
    def _():
        live = lax.broadcasted_iota(jnp.int32, (BM_MOE, 1), 0) < nv_ref[i]
        xb = _unpack_bf16_pairs(jnp.where(live, hs_ref[:, 0:D_PACK], jnp.uint32(0)))
        w_ab = lax.bitcast_convert_type(hs_ref[:, D_PACK:D_ROW], F32)
        w_a = jnp.where(live, w_ab[:, 2:3], 0.0)
        w_b = jnp.where(live, w_ab[:, 3:4], 0.0)
        ga = _dot(xb, wg_s[0])
        act_a = (_silu(ga[:, :D_EXPERT]) * ga[:, D_EXPERT:] * w_a).astype(BF16)
        gb = _dot(xb, wg_s[1])
        act_b = (_silu(gb[:, :D_EXPERT]) * gb[:, D_EXPERT:] * w_b).astype(BF16)
        o_ref[...] = _pack_bf16_pairs(_dot(act_a, wd_s[0]) + _dot(act_b, wd_s[1]))

    @pl.when(nv_ref[i] <= 0)
    def _():
        o_ref[...] = jnp.zeros_like(o_ref)


def _moe(ea, eb, nvalid, hs, w_gu, w_dn):
    nblk = ea.shape[0]
    S = nblk * BM_MOE
    grid_spec = pltpu.PrefetchScalarGridSpec(
        num_scalar_prefetch=3,
        grid=(nblk,),
        in_specs=[
            pl.BlockSpec((BM_MOE, D_ROW), lambda i, ea, eb, nv: (i, 0)),
            pl.BlockSpec((1, D, 2 * D_EXPERT), lambda i, ea, eb, nv: (ea[i], 0, 0)),
            pl.BlockSpec((1, D, 2 * D_EXPERT), lambda i, ea, eb, nv: (eb[i], 0, 0)),
            pl.BlockSpec((1, D_EXPERT, D), lambda i, ea, eb, nv: (ea[i], 0, 0)),
            pl.BlockSpec((1, D_EXPERT, D), lambda i, ea, eb, nv: (eb[i], 0, 0)),
        ],
        out_specs=pl.BlockSpec((BM_MOE, D_PACK), lambda i, ea, eb, nv: (i, 0)),
        scratch_shapes=[pltpu.VMEM((2, D, 2 * D_EXPERT), BF16), pltpu.VMEM((2, D_EXPERT, D), BF16)],
    )
    return pl.pallas_call(
        _moe_kernel,
        out_shape=jax.ShapeDtypeStruct((S, D_PACK), jnp.uint32),
        grid_spec=grid_spec,
        compiler_params=pltpu.CompilerParams(dimension_semantics=("arbitrary",),
                                             vmem_limit_bytes=VMEM_LIMIT),
    )(ea, eb, nvalid, hs, w_gu, w_gu, w_dn, w_dn)


def _final_kernel(x2_ref, m_ref, mod_ref, fg_ref, *rest):
    o_ref = rest[-1]
    gt2 = mod_ref[0][:, 5 * D:6 * D]
    y = x2_ref[...] + gt2 * _unpack_bf16_pairs(m_ref[...]).astype(F32)
    ms = jnp.mean(y * y, axis=-1, keepdims=True)
    o_ref[...] = y * lax.rsqrt(ms + EPS) * fg_ref[...]


def _final(x2, m_part, mod3, fg, L, part, prev_out):
    T = x2.shape[0]
    steps = m_part.shape[0] // TM_FIN
    off = part * steps
    per_b = L // TM_FIN
    in_specs = [
        pl.BlockSpec((TM_FIN, D), lambda i: (i + off, 0)),
        pl.BlockSpec((TM_FIN, D_PACK), lambda i: (i, 0)),
        pl.BlockSpec((1, 1, 6 * D), lambda i: ((i + off) // per_b, 0, 0)),
        pl.BlockSpec((1, D), lambda i: (0, 0)),
    ]
    args = [x2, m_part, mod3, fg]
    aliases = {}
    if prev_out is not None:
        in_specs.append(pl.BlockSpec(memory_space=pl.ANY))
        args.append(prev_out)
        aliases = {len(args) - 1: 0}
    return pl.pallas_call(
        _final_kernel,
        out_shape=jax.ShapeDtypeStruct((T, D), F32),
        grid=(steps,),
        in_specs=in_specs,
        out_specs=pl.BlockSpec((TM_FIN, D), lambda i: (i + off, 0)),
        input_output_aliases=aliases,
        compiler_params=pltpu.CompilerParams(dimension_semantics=("arbitrary",),
                                             vmem_limit_bytes=VMEM_LIMIT),
    )(*args)


def kernel(x, c, ctx, c_ctx, w_ada, b_ada, norm1_g, w_in, ln_a_g, ln_a_b, w_spatial, b_spatial, conv_qkv, a_log,
           dt_bias, onorm_g, w_out, norm2_g, w_group, b_group, w_router, b_router, w_gate_up, w_down, final_g):
    B, L, _ = x.shape
    T = B * L
    assert w_ada.shape[0] == 1 and ctx.shape[1] == TM_IN and L % TM_OUT == 0 and T % (N_COMBINE_PARTS * TM_FIN) == 0

    cond = jnp.concatenate([c, c_ctx[None, :], jnp.zeros((7, D), F32)], axis=0)
    mod = _modulation(cond, w_ada[0], b_ada[0][None, :])
    mod_lat = mod[:B].reshape(B, 1, 6 * D)
    mod_ctx = mod[B:B + 1]

    alog = a_log[0].reshape(1, 2 * HEADS)
    dtb = dt_bias[0].reshape(1, 2 * HEADS)
    alog_row = jnp.pad(alog, ((0, 0), (0, 128 - 2 * HEADS)))
    dtb_row = jnp.pad(dtb, ((0, 0), (0, 128 - 2 * HEADS)))
    alog_col = jnp.pad(alog, ((0, 0), (0, N_AB - 2 * HEADS))).T
    dtb_col = jnp.pad(dtb, ((0, 0), (0, N_AB - 2 * HEADS))).T

    ya, qkv, z, gb, gbt = _in_proj(
        x, ctx, mod_lat, mod_ctx, norm1_g, jnp.swapaxes(w_in, 1, 2), ln_a_g, ln_a_b,
        w_spatial[0].astype(BF16), b_spatial[0].T, conv_qkv[0], alog_row, dtb_row, alog_col, dtb_col)

    yb = _delta(qkv, z, gb, gbt, onorm_g, L)

    wrt = jnp.concatenate([w_group[0].T, w_router[0].T, jnp.zeros((32 - N_GROUPS - N_EXPERTS, D), F32)], axis=0)
    brt = jnp.concatenate([b_group[0], b_router[0], jnp.zeros((32 - N_GROUPS - N_EXPERTS,), F32)])[:, None]
    x2, h, route, cnt = _out_proj(x.reshape(T, D), ya.reshape(T, D_A), yb.reshape(T, D_B), mod_lat,
                                  w_out[0], norm2_g, wrt.astype(BF16), brt, L)

    bucket = route[:, 0, :].reshape(T).astype(jnp.int32)
    rank = route[:, 1, :].reshape(T).astype(jnp.int32)
    counts = cnt[:N_BUCKETS, 0].astype(jnp.int32)
    nblk_b = (counts + BM_MOE - 1) // BM_MOE
    blk_end = jnp.cumsum(nblk_b)
    blk_start = blk_end - nblk_b
    kk = jnp.arange(N_BUCKETS, dtype=jnp.int32)

    def pick(idx, table):
        return jnp.sum(jnp.where(idx[:, None] == kk[None, :], table[None, :], 0), axis=1)

    pos = pick(bucket, blk_start) * BM_MOE + rank
    n_blocks = T // BM_MOE + N_BUCKETS
    S = n_blocks * BM_MOE
    blk = jnp.arange(n_blocks, dtype=jnp.int32)
    used = blk < blk_end[-1]
    bkt = jnp.sum((jnp.minimum(blk, blk_end[-1] - 1)[:, None] >= blk_end[None, :]).astype(jnp.int32), axis=1)
    nvalid = jnp.where(used, jnp.clip(pick(bkt, counts) - (blk - pick(bkt, blk_start)) * BM_MOE, 0, BM_MOE),
                       0).astype(jnp.int32)
    ea = pick(bkt, jnp.asarray(_SLOT_A_EXPERT))
    eb = pick(bkt, jnp.asarray(_SLOT_B_EXPERT))

    hs = _sc_row_scatter(h, pos, S)
    ms = _moe(ea, eb, nvalid, hs, w_gate_up[0], w_down[0])
    pos_parts = pos.reshape(N_COMBINE_PARTS, T // N_COMBINE_PARTS)
    m_parts = [_sc_row_gather(ms, pos_parts[q]) for q in range(N_COMBINE_PARTS)]
    out = None
    for q in range(N_COMBINE_PARTS):
        out = _final(x2, m_parts[q], mod_lat, final_g[None, :], L, q, out)
    return out.reshape(B, L, D)
```

```python
import functools

import jax
import jax.numpy as jnp
import numpy as np
from jax import lax
from jax.experimental import pallas as pl
from jax.experimental.pallas import tpu as pltpu
from jax.experimental.pallas import tpu_sc as plsc

F32 = jnp.float32
BF16 = jnp.bfloat16
EPS = 1e-6

D = 1024
D_A = 512
D_B = 512
HEADS = 4
HD = 128
CHUNK = 128
CONV_W = 5
N_QKV = 3 * D_B
N_MAIN = 2 * D_A + 4 * D_B
N_AB = 16
N_GROUPS = 4
EPG = 4
N_EXPERTS = 16
D_EXPERT = 512
N_PAIRS = 6
N_BUCKETS = N_GROUPS * N_PAIRS
D_PACK = D // 2
D_ROW = D_PACK + 128
PAIR_A = (0, 0, 0, 1, 1, 3)
PAIR_B = (1, 2, 3, 3, 2, 2)
_SLOT_A_EXPERT = np.array([g * EPG + PAIR_A[p] for g in range(N_GROUPS) for p in range(N_PAIRS)], np.int32)
_SLOT_B_EXPERT = np.array([g * EPG + PAIR_B[p] for g in range(N_GROUPS) for p in range(N_PAIRS)], np.int32)

TM_IN = 256
HALO = 8
NB_IN = 2
NB_DELTA = 2
TM_OUT = 1024
OUT_SPLIT = 4
BM_MOE = 256
TM_FIN = 1024
N_COMBINE_PARTS = 4
VMEM_LIMIT = 56 * 1024 * 1024
SC_CORES = 2
SC_SUBCORES = 16
SC_WINDOW = 32

HI = lax.Precision.HIGHEST

_CHUNK_TRIL = np.kron(np.eye(TM_IN // CHUNK, dtype=np.float32), np.tril(np.ones((CHUNK, CHUNK), np.float32)))


def _dot(a, b, precision=None):
    return jnp.dot(a, b, preferred_element_type=F32, precision=precision)


def _dot_nt(a, b):
    return lax.dot_general(a, b, (((1,), (1,)), ((), ())), preferred_element_type=F32)


def _dot_tn(a, b):
    return lax.dot_general(a, b, (((0,), (0,)), ((), ())), preferred_element_type=F32)


def _sigmoid(x):
    return 0.5 + 0.5 * jnp.tanh(0.5 * x)


def _silu(x):
    h = 0.5 * x
    return h + h * jnp.tanh(h)


def _softplus(x):
    return jnp.maximum(x, 0.0) + jnp.log(1.0 + jnp.exp(-jnp.abs(x)))


def _pack_bf16_pairs(x):
    bits = lax.bitcast_convert_type(x.astype(BF16).astype(F32), jnp.uint32)
    return (bits[:, D_PACK:] & jnp.uint32(0xFFFF0000)) | (bits[:, :D_PACK] >> 16)


def _unpack_bf16_pairs(w):
    bits = w
    lo = lax.bitcast_convert_type(bits << 16, F32)
    hi = lax.bitcast_convert_type(bits & jnp.uint32(0xFFFF0000), F32)
    return jnp.concatenate([lo, hi], axis=1).astype(BF16)


def _gelu_tanh(x):
    return 0.5 * x * (1.0 + jnp.tanh(np.sqrt(2.0 / np.pi).astype(np.float32) * (x + 0.044715 * (x * x * x))))


def _mod_kernel(c_ref, w_ref, b_ref, o_ref):
    c = c_ref[...]
    o_ref[...] = _dot(_silu(c), w_ref[...], precision=HI) + b_ref[...]


def _modulation(cond, w_ada, b_ada):
    rows = cond.shape[0]
    tn = 1536
    return pl.pallas_call(
        _mod_kernel,
        out_shape=jax.ShapeDtypeStruct((rows, 6 * D), F32),
        grid=(6 * D // tn,),
        in_specs=[pl.BlockSpec((rows, D), lambda i: (0, 0)),
                  pl.BlockSpec((D, tn), lambda i: (0, i)),
                  pl.BlockSpec((1, tn), lambda i: (0, i))],
        out_specs=pl.BlockSpec((rows, tn), lambda i: (0, i)),
        compiler_params=pltpu.CompilerParams(dimension_semantics=("arbitrary",),
                                             vmem_limit_bytes=VMEM_LIMIT),
    )(cond, w_ada, b_ada)


def _in_kernel(x_ref, xp_ref, xn_ref, ctx_ref, mod_ref, cmod_ref, g1_ref, wt_ref,
               lng_ref, lnb_ref, ws_ref, bst_ref, conv_ref, alog_ref, dtb_ref, alogt_ref, dtbt_ref, tril_ref, triu_ref,
               ya_ref, qkv_ref, z_ref, gb_ref, gbt_ref, wbf_ref, wab_ref, wabt_ref):
    j = pl.program_id(1)

    @pl.when(jnp.logical_and(pl.program_id(0) == 0, j == 0))
    def _():
        for c0 in range(0, N_MAIN, 512):
            wbf_ref[:, c0:c0 + 512] = jnp.transpose(wt_ref[0, c0:c0 + 512, :]).astype(BF16)
        tail = jnp.concatenate([wt_ref[0, N_MAIN:N_MAIN + N_AB, :], jnp.zeros((128 - N_AB, D), F32)], axis=0)
        wabt_ref[...] = tail.astype(BF16)
        wab_ref[...] = jnp.transpose(tail).astype(BF16)

    is_ctx = j == 0
    n_lat_blocks = pl.num_programs(1) - 1

    def one_batch_element(bb):
        mod = mod_ref[bb]
        cm = cmod_ref[...]
        sh = jnp.where(is_ctx, cm[:, 0:D], mod[:, 0:D])
        sc = jnp.where(is_ctx, cm[:, D:2 * D], mod[:, D:2 * D])
        scale = g1_ref[...] * (1.0 + sc)

        xmain = jnp.where(is_ctx, ctx_ref[bb], x_ref[bb])
        xv = jnp.concatenate([xp_ref[bb], xmain, xn_ref[bb]], axis=0)
        xnorm = xv * lax.rsqrt(jnp.mean(xv * xv, axis=-1, keepdims=True) + EPS) * scale + sh
        xe = xnorm.astype(BF16)
        xb = xnorm[HALO:HALO + TM_IN].astype(BF16)

        rid = lax.broadcasted_iota(jnp.int32, (TM_IN + 2 * HALO, 1), 0)
        prev_ok = j >= 2
        next_ok = jnp.logical_and(j >= 1, j < n_lat_blocks)
        valid = jnp.logical_or(jnp.logical_and(rid >= HALO, rid < HALO + TM_IN),
                               jnp.logical_or(jnp.logical_and(rid < HALO, prev_ok),
                                              jnp.logical_and(rid >= HALO + TM_IN, next_ok)))
        pad = (CONV_W - 1) // 2
        c_qkv = 2 * D_A

        def proj(c0, width, halo=True):
            return _dot(xe if halo else xb, wbf_ref[:, c0:c0 + width])

        def conv_act(pq, c0):
            groups = (TM_IN + 2 * HALO) // 8
            x3 = jnp.where(valid, pq, 0.0).reshape(groups, 8, D_B)
            sub = lax.broadcasted_iota(jnp.int32, (1, 8, 1), 1)
            lo, hi = HALO // 8, HALO // 8 + TM_IN // 8
            acc = conv_ref[pad:pad + 1, c0:c0 + D_B] * x3[lo:hi]
            for t in range(CONV_W):
                s = t - pad
                if s == 0:
                    continue
                r = pltpu.roll(x3, (-s) % 8, axis=1)
                if s > 0:
                    sh = jnp.where(sub < 8 - s, r[lo:hi], r[lo + 1:hi + 1])
                else:
                    sh = jnp.where(sub >= -s, r[lo:hi], r[lo - 1:hi - 1])
                acc = acc + conv_ref[t:t + 1, c0:c0 + D_B] * sh
            return _silu(acc.reshape(TM_IN, D_B))

        def store_unit_heads(act, c0, gain):
            for h in range(HEADS):
                t = act[:, h * HD:(h + 1) * HD]
                nrm = lax.rsqrt(jnp.sum(t * t, axis=-1, keepdims=True) + EPS) * gain
                qkv_ref[bb, :, c0 + h * HD:c0 + (h + 1) * HD] = (t * nrm).astype(BF16)

        pq_q = proj(c_qkv, D_B)
        pq_k = proj(c_qkv + D_B, D_B)
        store_unit_heads(conv_act(pq_q, 0), 0, HD ** -0.5)
        pq_v = proj(c_qkv + 2 * D_B, D_B)
        store_unit_heads(conv_act(pq_k, D_B), D_B, 1.0)
        pa_u = proj(0, D_A, halo=False)
        qkv_ref[bb, :, 2 * D_B:] = conv_act(pq_v, 2 * D_B).astype(BF16)
        pa_v = proj(D_A, D_A, halo=False)
        u = _gelu_tanh(pa_u)
        pz = proj(c_qkv + N_QKV, D_B, halo=False)
        v = _gelu_tanh(pa_v)
        mu = jnp.mean(v, axis=-1, keepdims=True)
        vc = v - mu
        var = jnp.mean(vc * vc, axis=-1, keepdims=True)
        vn = (vc * lax.rsqrt(var + EPS) * lng_ref[...] + lnb_ref[...]).astype(BF16)
        z_ref[bb] = pz.astype(BF16)

        bst = bst_ref[...]
        for n in range(TM_IN // CHUNK):
            rows = slice(n * CHUNK, (n + 1) * CHUNK)
            for h in range(HEADS):
                cols = slice(h * HD, (h + 1) * HD)
                s = _dot(ws_ref[h], vn[rows, cols]) + bst[:, h:h + 1]
                ya_ref[bb, rows, cols] = (u[rows, cols] * s).astype(BF16)

        tri_l = tril_ref[...]
        tri_u = triu_ref[...]

        def split3(g):
            hi = g.astype(BF16)
            r1 = g - hi.astype(F32)
            mid = r1.astype(BF16)
            return hi, mid, (r1 - mid.astype(F32)).astype(BF16)

        ab = _dot(xb, wab_ref[...])
        g3 = split3(-jnp.exp(alog_ref[...]) * _softplus(ab + dtb_ref[...]))
        lane = lax.broadcasted_iota(jnp.int32, ab.shape, 1)
        gb = jnp.where(lane < HEADS, _dot(tri_l, g3[0]) + _dot(tri_l, g3[1]) + _dot(tri_l, g3[2]),
                       jnp.where(lane < 2 * HEADS, _dot(tri_u, g3[0]) + _dot(tri_u, g3[1]) + _dot(tri_u, g3[2]),
                                 _sigmoid(ab)))
        gb_ref[bb] = gb[:, 0:N_AB]

        abt = _dot_nt(wabt_ref[0:N_AB, :], xb)
        t3 = split3(-jnp.exp(alogt_ref[...]) * _softplus(abt + dtbt_ref[...]))
        row = lax.broadcasted_iota(jnp.int32, abt.shape, 0)
        gbt_ref[bb] = jnp.where(row < HEADS, _dot(t3[0], tri_u) + _dot(t3[1], tri_u) + _dot(t3[2], tri_u),
                               jnp.where(row < 2 * HEADS, _dot(t3[0], tri_l) + _dot(t3[1], tri_l) + _dot(t3[2], tri_l),
                                         _sigmoid(abt)))

    for bb in range(x_ref.shape[0]):
        one_batch_element(bb)


def _in_proj(x, ctx, mod_lat, mod_ctx, g1, w_in_t, lng, lnb, ws, bst, conv, alog, dtb, alogt, dtbt):
    B, L, _ = x.shape
    n_lat = L // TM_IN
    n_steps = n_lat + 1
    LC = L + TM_IN
    hb = TM_IN // HALO

    def full(shape):
        return pl.BlockSpec(shape, lambda b, j: (0,) * len(shape))

    in_specs = [
        pl.BlockSpec((NB_IN, TM_IN, D), lambda b, j: (b, jnp.maximum(j - 1, 0), 0)),
        pl.BlockSpec((NB_IN, HALO, D), lambda b, j: (b, jnp.clip((j - 1) * hb - 1, 0, L // HALO - 1), 0)),
        pl.BlockSpec((NB_IN, HALO, D), lambda b, j: (b, jnp.clip(j * hb, 0, L // HALO - 1), 0)),
        pl.BlockSpec((NB_IN, TM_IN, D), lambda b, j: (b, 0, 0)),
        pl.BlockSpec((NB_IN, 1, 6 * D), lambda b, j: (b, 0, 0)),
        full((1, 6 * D)), full((1, D)),
        pl.BlockSpec((1, N_MAIN + N_AB, D), lambda b, j: (0, 0, 0), pipeline_mode=pl.Buffered(1)),
        full((1, D_A)), full((1, D_A)), full((HEADS, CHUNK, CHUNK)), full((CHUNK, HEADS)),
        full((CONV_W, N_QKV)), full((1, 128)), full((1, 128)), full((N_AB, 1)), full((N_AB, 1)),
        full((TM_IN, TM_IN)), full((TM_IN, TM_IN)),
    ]
    out_shape = [
        jax.ShapeDtypeStruct((B, L, D_A), BF16),
        jax.ShapeDtypeStruct((B, LC, N_QKV), BF16),
        jax.ShapeDtypeStruct((B, LC, D_B), BF16),
        jax.ShapeDtypeStruct((B, LC, N_AB), F32),
        jax.ShapeDtypeStruct((B, N_AB, LC), F32),
    ]
    out_specs = [
        pl.BlockSpec((NB_IN, TM_IN, D_A), lambda b, j: (b, jnp.maximum(j - 1, 0), 0)),
        pl.BlockSpec((NB_IN, TM_IN, N_QKV), lambda b, j: (b, j, 0)),
        pl.BlockSpec((NB_IN, TM_IN, D_B), lambda b, j: (b, j, 0)),
        pl.BlockSpec((NB_IN, TM_IN, N_AB), lambda b, j: (b, j, 0)),
        pl.BlockSpec((NB_IN, N_AB, TM_IN), lambda b, j: (b, 0, j)),
    ]
    return pl.pallas_call(
        _in_kernel,
        out_shape=out_shape,
        grid=(B // NB_IN, n_steps),
        in_specs=in_specs,
        out_specs=out_specs,
        scratch_shapes=[pltpu.VMEM((D, N_MAIN), BF16), pltpu.VMEM((D, 128), BF16), pltpu.VMEM((128, D), BF16)],
        compiler_params=pltpu.CompilerParams(dimension_semantics=("arbitrary", "arbitrary"),
                                             vmem_limit_bytes=VMEM_LIMIT),
    )(x, x, x, ctx, mod_lat, mod_ctx, g1, w_in_t, lng, lnb, ws, bst, conv, alog, dtb, alogt, dtbt,
      jnp.asarray(_CHUNK_TRIL, BF16), jnp.asarray(_CHUNK_TRIL.T, BF16))


def _delta_kernel(qf_ref, qb_ref, zf_ref, zb_ref, gf_ref, gbk_ref, gtf_ref, gtb_ref, on_ref,
                  y_ref, s_ref, oacc_ref, *, n_ctx, n_lat):
    s = pl.program_id(1)

    @pl.when(s == 0)
    def _():
        s_ref[...] = jnp.zeros_like(s_ref)
        oacc_ref[...] = jnp.zeros_like(oacc_ref)

    row = lax.broadcasted_iota(jnp.int32, (CHUNK, CHUNK), 0)
    col = lax.broadcasted_iota(jnp.int32, (CHUNK, CHUNK), 1)
    low = row > col
    upp = row < col
    same_blk = (row // 16) == (col // 16)
    eye = jnp.where(row == col, 1.0, 0.0).astype(BF16)
    zero = jnp.zeros((CHUNK, CHUNK), BF16)
    onorm = on_ref[...]
    half = n_ctx + n_lat // 2
    second = s >= half
    g_refs = (gf_ref, gbk_ref)
    gt_refs = (gtf_ref, gtb_ref)
    qkv_refs = (qf_ref, qb_ref)
    z_refs = (zf_ref, zb_ref)
    nb = qf_ref.shape[0]
    ps = range(nb * HEADS)

    def halves(xc, unit):
        xb = xc.astype(BF16)
        fill = eye if unit else zero
        return jnp.where(low, xb, fill), jnp.where(upp, xb, fill)

    def as_lhs(hv):
        return jnp.concatenate(hv, axis=1)

    def as_rhs(*hvs):
        cols_ = [jnp.concatenate(hv, axis=0) for hv in hvs]
        return cols_[0] if len(cols_) == 1 else jnp.concatenate(cols_, axis=1)

    def load(d, p, part):
        bb, h = divmod(p, HEADS)
        return qkv_refs[d][bb, :, part * D_B + h * HD:part * D_B + (h + 1) * HD]

    def gcol(d, p, base):
        bb, h = divmod(p, HEADS)
        c = base + d * HEADS + h
        return g_refs[d][bb, :, c:c + 1]

    def grow(d, p, base):
        bb, h = divmod(p, HEADS)
        r = base + d * HEADS + h
        return gt_refs[d][bb, r:r + 1, :]

    def lanes(col):
        return jnp.broadcast_to(col, (CHUNK, HD))

    q = [[load(d, p, 0) for p in ps] for d in range(2)]
    k = [[load(d, p, 1) for p in ps] for d in range(2)]
    v = [[load(d, p, 2) for p in ps] for d in range(2)]
    gcl = [[lanes(gcol(d, p, 0)) for p in ps] for d in range(2)]
    betal = [[lanes(gcol(d, p, 2 * HEADS)) for p in ps] for d in range(2)]
    gr = [[grow(d, p, 0) for p in ps] for d in range(2)]
    betar = [[grow(d, p, 2 * HEADS) for p in ps] for d in range(2)]
    glast = [[gr[0][p][:, CHUNK - 1:CHUNK] for p in ps], [gr[1][p][:, 0:1] for p in ps]]

    gram = [[_dot_nt(jnp.concatenate([q[d][p], k[d][p]], axis=0), k[d][p]) for p in ps] for d in range(2)]
    dec = [jnp.exp(jnp.where(low, gcl[0][p] - gr[0][p], jnp.where(upp, gcl[1][p] - gr[1][p], 0.0))) for p in ps]
    lc = [jnp.where(low, gram[0][p][CHUNK:] * betar[0][p], jnp.where(upp, gram[1][p][CHUNK:] * betar[1][p], 0.0))
          * dec[p] for p in ps]
    qk = [[jnp.where(upp, 0.0, gram[0][p][:CHUNK] * dec[p]).astype(BF16) for p in ps],
          [jnp.where(low, 0.0, gram[1][p][:CHUNK] * dec[p]).astype(BF16) for p in ps]]

    dg = [jnp.where(same_blk, lc[p], 0.0) for p in ps]
    ob = [lc[p] - dg[p] for p in ps]
    d1h = [halves(dg[p], False) for p in ps]
    d2 = [_dot(as_lhs(d1h[p]), as_rhs(d1h[p])) for p in ps]
    p0s = [-dg[p] for p in ps]
    d2h = [halves(d2[p], False) for p in ps]
    p0h = [halves(p0s[p], True) for p in ps]
    o2 = [_dot(as_lhs(d2h[p]), as_rhs(d2h[p], p0h[p])) for p in ps]
    p1s = [p0s[p] + o2[p][:, CHUNK:] for p in ps]
    d4h = [halves(o2[p][:, :CHUNK], False) for p in ps]
    p1h = [halves(p1s[p], True) for p in ps]
    o3 = [_dot(as_lhs(d4h[p]), as_rhs(d4h[p], p1h[p])) for p in ps]
    p2s = [p1s[p] + o3[p][:, CHUNK:] for p in ps]
    d8h = [halves(o3[p][:, :CHUNK], False) for p in ps]
    p2h = [halves(p2s[p], True) for p in ps]
    p3s = [p2s[p] + _dot(as_lhs(d8h[p]), as_rhs(p2h[p])) for p in ps]
    p3h = [halves(p3s[p], True) for p in ps]
    obh = [halves(ob[p], False) for p in ps]
    n1h = [halves(_dot(as_lhs(p3h[p]), as_rhs(obh[p])), False) for p in ps]
    o6 = [_dot(as_lhs(n1h[p]), as_rhs(n1h[p], p3h[p])) for p in ps]
    r0s = [p3s[p] - o6[p][:, CHUNK:] for p in ps]
    n2h = [halves(o6[p][:, :CHUNK], False) for p in ps]
    r0h = [halves(r0s[p], True) for p in ps]
    o7 = [_dot(as_lhs(n2h[p]), as_rhs(n2h[p], r0h[p])) for p in ps]
    r1s = [r0s[p] + o7[p][:, CHUNK:] for p in ps]
    n4h = [halves(o7[p][:, :CHUNK], False) for p in ps]
    r1h = [halves(r1s[p], True) for p in ps]
    tinv = [halves(r1s[p] + _dot(as_lhs(n4h[p]), as_rhs(r1h[p])), True) for p in ps]

    offs = []
    for d in range(2):
        lat_chunk = (s - n_ctx) if d == 0 else (n_ctx + n_lat - 1 - s)
        off = pl.multiple_of(jnp.clip(lat_chunk, 0, n_lat - 1) * CHUNK, CHUNK)
        sidx = [(p // HEADS * 2 + d) * HEADS + p % HEADS for p in ps]
        egc = [jnp.exp(gcl[d][p]) for p in ps]
        kf = [k[d][p].astype(F32) for p in ps]
        rhs = [jnp.concatenate([v[d][p], (kf[p] * egc[p]).astype(BF16)], axis=1) for p in ps]
        uw = [_dot(tinv[p][d], rhs[p]) for p in ps]
        qd = [q[d][p].astype(F32) * egc[p] for p in ps]
        kd = [(kf[p] * jnp.exp(glast[d][p] - gcl[d][p])).astype(BF16) for p in ps]
        st = [s_ref[sidx[p]] for p in ps]
        a1 = [_dot(jnp.concatenate([uw[p][:, HD:] * betal[d][p], qd[p]], axis=0).astype(BF16), st[p].astype(BF16))
              for p in ps]
        vnew = [(uw[p][:, :HD] * betal[d][p] - a1[p][:CHUNK]).astype(BF16) for p in ps]
        o = [a1[p][CHUNK:] + _dot(qk[d][p], vnew[p]) for p in ps]
        for p in ps:
            s_ref[sidx[p]] = st[p] * jnp.exp(glast[d][p]) + _dot_tn(kd[p], vnew[p])
        for p in ps:
            bb, h = divmod(p, HEADS)
            cols = slice(h * HD, (h + 1) * HD)
            oacc_ref[bb, pl.ds(off, CHUNK), cols] = (
                jnp.where(second, oacc_ref[bb, pl.ds(off, CHUNK), cols], 0.0) + o[p])
        offs.append(off)

    @pl.when(second)
    def _():
        for d in range(2):
            for p in ps:
                bb, h = divmod(p, HEADS)
                cols = slice(h * HD, (h + 1) * HD)
                tot = oacc_ref[bb, pl.ds(offs[d], CHUNK), cols]
                ms = jnp.mean(tot * tot, axis=-1, keepdims=True)
                zz = z_refs[d][bb, :, cols].astype(F32)
                y_ref[bb, pl.ds(offs[d], CHUNK), cols] = (
                    tot * lax.rsqrt(ms + EPS) * onorm * _silu(zz)).astype(BF16)


def _delta(qkv, z, gb, gbt, onorm, L):
    B, LC, _ = qkv.shape
    n_all = LC // CHUNK
    n_lat = L // CHUNK
    n_ctx = n_all - n_lat

    def cf(s):
        return s

    def cb(s):
        return jnp.where(s < n_ctx, n_ctx - 1 - s, n_all + n_ctx - 1 - s)

    in_specs = [
        pl.BlockSpec((NB_DELTA, CHUNK, N_QKV), lambda b, s: (b, cf(s), 0)),
        pl.BlockSpec((NB_DELTA, CHUNK, N_QKV), lambda b, s: (b, cb(s), 0)),
        pl.BlockSpec((NB_DELTA, CHUNK, D_B), lambda b, s: (b, cf(s), 0)),
        pl.BlockSpec((NB_DELTA, CHUNK, D_B), lambda b, s: (b, cb(s), 0)),
        pl.BlockSpec((NB_DELTA, CHUNK, N_AB), lambda b, s: (b, cf(s), 0)),
        pl.BlockSpec((NB_DELTA, CHUNK, N_AB), lambda b, s: (b, cb(s), 0)),
        pl.BlockSpec((NB_DELTA, N_AB, CHUNK), lambda b, s: (b, 0, cf(s))),
        pl.BlockSpec((NB_DELTA, N_AB, CHUNK), lambda b, s: (b, 0, cb(s))),
        pl.BlockSpec((1, HD), lambda b, s: (0, 0)),
    ]
    return pl.pallas_call(
        functools.partial(_delta_kernel, n_ctx=n_ctx, n_lat=n_lat),
        out_shape=jax.ShapeDtypeStruct((B, L, D_B), BF16),
        grid=(B // NB_DELTA, n_all),
        in_specs=in_specs,
        out_specs=pl.BlockSpec((NB_DELTA, L, D_B), lambda b, s: (b, 0, 0)),
        scratch_shapes=[pltpu.VMEM((NB_DELTA * 2 * HEADS, HD, HD), F32), pltpu.VMEM((NB_DELTA, L, D_B), F32)],
        compiler_params=pltpu.CompilerParams(dimension_semantics=("arbitrary", "arbitrary"),
                                             vmem_limit_bytes=VMEM_LIMIT),
    )(qkv, qkv, z, z, gb, gb, gbt, gbt, onorm)


def _out_kernel(x_ref, ya_ref, yb_ref, mod_ref, wo_ref, g2_ref, wrt_ref, brt_ref, tri_ref,
                x2_ref, h_ref, route_ref, cnt_ref, base_ref, wbf_ref):
    i = pl.program_id(0)

    @pl.when(i == 0)
    def _():
        base_ref[...] = jnp.zeros_like(base_ref)
        wbf_ref[...] = wo_ref[...].astype(BF16)

    mod = mod_ref[0]
    gt1 = mod[:, 2 * D:3 * D]
    sh2 = mod[:, 3 * D:4 * D]
    sc2 = mod[:, 4 * D:5 * D]
    scale2 = g2_ref[...] * (1.0 + sc2)
    sub = TM_OUT // OUT_SPLIT
    hbs = []
    for r in range(OUT_SPLIT):
        rows = slice(r * sub, (r + 1) * sub)
        mix = _dot(ya_ref[rows, :], wbf_ref[0:D_A, :]) + _dot(yb_ref[rows, :], wbf_ref[D_A:, :])
        x2 = x_ref[rows, :] + gt1 * mix
        x2_ref[rows, :] = x2
        ms = jnp.mean(x2 * x2, axis=-1, keepdims=True)
        hv = x2 * lax.rsqrt(ms + EPS) * scale2 + sh2
        hbs.append(hv.astype(BF16))
        h_ref[rows, 0:D_PACK] = _pack_bf16_pairs(hv)
    hb = jnp.concatenate(hbs, axis=0)

    lt = _dot_nt(wrt_ref[...], hb) + brt_ref[...]
    gl = [lt[r:r + 1, :] for r in range(N_GROUPS)]
    gmax = jnp.maximum(jnp.maximum(gl[0], gl[1]), jnp.maximum(gl[2], gl[3]))
    gsel = jnp.where(gl[0] == gmax, 0, jnp.where(gl[1] == gmax, 1, jnp.where(gl[2] == gmax, 2, 3)))
    p_g = 1.0 / (jnp.exp(gl[0] - gmax) + jnp.exp(gl[1] - gmax) + jnp.exp(gl[2] - gmax) + jnp.exp(gl[3] - gmax))
    el = []
    for e in range(EPG):
        r = [lt[N_GROUPS + g * EPG + e:N_GROUPS + g * EPG + e + 1, :] for g in range(N_GROUPS)]
        el.append(jnp.where(gsel == 0, r[0], jnp.where(gsel == 1, r[1], jnp.where(gsel == 2, r[2], r[3]))))
    m1 = jnp.maximum(jnp.maximum(el[0], el[1]), jnp.maximum(el[2], el[3]))
    i1 = jnp.where(el[0] == m1, 0, jnp.where(el[1] == m1, 1, jnp.where(el[2] == m1, 2, 3)))
    neg = jnp.float32(-jnp.inf)
    el2 = [jnp.where(i1 == e, neg, el[e]) for e in range(EPG)]
    m2 = jnp.maximum(jnp.maximum(el2[0], el2[1]), jnp.maximum(el2[2], el2[3]))
    i2 = jnp.where(jnp.logical_and(el2[0] == m2, i1 != 0), 0,
                   jnp.where(jnp.logical_and(el2[1] == m2, i1 != 1), 1,
                             jnp.where(jnp.logical_and(el2[2] == m2, i1 != 2), 2, 3)))
    t = jnp.exp(m2 - m1)
    w1 = p_g / (1.0 + t)
    w2 = p_g * t / (1.0 + t)
    first_low = i1 < i2
    ea = jnp.where(first_low, i1, i2)
    eb = jnp.where(first_low, i2, i1)
    w_a = jnp.where(first_low, w1, w2)
    w_b = jnp.where(first_low, w2, w1)
    pair = jnp.where(ea == 0, eb - 1, jnp.where(ea == 1, jnp.where(eb == 3, 3, 4), 5))
    swap = pair == 5
    w_a, w_b = jnp.where(swap, w_b, w_a), jnp.where(swap, w_a, w_b)
    bucket = gsel * N_PAIRS + pair

    tm = bucket.shape[1]
    rows = lax.broadcasted_iota(jnp.int32, (32, tm), 0)
    onehot = jnp.where(rows == bucket, 1.0, 0.0).astype(F32)
    prefix = _dot(onehot.astype(BF16), tri_ref[...])
    base = base_ref[:, 0:1]
    rank = jnp.sum(onehot * (prefix - 1.0 + base), axis=0, keepdims=True)
    newbase = base + prefix[:, tm - 1:tm]
    base_ref[...] = jnp.broadcast_to(newbase, base_ref.shape)
    cnt_ref[...] = jnp.broadcast_to(newbase, cnt_ref.shape)
    route = jnp.concatenate([bucket.astype(F32), rank, w_a, w_b, jnp.zeros((4, tm), F32)], axis=0)
    route_ref[0] = route
    route_t = jnp.transpose(jnp.concatenate([route, jnp.zeros((120, tm), F32)], axis=0))
    h_ref[:, D_PACK:D_ROW] = lax.bitcast_convert_type(route_t, jnp.uint32)


def _out_proj(x2d, ya, yb, mod3, w_o, g2, wrt, brt, L):
    T = x2d.shape[0]
    nb = T // TM_OUT
    per_b = L // TM_OUT
    out_shape = [
        jax.ShapeDtypeStruct((T, D), F32),
        jax.ShapeDtypeStruct((T, D_ROW), jnp.uint32),
        jax.ShapeDtypeStruct((nb, 8, TM_OUT), F32),
        jax.ShapeDtypeStruct((32, 128), F32),
    ]
    return pl.pallas_call(
        _out_kernel,
        out_shape=out_shape,
        grid=(nb,),
        in_specs=[
            pl.BlockSpec((TM_OUT, D), lambda i: (i, 0)),
            pl.BlockSpec((TM_OUT, D_A), lambda i: (i, 0)),
            pl.BlockSpec((TM_OUT, D_B), lambda i: (i, 0)),
            pl.BlockSpec((1, 1, 6 * D), lambda i: (i // per_b, 0, 0)),
            pl.BlockSpec((D, D), lambda i: (0, 0), pipeline_mode=pl.Buffered(1)),
            pl.BlockSpec((1, D), lambda i: (0, 0)),
            pl.BlockSpec((32, D), lambda i: (0, 0)),
            pl.BlockSpec((32, 1), lambda i: (0, 0)),
            pl.BlockSpec((TM_OUT, TM_OUT), lambda i: (0, 0), pipeline_mode=pl.Buffered(1)),
        ],
        out_specs=[
            pl.BlockSpec((TM_OUT, D), lambda i: (i, 0)),
            pl.BlockSpec((TM_OUT, D_ROW), lambda i: (i, 0)),
            pl.BlockSpec((1, 8, TM_OUT), lambda i: (i, 0, 0)),
            pl.BlockSpec((32, 128), lambda i: (0, 0)),
        ],
        scratch_shapes=[pltpu.VMEM((32, 128), F32), pltpu.VMEM((D, D), BF16)],
        compiler_params=pltpu.CompilerParams(dimension_semantics=("arbitrary",),
                                             vmem_limit_bytes=VMEM_LIMIT),
    )(x2d, ya, yb, mod3, w_o, g2, wrt, brt, jnp.asarray(np.triu(np.ones((TM_OUT, TM_OUT), np.float32)), BF16))


def _sc_mesh_and_split(n_rows):
    n_workers = SC_CORES * SC_SUBCORES
    per_worker = n_rows // n_workers
    n_chunks = per_worker // SC_WINDOW
    assert per_worker * n_workers == n_rows and n_chunks * SC_WINDOW == per_worker and n_chunks % 2 == 0
    mesh = plsc.VectorSubcoreMesh(core_axis_name="c", subcore_axis_name="s")
    return mesh, n_workers, per_worker, n_chunks


def _sc_two_buffer_loop(n_chunks, first, second):
    first(0, 0).start()

    @pl.loop(0, n_chunks, step=2)
    def _(j):
        for b in range(2):
            jj = j + b
            first(jj, b).wait()

            @pl.when(jj + 1 < n_chunks)
            def _():
                @pl.when(jj >= 1)
                def _():
                    second(jj - 1, 1 - b).wait()
                first(jj + 1, 1 - b).start()

            second(jj, b).start()

    second(n_chunks - 2, 0).wait()
    second(n_chunks - 1, 1).wait()


def _sc_row_gather(table, idx):
    n_rows = idx.shape[0]
    width = table.shape[1]
    mesh, n_workers, per_worker, n_chunks = _sc_mesh_and_split(n_rows)

    @functools.partial(
        pl.kernel, mesh=mesh,
        out_type=jax.ShapeDtypeStruct((n_rows, width), table.dtype),
        scratch_types=[pltpu.VMEM((n_chunks, SC_WINDOW), jnp.int32),
                       pltpu.VMEM((2, SC_WINDOW, width), table.dtype),
                       pltpu.SemaphoreType.DMA((2,)),
                       pltpu.SemaphoreType.DMA((2,))],
    )
    def gather(table_hbm, idx_hbm, out_hbm, idx_v, rows_v, sem_in, sem_out):
        wid = lax.axis_index("s") * SC_CORES + lax.axis_index("c")
        base = wid * per_worker
        pltpu.sync_copy(idx_hbm.at[wid], idx_v)

        def fetch(j, b):
            return pltpu.make_async_copy(table_hbm.at[idx_v.at[j]], rows_v.at[b], sem_in.at[b])

        def put(j, b):
            off = pl.multiple_of(base + j * SC_WINDOW, 8)
            return pltpu.make_async_copy(rows_v.at[b], out_hbm.at[pl.ds(off, SC_WINDOW)], sem_out.at[b])

        _sc_two_buffer_loop(n_chunks, fetch, put)

    return gather(table, idx.reshape(n_workers, n_chunks, SC_WINDOW))


def _sc_row_scatter(rows, pos, n_out):
    n_rows, width = rows.shape
    mesh, n_workers, per_worker, n_chunks = _sc_mesh_and_split(n_rows)

    @functools.partial(
        pl.kernel, mesh=mesh,
        out_type=jax.ShapeDtypeStruct((n_out, width), rows.dtype),
        scratch_types=[pltpu.VMEM((n_chunks, SC_WINDOW), jnp.int32),
                       pltpu.VMEM((2, SC_WINDOW, width), rows.dtype),
                       pltpu.SemaphoreType.DMA((2,)),
                       pltpu.SemaphoreType.DMA((2,))],
    )
    def scatter(rows_hbm, pos_hbm, out_hbm, idx_v, rows_v, sem_in, sem_out):
        wid = lax.axis_index("s") * SC_CORES + lax.axis_index("c")
        base = wid * per_worker
        pltpu.sync_copy(pos_hbm.at[wid], idx_v)

        def fetch(j, b):
            off = pl.multiple_of(base + j * SC_WINDOW, 8)
            return pltpu.make_async_copy(rows_hbm.at[pl.ds(off, SC_WINDOW)], rows_v.at[b], sem_in.at[b])

        def put(j, b):
            return pltpu.make_async_copy(rows_v.at[b], out_hbm.at[idx_v.at[j]], sem_out.at[b])

        _sc_two_buffer_loop(n_chunks, fetch, put)

    return scatter(rows, pos.reshape(n_workers, n_chunks, SC_WINDOW))


def _cast_kernel(x_ref, o_ref):
    o_ref[...] = x_ref[...].astype(BF16)


def _expert_weights_bf16(w):
    n, r, c = w.shape
    return pl.pallas_call(
        _cast_kernel,
        out_shape=jax.ShapeDtypeStruct(w.shape, BF16),
        grid=(n,),
        in_specs=[pl.BlockSpec((1, r, c), lambda e: (e, 0, 0))],
        out_specs=pl.BlockSpec((1, r, c), lambda e: (e, 0, 0)),
        compiler_params=pltpu.CompilerParams(dimension_semantics=("arbitrary",),
                                             vmem_limit_bytes=VMEM_LIMIT),
    )(w)


def _moe_kernel(ea_ref, eb_ref, nv_ref, hs_ref, wga_ref, wgb_ref, wda_ref, wdb_ref, o_ref):
    i = pl.program_id(0)

    @pl.when(nv_ref[i] > 0)
    def _():
        live = lax.broadcasted_iota(jnp.int32, (BM_MOE, 1), 0) < nv_ref[i]
        xb = _unpack_bf16_pairs(jnp.where(live, hs_ref[:, 0:D_PACK], jnp.uint32(0)))
        w_ab = lax.bitcast_convert_type(hs_ref[:, D_PACK:D_ROW], F32)
        w_a = jnp.where(live, w_ab[:, 2:3], 0.0)
        w_b = jnp.where(live, w_ab[:, 3:4], 0.0)
        ga = _dot(xb, wga_ref[0])
        act_a = (_silu(ga[:, :D_EXPERT]) * ga[:, D_EXPERT:] * w_a).astype(BF16)
        gb = _dot(xb, wgb_ref[0])
        act_b = (_silu(gb[:, :D_EXPERT]) * gb[:, D_EXPERT:] * w_b).astype(BF16)
        o_ref[...] = _pack_bf16_pairs(_dot(act_a, wda_ref[0]) + _dot(act_b, wdb_ref[0]))

    @pl.when(nv_ref[i] <= 0)
    def _():
        o_ref[...] = jnp.zeros_like(o_ref)


def _moe(ea, eb, nvalid, hs, w_gu, w_dn):
    nblk = ea.shape[0]
    S = nblk * BM_MOE
    grid_spec = pltpu.PrefetchScalarGridSpec(
        num_scalar_prefetch=3,
        grid=(nblk,),
        in_specs=[
            pl.BlockSpec((BM_MOE, D_ROW), lambda i, ea, eb, nv: (i, 0)),
            pl.BlockSpec((1, D, 2 * D_EXPERT), lambda i, ea, eb, nv: (ea[i], 0, 0)),
            pl.BlockSpec((1, D, 2 * D_EXPERT), lambda i, ea, eb, nv: (eb[i], 0, 0)),
            pl.BlockSpec((1, D_EXPERT, D), lambda i, ea, eb, nv: (ea[i], 0, 0)),
            pl.BlockSpec((1, D_EXPERT, D), lambda i, ea, eb, nv: (eb[i], 0, 0)),
        ],
        out_specs=pl.BlockSpec((BM_MOE, D_PACK), lambda i, ea, eb, nv: (i, 0)),
    )
    return pl.pallas_call(
        _moe_kernel,
        out_shape=jax.ShapeDtypeStruct((S, D_PACK), jnp.uint32),
        grid_spec=grid_spec,
        compiler_params=pltpu.CompilerParams(dimension_semantics=("arbitrary",),
                                             vmem_limit_bytes=VMEM_LIMIT),
    )(ea, eb, nvalid, hs, w_gu, w_gu, w_dn, w_dn)


def _final_kernel(x2_ref, m_ref, mod_ref, fg_ref, *rest):
    o_ref = rest[-1]
    gt2 = mod_ref[0][:, 5 * D:6 * D]
    y = x2_ref[...] + gt2 * _unpack_bf16_pairs(m_ref[...]).astype(F32)
    ms = jnp.mean(y * y, axis=-1, keepdims=True)
    o_ref[...] = y * lax.rsqrt(ms + EPS) * fg_ref[...]


def _final(x2, m_part, mod3, fg, L, part, prev_out):
    T = x2.shape[0]
    steps = m_part.shape[0] // TM_FIN
    off = part * steps
    per_b = L // TM_FIN
    in_specs = [
        pl.BlockSpec((TM_FIN, D), lambda i: (i + off, 0)),
        pl.BlockSpec((TM_FIN, D_PACK), lambda i: (i, 0)),
        pl.BlockSpec((1, 1, 6 * D), lambda i: ((i + off) // per_b, 0, 0)),
        pl.BlockSpec((1, D), lambda i: (0, 0)),
    ]
    args = [x2, m_part, mod3, fg]
    aliases = {}
    if prev_out is not None:
        in_specs.append(pl.BlockSpec(memory_space=pl.ANY))
        args.append(prev_out)
        aliases = {len(args) - 1: 0}
    return pl.pallas_call(
        _final_kernel,
        out_shape=jax.ShapeDtypeStruct((T, D), F32),
        grid=(steps,),
        in_specs=in_specs,
        out_specs=pl.BlockSpec((TM_FIN, D), lambda i: (i + off, 0)),
        input_output_aliases=aliases,
        compiler_params=pltpu.CompilerParams(dimension_semantics=("arbitrary",),
                                             vmem_limit_bytes=VMEM_LIMIT),
    )(*args)


def kernel(x, c, ctx, c_ctx, w_ada, b_ada, norm1_g, w_in, ln_a_g, ln_a_b, w_spatial, b_spatial, conv_qkv, a_log,
           dt_bias, onorm_g, w_out, norm2_g, w_group, b_group, w_router, b_router, w_gate_up, w_down, final_g):
    B, L, _ = x.shape
    T = B * L
    assert w_ada.shape[0] == 1 and ctx.shape[1] == TM_IN and L % TM_OUT == 0 and T % (N_COMBINE_PARTS * TM_FIN) == 0

    cond = jnp.concatenate([c, c_ctx[None, :], jnp.zeros((7, D), F32)], axis=0)
    mod = _modulation(cond, w_ada[0], b_ada[0][None, :])
    mod_lat = mod[:B].reshape(B, 1, 6 * D)
    mod_ctx = mod[B:B + 1]

    alog = a_log[0].reshape(1, 2 * HEADS)
    dtb = dt_bias[0].reshape(1, 2 * HEADS)
    alog_row = jnp.pad(alog, ((0, 0), (0, 128 - 2 * HEADS)))
    dtb_row = jnp.pad(dtb, ((0, 0), (0, 128 - 2 * HEADS)))
    alog_col = jnp.pad(alog, ((0, 0), (0, N_AB - 2 * HEADS))).T
    dtb_col = jnp.pad(dtb, ((0, 0), (0, N_AB - 2 * HEADS))).T

    ya, qkv, z, gb, gbt = _in_proj(
        x, ctx, mod_lat, mod_ctx, norm1_g, jnp.swapaxes(w_in, 1, 2), ln_a_g, ln_a_b,
        w_spatial[0].astype(BF16), b_spatial[0].T, conv_qkv[0], alog_row, dtb_row, alog_col, dtb_col)

    yb = _delta(qkv, z, gb, gbt, onorm_g, L)

    wrt = jnp.concatenate([w_group[0].T, w_router[0].T, jnp.zeros((32 - N_GROUPS - N_EXPERTS, D), F32)], axis=0)
    brt = jnp.concatenate([b_group[0], b_router[0], jnp.zeros((32 - N_GROUPS - N_EXPERTS,), F32)])[:, None]
    x2, h, route, cnt = _out_proj(x.reshape(T, D), ya.reshape(T, D_A), yb.reshape(T, D_B), mod_lat,
                                  w_out[0], norm2_g, wrt.astype(BF16), brt, L)

    bucket = route[:, 0, :].reshape(T).astype(jnp.int32)
    rank = route[:, 1, :].reshape(T).astype(jnp.int32)
    counts = cnt[:N_BUCKETS, 0].astype(jnp.int32)
    nblk_b = (counts + BM_MOE - 1) // BM_MOE
    blk_end = jnp.cumsum(nblk_b)
    blk_start = blk_end - nblk_b
    kk = jnp.arange(N_BUCKETS, dtype=jnp.int32)

    def pick(idx, table):
        return jnp.sum(jnp.where(idx[:, None] == kk[None, :], table[None, :], 0), axis=1)

    pos = pick(bucket, blk_start) * BM_MOE + rank
    n_blocks = T // BM_MOE + N_BUCKETS
    S = n_blocks * BM_MOE
    blk = jnp.arange(n_blocks, dtype=jnp.int32)
    used = blk < blk_end[-1]
    bkt = jnp.sum((jnp.minimum(blk, blk_end[-1] - 1)[:, None] >= blk_end[None, :]).astype(jnp.int32), axis=1)
    nvalid = jnp.where(used, jnp.clip(pick(bkt, counts) - (blk - pick(bkt, blk_start)) * BM_MOE, 0, BM_MOE),
                       0).astype(jnp.int32)
    ea = pick(bkt, jnp.asarray(_SLOT_A_EXPERT))
    eb = pick(bkt, jnp.asarray(_SLOT_B_EXPERT))

    hs = _sc_row_scatter(h, pos, S)
    ms = _moe(ea, eb, nvalid, hs, _expert_weights_bf16(w_gate_up[0]), _expert_weights_bf16(w_down[0]))
    pos_parts = pos.reshape(N_COMBINE_PARTS, T // N_COMBINE_PARTS)
    m_parts = [_sc_row_gather(ms, pos_parts[q]) for q in range(N_COMBINE_PARTS)]
    out = None
    for q in range(N_COMBINE_PARTS):
        out = _final(x2, m_parts[q], mod_lat, final_g[None, :], L, q, out)
    return out.reshape(B, L, D)
```

```python
import functools

import jax
import jax.numpy as jnp
import numpy as np
from jax import lax
from jax.experimental import pallas as pl
from jax.experimental.pallas import tpu as pltpu
from jax.experimental.pallas import tpu_sc as plsc

F32 = jnp.float32
BF16 = jnp.bfloat16
EPS = 1e-6

D = 1024
D_A = 512
D_B = 512
HEADS = 4
HD = 128
CHUNK = 128
CONV_W = 5
N_QKV = 3 * D_B
N_MAIN = 2 * D_A + 4 * D_B
N_AB = 16
N_GROUPS = 4
EPG = 4
N_EXPERTS = 16
D_EXPERT = 512
N_PAIRS = 6
N_BUCKETS = N_GROUPS * N_PAIRS
D_PACK = D // 2
D_ROW = D_PACK + 128
PAIR_A = (0, 0, 0, 1, 1, 3)
PAIR_B = (1, 2, 3, 3, 2, 2)
_SLOT_A_EXPERT = np.array([g * EPG + PAIR_A[p] for g in range(N_GROUPS) for p in range(N_PAIRS)], np.int32)
_SLOT_B_EXPERT = np.array([g * EPG + PAIR_B[p] for g in range(N_GROUPS) for p in range(N_PAIRS)], np.int32)

TM_IN = 256
HALO = 8
NB_IN = 2
NB_DELTA = 2
TM_OUT = 1024
OUT_SPLIT = 4
BM_MOE = 256
TM_FIN = 1024
N_COMBINE_PARTS = 4
VMEM_LIMIT = 56 * 1024 * 1024
SC_CORES = 2
SC_SUBCORES = 16
SC_WINDOW = 32

HI = lax.Precision.HIGHEST

_CHUNK_TRIL = np.kron(np.eye(TM_IN // CHUNK, dtype=np.float32), np.tril(np.ones((CHUNK, CHUNK), np.float32)))


def _dot(a, b, precision=None):
    return jnp.dot(a, b, preferred_element_type=F32, precision=precision)


def _dot_nt(a, b):
    return lax.dot_general(a, b, (((1,), (1,)), ((), ())), preferred_element_type=F32)


def _dot_tn(a, b):
    return lax.dot_general(a, b, (((0,), (0,)), ((), ())), preferred_element_type=F32)


def _sigmoid(x):
    return 0.5 + 0.5 * jnp.tanh(0.5 * x)


def _silu(x):
    h = 0.5 * x
    return h + h * jnp.tanh(h)


def _softplus(x):
    return jnp.maximum(x, 0.0) + jnp.log(1.0 + jnp.exp(-jnp.abs(x)))


def _pack_bf16_pairs(x):
    bits = lax.bitcast_convert_type(x.astype(BF16).astype(F32), jnp.uint32)
    return (bits[:, D_PACK:] & jnp.uint32(0xFFFF0000)) | (bits[:, :D_PACK] >> 16)


def _unpack_bf16_pairs(w):
    bits = w
    lo = lax.bitcast_convert_type(bits << 16, F32)
    hi = lax.bitcast_convert_type(bits & jnp.uint32(0xFFFF0000), F32)
    return jnp.concatenate([lo, hi], axis=1).astype(BF16)


def _gelu_tanh(x):
    return 0.5 * x * (1.0 + jnp.tanh(np.sqrt(2.0 / np.pi).astype(np.float32) * (x + 0.044715 * (x * x * x))))


def _mod_kernel(c_ref, w_ref, b_ref, o_ref):
    c = c_ref[...]
    o_ref[...] = _dot(_silu(c), w_ref[...], precision=HI) + b_ref[...]


def _modulation(cond, w_ada, b_ada):
    rows = cond.shape[0]
    tn = 1536
    return pl.pallas_call(
        _mod_kernel,
        out_shape=jax.ShapeDtypeStruct((rows, 6 * D), F32),
        grid=(6 * D // tn,),
        in_specs=[pl.BlockSpec((rows, D), lambda i: (0, 0)),
                  pl.BlockSpec((D, tn), lambda i: (0, i)),
                  pl.BlockSpec((1, tn), lambda i: (0, i))],
        out_specs=pl.BlockSpec((rows, tn), lambda i: (0, i)),
        compiler_params=pltpu.CompilerParams(dimension_semantics=("arbitrary",),
                                             vmem_limit_bytes=VMEM_LIMIT),
    )(cond, w_ada, b_ada)


def _in_kernel(x_ref, xp_ref, xn_ref, ctx_ref, mod_ref, cmod_ref, g1_ref, wt_ref,
               lng_ref, lnb_ref, ws_ref, bst_ref, conv_ref, alog_ref, dtb_ref, alogt_ref, dtbt_ref, tril_ref, triu_ref,
               ya_ref, qkv_ref, z_ref, gb_ref, gbt_ref, wbf_ref, wab_ref, wabt_ref):
    j = pl.program_id(1)

    @pl.when(jnp.logical_and(pl.program_id(0) == 0, j == 0))
    def _():
        for c0 in range(0, N_MAIN, 512):
            wbf_ref[:, c0:c0 + 512] = jnp.transpose(wt_ref[0, c0:c0 + 512, :]).astype(BF16)
        tail = jnp.concatenate([wt_ref[0, N_MAIN:N_MAIN + N_AB, :], jnp.zeros((128 - N_AB, D), F32)], axis=0)
        wabt_ref[...] = tail.astype(BF16)
        wab_ref[...] = jnp.transpose(tail).astype(BF16)

    is_ctx = j == 0
    n_lat_blocks = pl.num_programs(1) - 1

    def one_batch_element(bb):
        mod = mod_ref[bb]
        cm = cmod_ref[...]
        sh = jnp.where(is_ctx, cm[:, 0:D], mod[:, 0:D])
        sc = jnp.where(is_ctx, cm[:, D:2 * D], mod[:, D:2 * D])
        scale = g1_ref[...] * (1.0 + sc)

        xmain = jnp.where(is_ctx, ctx_ref[bb], x_ref[bb])
        xv = jnp.concatenate([xp_ref[bb], xmain, xn_ref[bb]], axis=0)
        xnorm = xv * lax.rsqrt(jnp.mean(xv * xv, axis=-1, keepdims=True) + EPS) * scale + sh
        xe = xnorm.astype(BF16)
        xb = xnorm[HALO:HALO + TM_IN].astype(BF16)

        rid = lax.broadcasted_iota(jnp.int32, (TM_IN + 2 * HALO, 1), 0)
        prev_ok = j >= 2
        next_ok = jnp.logical_and(j >= 1, j < n_lat_blocks)
        valid = jnp.logical_or(jnp.logical_and(rid >= HALO, rid < HALO + TM_IN),
                               jnp.logical_or(jnp.logical_and(rid < HALO, prev_ok),
                                              jnp.logical_and(rid >= HALO + TM_IN, next_ok)))
        pad = (CONV_W - 1) // 2
        c_qkv = 2 * D_A

        def proj(c0, width, halo=True):
            return _dot(xe if halo else xb, wbf_ref[:, c0:c0 + width])

        def conv_act(pq, c0):
            groups = (TM_IN + 2 * HALO) // 8
            x3 = jnp.where(valid, pq, 0.0).reshape(groups, 8, D_B)
            sub = lax.broadcasted_iota(jnp.int32, (1, 8, 1), 1)
            lo, hi = HALO // 8, HALO // 8 + TM_IN // 8
            acc = conv_ref[pad:pad + 1, c0:c0 + D_B] * x3[lo:hi]
            for t in range(CONV_W):
                s = t - pad
                if s == 0:
                    continue
                r = pltpu.roll(x3, (-s) % 8, axis=1)
                if s > 0:
                    sh = jnp.where(sub < 8 - s, r[lo:hi], r[lo + 1:hi + 1])
                else:
                    sh = jnp.where(sub >= -s, r[lo:hi], r[lo - 1:hi - 1])
                acc = acc + conv_ref[t:t + 1, c0:c0 + D_B] * sh
            return _silu(acc.reshape(TM_IN, D_B))

        def store_unit_heads(act, c0, gain):
            for h in range(HEADS):
                t = act[:, h * HD:(h + 1) * HD]
                nrm = lax.rsqrt(jnp.sum(t * t, axis=-1, keepdims=True) + EPS) * gain
                qkv_ref[bb, :, c0 + h * HD:c0 + (h + 1) * HD] = (t * nrm).astype(BF16)

        pq_q = proj(c_qkv, D_B)
        pq_k = proj(c_qkv + D_B, D_B)
        store_unit_heads(conv_act(pq_q, 0), 0, HD ** -0.5)
        pq_v = proj(c_qkv + 2 * D_B, D_B)
        store_unit_heads(conv_act(pq_k, D_B), D_B, 1.0)
        pa_u = proj(0, D_A, halo=False)
        qkv_ref[bb, :, 2 * D_B:] = conv_act(pq_v, 2 * D_B).astype(BF16)
        pa_v = proj(D_A, D_A, halo=False)
        u = _gelu_tanh(pa_u)
        pz = proj(c_qkv + N_QKV, D_B, halo=False)
        v = _gelu_tanh(pa_v)
        mu = jnp.mean(v, axis=-1, keepdims=True)
        vc = v - mu
        var = jnp.mean(vc * vc, axis=-1, keepdims=True)
        vn = (vc * lax.rsqrt(var + EPS) * lng_ref[...] + lnb_ref[...]).astype(BF16)
        z_ref[bb] = pz.astype(BF16)

        bst = bst_ref[...]
        for n in range(TM_IN // CHUNK):
            rows = slice(n * CHUNK, (n + 1) * CHUNK)
            for h in range(HEADS):
                cols = slice(h * HD, (h + 1) * HD)
                s = _dot(ws_ref[h], vn[rows, cols]) + bst[:, h:h + 1]
                ya_ref[bb, rows, cols] = (u[rows, cols] * s).astype(BF16)

        tri_l = tril_ref[...]
        tri_u = triu_ref[...]

        def split3(g):
            hi = g.astype(BF16)
            r1 = g - hi.astype(F32)
            mid = r1.astype(BF16)
            return hi, mid, (r1 - mid.astype(F32)).astype(BF16)

        ab = _dot(xb, wab_ref[...])
        g3 = split3(-jnp.exp(alog_ref[...]) * _softplus(ab + dtb_ref[...]))
        lane = lax.broadcasted_iota(jnp.int32, ab.shape, 1)
        gb = jnp.where(lane < HEADS, _dot(tri_l, g3[0]) + _dot(tri_l, g3[1]) + _dot(tri_l, g3[2]),
                       jnp.where(lane < 2 * HEADS, _dot(tri_u, g3[0]) + _dot(tri_u, g3[1]) + _dot(tri_u, g3[2]),
                                 _sigmoid(ab)))
        gb_ref[bb] = gb[:, 0:N_AB]

        abt = _dot_nt(wabt_ref[0:N_AB, :], xb)
        t3 = split3(-jnp.exp(alogt_ref[...]) * _softplus(abt + dtbt_ref[...]))
        row = lax.broadcasted_iota(jnp.int32, abt.shape, 0)
        gbt_ref[bb] = jnp.where(row < HEADS, _dot(t3[0], tri_u) + _dot(t3[1], tri_u) + _dot(t3[2], tri_u),
                               jnp.where(row < 2 * HEADS, _dot(t3[0], tri_l) + _dot(t3[1], tri_l) + _dot(t3[2], tri_l),
                                         _sigmoid(abt)))

    for bb in range(x_ref.shape[0]):
        one_batch_element(bb)


def _in_proj(x, ctx, mod_lat, mod_ctx, g1, w_in_t, lng, lnb, ws, bst, conv, alog, dtb, alogt, dtbt):
    B, L, _ = x.shape
    n_lat = L // TM_IN
    n_steps = n_lat + 1
    LC = L + TM_IN
    hb = TM_IN // HALO

    def full(shape):
        return pl.BlockSpec(shape, lambda b, j: (0,) * len(shape))

    in_specs = [
        pl.BlockSpec((NB_IN, TM_IN, D), lambda b, j: (b, jnp.maximum(j - 1, 0), 0)),
        pl.BlockSpec((NB_IN, HALO, D), lambda b, j: (b, jnp.clip((j - 1) * hb - 1, 0, L // HALO - 1), 0)),
        pl.BlockSpec((NB_IN, HALO, D), lambda b, j: (b, jnp.clip(j * hb, 0, L // HALO - 1), 0)),
        pl.BlockSpec((NB_IN, TM_IN, D), lambda b, j: (b, 0, 0)),
        pl.BlockSpec((NB_IN, 1, 6 * D), lambda b, j: (b, 0, 0)),
        full((1, 6 * D)), full((1, D)),
        pl.BlockSpec((1, N_MAIN + N_AB, D), lambda b, j: (0, 0, 0), pipeline_mode=pl.Buffered(1)),
        full((1, D_A)), full((1, D_A)), full((HEADS, CHUNK, CHUNK)), full((CHUNK, HEADS)),
        full((CONV_W, N_QKV)), full((1, 128)), full((1, 128)), full((N_AB, 1)), full((N_AB, 1)),
        full((TM_IN, TM_IN)), full((TM_IN, TM_IN)),
    ]
    out_shape = [
        jax.ShapeDtypeStruct((B, L, D_A), BF16),
        jax.ShapeDtypeStruct((B, LC, N_QKV), BF16),
        jax.ShapeDtypeStruct((B, LC, D_B), BF16),
        jax.ShapeDtypeStruct((B, LC, N_AB), F32),
        jax.ShapeDtypeStruct((B, N_AB, LC), F32),
    ]
    out_specs = [
        pl.BlockSpec((NB_IN, TM_IN, D_A), lambda b, j: (b, jnp.maximum(j - 1, 0), 0)),
        pl.BlockSpec((NB_IN, TM_IN, N_QKV), lambda b, j: (b, j, 0)),
        pl.BlockSpec((NB_IN, TM_IN, D_B), lambda b, j: (b, j, 0)),
        pl.BlockSpec((NB_IN, TM_IN, N_AB), lambda b, j: (b, j, 0)),
        pl.BlockSpec((NB_IN, N_AB, TM_IN), lambda b, j: (b, 0, j)),
    ]
    return pl.pallas_call(
        _in_kernel,
        out_shape=out_shape,
        grid=(B // NB_IN, n_steps),
        in_specs=in_specs,
        out_specs=out_specs,
        scratch_shapes=[pltpu.VMEM((D, N_MAIN), BF16), pltpu.VMEM((D, 128), BF16), pltpu.VMEM((128, D), BF16)],
        compiler_params=pltpu.CompilerParams(dimension_semantics=("arbitrary", "arbitrary"),
                                             vmem_limit_bytes=VMEM_LIMIT),
    )(x, x, x, ctx, mod_lat, mod_ctx, g1, w_in_t, lng, lnb, ws, bst, conv, alog, dtb, alogt, dtbt,
      jnp.asarray(_CHUNK_TRIL, BF16), jnp.asarray(_CHUNK_TRIL.T, BF16))


def _delta_kernel(qf_ref, qb_ref, zf_ref, zb_ref, gf_ref, gbk_ref, gtf_ref, gtb_ref, on_ref,
                  y_ref, s_ref, oacc_ref, *, n_ctx, n_lat):
    s = pl.program_id(1)

    @pl.when(s == 0)
    def _():
        s_ref[...] = jnp.zeros_like(s_ref)
        oacc_ref[...] = jnp.zeros_like(oacc_ref)

    row = lax.broadcasted_iota(jnp.int32, (CHUNK, CHUNK), 0)
    col = lax.broadcasted_iota(jnp.int32, (CHUNK, CHUNK), 1)
    low = row > col
    upp = row < col
    same_blk = (row // 16) == (col // 16)
    eye = jnp.where(row == col, 1.0, 0.0).astype(BF16)
    zero = jnp.zeros((CHUNK, CHUNK), BF16)
    onorm = on_ref[...]
    half = n_ctx + n_lat // 2
    second = s >= half
    g_refs = (gf_ref, gbk_ref)
    gt_refs = (gtf_ref, gtb_ref)
    qkv_refs = (qf_ref, qb_ref)
    z_refs = (zf_ref, zb_ref)
    nb = qf_ref.shape[0]
    ps = range(nb * HEADS)

    def halves(xc, unit):
        xb = xc.astype(BF16)
        fill = eye if unit else zero
        return jnp.where(low, xb, fill), jnp.where(upp, xb, fill)

    def as_lhs(hv):
        return jnp.concatenate(hv, axis=1)

    def as_rhs(*hvs):
        cols_ = [jnp.concatenate(hv, axis=0) for hv in hvs]
        return cols_[0] if len(cols_) == 1 else jnp.concatenate(cols_, axis=1)

    def load(d, p, part):
        bb, h = divmod(p, HEADS)
        return qkv_refs[d][bb, :, part * D_B + h * HD:part * D_B + (h + 1) * HD]

    def gcol(d, p, base):
        bb, h = divmod(p, HEADS)
        c = base + d * HEADS + h
        return g_refs[d][bb, :, c:c + 1]

    def grow(d, p, base):
        bb, h = divmod(p, HEADS)
        r = base + d * HEADS + h
        return gt_refs[d][bb, r:r + 1, :]

    def lanes(col):
        return jnp.broadcast_to(col, (CHUNK, HD))

    q = [[load(d, p, 0) for p in ps] for d in range(2)]
    k = [[load(d, p, 1) for p in ps] for d in range(2)]
    v = [[load(d, p, 2) for p in ps] for d in range(2)]
    gcl = [[lanes(gcol(d, p, 0)) for p in ps] for d in range(2)]
    betal = [[lanes(gcol(d, p, 2 * HEADS)) for p in ps] for d in range(2)]
    gr = [[grow(d, p, 0) for p in ps] for d in range(2)]
    betar = [[grow(d, p, 2 * HEADS) for p in ps] for d in range(2)]
    glast = [[gr[0][p][:, CHUNK - 1:CHUNK] for p in ps], [gr[1][p][:, 0:1] for p in ps]]

    gram = [[_dot_nt(jnp.concatenate([q[d][p], k[d][p]], axis=0), k[d][p]) for p in ps] for d in range(2)]
    dec = [jnp.exp(jnp.where(low, gcl[0][p] - gr[0][p], jnp.where(upp, gcl[1][p] - gr[1][p], 0.0))) for p in ps]
    lc = [jnp.where(low, gram[0][p][CHUNK:] * betar[0][p], jnp.where(upp, gram[1][p][CHUNK:] * betar[1][p], 0.0))
          * dec[p] for p in ps]
    qk = [[jnp.where(upp, 0.0, gram[0][p][:CHUNK] * dec[p]).astype(BF16) for p in ps],
          [jnp.where(low, 0.0, gram[1][p][:CHUNK] * dec[p]).astype(BF16) for p in ps]]

    dg = [jnp.where(same_blk, lc[p], 0.0) for p in ps]
    ob = [lc[p] - dg[p] for p in ps]
    d1h = [halves(dg[p], False) for p in ps]
    d2 = [_dot(as_lhs(d1h[p]), as_rhs(d1h[p])) for p in ps]
    p0s = [-dg[p] for p in ps]
    d2h = [halves(d2[p], False) for p in ps]
    p0h = [halves(p0s[p], True) for p in ps]
    o2 = [_dot(as_lhs(d2h[p]), as_rhs(d2h[p], p0h[p])) for p in ps]
    p1s = [p0s[p] + o2[p][:, CHUNK:] for p in ps]
    d4h = [halves(o2[p][:, :CHUNK], False) for p in ps]
    p1h = [halves(p1s[p], True) for p in ps]
    o3 = [_dot(as_lhs(d4h[p]), as_rhs(d4h[p], p1h[p])) for p in ps]
    p2s = [p1s[p] + o3[p][:, CHUNK:] for p in ps]
    d8h = [halves(o3[p][:, :CHUNK], False) for p in ps]
    p2h = [halves(p2s[p], True) for p in ps]
    p3s = [p2s[p] + _dot(as_lhs(d8h[p]), as_rhs(p2h[p])) for p in ps]
    p3h = [halves(p3s[p], True) for p in ps]
    obh = [halves(ob[p], False) for p in ps]
    n1h = [halves(_dot(as_lhs(p3h[p]), as_rhs(obh[p])), False) for p in ps]
    o6 = [_dot(as_lhs(n1h[p]), as_rhs(n1h[p], p3h[p])) for p in ps]
    r0s = [p3s[p] - o6[p][:, CHUNK:] for p in ps]
    n2h = [halves(o6[p][:, :CHUNK], False) for p in ps]
    r0h = [halves(r0s[p], True) for p in ps]
    o7 = [_dot(as_lhs(n2h[p]), as_rhs(n2h[p], r0h[p])) for p in ps]
    r1s = [r0s[p] + o7[p][:, CHUNK:] for p in ps]
    n4h = [halves(o7[p][:, :CHUNK], False) for p in ps]
    r1h = [halves(r1s[p], True) for p in ps]
    tinv = [halves(r1s[p] + _dot(as_lhs(n4h[p]), as_rhs(r1h[p])), True) for p in ps]

    offs = []
    for d in range(2):
        lat_chunk = (s - n_ctx) if d == 0 else (n_ctx + n_lat - 1 - s)
        off = pl.multiple_of(jnp.clip(lat_chunk, 0, n_lat - 1) * CHUNK, CHUNK)
        sidx = [(p // HEADS * 2 + d) * HEADS + p % HEADS for p in ps]
        egc = [jnp.exp(gcl[d][p]) for p in ps]
        kf = [k[d][p].astype(F32) for p in ps]
        rhs = [jnp.concatenate([v[d][p], (kf[p] * egc[p]).astype(BF16)], axis=1) for p in ps]
        uw = [_dot(tinv[p][d], rhs[p]) for p in ps]
        qd = [q[d][p].astype(F32) * egc[p] for p in ps]
        kd = [(kf[p] * jnp.exp(glast[d][p] - gcl[d][p])).astype(BF16) for p in ps]
        st = [s_ref[sidx[p]] for p in ps]
        a1 = [_dot(jnp.concatenate([uw[p][:, HD:] * betal[d][p], qd[p]], axis=0).astype(BF16), st[p].astype(BF16))
              for p in ps]
        vnew = [(uw[p][:, :HD] * betal[d][p] - a1[p][:CHUNK]).astype(BF16) for p in ps]
        o = [a1[p][CHUNK:] + _dot(qk[d][p], vnew[p]) for p in ps]
        for p in ps:
            s_ref[sidx[p]] = st[p] * jnp.exp(glast[d][p]) + _dot_tn(kd[p], vnew[p])
        for p in ps:
            bb, h = divmod(p, HEADS)
            cols = slice(h * HD, (h + 1) * HD)
            oacc_ref[bb, pl.ds(off, CHUNK), cols] = (
                jnp.where(second, oacc_ref[bb, pl.ds(off, CHUNK), cols], 0.0) + o[p])
        offs.append(off)

    @pl.when(second)
    def _():
        for d in range(2):
            for p in ps:
                bb, h = divmod(p, HEADS)
                cols = slice(h * HD, (h + 1) * HD)
                tot = oacc_ref[bb, pl.ds(offs[d], CHUNK), cols]
                ms = jnp.mean(tot * tot, axis=-1, keepdims=True)
                zz = z_refs[d][bb, :, cols].astype(F32)
                y_ref[bb, pl.ds(offs[d], CHUNK), cols] = (
                    tot * lax.rsqrt(ms + EPS) * onorm * _silu(zz)).astype(BF16)


def _delta(qkv, z, gb, gbt, onorm, L):
    B, LC, _ = qkv.shape
    n_all = LC // CHUNK
    n_lat = L // CHUNK
    n_ctx = n_all - n_lat

    def cf(s):
        return s

    def cb(s):
        return jnp.where(s < n_ctx, n_ctx - 1 - s, n_all + n_ctx - 1 - s)

    in_specs = [
        pl.BlockSpec((NB_DELTA, CHUNK, N_QKV), lambda b, s: (b, cf(s), 0)),
        pl.BlockSpec((NB_DELTA, CHUNK, N_QKV), lambda b, s: (b, cb(s), 0)),
        pl.BlockSpec((NB_DELTA, CHUNK, D_B), lambda b, s: (b, cf(s), 0)),
        pl.BlockSpec((NB_DELTA, CHUNK, D_B), lambda b, s: (b, cb(s), 0)),
        pl.BlockSpec((NB_DELTA, CHUNK, N_AB), lambda b, s: (b, cf(s), 0)),
        pl.BlockSpec((NB_DELTA, CHUNK, N_AB), lambda b, s: (b, cb(s), 0)),
        pl.BlockSpec((NB_DELTA, N_AB, CHUNK), lambda b, s: (b, 0, cf(s))),
        pl.BlockSpec((NB_DELTA, N_AB, CHUNK), lambda b, s: (b, 0, cb(s))),
        pl.BlockSpec((1, HD), lambda b, s: (0, 0)),
    ]
    return pl.pallas_call(
        functools.partial(_delta_kernel, n_ctx=n_ctx, n_lat=n_lat),
        out_shape=jax.ShapeDtypeStruct((B, L, D_B), BF16),
        grid=(B // NB_DELTA, n_all),
        in_specs=in_specs,
        out_specs=pl.BlockSpec((NB_DELTA, L, D_B), lambda b, s: (b, 0, 0)),
        scratch_shapes=[pltpu.VMEM((NB_DELTA * 2 * HEADS, HD, HD), F32), pltpu.VMEM((NB_DELTA, L, D_B), F32)],
        compiler_params=pltpu.CompilerParams(dimension_semantics=("arbitrary", "arbitrary"),
                                             vmem_limit_bytes=VMEM_LIMIT),
    )(qkv, qkv, z, z, gb, gb, gbt, gbt, onorm)


def _out_kernel(x_ref, ya_ref, yb_ref, mod_ref, wo_ref, g2_ref, wrt_ref, brt_ref, tri_ref,
                x2_ref, h_ref, route_ref, cnt_ref, base_ref, wbf_ref):
    i = pl.program_id(0)

    @pl.when(i == 0)
    def _():
        base_ref[...] = jnp.zeros_like(base_ref)
        wbf_ref[...] = wo_ref[...].astype(BF16)

    mod = mod_ref[0]
    gt1 = mod[:, 2 * D:3 * D]
    sh2 = mod[:, 3 * D:4 * D]
    sc2 = mod[:, 4 * D:5 * D]
    scale2 = g2_ref[...] * (1.0 + sc2)
    sub = TM_OUT // OUT_SPLIT
    hbs = []
    for r in range(OUT_SPLIT):
        rows = slice(r * sub, (r + 1) * sub)
        mix = _dot(ya_ref[rows, :], wbf_ref[0:D_A, :]) + _dot(yb_ref[rows, :], wbf_ref[D_A:, :])
        x2 = x_ref[rows, :] + gt1 * mix
        x2_ref[rows, :] = x2
        ms = jnp.mean(x2 * x2, axis=-1, keepdims=True)
        hv = x2 * lax.rsqrt(ms + EPS) * scale2 + sh2
        hbs.append(hv.astype(BF16))
        h_ref[rows, 0:D_PACK] = _pack_bf16_pairs(hv)
    hb = jnp.concatenate(hbs, axis=0)

    lt = _dot_nt(wrt_ref[...], hb) + brt_ref[...]
    gl = [lt[r:r + 1, :] for r in range(N_GROUPS)]
    gmax = jnp.maximum(jnp.maximum(gl[0], gl[1]), jnp.maximum(gl[2], gl[3]))
    gsel = jnp.where(gl[0] == gmax, 0, jnp.where(gl[1] == gmax, 1, jnp.where(gl[2] == gmax, 2, 3)))
    p_g = 1.0 / (jnp.exp(gl[0] - gmax) + jnp.exp(gl[1] - gmax) + jnp.exp(gl[2] - gmax) + jnp.exp(gl[3] - gmax))
    el = []
    for e in range(EPG):
        r = [lt[N_GROUPS + g * EPG + e:N_GROUPS + g * EPG + e + 1, :] for g in range(N_GROUPS)]
        el.append(jnp.where(gsel == 0, r[0], jnp.where(gsel == 1, r[1], jnp.where(gsel == 2, r[2], r[3]))))
    m1 = jnp.maximum(jnp.maximum(el[0], el[1]), jnp.maximum(el[2], el[3]))
    i1 = jnp.where(el[0] == m1, 0, jnp.where(el[1] == m1, 1, jnp.where(el[2] == m1, 2, 3)))
    neg = jnp.float32(-jnp.inf)
    el2 = [jnp.where(i1 == e, neg, el[e]) for e in range(EPG)]
    m2 = jnp.maximum(jnp.maximum(el2[0], el2[1]), jnp.maximum(el2[2], el2[3]))
    i2 = jnp.where(jnp.logical_and(el2[0] == m2, i1 != 0), 0,
                   jnp.where(jnp.logical_and(el2[1] == m2, i1 != 1), 1,
                             jnp.where(jnp.logical_and(el2[2] == m2, i1 != 2), 2, 3)))
    t = jnp.exp(m2 - m1)
    w1 = p_g / (1.0 + t)
    w2 = p_g * t / (1.0 + t)
    first_low = i1 < i2
    ea = jnp.where(first_low, i1, i2)
    eb = jnp.where(first_low, i2, i1)
    w_a = jnp.where(first_low, w1, w2)
    w_b = jnp.where(first_low, w2, w1)
    pair = jnp.where(ea == 0, eb - 1, jnp.where(ea == 1, jnp.where(eb == 3, 3, 4), 5))
    swap = pair == 5
    w_a, w_b = jnp.where(swap, w_b, w_a), jnp.where(swap, w_a, w_b)
    bucket = gsel * N_PAIRS + pair

    tm = bucket.shape[1]
    rows = lax.broadcasted_iota(jnp.int32, (32, tm), 0)
    onehot = jnp.where(rows == bucket, 1.0, 0.0).astype(F32)
    prefix = _dot(onehot.astype(BF16), tri_ref[...])
    base = base_ref[:, 0:1]
    rank = jnp.sum(onehot * (prefix - 1.0 + base), axis=0, keepdims=True)
    newbase = base + prefix[:, tm - 1:tm]
    base_ref[...] = jnp.broadcast_to(newbase, base_ref.shape)
    cnt_ref[...] = jnp.broadcast_to(newbase, cnt_ref.shape)
    route = jnp.concatenate([bucket.astype(F32), rank, w_a, w_b, jnp.zeros((4, tm), F32)], axis=0)
    route_ref[0] = route
    route_t = jnp.transpose(jnp.concatenate([route, jnp.zeros((120, tm), F32)], axis=0))
    h_ref[:, D_PACK:D_ROW] = lax.bitcast_convert_type(route_t, jnp.uint32)


def _out_proj(x2d, ya, yb, mod3, w_o, g2, wrt, brt, L):
    T = x2d.shape[0]
    nb = T // TM_OUT
    per_b = L // TM_OUT
    out_shape = [
        jax.ShapeDtypeStruct((T, D), F32),
        jax.ShapeDtypeStruct((T, D_ROW), jnp.uint32),
        jax.ShapeDtypeStruct((nb, 8, TM_OUT), F32),
        jax.ShapeDtypeStruct((32, 128), F32),
    ]
    return pl.pallas_call(
        _out_kernel,
        out_shape=out_shape,
        grid=(nb,),
        in_specs=[
            pl.BlockSpec((TM_OUT, D), lambda i: (i, 0)),
            pl.BlockSpec((TM_OUT, D_A), lambda i: (i, 0)),
            pl.BlockSpec((TM_OUT, D_B), lambda i: (i, 0)),
            pl.BlockSpec((1, 1, 6 * D), lambda i: (i // per_b, 0, 0)),
            pl.BlockSpec((D, D), lambda i: (0, 0), pipeline_mode=pl.Buffered(1)),
            pl.BlockSpec((1, D), lambda i: (0, 0)),
            pl.BlockSpec((32, D), lambda i: (0, 0)),
            pl.BlockSpec((32, 1), lambda i: (0, 0)),
            pl.BlockSpec((TM_OUT, TM_OUT), lambda i: (0, 0), pipeline_mode=pl.Buffered(1)),
        ],
        out_specs=[
            pl.BlockSpec((TM_OUT, D), lambda i: (i, 0)),
            pl.BlockSpec((TM_OUT, D_ROW), lambda i: (i, 0)),
            pl.BlockSpec((1, 8, TM_OUT), lambda i: (i, 0, 0)),
            pl.BlockSpec((32, 128), lambda i: (0, 0)),
        ],
        scratch_shapes=[pltpu.VMEM((32, 128), F32), pltpu.VMEM((D, D), BF16)],
        compiler_params=pltpu.CompilerParams(dimension_semantics=("arbitrary",),
                                             vmem_limit_bytes=VMEM_LIMIT),
    )(x2d, ya, yb, mod3, w_o, g2, wrt, brt, jnp.asarray(np.triu(np.ones((TM_OUT, TM_OUT), np.float32)), BF16))


def _sc_mesh_and_split(n_rows):
    n_workers = SC_CORES * SC_SUBCORES
    per_worker = n_rows // n_workers
    n_chunks = per_worker // SC_WINDOW
    assert per_worker * n_workers == n_rows and n_chunks * SC_WINDOW == per_worker and n_chunks % 2 == 0
    mesh = plsc.VectorSubcoreMesh(core_axis_name="c", subcore_axis_name="s")
    return mesh, n_workers, per_worker, n_chunks


def _sc_two_buffer_loop(n_chunks, first, second):
    first(0, 0).start()

    @pl.loop(0, n_chunks, step=2)
    def _(j):
        for b in range(2):
            jj = j + b
            first(jj, b).wait()

            @pl.when(jj + 1 < n_chunks)
            def _():
                @pl.when(jj >= 1)
                def _():
                    second(jj - 1, 1 - b).wait()
                first(jj + 1, 1 - b).start()

            second(jj, b).start()

    second(n_chunks - 2, 0).wait()
    second(n_chunks - 1, 1).wait()


def _sc_row_gather(table, idx):
    n_rows = idx.shape[0]
    width = table.shape[1]
    mesh, n_workers, per_worker, n_chunks = _sc_mesh_and_split(n_rows)

    @functools.partial(
        pl.kernel, mesh=mesh,
        out_type=jax.ShapeDtypeStruct((n_rows, width), table.dtype),
        scratch_types=[pltpu.VMEM((n_chunks, SC_WINDOW), jnp.int32),
                       pltpu.VMEM((2, SC_WINDOW, width), table.dtype),
                       pltpu.SemaphoreType.DMA((2,)),
                       pltpu.SemaphoreType.DMA((2,))],
    )
    def gather(table_hbm, idx_hbm, out_hbm, idx_v, rows_v, sem_in, sem_out):
        wid = lax.axis_index("s") * SC_CORES + lax.axis_index("c")
        base = wid * per_worker
        pltpu.sync_copy(idx_hbm.at[wid], idx_v)

        def fetch(j, b):
            return pltpu.make_async_copy(table_hbm.at[idx_v.at[j]], rows_v.at[b], sem_in.at[b])

        def put(j, b):
            off = pl.multiple_of(base + j * SC_WINDOW, 8)
            return pltpu.make_async_copy(rows_v.at[b], out_hbm.at[pl.ds(off, SC_WINDOW)], sem_out.at[b])

        _sc_two_buffer_loop(n_chunks, fetch, put)

    return gather(table, idx.reshape(n_workers, n_chunks, SC_WINDOW))


def _sc_row_scatter(rows, pos, n_out):
    n_rows, width = rows.shape
    mesh, n_workers, per_worker, n_chunks = _sc_mesh_and_split(n_rows)

    @functools.partial(
        pl.kernel, mesh=mesh,
        out_type=jax.ShapeDtypeStruct((n_out, width), rows.dtype),
        scratch_types=[pltpu.VMEM((n_chunks, SC_WINDOW), jnp.int32),
                       pltpu.VMEM((2, SC_WINDOW, width), rows.dtype),
                       pltpu.SemaphoreType.DMA((2,)),
                       pltpu.SemaphoreType.DMA((2,))],
    )
    def scatter(rows_hbm, pos_hbm, out_hbm, idx_v, rows_v, sem_in, sem_out):
        wid = lax.axis_index("s") * SC_CORES + lax.axis_index("c")
        base = wid * per_worker
        pltpu.sync_copy(pos_hbm.at[wid], idx_v)

        def fetch(j, b):
            off = pl.multiple_of(base + j * SC_WINDOW, 8)
            return pltpu.make_async_copy(rows_hbm.at[pl.ds(off, SC_WINDOW)], rows_v.at[b], sem_in.at[b])

        def put(j, b):
            return pltpu.make_async_copy(rows_v.at[b], out_hbm.at[idx_v.at[j]], sem_out.at[b])

        _sc_two_buffer_loop(n_chunks, fetch, put)

    return scatter(rows, pos.reshape(n_workers, n_chunks, SC_WINDOW))


def _cast_kernel(x_ref, o_ref):
    o_ref[...] = x_ref[...].astype(BF16)


def _expert_weights_bf16(w):
    n, r, c = w.shape
    return pl.pallas_call(
        _cast_kernel,
        out_shape=jax.ShapeDtypeStruct(w.shape, BF16),
        grid=(n,),
        in_specs=[pl.BlockSpec((1, r, c), lambda e: (e, 0, 0))],
        out_specs=pl.BlockSpec((1, r, c), lambda e: (e, 0, 0)),
        compiler_params=pltpu.CompilerParams(dimension_semantics=("arbitrary",),
                                             vmem_limit_bytes=VMEM_LIMIT),
    )(w)


def _moe_kernel(ea_ref, eb_ref, nv_ref, hs_ref, wga_ref, wgb_ref, wda_ref, wdb_ref, o_ref, wd_s):
    i = pl.program_id(0)
    prev = jnp.maximum(i - 1, 0)

    @pl.when(jnp.logical_or(i == 0, ea_ref[i] != ea_ref[prev]))
    def _():
        wd_s[0] = wda_ref[0].astype(BF16)

    @pl.when(jnp.logical_or(i == 0, eb_ref[i] != eb_ref[prev]))
    def _():
        wd_s[1] = wdb_ref[0].astype(BF16)

    @pl.when(nv_ref[i] > 0)
    def _():
        live = lax.broadcasted_iota(jnp.int32, (BM_MOE, 1), 0) < nv_ref[i]
        xb = _unpack_bf16_pairs(jnp.where(live, hs_ref[:, 0:D_PACK], jnp.uint32(0)))
        w_ab = lax.bitcast_convert_type(hs_ref[:, D_PACK:D_ROW], F32)
        w_a = jnp.where(live, w_ab[:, 2:3], 0.0)
        w_b = jnp.where(live, w_ab[:, 3:4], 0.0)
        ga = _dot(xb, wga_ref[0])
        act_a = (_silu(ga[:, :D_EXPERT]) * ga[:, D_EXPERT:] * w_a).astype(BF16)
        gb = _dot(xb, wgb_ref[0])
        act_b = (_silu(gb[:, :D_EXPERT]) * gb[:, D_EXPERT:] * w_b).astype(BF16)
        o_ref[...] = _pack_bf16_pairs(_dot(act_a, wd_s[0]) + _dot(act_b, wd_s[1]))

    @pl.when(nv_ref[i] <= 0)
    def _():
        o_ref[...] = jnp.zeros_like(o_ref)


def _moe(ea, eb, nvalid, hs, w_gu, w_dn):
    nblk = ea.shape[0]
    S = nblk * BM_MOE
    grid_spec = pltpu.PrefetchScalarGridSpec(
        num_scalar_prefetch=3,
        grid=(nblk,),
        in_specs=[
            pl.BlockSpec((BM_MOE, D_ROW), lambda i, ea, eb, nv: (i, 0)),
            pl.BlockSpec((1, D, 2 * D_EXPERT), lambda i, ea, eb, nv: (ea[i], 0, 0)),
            pl.BlockSpec((1, D, 2 * D_EXPERT), lambda i, ea, eb, nv: (eb[i], 0, 0)),
            pl.BlockSpec((1, D_EXPERT, D), lambda i, ea, eb, nv: (ea[i], 0, 0)),
            pl.BlockSpec((1, D_EXPERT, D), lambda i, ea, eb, nv: (eb[i], 0, 0)),
        ],
        out_specs=pl.BlockSpec((BM_MOE, D_PACK), lambda i, ea, eb, nv: (i, 0)),
        scratch_shapes=[pltpu.VMEM((2, D_EXPERT, D), BF16)],
    )
    return pl.pallas_call(
        _moe_kernel,
        out_shape=jax.ShapeDtypeStruct((S, D_PACK), jnp.uint32),
        grid_spec=grid_spec,
        compiler_params=pltpu.CompilerParams(dimension_semantics=("arbitrary",),
                                             vmem_limit_bytes=VMEM_LIMIT),
    )(ea, eb, nvalid, hs, w_gu, w_gu, w_dn, w_dn)


def _final_kernel(x2_ref, m_ref, mod_ref, fg_ref, *rest):
    o_ref = rest[-1]
    gt2 = mod_ref[0][:, 5 * D:6 * D]
    y = x2_ref[...] + gt2 * _unpack_bf16_pairs(m_ref[...]).astype(F32)
    ms = jnp.mean(y * y, axis=-1, keepdims=True)
    o_ref[...] = y * lax.rsqrt(ms + EPS) * fg_ref[...]


def _final(x2, m_part, mod3, fg, L, part, prev_out):
    T = x2.shape[0]
    steps = m_part.shape[0] // TM_FIN
    off = part * steps
    per_b = L // TM_FIN
    in_specs = [
        pl.BlockSpec((TM_FIN, D), lambda i: (i + off, 0)),
        pl.BlockSpec((TM_FIN, D_PACK), lambda i: (i, 0)),
        pl.BlockSpec((1, 1, 6 * D), lambda i: ((i + off) // per_b, 0, 0)),
        pl.BlockSpec((1, D), lambda i: (0, 0)),
    ]
    args = [x2, m_part, mod3, fg]
    aliases = {}
    if prev_out is not None:
        in_specs.append(pl.BlockSpec(memory_space=pl.ANY))
        args.append(prev_out)
        aliases = {len(args) - 1: 0}
    return pl.pallas_call(
        _final_kernel,
        out_shape=jax.ShapeDtypeStruct((T, D), F32),
        grid=(steps,),
        in_specs=in_specs,
        out_specs=pl.BlockSpec((TM_FIN, D), lambda i: (i + off, 0)),
        input_output_aliases=aliases,
        compiler_params=pltpu.CompilerParams(dimension_semantics=("arbitrary",),
                                             vmem_limit_bytes=VMEM_LIMIT),
    )(*args)


def kernel(x, c, ctx, c_ctx, w_ada, b_ada, norm1_g, w_in, ln_a_g, ln_a_b, w_spatial, b_spatial, conv_qkv, a_log,
           dt_bias, onorm_g, w_out, norm2_g, w_group, b_group, w_router, b_router, w_gate_up, w_down, final_g):
    B, L, _ = x.shape
    T = B * L
    assert w_ada.shape[0] == 1 and ctx.shape[1] == TM_IN and L % TM_OUT == 0 and T % (N_COMBINE_PARTS * TM_FIN) == 0

    cond = jnp.concatenate([c, c_ctx[None, :], jnp.zeros((7, D), F32)], axis=0)
    mod = _modulation(cond, w_ada[0], b_ada[0][None, :])
    mod_lat = mod[:B].reshape(B, 1, 6 * D)
    mod_ctx = mod[B:B + 1]

    alog = a_log[0].reshape(1, 2 * HEADS)
    dtb = dt_bias[0].reshape(1, 2 * HEADS)
    alog_row = jnp.pad(alog, ((0, 0), (0, 128 - 2 * HEADS)))
    dtb_row = jnp.pad(dtb, ((0, 0), (0, 128 - 2 * HEADS)))
    alog_col = jnp.pad(alog, ((0, 0), (0, N_AB - 2 * HEADS))).T
    dtb_col = jnp.pad(dtb, ((0, 0), (0, N_AB - 2 * HEADS))).T

    ya, qkv, z, gb, gbt = _in_proj(
        x, ctx, mod_lat, mod_ctx, norm1_g, jnp.swapaxes(w_in, 1, 2), ln_a_g, ln_a_b,
        w_spatial[0].astype(BF16), b_spatial[0].T, conv_qkv[0], alog_row, dtb_row, alog_col, dtb_col)

    yb = _delta(qkv, z, gb, gbt, onorm_g, L)

    wrt = jnp.concatenate([w_group[0].T, w_router[0].T, jnp.zeros((32 - N_GROUPS - N_EXPERTS, D), F32)], axis=0)
    brt = jnp.concatenate([b_group[0], b_router[0], jnp.zeros((32 - N_GROUPS - N_EXPERTS,), F32)])[:, None]
    x2, h, route, cnt = _out_proj(x.reshape(T, D), ya.reshape(T, D_A), yb.reshape(T, D_B), mod_lat,
                                  w_out[0], norm2_g, wrt.astype(BF16), brt, L)

    bucket = route[:, 0, :].reshape(T).astype(jnp.int32)
    rank = route[:, 1, :].reshape(T).astype(jnp.int32)
    counts = cnt[:N_BUCKETS, 0].astype(jnp.int32)
    nblk_b = (counts + BM_MOE - 1) // BM_MOE
    blk_end = jnp.cumsum(nblk_b)
    blk_start = blk_end - nblk_b
    kk = jnp.arange(N_BUCKETS, dtype=jnp.int32)

    def pick(idx, table):
        return jnp.sum(jnp.where(idx[:, None] == kk[None, :], table[None, :], 0), axis=1)

    pos = pick(bucket, blk_start) * BM_MOE + rank
    n_blocks = T // BM_MOE + N_BUCKETS
    S = n_blocks * BM_MOE
    blk = jnp.arange(n_blocks, dtype=jnp.int32)
    used = blk < blk_end[-1]
    bkt = jnp.sum((jnp.minimum(blk, blk_end[-1] - 1)[:, None] >= blk_end[None, :]).astype(jnp.int32), axis=1)
    nvalid = jnp.where(used, jnp.clip(pick(bkt, counts) - (blk - pick(bkt, blk_start)) * BM_MOE, 0, BM_MOE),
                       0).astype(jnp.int32)
    ea = pick(bkt, jnp.asarray(_SLOT_A_EXPERT))
    eb = pick(bkt, jnp.asarray(_SLOT_B_EXPERT))

    hs = _sc_row_scatter(h, pos, S)
    ms = _moe(ea, eb, nvalid, hs, _expert_weights_bf16(w_gate_up[0]), w_down[0])
    pos_parts = pos.reshape(N_COMBINE_PARTS, T // N_COMBINE_PARTS)
    m_parts = [_sc_row_gather(ms, pos_parts[q]) for q in range(N_COMBINE_PARTS)]
    out = None
    for q in range(N_COMBINE_PARTS):
        out = _final(x2, m_parts[q], mod_lat, final_g[None, :], L, q, out)
    return out.reshape(B, L, D)
```

```python
import functools

import jax
import jax.numpy as jnp
import numpy as np
from jax import lax
from jax.experimental import pallas as pl
from jax.experimental.pallas import tpu as pltpu
from jax.experimental.pallas import tpu_sc as plsc

F32 = jnp.float32
BF16 = jnp.bfloat16
EPS = 1e-6

D = 1024
D_A = 512
D_B = 512
HEADS = 4
HD = 128
CHUNK = 128
CONV_W = 5
N_QKV = 3 * D_B
N_MAIN = 2 * D_A + 4 * D_B
N_AB = 16
N_GROUPS = 4
EPG = 4
N_EXPERTS = 16
D_EXPERT = 512
N_PAIRS = 6
N_BUCKETS = N_GROUPS * N_PAIRS
D_PACK = D // 2
D_ROW = D_PACK + 128
PAIR_A = (0, 0, 0, 1, 1, 3)
PAIR_B = (1, 2, 3, 3, 2, 2)
_SLOT_A_EXPERT = np.array([g * EPG + PAIR_A[p] for g in range(N_GROUPS) for p in range(N_PAIRS)], np.int32)
_SLOT_B_EXPERT = np.array([g * EPG + PAIR_B[p] for g in range(N_GROUPS) for p in range(N_PAIRS)], np.int32)

TM_IN = 256
HALO = 8
NB_IN = 2
NB_DELTA = 2
TM_OUT = 1024
OUT_SPLIT = 4
BM_MOE = 256
TM_FIN = 1024
N_COMBINE_PARTS = 4
VMEM_LIMIT = 56 * 1024 * 1024
SC_CORES = 2
SC_SUBCORES = 16
SC_WINDOW = 32

HI = lax.Precision.HIGHEST

_CHUNK_TRIL = np.kron(np.eye(TM_IN // CHUNK, dtype=np.float32), np.tril(np.ones((CHUNK, CHUNK), np.float32)))


def _dot(a, b, precision=None):
    return jnp.dot(a, b, preferred_element_type=F32, precision=precision)


def _dot_nt(a, b):
    return lax.dot_general(a, b, (((1,), (1,)), ((), ())), preferred_element_type=F32)


def _dot_tn(a, b):
    return lax.dot_general(a, b, (((0,), (0,)), ((), ())), preferred_element_type=F32)


def _sigmoid(x):
    return 0.5 + 0.5 * jnp.tanh(0.5 * x)


def _silu(x):
    h = 0.5 * x
    return h + h * jnp.tanh(h)


def _softplus(x):
    return jnp.maximum(x, 0.0) + jnp.log(1.0 + jnp.exp(-jnp.abs(x)))


def _pack_bf16_pairs(x):
    bits = lax.bitcast_convert_type(x.astype(BF16).astype(F32), jnp.uint32)
    return (bits[:, D_PACK:] & jnp.uint32(0xFFFF0000)) | (bits[:, :D_PACK] >> 16)


def _unpack_bf16_pairs(w):
    bits = w
    lo = lax.bitcast_convert_type(bits << 16, F32)
    hi = lax.bitcast_convert_type(bits & jnp.uint32(0xFFFF0000), F32)
    return jnp.concatenate([lo, hi], axis=1).astype(BF16)


def _gelu_tanh(x):
    return 0.5 * x * (1.0 + jnp.tanh(np.sqrt(2.0 / np.pi).astype(np.float32) * (x + 0.044715 * (x * x * x))))


def _mod_kernel(c_ref, w_ref, b_ref, o_ref):
    c = c_ref[...]
    o_ref[...] = _dot(_silu(c), w_ref[...], precision=HI) + b_ref[...]


def _modulation(cond, w_ada, b_ada):
    rows = cond.shape[0]
    tn = 1536
    return pl.pallas_call(
        _mod_kernel,
        out_shape=jax.ShapeDtypeStruct((rows, 6 * D), F32),
        grid=(6 * D // tn,),
        in_specs=[pl.BlockSpec((rows, D), lambda i: (0, 0)),
                  pl.BlockSpec((D, tn), lambda i: (0, i)),
                  pl.BlockSpec((1, tn), lambda i: (0, i))],
        out_specs=pl.BlockSpec((rows, tn), lambda i: (0, i)),
        compiler_params=pltpu.CompilerParams(dimension_semantics=("arbitrary",),
                                             vmem_limit_bytes=VMEM_LIMIT),
    )(cond, w_ada, b_ada)


def _in_kernel(x_ref, xp_ref, xn_ref, ctx_ref, mod_ref, cmod_ref, g1_ref, wt_ref,
               lng_ref, lnb_ref, ws_ref, bst_ref, conv_ref, alog_ref, dtb_ref, alogt_ref, dtbt_ref, tril_ref, triu_ref,
               ya_ref, qkv_ref, z_ref, gb_ref, gbt_ref, wbf_ref, wab_ref, wabt_ref):
    j = pl.program_id(1)

    @pl.when(jnp.logical_and(pl.program_id(0) == 0, j == 0))
    def _():
        for c0 in range(0, N_MAIN, 512):
            wbf_ref[:, c0:c0 + 512] = jnp.transpose(wt_ref[0, c0:c0 + 512, :]).astype(BF16)
        tail = jnp.concatenate([wt_ref[0, N_MAIN:N_MAIN + N_AB, :], jnp.zeros((128 - N_AB, D), F32)], axis=0)
        wabt_ref[...] = tail.astype(BF16)
        wab_ref[...] = jnp.transpose(tail).astype(BF16)

    is_ctx = j == 0
    n_lat_blocks = pl.num_programs(1) - 1

    def one_batch_element(bb):
        mod = mod_ref[bb]
        cm = cmod_ref[...]
        sh = jnp.where(is_ctx, cm[:, 0:D], mod[:, 0:D])
        sc = jnp.where(is_ctx, cm[:, D:2 * D], mod[:, D:2 * D])
        scale = g1_ref[...] * (1.0 + sc)

        xmain = jnp.where(is_ctx, ctx_ref[bb], x_ref[bb])
        xv = jnp.concatenate([xp_ref[bb], xmain, xn_ref[bb]], axis=0)
        xnorm = xv * lax.rsqrt(jnp.mean(xv * xv, axis=-1, keepdims=True) + EPS) * scale + sh
        xe = xnorm.astype(BF16)
        xb = xnorm[HALO:HALO + TM_IN].astype(BF16)

        rid = lax.broadcasted_iota(jnp.int32, (TM_IN + 2 * HALO, 1), 0)
        prev_ok = j >= 2
        next_ok = jnp.logical_and(j >= 1, j < n_lat_blocks)
        valid = jnp.logical_or(jnp.logical_and(rid >= HALO, rid < HALO + TM_IN),
                               jnp.logical_or(jnp.logical_and(rid < HALO, prev_ok),
                                              jnp.logical_and(rid >= HALO + TM_IN, next_ok)))
        pad = (CONV_W - 1) // 2
        c_qkv = 2 * D_A

        def proj(c0, width, halo=True):
            return _dot(xe if halo else xb, wbf_ref[:, c0:c0 + width])

        def conv_act(pq, c0):
            groups = (TM_IN + 2 * HALO) // 8
            x3 = jnp.where(valid, pq, 0.0).reshape(groups, 8, D_B)
            sub = lax.broadcasted_iota(jnp.int32, (1, 8, 1), 1)
            lo, hi = HALO // 8, HALO // 8 + TM_IN // 8
            acc = conv_ref[pad:pad + 1, c0:c0 + D_B] * x3[lo:hi]
            for t in range(CONV_W):
                s = t - pad
                if s == 0:
                    continue
                r = pltpu.roll(x3, (-s) % 8, axis=1)
                if s > 0:
                    sh = jnp.where(sub < 8 - s, r[lo:hi], r[lo + 1:hi + 1])
                else:
                    sh = jnp.where(sub >= -s, r[lo:hi], r[lo - 1:hi - 1])
                acc = acc + conv_ref[t:t + 1, c0:c0 + D_B] * sh
            return _silu(acc.reshape(TM_IN, D_B))

        def store_unit_heads(act, c0, gain):
            for h in range(HEADS):
                t = act[:, h * HD:(h + 1) * HD]
                nrm = lax.rsqrt(jnp.sum(t * t, axis=-1, keepdims=True) + EPS) * gain
                qkv_ref[bb, :, c0 + h * HD:c0 + (h + 1) * HD] = (t * nrm).astype(BF16)

        pq_q = proj(c_qkv, D_B)
        pq_k = proj(c_qkv + D_B, D_B)
        store_unit_heads(conv_act(pq_q, 0), 0, HD ** -0.5)
        pq_v = proj(c_qkv + 2 * D_B, D_B)
        store_unit_heads(conv_act(pq_k, D_B), D_B, 1.0)
        pa_u = proj(0, D_A, halo=False)
        qkv_ref[bb, :, 2 * D_B:] = conv_act(pq_v, 2 * D_B).astype(BF16)
        pa_v = proj(D_A, D_A, halo=False)
        u = _gelu_tanh(pa_u)
        pz = proj(c_qkv + N_QKV, D_B, halo=False)
        v = _gelu_tanh(pa_v)
        mu = jnp.mean(v, axis=-1, keepdims=True)
        vc = v - mu
        var = jnp.mean(vc * vc, axis=-1, keepdims=True)
        vn = (vc * lax.rsqrt(var + EPS) * lng_ref[...] + lnb_ref[...]).astype(BF16)
        z_ref[bb] = pz.astype(BF16)

        bst = bst_ref[...]
        for n in range(TM_IN // CHUNK):
            rows = slice(n * CHUNK, (n + 1) * CHUNK)
            for h in range(HEADS):
                cols = slice(h * HD, (h + 1) * HD)
                s = _dot(ws_ref[h], vn[rows, cols]) + bst[:, h:h + 1]
                ya_ref[bb, rows, cols] = (u[rows, cols] * s).astype(BF16)

        tri_l = tril_ref[...]
        tri_u = triu_ref[...]

        def split3(g):
            hi = g.astype(BF16)
            r1 = g - hi.astype(F32)
            mid = r1.astype(BF16)
            return hi, mid, (r1 - mid.astype(F32)).astype(BF16)

        ab = _dot(xb, wab_ref[...])
        g3 = split3(-jnp.exp(alog_ref[...]) * _softplus(ab + dtb_ref[...]))
        lane = lax.broadcasted_iota(jnp.int32, ab.shape, 1)
        gb = jnp.where(lane < HEADS, _dot(tri_l, g3[0]) + _dot(tri_l, g3[1]) + _dot(tri_l, g3[2]),
                       jnp.where(lane < 2 * HEADS, _dot(tri_u, g3[0]) + _dot(tri_u, g3[1]) + _dot(tri_u, g3[2]),
                                 _sigmoid(ab)))
        gb_ref[bb] = gb[:, 0:N_AB]

        abt = _dot_nt(wabt_ref[0:N_AB, :], xb)
        t3 = split3(-jnp.exp(alogt_ref[...]) * _softplus(abt + dtbt_ref[...]))
        row = lax.broadcasted_iota(jnp.int32, abt.shape, 0)
        gbt_ref[bb] = jnp.where(row < HEADS, _dot(t3[0], tri_u) + _dot(t3[1], tri_u) + _dot(t3[2], tri_u),
                               jnp.where(row < 2 * HEADS, _dot(t3[0], tri_l) + _dot(t3[1], tri_l) + _dot(t3[2], tri_l),
                                         _sigmoid(abt)))

    for bb in range(x_ref.shape[0]):
        one_batch_element(bb)


def _in_proj(x, ctx, mod_lat, mod_ctx, g1, w_in_t, lng, lnb, ws, bst, conv, alog, dtb, alogt, dtbt):
    B, L, _ = x.shape
    n_lat = L // TM_IN
    n_steps = n_lat + 1
    LC = L + TM_IN
    hb = TM_IN // HALO

    def full(shape):
        return pl.BlockSpec(shape, lambda b, j: (0,) * len(shape))

    in_specs = [
        pl.BlockSpec((NB_IN, TM_IN, D), lambda b, j: (b, jnp.maximum(j - 1, 0), 0)),
        pl.BlockSpec((NB_IN, HALO, D), lambda b, j: (b, jnp.clip((j - 1) * hb - 1, 0, L // HALO - 1), 0)),
        pl.BlockSpec((NB_IN, HALO, D), lambda b, j: (b, jnp.clip(j * hb, 0, L // HALO - 1), 0)),
        pl.BlockSpec((NB_IN, TM_IN, D), lambda b, j: (b, 0, 0)),
        pl.BlockSpec((NB_IN, 1, 6 * D), lambda b, j: (b, 0, 0)),
        full((1, 6 * D)), full((1, D)),
        pl.BlockSpec((1, N_MAIN + N_AB, D), lambda b, j: (0, 0, 0), pipeline_mode=pl.Buffered(1)),
        full((1, D_A)), full((1, D_A)), full((HEADS, CHUNK, CHUNK)), full((CHUNK, HEADS)),
        full((CONV_W, N_QKV)), full((1, 128)), full((1, 128)), full((N_AB, 1)), full((N_AB, 1)),
        full((TM_IN, TM_IN)), full((TM_IN, TM_IN)),
    ]
    out_shape = [
        jax.ShapeDtypeStruct((B, L, D_A), BF16),
        jax.ShapeDtypeStruct((B, LC, N_QKV), BF16),
        jax.ShapeDtypeStruct((B, LC, D_B), BF16),
        jax.ShapeDtypeStruct((B, LC, N_AB), F32),
        jax.ShapeDtypeStruct((B, N_AB, LC), F32),
    ]
    out_specs = [
        pl.BlockSpec((NB_IN, TM_IN, D_A), lambda b, j: (b, jnp.maximum(j - 1, 0), 0)),
        pl.BlockSpec((NB_IN, TM_IN, N_QKV), lambda b, j: (b, j, 0)),
        pl.BlockSpec((NB_IN, TM_IN, D_B), lambda b, j: (b, j, 0)),
        pl.BlockSpec((NB_IN, TM_IN, N_AB), lambda b, j: (b, j, 0)),
        pl.BlockSpec((NB_IN, N_AB, TM_IN), lambda b, j: (b, 0, j)),
    ]
    return pl.pallas_call(
        _in_kernel,
        out_shape=out_shape,
        grid=(B // NB_IN, n_steps),
        in_specs=in_specs,
        out_specs=out_specs,
        scratch_shapes=[pltpu.VMEM((D, N_MAIN), BF16), pltpu.VMEM((D, 128), BF16), pltpu.VMEM((128, D), BF16)],
        compiler_params=pltpu.CompilerParams(dimension_semantics=("arbitrary", "arbitrary"),
                                             vmem_limit_bytes=VMEM_LIMIT),
    )(x, x, x, ctx, mod_lat, mod_ctx, g1, w_in_t, lng, lnb, ws, bst, conv, alog, dtb, alogt, dtbt,
      jnp.asarray(_CHUNK_TRIL, BF16), jnp.asarray(_CHUNK_TRIL.T, BF16))


def _delta_kernel(qf_ref, qb_ref, zf_ref, zb_ref, gf_ref, gbk_ref, gtf_ref, gtb_ref, on_ref, wgu_ref, wdn_ref,
                  y_ref, wgu_bf_ref, wdn_bf_ref, s_ref, oacc_ref, *, n_ctx, n_lat):
    s = pl.program_id(1)

    wgu_bf_ref[...] = wgu_ref[...].astype(BF16)
    wdn_bf_ref[...] = wdn_ref[...].astype(BF16)

    @pl.when(s == 0)
    def _():
        s_ref[...] = jnp.zeros_like(s_ref)
        oacc_ref[...] = jnp.zeros_like(oacc_ref)

    row = lax.broadcasted_iota(jnp.int32, (CHUNK, CHUNK), 0)
    col = lax.broadcasted_iota(jnp.int32, (CHUNK, CHUNK), 1)
    low = row > col
    upp = row < col
    same_blk = (row // 16) == (col // 16)
    eye = jnp.where(row == col, 1.0, 0.0).astype(BF16)
    zero = jnp.zeros((CHUNK, CHUNK), BF16)
    onorm = on_ref[...]
    half = n_ctx + n_lat // 2
    second = s >= half
    g_refs = (gf_ref, gbk_ref)
    gt_refs = (gtf_ref, gtb_ref)
    qkv_refs = (qf_ref, qb_ref)
    z_refs = (zf_ref, zb_ref)
    nb = qf_ref.shape[0]
    ps = range(nb * HEADS)

    def halves(xc, unit):
        xb = xc.astype(BF16)
        fill = eye if unit else zero
        return jnp.where(low, xb, fill), jnp.where(upp, xb, fill)

    def as_lhs(hv):
        return jnp.concatenate(hv, axis=1)

    def as_rhs(*hvs):
        cols_ = [jnp.concatenate(hv, axis=0) for hv in hvs]
        return cols_[0] if len(cols_) == 1 else jnp.concatenate(cols_, axis=1)

    def load(d, p, part):
        bb, h = divmod(p, HEADS)
        return qkv_refs[d][bb, :, part * D_B + h * HD:part * D_B + (h + 1) * HD]

    def gcol(d, p, base):
        bb, h = divmod(p, HEADS)
        c = base + d * HEADS + h
        return g_refs[d][bb, :, c:c + 1]

    def grow(d, p, base):
        bb, h = divmod(p, HEADS)
        r = base + d * HEADS + h
        return gt_refs[d][bb, r:r + 1, :]

    def lanes(col):
        return jnp.broadcast_to(col, (CHUNK, HD))

    q = [[load(d, p, 0) for p in ps] for d in range(2)]
    k = [[load(d, p, 1) for p in ps] for d in range(2)]
    v = [[load(d, p, 2) for p in ps] for d in range(2)]
    gcl = [[lanes(gcol(d, p, 0)) for p in ps] for d in range(2)]
    betal = [[lanes(gcol(d, p, 2 * HEADS)) for p in ps] for d in range(2)]
    gr = [[grow(d, p, 0) for p in ps] for d in range(2)]
    betar = [[grow(d, p, 2 * HEADS) for p in ps] for d in range(2)]
    glast = [[gr[0][p][:, CHUNK - 1:CHUNK] for p in ps], [gr[1][p][:, 0:1] for p in ps]]

    gram = [[_dot_nt(jnp.concatenate([q[d][p], k[d][p]], axis=0), k[d][p]) for p in ps] for d in range(2)]
    dec = [jnp.exp(jnp.where(low, gcl[0][p] - gr[0][p], jnp.where(upp, gcl[1][p] - gr[1][p], 0.0))) for p in ps]
    lc = [jnp.where(low, gram[0][p][CHUNK:] * betar[0][p], jnp.where(upp, gram[1][p][CHUNK:] * betar[1][p], 0.0))
          * dec[p] for p in ps]
    qk = [[jnp.where(upp, 0.0, gram[0][p][:CHUNK] * dec[p]).astype(BF16) for p in ps],
          [jnp.where(low, 0.0, gram[1][p][:CHUNK] * dec[p]).astype(BF16) for p in ps]]

    dg = [jnp.where(same_blk, lc[p], 0.0) for p in ps]
    ob = [lc[p] - dg[p] for p in ps]
    d1h = [halves(dg[p], False) for p in ps]
    d2 = [_dot(as_lhs(d1h[p]), as_rhs(d1h[p])) for p in ps]
    p0s = [-dg[p] for p in ps]
    d2h = [halves(d2[p], False) for p in ps]
    p0h = [halves(p0s[p], True) for p in ps]
    o2 = [_dot(as_lhs(d2h[p]), as_rhs(d2h[p], p0h[p])) for p in ps]
    p1s = [p0s[p] + o2[p][:, CHUNK:] for p in ps]
    d4h = [halves(o2[p][:, :CHUNK], False) for p in ps]
    p1h = [halves(p1s[p], True) for p in ps]
    o3 = [_dot(as_lhs(d4h[p]), as_rhs(d4h[p], p1h[p])) for p in ps]
    p2s = [p1s[p] + o3[p][:, CHUNK:] for p in ps]
    d8h = [halves(o3[p][:, :CHUNK], False) for p in ps]
    p2h = [halves(p2s[p], True) for p in ps]
    p3s = [p2s[p] + _dot(as_lhs(d8h[p]), as_rhs(p2h[p])) for p in ps]
    p3h = [halves(p3s[p], True) for p in ps]
    obh = [halves(ob[p], False) for p in ps]
    n1h = [halves(_dot(as_lhs(p3h[p]), as_rhs(obh[p])), False) for p in ps]
    o6 = [_dot(as_lhs(n1h[p]), as_rhs(n1h[p], p3h[p])) for p in ps]
    r0s = [p3s[p] - o6[p][:, CHUNK:] for p in ps]
    n2h = [halves(o6[p][:, :CHUNK], False) for p in ps]
    r0h = [halves(r0s[p], True) for p in ps]
    o7 = [_dot(as_lhs(n2h[p]), as_rhs(n2h[p], r0h[p])) for p in ps]
    r1s = [r0s[p] + o7[p][:, CHUNK:] for p in ps]
    n4h = [halves(o7[p][:, :CHUNK], False) for p in ps]
    r1h = [halves(r1s[p], True) for p in ps]
    tinv = [halves(r1s[p] + _dot(as_lhs(n4h[p]), as_rhs(r1h[p])), True) for p in ps]

    offs = []
    for d in range(2):
        lat_chunk = (s - n_ctx) if d == 0 else (n_ctx + n_lat - 1 - s)
        off = pl.multiple_of(jnp.clip(lat_chunk, 0, n_lat - 1) * CHUNK, CHUNK)
        sidx = [(p // HEADS * 2 + d) * HEADS + p % HEADS for p in ps]
        egc = [jnp.exp(gcl[d][p]) for p in ps]
        kf = [k[d][p].astype(F32) for p in ps]
        rhs = [jnp.concatenate([v[d][p], (kf[p] * egc[p]).astype(BF16)], axis=1) for p in ps]
        uw = [_dot(tinv[p][d], rhs[p]) for p in ps]
        qd = [q[d][p].astype(F32) * egc[p] for p in ps]
        kd = [(kf[p] * jnp.exp(glast[d][p] - gcl[d][p])).astype(BF16) for p in ps]
        st = [s_ref[sidx[p]] for p in ps]
        a1 = [_dot(jnp.concatenate([uw[p][:, HD:] * betal[d][p], qd[p]], axis=0).astype(BF16), st[p].astype(BF16))
              for p in ps]
        vnew = [(uw[p][:, :HD] * betal[d][p] - a1[p][:CHUNK]).astype(BF16) for p in ps]
        o = [a1[p][CHUNK:] + _dot(qk[d][p], vnew[p]) for p in ps]
        for p in ps:
            s_ref[sidx[p]] = st[p] * jnp.exp(glast[d][p]) + _dot_tn(kd[p], vnew[p])
        for p in ps:
            bb, h = divmod(p, HEADS)
            cols = slice(h * HD, (h + 1) * HD)
            oacc_ref[bb, pl.ds(off, CHUNK), cols] = (
                jnp.where(second, oacc_ref[bb, pl.ds(off, CHUNK), cols], 0.0) + o[p])
        offs.append(off)

    @pl.when(second)
    def _():
        for d in range(2):
            for p in ps:
                bb, h = divmod(p, HEADS)
                cols = slice(h * HD, (h + 1) * HD)
                tot = oacc_ref[bb, pl.ds(offs[d], CHUNK), cols]
                ms = jnp.mean(tot * tot, axis=-1, keepdims=True)
                zz = z_refs[d][bb, :, cols].astype(F32)
                y_ref[bb, pl.ds(offs[d], CHUNK), cols] = (
                    tot * lax.rsqrt(ms + EPS) * onorm * _silu(zz)).astype(BF16)


def _delta(qkv, z, gb, gbt, onorm, w_gu, w_dn, L):
    B, LC, _ = qkv.shape
    n_all = LC // CHUNK
    n_lat = L // CHUNK
    n_ctx = n_all - n_lat

    def cf(s):
        return s

    def cb(s):
        return jnp.where(s < n_ctx, n_ctx - 1 - s, n_all + n_ctx - 1 - s)

    in_specs = [
        pl.BlockSpec((NB_DELTA, CHUNK, N_QKV), lambda b, s: (b, cf(s), 0)),
        pl.BlockSpec((NB_DELTA, CHUNK, N_QKV), lambda b, s: (b, cb(s), 0)),
        pl.BlockSpec((NB_DELTA, CHUNK, D_B), lambda b, s: (b, cf(s), 0)),
        pl.BlockSpec((NB_DELTA, CHUNK, D_B), lambda b, s: (b, cb(s), 0)),
        pl.BlockSpec((NB_DELTA, CHUNK, N_AB), lambda b, s: (b, cf(s), 0)),
        pl.BlockSpec((NB_DELTA, CHUNK, N_AB), lambda b, s: (b, cb(s), 0)),
        pl.BlockSpec((NB_DELTA, N_AB, CHUNK), lambda b, s: (b, 0, cf(s))),
        pl.BlockSpec((NB_DELTA, N_AB, CHUNK), lambda b, s: (b, 0, cb(s))),
        pl.BlockSpec((1, HD), lambda b, s: (0, 0)),
    ]
    n_steps = (B // NB_DELTA) * n_all
    n_exp, gu_rows, gu_cols = w_gu.shape
    _, dn_rows, dn_cols = w_dn.shape
    assert n_steps >= n_exp
    per_exp = 1
    while 2 * per_exp * n_exp <= n_steps and dn_rows % (16 * per_exp) == 0:
        per_exp *= 2
    n_slabs = n_exp * per_exp

    def slab(b, s):
        t = jnp.minimum(b * n_all + s, n_slabs - 1)
        return t // per_exp, t % per_exp, 0

    in_specs += [pl.BlockSpec((1, gu_rows // per_exp, gu_cols), slab),
                 pl.BlockSpec((1, dn_rows // per_exp, dn_cols), slab)]
    return pl.pallas_call(
        functools.partial(_delta_kernel, n_ctx=n_ctx, n_lat=n_lat),
        out_shape=[jax.ShapeDtypeStruct((B, L, D_B), BF16),
                   jax.ShapeDtypeStruct(w_gu.shape, BF16),
                   jax.ShapeDtypeStruct(w_dn.shape, BF16)],
        grid=(B // NB_DELTA, n_all),
        in_specs=in_specs,
        out_specs=[pl.BlockSpec((NB_DELTA, L, D_B), lambda b, s: (b, 0, 0)),
                   pl.BlockSpec((1, gu_rows // per_exp, gu_cols), slab),
                   pl.BlockSpec((1, dn_rows // per_exp, dn_cols), slab)],
        scratch_shapes=[pltpu.VMEM((NB_DELTA * 2 * HEADS, HD, HD), F32), pltpu.VMEM((NB_DELTA, L, D_B), F32)],
        compiler_params=pltpu.CompilerParams(dimension_semantics=("arbitrary", "arbitrary"),
                                             vmem_limit_bytes=VMEM_LIMIT),
    )(qkv, qkv, z, z, gb, gb, gbt, gbt, onorm, w_gu, w_dn)


def _out_kernel(x_ref, ya_ref, yb_ref, mod_ref, wo_ref, g2_ref, wrt_ref, brt_ref, tri_ref,
                x2_ref, h_ref, route_ref, cnt_ref, base_ref, wbf_ref):
    i = pl.program_id(0)

    @pl.when(i == 0)
    def _():
        base_ref[...] = jnp.zeros_like(base_ref)
        wbf_ref[...] = wo_ref[...].astype(BF16)

    mod = mod_ref[0]
    gt1 = mod[:, 2 * D:3 * D]
    sh2 = mod[:, 3 * D:4 * D]
    sc2 = mod[:, 4 * D:5 * D]
    scale2 = g2_ref[...] * (1.0 + sc2)
    sub = TM_OUT // OUT_SPLIT
    hbs = []
    for r in range(OUT_SPLIT):
        rows = slice(r * sub, (r + 1) * sub)
        mix = _dot(ya_ref[rows, :], wbf_ref[0:D_A, :]) + _dot(yb_ref[rows, :], wbf_ref[D_A:, :])
        x2 = x_ref[rows, :] + gt1 * mix
        x2_ref[rows, :] = x2
        ms = jnp.mean(x2 * x2, axis=-1, keepdims=True)
        hv = x2 * lax.rsqrt(ms + EPS) * scale2 + sh2
        hbs.append(hv.astype(BF16))
        h_ref[rows, 0:D_PACK] = _pack_bf16_pairs(hv)
    hb = jnp.concatenate(hbs, axis=0)

    lt = _dot_nt(wrt_ref[...], hb) + brt_ref[...]
    gl = [lt[r:r + 1, :] for r in range(N_GROUPS)]
    gmax = jnp.maximum(jnp.maximum(gl[0], gl[1]), jnp.maximum(gl[2], gl[3]))
    gsel = jnp.where(gl[0] == gmax, 0, jnp.where(gl[1] == gmax, 1, jnp.where(gl[2] == gmax, 2, 3)))
    p_g = 1.0 / (jnp.exp(gl[0] - gmax) + jnp.exp(gl[1] - gmax) + jnp.exp(gl[2] - gmax) + jnp.exp(gl[3] - gmax))
    el = []
    for e in range(EPG):
        r = [lt[N_GROUPS + g * EPG + e:N_GROUPS + g * EPG + e + 1, :] for g in range(N_GROUPS)]
        el.append(jnp.where(gsel == 0, r[0], jnp.where(gsel == 1, r[1], jnp.where(gsel == 2, r[2], r[3]))))
    m1 = jnp.maximum(jnp.maximum(el[0], el[1]), jnp.maximum(el[2], el[3]))
    i1 = jnp.where(el[0] == m1, 0, jnp.where(el[1] == m1, 1, jnp.where(el[2] == m1, 2, 3)))
    neg = jnp.float32(-jnp.inf)
    el2 = [jnp.where(i1 == e, neg, el[e]) for e in range(EPG)]
    m2 = jnp.maximum(jnp.maximum(el2[0], el2[1]), jnp.maximum(el2[2], el2[3]))
    i2 = jnp.where(jnp.logical_and(el2[0] == m2, i1 != 0), 0,
                   jnp.where(jnp.logical_and(el2[1] == m2, i1 != 1), 1,
                             jnp.where(jnp.logical_and(el2[2] == m2, i1 != 2), 2, 3)))
    t = jnp.exp(m2 - m1)
    w1 = p_g / (1.0 + t)
    w2 = p_g * t / (1.0 + t)
    first_low = i1 < i2
    ea = jnp.where(first_low, i1, i2)
    eb = jnp.where(first_low, i2, i1)
    w_a = jnp.where(first_low, w1, w2)
    w_b = jnp.where(first_low, w2, w1)
    pair = jnp.where(ea == 0, eb - 1, jnp.where(ea == 1, jnp.where(eb == 3, 3, 4), 5))
    swap = pair == 5
    w_a, w_b = jnp.where(swap, w_b, w_a), jnp.where(swap, w_a, w_b)
    bucket = gsel * N_PAIRS + pair

    tm = bucket.shape[1]
    rows = lax.broadcasted_iota(jnp.int32, (32, tm), 0)
    onehot = jnp.where(rows == bucket, 1.0, 0.0).astype(F32)
    prefix = _dot(onehot.astype(BF16), tri_ref[...])
    base = base_ref[:, 0:1]
    rank = jnp.sum(onehot * (prefix - 1.0 + base), axis=0, keepdims=True)
    newbase = base + prefix[:, tm - 1:tm]
    base_ref[...] = jnp.broadcast_to(newbase, base_ref.shape)
    cnt_ref[...] = jnp.broadcast_to(newbase, cnt_ref.shape)
    route = jnp.concatenate([bucket.astype(F32), rank, w_a, w_b, jnp.zeros((4, tm), F32)], axis=0)
    route_ref[0] = route
    route_t = jnp.transpose(jnp.concatenate([route, jnp.zeros((120, tm), F32)], axis=0))
    h_ref[:, D_PACK:D_ROW] = lax.bitcast_convert_type(route_t, jnp.uint32)


def _out_proj(x2d, ya, yb, mod3, w_o, g2, wrt, brt, L):
    T = x2d.shape[0]
    nb = T // TM_OUT
    per_b = L // TM_OUT
    out_shape = [
        jax.ShapeDtypeStruct((T, D), F32),
        jax.ShapeDtypeStruct((T, D_ROW), jnp.uint32),
        jax.ShapeDtypeStruct((nb, 8, TM_OUT), F32),
        jax.ShapeDtypeStruct((32, 128), F32),
    ]
    return pl.pallas_call(
        _out_kernel,
        out_shape=out_shape,
        grid=(nb,),
        in_specs=[
            pl.BlockSpec((TM_OUT, D), lambda i: (i, 0)),
            pl.BlockSpec((TM_OUT, D_A), lambda i: (i, 0)),
            pl.BlockSpec((TM_OUT, D_B), lambda i: (i, 0)),
            pl.BlockSpec((1, 1, 6 * D), lambda i: (i // per_b, 0, 0)),
            pl.BlockSpec((D, D), lambda i: (0, 0), pipeline_mode=pl.Buffered(1)),
            pl.BlockSpec((1, D), lambda i: (0, 0)),
            pl.BlockSpec((32, D), lambda i: (0, 0)),
            pl.BlockSpec((32, 1), lambda i: (0, 0)),
            pl.BlockSpec((TM_OUT, TM_OUT), lambda i: (0, 0), pipeline_mode=pl.Buffered(1)),
        ],
        out_specs=[
            pl.BlockSpec((TM_OUT, D), lambda i: (i, 0)),
            pl.BlockSpec((TM_OUT, D_ROW), lambda i: (i, 0)),
            pl.BlockSpec((1, 8, TM_OUT), lambda i: (i, 0, 0)),
            pl.BlockSpec((32, 128), lambda i: (0, 0)),
        ],
        scratch_shapes=[pltpu.VMEM((32, 128), F32), pltpu.VMEM((D, D), BF16)],
        compiler_params=pltpu.CompilerParams(dimension_semantics=("arbitrary",),
                                             vmem_limit_bytes=VMEM_LIMIT),
    )(x2d, ya, yb, mod3, w_o, g2, wrt, brt, jnp.asarray(np.triu(np.ones((TM_OUT, TM_OUT), np.float32)), BF16))


def _sc_mesh_and_split(n_rows):
    n_workers = SC_CORES * SC_SUBCORES
    per_worker = n_rows // n_workers
    n_chunks = per_worker // SC_WINDOW
    assert per_worker * n_workers == n_rows and n_chunks * SC_WINDOW == per_worker and n_chunks % 2 == 0
    mesh = plsc.VectorSubcoreMesh(core_axis_name="c", subcore_axis_name="s")
    return mesh, n_workers, per_worker, n_chunks


def _sc_two_buffer_loop(n_chunks, first, second):
    first(0, 0).start()

    @pl.loop(0, n_chunks, step=2)
    def _(j):
        for b in range(2):
            jj = j + b
            first(jj, b).wait()

            @pl.when(jj + 1 < n_chunks)
            def _():
                @pl.when(jj >= 1)
                def _():
                    second(jj - 1, 1 - b).wait()
                first(jj + 1, 1 - b).start()

            second(jj, b).start()

    second(n_chunks - 2, 0).wait()
    second(n_chunks - 1, 1).wait()


def _sc_row_gather(table, idx):
    n_rows = idx.shape[0]
    width = table.shape[1]
    mesh, n_workers, per_worker, n_chunks = _sc_mesh_and_split(n_rows)

    @functools.partial(
        pl.kernel, mesh=mesh,
        out_type=jax.ShapeDtypeStruct((n_rows, width), table.dtype),
        scratch_types=[pltpu.VMEM((n_chunks, SC_WINDOW), jnp.int32),
                       pltpu.VMEM((2, SC_WINDOW, width), table.dtype),
                       pltpu.SemaphoreType.DMA((2,)),
                       pltpu.SemaphoreType.DMA((2,))],
    )
    def gather(table_hbm, idx_hbm, out_hbm, idx_v, rows_v, sem_in, sem_out):
        wid = lax.axis_index("s") * SC_CORES + lax.axis_index("c")
        base = wid * per_worker
        pltpu.sync_copy(idx_hbm.at[wid], idx_v)

        def fetch(j, b):
            return pltpu.make_async_copy(table_hbm.at[idx_v.at[j]], rows_v.at[b], sem_in.at[b])

        def put(j, b):
            off = pl.multiple_of(base + j * SC_WINDOW, 8)
            return pltpu.make_async_copy(rows_v.at[b], out_hbm.at[pl.ds(off, SC_WINDOW)], sem_out.at[b])

        _sc_two_buffer_loop(n_chunks, fetch, put)

    return gather(table, idx.reshape(n_workers, n_chunks, SC_WINDOW))


def _sc_row_scatter(rows, pos, n_out):
    n_rows, width = rows.shape
    mesh, n_workers, per_worker, n_chunks = _sc_mesh_and_split(n_rows)

    @functools.partial(
        pl.kernel, mesh=mesh,
        out_type=jax.ShapeDtypeStruct((n_out, width), rows.dtype),
        scratch_types=[pltpu.VMEM((n_chunks, SC_WINDOW), jnp.int32),
                       pltpu.VMEM((2, SC_WINDOW, width), rows.dtype),
                       pltpu.SemaphoreType.DMA((2,)),
                       pltpu.SemaphoreType.DMA((2,))],
    )
    def scatter(rows_hbm, pos_hbm, out_hbm, idx_v, rows_v, sem_in, sem_out):
        wid = lax.axis_index("s") * SC_CORES + lax.axis_index("c")
        base = wid * per_worker
        pltpu.sync_copy(pos_hbm.at[wid], idx_v)

        def fetch(j, b):
            off = pl.multiple_of(base + j * SC_WINDOW, 8)
            return pltpu.make_async_copy(rows_hbm.at[pl.ds(off, SC_WINDOW)], rows_v.at[b], sem_in.at[b])

        def put(j, b):
            return pltpu.make_async_copy(rows_v.at[b], out_hbm.at[idx_v.at[j]], sem_out.at[b])

        _sc_two_buffer_loop(n_chunks, fetch, put)

    return scatter(rows, pos.reshape(n_workers, n_chunks, SC_WINDOW))


def _moe_kernel(ea_ref, eb_ref, nv_ref, hs_ref, wga_ref, wgb_ref, wda_ref, wdb_ref, o_ref):
    i = pl.program_id(0)

    @pl.when(nv_ref[i] > 0)
    def _():
        live = lax.broadcasted_iota(jnp.int32, (BM_MOE, 1), 0) < nv_ref[i]
        xb = _unpack_bf16_pairs(jnp.where(live, hs_ref[:, 0:D_PACK], jnp.uint32(0)))
        w_ab = lax.bitcast_convert_type(hs_ref[:, D_PACK:D_ROW], F32)
        w_a = jnp.where(live, w_ab[:, 2:3], 0.0)
        w_b = jnp.where(live, w_ab[:, 3:4], 0.0)
        ga = _dot(xb, wga_ref[0])
        act_a = (_silu(ga[:, :D_EXPERT]) * ga[:, D_EXPERT:] * w_a).astype(BF16)
        gb = _dot(xb, wgb_ref[0])
        act_b = (_silu(gb[:, :D_EXPERT]) * gb[:, D_EXPERT:] * w_b).astype(BF16)
        o_ref[...] = _pack_bf16_pairs(_dot(act_a, wda_ref[0]) + _dot(act_b, wdb_ref[0]))

    @pl.when(nv_ref[i] <= 0)
    def _():
        o_ref[...] = jnp.zeros_like(o_ref)


def _moe(ea, eb, nvalid, hs, w_gu, w_dn):
    nblk = ea.shape[0]
    S = nblk * BM_MOE
    grid_spec = pltpu.PrefetchScalarGridSpec(
        num_scalar_prefetch=3,
        grid=(nblk,),
        in_specs=[
            pl.BlockSpec((BM_MOE, D_ROW), lambda i, ea, eb, nv: (i, 0)),
            pl.BlockSpec((1, D, 2 * D_EXPERT), lambda i, ea, eb, nv: (ea[i], 0, 0)),
            pl.BlockSpec((1, D, 2 * D_EXPERT), lambda i, ea, eb, nv: (eb[i], 0, 0)),
            pl.BlockSpec((1, D_EXPERT, D), lambda i, ea, eb, nv: (ea[i], 0, 0)),
            pl.BlockSpec((1, D_EXPERT, D), lambda i, ea, eb, nv: (eb[i], 0, 0)),
        ],
        out_specs=pl.BlockSpec((BM_MOE, D_PACK), lambda i, ea, eb, nv: (i, 0)),
    )
    return pl.pallas_call(
        _moe_kernel,
        out_shape=jax.ShapeDtypeStruct((S, D_PACK), jnp.uint32),
        grid_spec=grid_spec,
        compiler_params=pltpu.CompilerParams(dimension_semantics=("arbitrary",),
                                             vmem_limit_bytes=VMEM_LIMIT),
    )(ea, eb, nvalid, hs, w_gu, w_gu, w_dn, w_dn)


def _final_kernel(x2_ref, m_ref, mod_ref, fg_ref, *rest):
    o_ref = rest[-1]
    gt2 = mod_ref[0][:, 5 * D:6 * D]
    y = x2_ref[...] + gt2 * _unpack_bf16_pairs(m_ref[...]).astype(F32)
    ms = jnp.mean(y * y, axis=-1, keepdims=True)
    o_ref[...] = y * lax.rsqrt(ms + EPS) * fg_ref[...]


def _final(x2, m_part, mod3, fg, L, part, prev_out):
    T = x2.shape[0]
    steps = m_part.shape[0] // TM_FIN
    off = part * steps
    per_b = L // TM_FIN
    in_specs = [
        pl.BlockSpec((TM_FIN, D), lambda i: (i + off, 0)),
        pl.BlockSpec((TM_FIN, D_PACK), lambda i: (i, 0)),
        pl.BlockSpec((1, 1, 6 * D), lambda i: ((i + off) // per_b, 0, 0)),
        pl.BlockSpec((1, D), lambda i: (0, 0)),
    ]
    args = [x2, m_part, mod3, fg]
    aliases = {}
    if prev_out is not None:
        in_specs.append(pl.BlockSpec(memory_space=pl.ANY))
        args.append(prev_out)
        aliases = {len(args) - 1: 0}
    return pl.pallas_call(
        _final_kernel,
        out_shape=jax.ShapeDtypeStruct((T, D), F32),
        grid=(steps,),
        in_specs=in_specs,
        out_specs=pl.BlockSpec((TM_FIN, D), lambda i: (i + off, 0)),
        input_output_aliases=aliases,
        compiler_params=pltpu.CompilerParams(dimension_semantics=("arbitrary",),
                                             vmem_limit_bytes=VMEM_LIMIT),
    )(*args)


def kernel(x, c, ctx, c_ctx, w_ada, b_ada, norm1_g, w_in, ln_a_g, ln_a_b, w_spatial, b_spatial, conv_qkv, a_log,
           dt_bias, onorm_g, w_out, norm2_g, w_group, b_group, w_router, b_router, w_gate_up, w_down, final_g):
    B, L, _ = x.shape
    T = B * L
    assert w_ada.shape[0] == 1 and ctx.shape[1] == TM_IN and L % TM_OUT == 0 and T % (N_COMBINE_PARTS * TM_FIN) == 0

    cond = jnp.concatenate([c, c_ctx[None, :], jnp.zeros((7, D), F32)], axis=0)
    mod = _modulation(cond, w_ada[0], b_ada[0][None, :])
    mod_lat = mod[:B].reshape(B, 1, 6 * D)
    mod_ctx = mod[B:B + 1]

    alog = a_log[0].reshape(1, 2 * HEADS)
    dtb = dt_bias[0].reshape(1, 2 * HEADS)
    alog_row = jnp.pad(alog, ((0, 0), (0, 128 - 2 * HEADS)))
    dtb_row = jnp.pad(dtb, ((0, 0), (0, 128 - 2 * HEADS)))
    alog_col = jnp.pad(alog, ((0, 0), (0, N_AB - 2 * HEADS))).T
    dtb_col = jnp.pad(dtb, ((0, 0), (0, N_AB - 2 * HEADS))).T

    ya, qkv, z, gb, gbt = _in_proj(
        x, ctx, mod_lat, mod_ctx, norm1_g, jnp.swapaxes(w_in, 1, 2), ln_a_g, ln_a_b,
        w_spatial[0].astype(BF16), b_spatial[0].T, conv_qkv[0], alog_row, dtb_row, alog_col, dtb_col)

    yb, w_gu_bf, w_dn_bf = _delta(qkv, z, gb, gbt, onorm_g, w_gate_up[0], w_down[0], L)

    wrt = jnp.concatenate([w_group[0].T, w_router[0].T, jnp.zeros((32 - N_GROUPS - N_EXPERTS, D), F32)], axis=0)
    brt = jnp.concatenate([b_group[0], b_router[0], jnp.zeros((32 - N_GROUPS - N_EXPERTS,), F32)])[:, None]
    x2, h, route, cnt = _out_proj(x.reshape(T, D), ya.reshape(T, D_A), yb.reshape(T, D_B), mod_lat,
                                  w_out[0], norm2_g, wrt.astype(BF16), brt, L)

    bucket = route[:, 0, :].reshape(T).astype(jnp.int32)
    rank = route[:, 1, :].reshape(T).astype(jnp.int32)
    counts = cnt[:N_BUCKETS, 0].astype(jnp.int32)
    nblk_b = (counts + BM_MOE - 1) // BM_MOE
    blk_end = jnp.cumsum(nblk_b)
    blk_start = blk_end - nblk_b
    kk = jnp.arange(N_BUCKETS, dtype=jnp.int32)

    def pick(idx, table):
        return jnp.sum(jnp.where(idx[:, None] == kk[None, :], table[None, :], 0), axis=1)

    pos = pick(bucket, blk_start) * BM_MOE + rank
    n_blocks = T // BM_MOE + N_BUCKETS
    S = n_blocks * BM_MOE
    blk = jnp.arange(n_blocks, dtype=jnp.int32)
    used = blk < blk_end[-1]
    bkt = jnp.sum((jnp.minimum(blk, blk_end[-1] - 1)[:, None] >= blk_end[None, :]).astype(jnp.int32), axis=1)
    nvalid = jnp.where(used, jnp.clip(pick(bkt, counts) - (blk - pick(bkt, blk_start)) * BM_MOE, 0, BM_MOE),
                       0).astype(jnp.int32)
    ea = pick(bkt, jnp.asarray(_SLOT_A_EXPERT))
    eb = pick(bkt, jnp.asarray(_SLOT_B_EXPERT))

    hs = _sc_row_scatter(h, pos, S)
    ms = _moe(ea, eb, nvalid, hs, w_gu_bf, w_dn_bf)
    pos_parts = pos.reshape(N_COMBINE_PARTS, T // N_COMBINE_PARTS)
    m_parts = [_sc_row_gather(ms, pos_parts[q]) for q in range(N_COMBINE_PARTS)]
    out = None
    for q in range(N_COMBINE_PARTS):
        out = _final(x2, m_parts[q], mod_lat, final_g[None, :], L, q, out)
    return out.reshape(B, L, D)
```

```python
import functools

import jax
import jax.numpy as jnp
import numpy as np
from jax import lax
from jax.experimental import pallas as pl
from jax.experimental.pallas import tpu as pltpu
from jax.experimental.pallas import tpu_sc as plsc

F32 = jnp.float32
BF16 = jnp.bfloat16
EPS = 1e-6

D = 1024
D_A = 512
D_B = 512
HEADS = 4
HD = 128
CHUNK = 128
CONV_W = 5
N_QKV = 3 * D_B
N_MAIN = 2 * D_A + 4 * D_B
N_AB = 16
N_GROUPS = 4
EPG = 4
N_EXPERTS = 16
D_EXPERT = 512
N_PAIRS = 6
N_BUCKETS = N_GROUPS * N_PAIRS
D_PACK = D // 2
D_ROW = D_PACK + 128
PAIR_A = (0, 0, 0, 1, 1, 3)
PAIR_B = (1, 2, 3, 3, 2, 2)
_SLOT_A_EXPERT = np.array([g * EPG + PAIR_A[p] for g in range(N_GROUPS) for p in range(N_PAIRS)], np.int32)
_SLOT_B_EXPERT = np.array([g * EPG + PAIR_B[p] for g in range(N_GROUPS) for p in range(N_PAIRS)], np.int32)

TM_IN = 256
HALO = 8
NB_IN = 2
NB_DELTA = 2
TM_OUT = 1024
OUT_SPLIT = 4
BM_MOE = 512
TM_FIN = 1024
N_COMBINE_PARTS = 4
VMEM_LIMIT = 56 * 1024 * 1024
SC_CORES = 2
SC_SUBCORES = 16
SC_WINDOW = 32

HI = lax.Precision.HIGHEST

_CHUNK_TRIL = np.kron(np.eye(TM_IN // CHUNK, dtype=np.float32), np.tril(np.ones((CHUNK, CHUNK), np.float32)))


def _dot(a, b, precision=None):
    return jnp.dot(a, b, preferred_element_type=F32, precision=precision)


def _dot_nt(a, b):
    return lax.dot_general(a, b, (((1,), (1,)), ((), ())), preferred_element_type=F32)


def _dot_tn(a, b):
    return lax.dot_general(a, b, (((0,), (0,)), ((), ())), preferred_element_type=F32)


def _sigmoid(x):
    return 0.5 + 0.5 * jnp.tanh(0.5 * x)


def _silu(x):
    h = 0.5 * x
    return h + h * jnp.tanh(h)


def _softplus(x):
    return jnp.maximum(x, 0.0) + jnp.log(1.0 + jnp.exp(-jnp.abs(x)))


def _pack_bf16_pairs(x):
    bits = lax.bitcast_convert_type(x.astype(BF16).astype(F32), jnp.uint32)
    return (bits[:, D_PACK:] & jnp.uint32(0xFFFF0000)) | (bits[:, :D_PACK] >> 16)


def _unpack_bf16_pairs(w):
    bits = w
    lo = lax.bitcast_convert_type(bits << 16, F32)
    hi = lax.bitcast_convert_type(bits & jnp.uint32(0xFFFF0000), F32)
    return jnp.concatenate([lo, hi], axis=1).astype(BF16)


def _gelu_tanh(x):
    return 0.5 * x * (1.0 + jnp.tanh(np.sqrt(2.0 / np.pi).astype(np.float32) * (x + 0.044715 * (x * x * x))))


def _mod_kernel(c_ref, w_ref, b_ref, o_ref):
    c = c_ref[...]
    o_ref[...] = _dot(_silu(c), w_ref[...], precision=HI) + b_ref[...]


def _modulation(cond, w_ada, b_ada):
    rows = cond.shape[0]
    tn = 1536
    return pl.pallas_call(
        _mod_kernel,
        out_shape=jax.ShapeDtypeStruct((rows, 6 * D), F32),
        grid=(6 * D // tn,),
        in_specs=[pl.BlockSpec((rows, D), lambda i: (0, 0)),
                  pl.BlockSpec((D, tn), lambda i: (0, i)),
                  pl.BlockSpec((1, tn), lambda i: (0, i))],
        out_specs=pl.BlockSpec((rows, tn), lambda i: (0, i)),
        compiler_params=pltpu.CompilerParams(dimension_semantics=("arbitrary",),
                                             vmem_limit_bytes=VMEM_LIMIT),
    )(cond, w_ada, b_ada)


def _in_kernel(x_ref, xp_ref, xn_ref, ctx_ref, mod_ref, cmod_ref, g1_ref, wt_ref,
               lng_ref, lnb_ref, ws_ref, bst_ref, conv_ref, alog_ref, dtb_ref, alogt_ref, dtbt_ref, tril_ref, triu_ref,
               ya_ref, qkv_ref, z_ref, gb_ref, gbt_ref, wbf_ref, wab_ref, wabt_ref):
    j = pl.program_id(1)

    @pl.when(jnp.logical_and(pl.program_id(0) == 0, j == 0))
    def _():
        for c0 in range(0, N_MAIN, 512):
            wbf_ref[:, c0:c0 + 512] = jnp.transpose(wt_ref[0, c0:c0 + 512, :]).astype(BF16)
        tail = jnp.concatenate([wt_ref[0, N_MAIN:N_MAIN + N_AB, :], jnp.zeros((128 - N_AB, D), F32)], axis=0)
        wabt_ref[...] = tail.astype(BF16)
        wab_ref[...] = jnp.transpose(tail).astype(BF16)

    is_ctx = j == 0
    n_lat_blocks = pl.num_programs(1) - 1

    def one_batch_element(bb):
        mod = mod_ref[bb]
        cm = cmod_ref[...]
        sh = jnp.where(is_ctx, cm[:, 0:D], mod[:, 0:D])
        sc = jnp.where(is_ctx, cm[:, D:2 * D], mod[:, D:2 * D])
        scale = g1_ref[...] * (1.0 + sc)

        xmain = jnp.where(is_ctx, ctx_ref[bb], x_ref[bb])
        xv = jnp.concatenate([xp_ref[bb], xmain, xn_ref[bb]], axis=0)
        xnorm = xv * lax.rsqrt(jnp.mean(xv * xv, axis=-1, keepdims=True) + EPS) * scale + sh
        xe = xnorm.astype(BF16)
        xb = xnorm[HALO:HALO + TM_IN].astype(BF16)

        rid = lax.broadcasted_iota(jnp.int32, (TM_IN + 2 * HALO, 1), 0)
        prev_ok = j >= 2
        next_ok = jnp.logical_and(j >= 1, j < n_lat_blocks)
        valid = jnp.logical_or(jnp.logical_and(rid >= HALO, rid < HALO + TM_IN),
                               jnp.logical_or(jnp.logical_and(rid < HALO, prev_ok),
                                              jnp.logical_and(rid >= HALO + TM_IN, next_ok)))
        pad = (CONV_W - 1) // 2
        c_qkv = 2 * D_A

        def proj(c0, width, halo=True):
            return _dot(xe if halo else xb, wbf_ref[:, c0:c0 + width])

        def conv_act(pq, c0):
            groups = (TM_IN + 2 * HALO) // 8
            x3 = jnp.where(valid, pq, 0.0).reshape(groups, 8, D_B)
            sub = lax.broadcasted_iota(jnp.int32, (1, 8, 1), 1)
            lo, hi = HALO // 8, HALO // 8 + TM_IN // 8
            acc = conv_ref[pad:pad + 1, c0:c0 + D_B] * x3[lo:hi]
            for t in range(CONV_W):
                s = t - pad
                if s == 0:
                    continue
                r = pltpu.roll(x3, (-s) % 8, axis=1)
                if s > 0:
                    sh = jnp.where(sub < 8 - s, r[lo:hi], r[lo + 1:hi + 1])
                else:
                    sh = jnp.where(sub >= -s, r[lo:hi], r[lo - 1:hi - 1])
                acc = acc + conv_ref[t:t + 1, c0:c0 + D_B] * sh
            return _silu(acc.reshape(TM_IN, D_B))

        def store_unit_heads(act, c0, gain):
            for h in range(HEADS):
                t = act[:, h * HD:(h + 1) * HD]
                nrm = lax.rsqrt(jnp.sum(t * t, axis=-1, keepdims=True) + EPS) * gain
                qkv_ref[bb, :, c0 + h * HD:c0 + (h + 1) * HD] = (t * nrm).astype(BF16)

        pq_q = proj(c_qkv, D_B)
        pq_k = proj(c_qkv + D_B, D_B)
        store_unit_heads(conv_act(pq_q, 0), 0, HD ** -0.5)
        pq_v = proj(c_qkv + 2 * D_B, D_B)
        store_unit_heads(conv_act(pq_k, D_B), D_B, 1.0)
        pa_u = proj(0, D_A, halo=False)
        qkv_ref[bb, :, 2 * D_B:] = conv_act(pq_v, 2 * D_B).astype(BF16)
        pa_v = proj(D_A, D_A, halo=False)
        u = _gelu_tanh(pa_u)
        pz = proj(c_qkv + N_QKV, D_B, halo=False)
        v = _gelu_tanh(pa_v)
        mu = jnp.mean(v, axis=-1, keepdims=True)
        vc = v - mu
        var = jnp.mean(vc * vc, axis=-1, keepdims=True)
        vn = (vc * lax.rsqrt(var + EPS) * lng_ref[...] + lnb_ref[...]).astype(BF16)
        z_ref[bb] = pz.astype(BF16)

        bst = bst_ref[...]
        for n in range(TM_IN // CHUNK):
            rows = slice(n * CHUNK, (n + 1) * CHUNK)
            for h in range(HEADS):
                cols = slice(h * HD, (h + 1) * HD)
                s = _dot(ws_ref[h], vn[rows, cols]) + bst[:, h:h + 1]
                ya_ref[bb, rows, cols] = (u[rows, cols] * s).astype(BF16)

        tri_l = tril_ref[...]
        tri_u = triu_ref[...]

        def split3(g):
            hi = g.astype(BF16)
            r1 = g - hi.astype(F32)
            mid = r1.astype(BF16)
            return hi, mid, (r1 - mid.astype(F32)).astype(BF16)

        ab = _dot(xb, wab_ref[...])
        g3 = split3(-jnp.exp(alog_ref[...]) * _softplus(ab + dtb_ref[...]))
        lane = lax.broadcasted_iota(jnp.int32, ab.shape, 1)
        gb = jnp.where(lane < HEADS, _dot(tri_l, g3[0]) + _dot(tri_l, g3[1]) + _dot(tri_l, g3[2]),
                       jnp.where(lane < 2 * HEADS, _dot(tri_u, g3[0]) + _dot(tri_u, g3[1]) + _dot(tri_u, g3[2]),
                                 _sigmoid(ab)))
        gb_ref[bb] = gb[:, 0:N_AB]

        abt = _dot_nt(wabt_ref[0:N_AB, :], xb)
        t3 = split3(-jnp.exp(alogt_ref[...]) * _softplus(abt + dtbt_ref[...]))
        row = lax.broadcasted_iota(jnp.int32, abt.shape, 0)
        gbt_ref[bb] = jnp.where(row < HEADS, _dot(t3[0], tri_u) + _dot(t3[1], tri_u) + _dot(t3[2], tri_u),
                               jnp.where(row < 2 * HEADS, _dot(t3[0], tri_l) + _dot(t3[1], tri_l) + _dot(t3[2], tri_l),
                                         _sigmoid(abt)))

    for bb in range(x_ref.shape[0]):
        one_batch_element(bb)


def _in_proj(x, ctx, mod_lat, mod_ctx, g1, w_in_t, lng, lnb, ws, bst, conv, alog, dtb, alogt, dtbt):
    B, L, _ = x.shape
    n_lat = L // TM_IN
    n_steps = n_lat + 1
    LC = L + TM_IN
    hb = TM_IN // HALO

    def full(shape):
        return pl.BlockSpec(shape, lambda b, j: (0,) * len(shape))

    in_specs = [
        pl.BlockSpec((NB_IN, TM_IN, D), lambda b, j: (b, jnp.maximum(j - 1, 0), 0)),
        pl.BlockSpec((NB_IN, HALO, D), lambda b, j: (b, jnp.clip((j - 1) * hb - 1, 0, L // HALO - 1), 0)),
        pl.BlockSpec((NB_IN, HALO, D), lambda b, j: (b, jnp.clip(j * hb, 0, L // HALO - 1), 0)),
        pl.BlockSpec((NB_IN, TM_IN, D), lambda b, j: (b, 0, 0)),
        pl.BlockSpec((NB_IN, 1, 6 * D), lambda b, j: (b, 0, 0)),
        full((1, 6 * D)), full((1, D)),
        pl.BlockSpec((1, N_MAIN + N_AB, D), lambda b, j: (0, 0, 0), pipeline_mode=pl.Buffered(1)),
        full((1, D_A)), full((1, D_A)), full((HEADS, CHUNK, CHUNK)), full((CHUNK, HEADS)),
        full((CONV_W, N_QKV)), full((1, 128)), full((1, 128)), full((N_AB, 1)), full((N_AB, 1)),
        full((TM_IN, TM_IN)), full((TM_IN, TM_IN)),
    ]
    out_shape = [
        jax.ShapeDtypeStruct((B, L, D_A), BF16),
        jax.ShapeDtypeStruct((B, LC, N_QKV), BF16),
        jax.ShapeDtypeStruct((B, LC, D_B), BF16),
        jax.ShapeDtypeStruct((B, LC, N_AB), F32),
        jax.ShapeDtypeStruct((B, N_AB, LC), F32),
    ]
    out_specs = [
        pl.BlockSpec((NB_IN, TM_IN, D_A), lambda b, j: (b, jnp.maximum(j - 1, 0), 0)),
        pl.BlockSpec((NB_IN, TM_IN, N_QKV), lambda b, j: (b, j, 0)),
        pl.BlockSpec((NB_IN, TM_IN, D_B), lambda b, j: (b, j, 0)),
        pl.BlockSpec((NB_IN, TM_IN, N_AB), lambda b, j: (b, j, 0)),
        pl.BlockSpec((NB_IN, N_AB, TM_IN), lambda b, j: (b, 0, j)),
    ]
    return pl.pallas_call(
        _in_kernel,
        out_shape=out_shape,
        grid=(B // NB_IN, n_steps),
        in_specs=in_specs,
        out_specs=out_specs,
        scratch_shapes=[pltpu.VMEM((D, N_MAIN), BF16), pltpu.VMEM((D, 128), BF16), pltpu.VMEM((128, D), BF16)],
        compiler_params=pltpu.CompilerParams(dimension_semantics=("arbitrary", "arbitrary"),
                                             vmem_limit_bytes=VMEM_LIMIT),
    )(x, x, x, ctx, mod_lat, mod_ctx, g1, w_in_t, lng, lnb, ws, bst, conv, alog, dtb, alogt, dtbt,
      jnp.asarray(_CHUNK_TRIL, BF16), jnp.asarray(_CHUNK_TRIL.T, BF16))


def _delta_kernel(qf_ref, qb_ref, zf_ref, zb_ref, gf_ref, gbk_ref, gtf_ref, gtb_ref, on_ref, wgu_ref, wdn_ref,
                  y_ref, wgu_bf_ref, wdn_bf_ref, s_ref, oacc_ref, *, n_ctx, n_lat):
    s = pl.program_id(1)

    wgu_bf_ref[...] = wgu_ref[...].astype(BF16)
    wdn_bf_ref[...] = wdn_ref[...].astype(BF16)

    @pl.when(s == 0)
    def _():
        s_ref[...] = jnp.zeros_like(s_ref)
        oacc_ref[...] = jnp.zeros_like(oacc_ref)

    row = lax.broadcasted_iota(jnp.int32, (CHUNK, CHUNK), 0)
    col = lax.broadcasted_iota(jnp.int32, (CHUNK, CHUNK), 1)
    low = row > col
    upp = row < col
    same_blk = (row // 16) == (col // 16)
    eye = jnp.where(row == col, 1.0, 0.0).astype(BF16)
    zero = jnp.zeros((CHUNK, CHUNK), BF16)
    onorm = on_ref[...]
    half = n_ctx + n_lat // 2
    second = s >= half
    g_refs = (gf_ref, gbk_ref)
    gt_refs = (gtf_ref, gtb_ref)
    qkv_refs = (qf_ref, qb_ref)
    z_refs = (zf_ref, zb_ref)
    nb = qf_ref.shape[0]
    ps = range(nb * HEADS)

    def halves(xc, unit):
        xb = xc.astype(BF16)
        fill = eye if unit else zero
        return jnp.where(low, xb, fill), jnp.where(upp, xb, fill)

    def as_lhs(hv):
        return jnp.concatenate(hv, axis=1)

    def as_rhs(*hvs):
        cols_ = [jnp.concatenate(hv, axis=0) for hv in hvs]
        return cols_[0] if len(cols_) == 1 else jnp.concatenate(cols_, axis=1)

    def load(d, p, part):
        bb, h = divmod(p, HEADS)
        return qkv_refs[d][bb, :, part * D_B + h * HD:part * D_B + (h + 1) * HD]

    def gcol(d, p, base):
        bb, h = divmod(p, HEADS)
        c = base + d * HEADS + h
        return g_refs[d][bb, :, c:c + 1]

    def grow(d, p, base):
        bb, h = divmod(p, HEADS)
        r = base + d * HEADS + h
        return gt_refs[d][bb, r:r + 1, :]

    def lanes(col):
        return jnp.broadcast_to(col, (CHUNK, HD))

    q = [[load(d, p, 0) for p in ps] for d in range(2)]
    k = [[load(d, p, 1) for p in ps] for d in range(2)]
    v = [[load(d, p, 2) for p in ps] for d in range(2)]
    gcl = [[lanes(gcol(d, p, 0)) for p in ps] for d in range(2)]
    betal = [[lanes(gcol(d, p, 2 * HEADS)) for p in ps] for d in range(2)]
    gr = [[grow(d, p, 0) for p in ps] for d in range(2)]
    betar = [[grow(d, p, 2 * HEADS) for p in ps] for d in range(2)]
    glast = [[gr[0][p][:, CHUNK - 1:CHUNK] for p in ps], [gr[1][p][:, 0:1] for p in ps]]

    gram = [[_dot_nt(jnp.concatenate([q[d][p], k[d][p]], axis=0), k[d][p]) for p in ps] for d in range(2)]
    dec = [jnp.exp(jnp.where(low, gcl[0][p] - gr[0][p], jnp.where(upp, gcl[1][p] - gr[1][p], 0.0))) for p in ps]
    lc = [jnp.where(low, gram[0][p][CHUNK:] * betar[0][p], jnp.where(upp, gram[1][p][CHUNK:] * betar[1][p], 0.0))
          * dec[p] for p in ps]
    qk = [[jnp.where(upp, 0.0, gram[0][p][:CHUNK] * dec[p]).astype(BF16) for p in ps],
          [jnp.where(low, 0.0, gram[1][p][:CHUNK] * dec[p]).astype(BF16) for p in ps]]

    dg = [jnp.where(same_blk, lc[p], 0.0) for p in ps]
    ob = [lc[p] - dg[p] for p in ps]
    d1h = [halves(dg[p], False) for p in ps]
    d2 = [_dot(as_lhs(d1h[p]), as_rhs(d1h[p])) for p in ps]
    p0s = [-dg[p] for p in ps]
    d2h = [halves(d2[p], False) for p in ps]
    p0h = [halves(p0s[p], True) for p in ps]
    o2 = [_dot(as_lhs(d2h[p]), as_rhs(d2h[p], p0h[p])) for p in ps]
    p1s = [p0s[p] + o2[p][:, CHUNK:] for p in ps]
    d4h = [halves(o2[p][:, :CHUNK], False) for p in ps]
    p1h = [halves(p1s[p], True) for p in ps]
    o3 = [_dot(as_lhs(d4h[p]), as_rhs(d4h[p], p1h[p])) for p in ps]
    p2s = [p1s[p] + o3[p][:, CHUNK:] for p in ps]
    d8h = [halves(o3[p][:, :CHUNK], False) for p in ps]
    p2h = [halves(p2s[p], True) for p in ps]
    p3s = [p2s[p] + _dot(as_lhs(d8h[p]), as_rhs(p2h[p])) for p in ps]
    p3h = [halves(p3s[p], True) for p in ps]
    obh = [halves(ob[p], False) for p in ps]
    n1h = [halves(_dot(as_lhs(p3h[p]), as_rhs(obh[p])), False) for p in ps]
    o6 = [_dot(as_lhs(n1h[p]), as_rhs(n1h[p], p3h[p])) for p in ps]
    r0s = [p3s[p] - o6[p][:, CHUNK:] for p in ps]
    n2h = [halves(o6[p][:, :CHUNK], False) for p in ps]
    r0h = [halves(r0s[p], True) for p in ps]
    o7 = [_dot(as_lhs(n2h[p]), as_rhs(n2h[p], r0h[p])) for p in ps]
    r1s = [r0s[p] + o7[p][:, CHUNK:] for p in ps]
    n4h = [halves(o7[p][:, :CHUNK], False) for p in ps]
    r1h = [halves(r1s[p], True) for p in ps]
    tinv = [halves(r1s[p] + _dot(as_lhs(n4h[p]), as_rhs(r1h[p])), True) for p in ps]

    offs = []
    for d in range(2):
        lat_chunk = (s - n_ctx) if d == 0 else (n_ctx + n_lat - 1 - s)
        off = pl.multiple_of(jnp.clip(lat_chunk, 0, n_lat - 1) * CHUNK, CHUNK)
        sidx = [(p // HEADS * 2 + d) * HEADS + p % HEADS for p in ps]
        egc = [jnp.exp(gcl[d][p]) for p in ps]
        kf = [k[d][p].astype(F32) for p in ps]
        rhs = [jnp.concatenate([v[d][p], (kf[p] * egc[p]).astype(BF16)], axis=1) for p in ps]
        uw = [_dot(tinv[p][d], rhs[p]) for p in ps]
        qd = [q[d][p].astype(F32) * egc[p] for p in ps]
        kd = [(kf[p] * jnp.exp(glast[d][p] - gcl[d][p])).astype(BF16) for p in ps]
        st = [s_ref[sidx[p]] for p in ps]
        a1 = [_dot(jnp.concatenate([uw[p][:, HD:] * betal[d][p], qd[p]], axis=0).astype(BF16), st[p].astype(BF16))
              for p in ps]
        vnew = [(uw[p][:, :HD] * betal[d][p] - a1[p][:CHUNK]).astype(BF16) for p in ps]
        o = [a1[p][CHUNK:] + _dot(qk[d][p], vnew[p]) for p in ps]
        for p in ps:
            s_ref[sidx[p]] = st[p] * jnp.exp(glast[d][p]) + _dot_tn(kd[p], vnew[p])
        for p in ps:
            bb, h = divmod(p, HEADS)
            cols = slice(h * HD, (h + 1) * HD)
            oacc_ref[bb, pl.ds(off, CHUNK), cols] = (
                jnp.where(second, oacc_ref[bb, pl.ds(off, CHUNK), cols], 0.0) + o[p])
        offs.append(off)

    @pl.when(second)
    def _():
        for d in range(2):
            for p in ps:
                bb, h = divmod(p, HEADS)
                cols = slice(h * HD, (h + 1) * HD)
                tot = oacc_ref[bb, pl.ds(offs[d], CHUNK), cols]
                ms = jnp.mean(tot * tot, axis=-1, keepdims=True)
                zz = z_refs[d][bb, :, cols].astype(F32)
                y_ref[bb, pl.ds(offs[d], CHUNK), cols] = (
                    tot * lax.rsqrt(ms + EPS) * onorm * _silu(zz)).astype(BF16)


def _delta(qkv, z, gb, gbt, onorm, w_gu, w_dn, L):
    B, LC, _ = qkv.shape
    n_all = LC // CHUNK
    n_lat = L // CHUNK
    n_ctx = n_all - n_lat

    def cf(s):
        return s

    def cb(s):
        return jnp.where(s < n_ctx, n_ctx - 1 - s, n_all + n_ctx - 1 - s)

    in_specs = [
        pl.BlockSpec((NB_DELTA, CHUNK, N_QKV), lambda b, s: (b, cf(s), 0)),
        pl.BlockSpec((NB_DELTA, CHUNK, N_QKV), lambda b, s: (b, cb(s), 0)),
        pl.BlockSpec((NB_DELTA, CHUNK, D_B), lambda b, s: (b, cf(s), 0)),
        pl.BlockSpec((NB_DELTA, CHUNK, D_B), lambda b, s: (b, cb(s), 0)),
        pl.BlockSpec((NB_DELTA, CHUNK, N_AB), lambda b, s: (b, cf(s), 0)),
        pl.BlockSpec((NB_DELTA, CHUNK, N_AB), lambda b, s: (b, cb(s), 0)),
        pl.BlockSpec((NB_DELTA, N_AB, CHUNK), lambda b, s: (b, 0, cf(s))),
        pl.BlockSpec((NB_DELTA, N_AB, CHUNK), lambda b, s: (b, 0, cb(s))),
        pl.BlockSpec((1, HD), lambda b, s: (0, 0)),
    ]
    n_steps = (B // NB_DELTA) * n_all
    n_exp, gu_rows, gu_cols = w_gu.shape
    _, dn_rows, dn_cols = w_dn.shape
    assert n_steps >= n_exp
    per_exp = 1
    while 2 * per_exp * n_exp <= n_steps and dn_rows % (16 * per_exp) == 0:
        per_exp *= 2
    n_slabs = n_exp * per_exp

    def slab(b, s):
        t = jnp.minimum(b * n_all + s, n_slabs - 1)
        return t // per_exp, t % per_exp, 0

    in_specs += [pl.BlockSpec((1, gu_rows // per_exp, gu_cols), slab),
                 pl.BlockSpec((1, dn_rows // per_exp, dn_cols), slab)]
    return pl.pallas_call(
        functools.partial(_delta_kernel, n_ctx=n_ctx, n_lat=n_lat),
        out_shape=[jax.ShapeDtypeStruct((B, L, D_B), BF16),
                   jax.ShapeDtypeStruct(w_gu.shape, BF16),
                   jax.ShapeDtypeStruct(w_dn.shape, BF16)],
        grid=(B // NB_DELTA, n_all),
        in_specs=in_specs,
        out_specs=[pl.BlockSpec((NB_DELTA, L, D_B), lambda b, s: (b, 0, 0)),
                   pl.BlockSpec((1, gu_rows // per_exp, gu_cols), slab),
                   pl.BlockSpec((1, dn_rows // per_exp, dn_cols), slab)],
        scratch_shapes=[pltpu.VMEM((NB_DELTA * 2 * HEADS, HD, HD), F32), pltpu.VMEM((NB_DELTA, L, D_B), F32)],
        compiler_params=pltpu.CompilerParams(dimension_semantics=("arbitrary", "arbitrary"),
                                             vmem_limit_bytes=VMEM_LIMIT),
    )(qkv, qkv, z, z, gb, gb, gbt, gbt, onorm, w_gu, w_dn)


def _out_kernel(x_ref, ya_ref, yb_ref, mod_ref, wo_ref, g2_ref, wrt_ref, brt_ref, tri_ref,
                x2_ref, h_ref, route_ref, cnt_ref, base_ref, wbf_ref):
    i = pl.program_id(0)

    @pl.when(i == 0)
    def _():
        base_ref[...] = jnp.zeros_like(base_ref)
        wbf_ref[...] = wo_ref[...].astype(BF16)

    mod = mod_ref[0]
    gt1 = mod[:, 2 * D:3 * D]
    sh2 = mod[:, 3 * D:4 * D]
    sc2 = mod[:, 4 * D:5 * D]
    scale2 = g2_ref[...] * (1.0 + sc2)
    sub = TM_OUT // OUT_SPLIT
    hbs = []
    for r in range(OUT_SPLIT):
        rows = slice(r * sub, (r + 1) * sub)
        mix = _dot(ya_ref[rows, :], wbf_ref[0:D_A, :]) + _dot(yb_ref[rows, :], wbf_ref[D_A:, :])
        x2 = x_ref[rows, :] + gt1 * mix
        x2_ref[rows, :] = x2
        ms = jnp.mean(x2 * x2, axis=-1, keepdims=True)
        hv = x2 * lax.rsqrt(ms + EPS) * scale2 + sh2
        hbs.append(hv.astype(BF16))
        h_ref[rows, 0:D_PACK] = _pack_bf16_pairs(hv)
    hb = jnp.concatenate(hbs, axis=0)

    lt = _dot_nt(wrt_ref[...], hb) + brt_ref[...]
    gl = [lt[r:r + 1, :] for r in range(N_GROUPS)]
    gmax = jnp.maximum(jnp.maximum(gl[0], gl[1]), jnp.maximum(gl[2], gl[3]))
    gsel = jnp.where(gl[0] == gmax, 0, jnp.where(gl[1] == gmax, 1, jnp.where(gl[2] == gmax, 2, 3)))
    p_g = 1.0 / (jnp.exp(gl[0] - gmax) + jnp.exp(gl[1] - gmax) + jnp.exp(gl[2] - gmax) + jnp.exp(gl[3] - gmax))
    el = []
    for e in range(EPG):
        r = [lt[N_GROUPS + g * EPG + e:N_GROUPS + g * EPG + e + 1, :] for g in range(N_GROUPS)]
        el.append(jnp.where(gsel == 0, r[0], jnp.where(gsel == 1, r[1], jnp.where(gsel == 2, r[2], r[3]))))
    m1 = jnp.maximum(jnp.maximum(el[0], el[1]), jnp.maximum(el[2], el[3]))
    i1 = jnp.where(el[0] == m1, 0, jnp.where(el[1] == m1, 1, jnp.where(el[2] == m1, 2, 3)))
    neg = jnp.float32(-jnp.inf)
    el2 = [jnp.where(i1 == e, neg, el[e]) for e in range(EPG)]
    m2 = jnp.maximum(jnp.maximum(el2[0], el2[1]), jnp.maximum(el2[2], el2[3]))
    i2 = jnp.where(jnp.logical_and(el2[0] == m2, i1 != 0), 0,
                   jnp.where(jnp.logical_and(el2[1] == m2, i1 != 1), 1,
                             jnp.where(jnp.logical_and(el2[2] == m2, i1 != 2), 2, 3)))
    t = jnp.exp(m2 - m1)
    w1 = p_g / (1.0 + t)
    w2 = p_g * t / (1.0 + t)
    first_low = i1 < i2
    ea = jnp.where(first_low, i1, i2)
    eb = jnp.where(first_low, i2, i1)
    w_a = jnp.where(first_low, w1, w2)
    w_b = jnp.where(first_low, w2, w1)
    pair = jnp.where(ea == 0, eb - 1, jnp.where(ea == 1, jnp.where(eb == 3, 3, 4), 5))
    swap = pair == 5
    w_a, w_b = jnp.where(swap, w_b, w_a), jnp.where(swap, w_a, w_b)
    bucket = gsel * N_PAIRS + pair

    tm = bucket.shape[1]
    rows = lax.broadcasted_iota(jnp.int32, (32, tm), 0)
    onehot = jnp.where(rows == bucket, 1.0, 0.0).astype(F32)
    prefix = _dot(onehot.astype(BF16), tri_ref[...])
    base = base_ref[:, 0:1]
    rank = jnp.sum(onehot * (prefix - 1.0 + base), axis=0, keepdims=True)
    newbase = base + prefix[:, tm - 1:tm]
    base_ref[...] = jnp.broadcast_to(newbase, base_ref.shape)
    cnt_ref[...] = jnp.broadcast_to(newbase, cnt_ref.shape)
    route = jnp.concatenate([bucket.astype(F32), rank, w_a, w_b, jnp.zeros((4, tm), F32)], axis=0)
    route_ref[0] = route
    route_t = jnp.transpose(jnp.concatenate([route, jnp.zeros((120, tm), F32)], axis=0))
    h_ref[:, D_PACK:D_ROW] = lax.bitcast_convert_type(route_t, jnp.uint32)


def _out_proj(x2d, ya, yb, mod3, w_o, g2, wrt, brt, L):
    T = x2d.shape[0]
    nb = T // TM_OUT
    per_b = L // TM_OUT
    out_shape = [
        jax.ShapeDtypeStruct((T, D), F32),
        jax.ShapeDtypeStruct((T, D_ROW), jnp.uint32),
        jax.ShapeDtypeStruct((nb, 8, TM_OUT), F32),
        jax.ShapeDtypeStruct((32, 128), F32),
    ]
    return pl.pallas_call(
        _out_kernel,
        out_shape=out_shape,
        grid=(nb,),
        in_specs=[
            pl.BlockSpec((TM_OUT, D), lambda i: (i, 0)),
            pl.BlockSpec((TM_OUT, D_A), lambda i: (i, 0)),
            pl.BlockSpec((TM_OUT, D_B), lambda i: (i, 0)),
            pl.BlockSpec((1, 1, 6 * D), lambda i: (i // per_b, 0, 0)),
            pl.BlockSpec((D, D), lambda i: (0, 0), pipeline_mode=pl.Buffered(1)),
            pl.BlockSpec((1, D), lambda i: (0, 0)),
            pl.BlockSpec((32, D), lambda i: (0, 0)),
            pl.BlockSpec((32, 1), lambda i: (0, 0)),
            pl.BlockSpec((TM_OUT, TM_OUT), lambda i: (0, 0), pipeline_mode=pl.Buffered(1)),
        ],
        out_specs=[
            pl.BlockSpec((TM_OUT, D), lambda i: (i, 0)),
            pl.BlockSpec((TM_OUT, D_ROW), lambda i: (i, 0)),
            pl.BlockSpec((1, 8, TM_OUT), lambda i: (i, 0, 0)),
            pl.BlockSpec((32, 128), lambda i: (0, 0)),
        ],
        scratch_shapes=[pltpu.VMEM((32, 128), F32), pltpu.VMEM((D, D), BF16)],
        compiler_params=pltpu.CompilerParams(dimension_semantics=("arbitrary",),
                                             vmem_limit_bytes=VMEM_LIMIT),
    )(x2d, ya, yb, mod3, w_o, g2, wrt, brt, jnp.asarray(np.triu(np.ones((TM_OUT, TM_OUT), np.float32)), BF16))


def _sc_mesh_and_split(n_rows):
    n_workers = SC_CORES * SC_SUBCORES
    per_worker = n_rows // n_workers
    n_chunks = per_worker // SC_WINDOW
    assert per_worker * n_workers == n_rows and n_chunks * SC_WINDOW == per_worker and n_chunks % 2 == 0
    mesh = plsc.VectorSubcoreMesh(core_axis_name="c", subcore_axis_name="s")
    return mesh, n_workers, per_worker, n_chunks


def _sc_two_buffer_loop(n_chunks, first, second):
    first(0, 0).start()

    @pl.loop(0, n_chunks, step=2)
    def _(j):
        for b in range(2):
            jj = j + b
            first(jj, b).wait()

            @pl.when(jj + 1 < n_chunks)
            def _():
                @pl.when(jj >= 1)
                def _():
                    second(jj - 1, 1 - b).wait()
                first(jj + 1, 1 - b).start()

            second(jj, b).start()

    second(n_chunks - 2, 0).wait()
    second(n_chunks - 1, 1).wait()


def _sc_row_gather(table, idx):
    n_rows = idx.shape[0]
    width = table.shape[1]
    mesh, n_workers, per_worker, n_chunks = _sc_mesh_and_split(n_rows)

    @functools.partial(
        pl.kernel, mesh=mesh,
        out_type=jax.ShapeDtypeStruct((n_rows, width), table.dtype),
        scratch_types=[pltpu.VMEM((n_chunks, SC_WINDOW), jnp.int32),
                       pltpu.VMEM((2, SC_WINDOW, width), table.dtype),
                       pltpu.SemaphoreType.DMA((2,)),
                       pltpu.SemaphoreType.DMA((2,))],
    )
    def gather(table_hbm, idx_hbm, out_hbm, idx_v, rows_v, sem_in, sem_out):
        wid = lax.axis_index("s") * SC_CORES + lax.axis_index("c")
        base = wid * per_worker
        pltpu.sync_copy(idx_hbm.at[wid], idx_v)

        def fetch(j, b):
            return pltpu.make_async_copy(table_hbm.at[idx_v.at[j]], rows_v.at[b], sem_in.at[b])

        def put(j, b):
            off = pl.multiple_of(base + j * SC_WINDOW, 8)
            return pltpu.make_async_copy(rows_v.at[b], out_hbm.at[pl.ds(off, SC_WINDOW)], sem_out.at[b])

        _sc_two_buffer_loop(n_chunks, fetch, put)

    return gather(table, idx.reshape(n_workers, n_chunks, SC_WINDOW))


def _sc_row_scatter(rows, pos, n_out):
    n_rows, width = rows.shape
    mesh, n_workers, per_worker, n_chunks = _sc_mesh_and_split(n_rows)

    @functools.partial(
        pl.kernel, mesh=mesh,
        out_type=jax.ShapeDtypeStruct((n_out, width), rows.dtype),
        scratch_types=[pltpu.VMEM((n_chunks, SC_WINDOW), jnp.int32),
                       pltpu.VMEM((2, SC_WINDOW, width), rows.dtype),
                       pltpu.SemaphoreType.DMA((2,)),
                       pltpu.SemaphoreType.DMA((2,))],
    )
    def scatter(rows_hbm, pos_hbm, out_hbm, idx_v, rows_v, sem_in, sem_out):
        wid = lax.axis_index("s") * SC_CORES + lax.axis_index("c")
        base = wid * per_worker
        pltpu.sync_copy(pos_hbm.at[wid], idx_v)

        def fetch(j, b):
            off = pl.multiple_of(base + j * SC_WINDOW, 8)
            return pltpu.make_async_copy(rows_hbm.at[pl.ds(off, SC_WINDOW)], rows_v.at[b], sem_in.at[b])

        def put(j, b):
            return pltpu.make_async_copy(rows_v.at[b], out_hbm.at[idx_v.at[j]], sem_out.at[b])

        _sc_two_buffer_loop(n_chunks, fetch, put)

    return scatter(rows, pos.reshape(n_workers, n_chunks, SC_WINDOW))


def _moe_kernel(ea_ref, eb_ref, nv_ref, hs_ref, wga_ref, wgb_ref, wda_ref, wdb_ref, o_ref):
    i = pl.program_id(0)
    nv = nv_ref[i]
    half = BM_MOE // 2

    def experts(n_rows):
        live = lax.broadcasted_iota(jnp.int32, (n_rows, 1), 0) < nv
        xb = _unpack_bf16_pairs(jnp.where(live, hs_ref[0:n_rows, 0:D_PACK], jnp.uint32(0)))
        w_ab = lax.bitcast_convert_type(hs_ref[0:n_rows, D_PACK:D_ROW], F32)
        w_a = jnp.where(live, w_ab[:, 2:3], 0.0)
        w_b = jnp.where(live, w_ab[:, 3:4], 0.0)
        ga = _dot(xb, wga_ref[0])
        act_a = (_silu(ga[:, :D_EXPERT]) * ga[:, D_EXPERT:] * w_a).astype(BF16)
        gb = _dot(xb, wgb_ref[0])
        act_b = (_silu(gb[:, :D_EXPERT]) * gb[:, D_EXPERT:] * w_b).astype(BF16)
        return _pack_bf16_pairs(_dot(act_a, wda_ref[0]) + _dot(act_b, wdb_ref[0]))

    @pl.when(nv > half)
    def _():
        o_ref[...] = experts(BM_MOE)

    @pl.when(jnp.logical_and(nv > 0, nv <= half))
    def _():
        o_ref[0:half, :] = experts(half)
        o_ref[half:, :] = jnp.zeros((BM_MOE - half, D_PACK), jnp.uint32)

    @pl.when(nv <= 0)
    def _():
        o_ref[...] = jnp.zeros_like(o_ref)


def _moe(ea, eb, nvalid, hs, w_gu, w_dn):
    nblk = ea.shape[0]
    S = nblk * BM_MOE
    grid_spec = pltpu.PrefetchScalarGridSpec(
        num_scalar_prefetch=3,
        grid=(nblk,),
        in_specs=[
            pl.BlockSpec((BM_MOE, D_ROW), lambda i, ea, eb, nv: (i, 0)),
            pl.BlockSpec((1, D, 2 * D_EXPERT), lambda i, ea, eb, nv: (ea[i], 0, 0)),
            pl.BlockSpec((1, D, 2 * D_EXPERT), lambda i, ea, eb, nv: (eb[i], 0, 0)),
            pl.BlockSpec((1, D_EXPERT, D), lambda i, ea, eb, nv: (ea[i], 0, 0)),
            pl.BlockSpec((1, D_EXPERT, D), lambda i, ea, eb, nv: (eb[i], 0, 0)),
        ],
        out_specs=pl.BlockSpec((BM_MOE, D_PACK), lambda i, ea, eb, nv: (i, 0)),
    )
    return pl.pallas_call(
        _moe_kernel,
        out_shape=jax.ShapeDtypeStruct((S, D_PACK), jnp.uint32),
        grid_spec=grid_spec,
        compiler_params=pltpu.CompilerParams(dimension_semantics=("arbitrary",),
                                             vmem_limit_bytes=VMEM_LIMIT),
    )(ea, eb, nvalid, hs, w_gu, w_gu, w_dn, w_dn)


def _final_kernel(x2_ref, m_ref, mod_ref, fg_ref, *rest):
    o_ref = rest[-1]
    gt2 = mod_ref[0][:, 5 * D:6 * D]
    y = x2_ref[...] + gt2 * _unpack_bf16_pairs(m_ref[...]).astype(F32)
    ms = jnp.mean(y * y, axis=-1, keepdims=True)
    o_ref[...] = y * lax.rsqrt(ms + EPS) * fg_ref[...]


def _final(x2, m_part, mod3, fg, L, part, prev_out):
    T = x2.shape[0]
    steps = m_part.shape[0] // TM_FIN
    off = part * steps
    per_b = L // TM_FIN
    in_specs = [
        pl.BlockSpec((TM_FIN, D), lambda i: (i + off, 0)),
        pl.BlockSpec((TM_FIN, D_PACK), lambda i: (i, 0)),
        pl.BlockSpec((1, 1, 6 * D), lambda i: ((i + off) // per_b, 0, 0)),
        pl.BlockSpec((1, D), lambda i: (0, 0)),
    ]
    args = [x2, m_part, mod3, fg]
    aliases = {}
    if prev_out is not None:
        in_specs.append(pl.BlockSpec(memory_space=pl.ANY))
        args.append(prev_out)
        aliases = {len(args) - 1: 0}
    return pl.pallas_call(
        _final_kernel,
        out_shape=jax.ShapeDtypeStruct((T, D), F32),
        grid=(steps,),
        in_specs=in_specs,
        out_specs=pl.BlockSpec((TM_FIN, D), lambda i: (i + off, 0)),
        input_output_aliases=aliases,
        compiler_params=pltpu.CompilerParams(dimension_semantics=("arbitrary",),
                                             vmem_limit_bytes=VMEM_LIMIT),
    )(*args)


def kernel(x, c, ctx, c_ctx, w_ada, b_ada, norm1_g, w_in, ln_a_g, ln_a_b, w_spatial, b_spatial, conv_qkv, a_log,
           dt_bias, onorm_g, w_out, norm2_g, w_group, b_group, w_router, b_router, w_gate_up, w_down, final_g):
    B, L, _ = x.shape
    T = B * L
    assert w_ada.shape[0] == 1 and ctx.shape[1] == TM_IN and L % TM_OUT == 0 and T % (N_COMBINE_PARTS * TM_FIN) == 0

    cond = jnp.concatenate([c, c_ctx[None, :], jnp.zeros((7, D), F32)], axis=0)
    mod = _modulation(cond, w_ada[0], b_ada[0][None, :])
    mod_lat = mod[:B].reshape(B, 1, 6 * D)
    mod_ctx = mod[B:B + 1]

    alog = a_log[0].reshape(1, 2 * HEADS)
    dtb = dt_bias[0].reshape(1, 2 * HEADS)
    alog_row = jnp.pad(alog, ((0, 0), (0, 128 - 2 * HEADS)))
    dtb_row = jnp.pad(dtb, ((0, 0), (0, 128 - 2 * HEADS)))
    alog_col = jnp.pad(alog, ((0, 0), (0, N_AB - 2 * HEADS))).T
    dtb_col = jnp.pad(dtb, ((0, 0), (0, N_AB - 2 * HEADS))).T

    ya, qkv, z, gb, gbt = _in_proj(
        x, ctx, mod_lat, mod_ctx, norm1_g, jnp.swapaxes(w_in, 1, 2), ln_a_g, ln_a_b,
        w_spatial[0].astype(BF16), b_spatial[0].T, conv_qkv[0], alog_row, dtb_row, alog_col, dtb_col)

    yb, w_gu_bf, w_dn_bf = _delta(qkv, z, gb, gbt, onorm_g, w_gate_up[0], w_down[0], L)

    wrt = jnp.concatenate([w_group[0].T, w_router[0].T, jnp.zeros((32 - N_GROUPS - N_EXPERTS, D), F32)], axis=0)
    brt = jnp.concatenate([b_group[0], b_router[0], jnp.zeros((32 - N_GROUPS - N_EXPERTS,), F32)])[:, None]
    x2, h, route, cnt = _out_proj(x.reshape(T, D), ya.reshape(T, D_A), yb.reshape(T, D_B), mod_lat,
                                  w_out[0], norm2_g, wrt.astype(BF16), brt, L)

    bucket = route[:, 0, :].reshape(T).astype(jnp.int32)
    rank = route[:, 1, :].reshape(T).astype(jnp.int32)
    counts = cnt[:N_BUCKETS, 0].astype(jnp.int32)
    nblk_b = (counts + BM_MOE - 1) // BM_MOE
    blk_end = jnp.cumsum(nblk_b)
    blk_start = blk_end - nblk_b
    kk = jnp.arange(N_BUCKETS, dtype=jnp.int32)

    def pick(idx, table):
        return jnp.sum(jnp.where(idx[:, None] == kk[None, :], table[None, :], 0), axis=1)

    pos = pick(bucket, blk_start) * BM_MOE + rank
    n_blocks = T // BM_MOE + N_BUCKETS
    S = n_blocks * BM_MOE
    blk = jnp.arange(n_blocks, dtype=jnp.int32)
    used = blk < blk_end[-1]
    bkt = jnp.sum((jnp.minimum(blk, blk_end[-1] - 1)[:, None] >= blk_end[None, :]).astype(jnp.int32), axis=1)
    nvalid = jnp.where(used, jnp.clip(pick(bkt, counts) - (blk - pick(bkt, blk_start)) * BM_MOE, 0, BM_MOE),
                       0).astype(jnp.int32)
    ea = pick(bkt, jnp.asarray(_SLOT_A_EXPERT))
    eb = pick(bkt, jnp.asarray(_SLOT_B_EXPERT))

    hs = _sc_row_scatter(h, pos, S)
    ms = _moe(ea, eb, nvalid, hs, w_gu_bf, w_dn_bf)
    pos_parts = pos.reshape(N_COMBINE_PARTS, T // N_COMBINE_PARTS)
    m_parts = [_sc_row_gather(ms, pos_parts[q]) for q in range(N_COMBINE_PARTS)]
    out = None
    for q in range(N_COMBINE_PARTS):
        out = _final(x2, m_parts[q], mod_lat, final_g[None, :], L, q, out)
    return out.reshape(B, L, D)
```

```python
import functools

import jax
import jax.numpy as jnp
import numpy as np
from jax import lax
from jax.experimental import pallas as pl
from jax.experimental.pallas import tpu as pltpu
from jax.experimental.pallas import tpu_sc as plsc

F32 = jnp.float32
BF16 = jnp.bfloat16
EPS = 1e-6

D = 1024
D_A = 512
D_B = 512
HEADS = 4
HD = 128
CHUNK = 128
CONV_W = 5
N_QKV = 3 * D_B
N_MAIN = 2 * D_A + 4 * D_B
N_AB = 16
N_GROUPS = 4
EPG = 4
N_EXPERTS = 16
D_EXPERT = 512
N_PAIRS = 6
N_BUCKETS = N_GROUPS * N_PAIRS
D_PACK = D // 2
D_ROW = D_PACK + 128
PAIR_A = (0, 0, 0, 1, 1, 3)
PAIR_B = (1, 2, 3, 3, 2, 2)
_SLOT_A_EXPERT = np.array([g * EPG + PAIR_A[p] for g in range(N_GROUPS) for p in range(N_PAIRS)], np.int32)
_SLOT_B_EXPERT = np.array([g * EPG + PAIR_B[p] for g in range(N_GROUPS) for p in range(N_PAIRS)], np.int32)

TM_IN = 256
HALO = 8
NB_IN = 2
NB_DELTA = 2
TM_OUT = 1024
OUT_SPLIT = 4
BM_MOE = 512
TM_FIN = 1024
N_COMBINE_PARTS = 4
VMEM_LIMIT = 56 * 1024 * 1024
SC_CORES = 2
SC_SUBCORES = 16
SC_WINDOW = 32

HI = lax.Precision.HIGHEST

_CHUNK_TRIL = np.kron(np.eye(TM_IN // CHUNK, dtype=np.float32), np.tril(np.ones((CHUNK, CHUNK), np.float32)))


def _dot(a, b, precision=None):
    return jnp.dot(a, b, preferred_element_type=F32, precision=precision)


def _dot_nt(a, b):
    return lax.dot_general(a, b, (((1,), (1,)), ((), ())), preferred_element_type=F32)


def _dot_tn(a, b):
    return lax.dot_general(a, b, (((0,), (0,)), ((), ())), preferred_element_type=F32)


def _sigmoid(x):
    return 0.5 + 0.5 * jnp.tanh(0.5 * x)


def _silu(x):
    h = 0.5 * x
    return h + h * jnp.tanh(h)


def _softplus(x):
    return jnp.maximum(x, 0.0) + jnp.log(1.0 + jnp.exp(-jnp.abs(x)))


def _pack_bf16_pairs(x):
    bits = lax.bitcast_convert_type(x.astype(BF16).astype(F32), jnp.uint32)
    return (bits[:, D_PACK:] & jnp.uint32(0xFFFF0000)) | (bits[:, :D_PACK] >> 16)


def _unpack_bf16_pairs(w):
    bits = w
    lo = lax.bitcast_convert_type(bits << 16, F32)
    hi = lax.bitcast_convert_type(bits & jnp.uint32(0xFFFF0000), F32)
    return jnp.concatenate([lo, hi], axis=1).astype(BF16)


def _gelu_tanh(x):
    return 0.5 * x * (1.0 + jnp.tanh(np.sqrt(2.0 / np.pi).astype(np.float32) * (x + 0.044715 * (x * x * x))))


def _mod_kernel(c_ref, w_ref, b_ref, o_ref):
    c = c_ref[...]
    o_ref[...] = _dot(_silu(c), w_ref[...], precision=HI) + b_ref[...]


def _modulation(cond, w_ada, b_ada):
    rows = cond.shape[0]
    tn = 1536
    return pl.pallas_call(
        _mod_kernel,
        out_shape=jax.ShapeDtypeStruct((rows, 6 * D), F32),
        grid=(6 * D // tn,),
        in_specs=[pl.BlockSpec((rows, D), lambda i: (0, 0)),
                  pl.BlockSpec((D, tn), lambda i: (0, i)),
                  pl.BlockSpec((1, tn), lambda i: (0, i))],
        out_specs=pl.BlockSpec((rows, tn), lambda i: (0, i)),
        compiler_params=pltpu.CompilerParams(dimension_semantics=("arbitrary",),
                                             vmem_limit_bytes=VMEM_LIMIT),
    )(cond, w_ada, b_ada)


def _in_kernel(x_ref, xp_ref, xn_ref, ctx_ref, mod_ref, cmod_ref, g1_ref, wt_ref,
               lng_ref, lnb_ref, ws_ref, bst_ref, conv_ref, alog_ref, dtb_ref, alogt_ref, dtbt_ref, tril_ref, triu_ref,
               ya_ref, qkv_ref, z_ref, gb_ref, gbt_ref, wbf_ref, wab_ref, wabt_ref):
    j = pl.program_id(1)

    @pl.when(jnp.logical_and(pl.program_id(0) == 0, j == 0))
    def _():
        for c0 in range(0, N_MAIN, 512):
            wbf_ref[:, c0:c0 + 512] = jnp.transpose(wt_ref[0, c0:c0 + 512, :]).astype(BF16)
        tail = jnp.concatenate([wt_ref[0, N_MAIN:N_MAIN + N_AB, :], jnp.zeros((128 - N_AB, D), F32)], axis=0)
        wabt_ref[...] = tail.astype(BF16)
        wab_ref[...] = jnp.transpose(tail).astype(BF16)

    is_ctx = j == 0
    n_lat_blocks = pl.num_programs(1) - 1

    def one_batch_element(bb):
        mod = mod_ref[bb]
        cm = cmod_ref[...]
        sh = jnp.where(is_ctx, cm[:, 0:D], mod[:, 0:D])
        sc = jnp.where(is_ctx, cm[:, D:2 * D], mod[:, D:2 * D])
        scale = g1_ref[...] * (1.0 + sc)

        xmain = jnp.where(is_ctx, ctx_ref[bb], x_ref[bb])
        xv = jnp.concatenate([xp_ref[bb], xmain, xn_ref[bb]], axis=0)
        xnorm = xv * lax.rsqrt(jnp.mean(xv * xv, axis=-1, keepdims=True) + EPS) * scale + sh
        xe = xnorm.astype(BF16)
        xb = xnorm[HALO:HALO + TM_IN].astype(BF16)

        rid = lax.broadcasted_iota(jnp.int32, (TM_IN + 2 * HALO, 1), 0)
        prev_ok = j >= 2
        next_ok = jnp.logical_and(j >= 1, j < n_lat_blocks)
        valid = jnp.logical_or(jnp.logical_and(rid >= HALO, rid < HALO + TM_IN),
                               jnp.logical_or(jnp.logical_and(rid < HALO, prev_ok),
                                              jnp.logical_and(rid >= HALO + TM_IN, next_ok)))
        pad = (CONV_W - 1) // 2
        c_qkv = 2 * D_A

        def proj(c0, width, halo=True):
            return _dot(xe if halo else xb, wbf_ref[:, c0:c0 + width])

        def conv_act(pq, c0):
            groups = (TM_IN + 2 * HALO) // 8
            x3 = jnp.where(valid, pq, 0.0).reshape(groups, 8, D_B)
            sub = lax.broadcasted_iota(jnp.int32, (1, 8, 1), 1)
            lo, hi = HALO // 8, HALO // 8 + TM_IN // 8
            acc = conv_ref[pad:pad + 1, c0:c0 + D_B] * x3[lo:hi]
            for t in range(CONV_W):
                s = t - pad
                if s == 0:
                    continue
                r = pltpu.roll(x3, (-s) % 8, axis=1)
                if s > 0:
                    sh = jnp.where(sub < 8 - s, r[lo:hi], r[lo + 1:hi + 1])
                else:
                    sh = jnp.where(sub >= -s, r[lo:hi], r[lo - 1:hi - 1])
                acc = acc + conv_ref[t:t + 1, c0:c0 + D_B] * sh
            return _silu(acc.reshape(TM_IN, D_B))

        def store_unit_heads(act, c0, gain):
            for h in range(HEADS):
                t = act[:, h * HD:(h + 1) * HD]
                nrm = lax.rsqrt(jnp.sum(t * t, axis=-1, keepdims=True) + EPS) * gain
                qkv_ref[bb, :, c0 + h * HD:c0 + (h + 1) * HD] = (t * nrm).astype(BF16)

        pq_q = proj(c_qkv, D_B)
        pq_k = proj(c_qkv + D_B, D_B)
        store_unit_heads(conv_act(pq_q, 0), 0, HD ** -0.5)
        pq_v = proj(c_qkv + 2 * D_B, D_B)
        store_unit_heads(conv_act(pq_k, D_B), D_B, 1.0)
        pa_u = proj(0, D_A, halo=False)
        qkv_ref[bb, :, 2 * D_B:] = conv_act(pq_v, 2 * D_B).astype(BF16)
        pa_v = proj(D_A, D_A, halo=False)
        u = _gelu_tanh(pa_u)
        pz = proj(c_qkv + N_QKV, D_B, halo=False)
        v = _gelu_tanh(pa_v)
        mu = jnp.mean(v, axis=-1, keepdims=True)
        vc = v - mu
        var = jnp.mean(vc * vc, axis=-1, keepdims=True)
        vn = (vc * lax.rsqrt(var + EPS) * lng_ref[...] + lnb_ref[...]).astype(BF16)
        z_ref[bb] = pz.astype(BF16)

        bst = bst_ref[...]
        for n in range(TM_IN // CHUNK):
            rows = slice(n * CHUNK, (n + 1) * CHUNK)
            for h in range(HEADS):
                cols = slice(h * HD, (h + 1) * HD)
                s = _dot(ws_ref[h], vn[rows, cols]) + bst[:, h:h + 1]
                ya_ref[bb, rows, cols] = (u[rows, cols] * s).astype(BF16)

        tri_l = tril_ref[...]
        tri_u = triu_ref[...]

        def split3(g):
            hi = g.astype(BF16)
            r1 = g - hi.astype(F32)
            mid = r1.astype(BF16)
            return hi, mid, (r1 - mid.astype(F32)).astype(BF16)

        ab = _dot(xb, wab_ref[...])
        g3 = split3(-jnp.exp(alog_ref[...]) * _softplus(ab + dtb_ref[...]))
        lane = lax.broadcasted_iota(jnp.int32, ab.shape, 1)
        gb = jnp.where(lane < HEADS, _dot(tri_l, g3[0]) + _dot(tri_l, g3[1]) + _dot(tri_l, g3[2]),
                       jnp.where(lane < 2 * HEADS, _dot(tri_u, g3[0]) + _dot(tri_u, g3[1]) + _dot(tri_u, g3[2]),
                                 _sigmoid(ab)))
        gb_ref[bb] = gb[:, 0:N_AB]

        abt = _dot_nt(wabt_ref[0:N_AB, :], xb)
        t3 = split3(-jnp.exp(alogt_ref[...]) * _softplus(abt + dtbt_ref[...]))
        row = lax.broadcasted_iota(jnp.int32, abt.shape, 0)
        gbt_ref[bb] = jnp.where(row < HEADS, _dot(t3[0], tri_u) + _dot(t3[1], tri_u) + _dot(t3[2], tri_u),
                               jnp.where(row < 2 * HEADS, _dot(t3[0], tri_l) + _dot(t3[1], tri_l) + _dot(t3[2], tri_l),
                                         _sigmoid(abt)))

    for bb in range(x_ref.shape[0]):
        one_batch_element(bb)


def _in_proj(x, ctx, mod_lat, mod_ctx, g1, w_in_t, lng, lnb, ws, bst, conv, alog, dtb, alogt, dtbt):
    B, L, _ = x.shape
    n_lat = L // TM_IN
    n_steps = n_lat + 1
    LC = L + TM_IN
    hb = TM_IN // HALO

    def full(shape):
        return pl.BlockSpec(shape, lambda b, j: (0,) * len(shape))

    in_specs = [
        pl.BlockSpec((NB_IN, TM_IN, D), lambda b, j: (b, jnp.maximum(j - 1, 0), 0)),
        pl.BlockSpec((NB_IN, HALO, D), lambda b, j: (b, jnp.clip((j - 1) * hb - 1, 0, L // HALO - 1), 0)),
        pl.BlockSpec((NB_IN, HALO, D), lambda b, j: (b, jnp.clip(j * hb, 0, L // HALO - 1), 0)),
        pl.BlockSpec((NB_IN, TM_IN, D), lambda b, j: (b, 0, 0)),
        pl.BlockSpec((NB_IN, 1, 6 * D), lambda b, j: (b, 0, 0)),
        full((1, 6 * D)), full((1, D)),
        pl.BlockSpec((1, N_MAIN + N_AB, D), lambda b, j: (0, 0, 0), pipeline_mode=pl.Buffered(1)),
        full((1, D_A)), full((1, D_A)), full((HEADS, CHUNK, CHUNK)), full((CHUNK, HEADS)),
        full((CONV_W, N_QKV)), full((1, 128)), full((1, 128)), full((N_AB, 1)), full((N_AB, 1)),
        full((TM_IN, TM_IN)), full((TM_IN, TM_IN)),
    ]
    out_shape = [
        jax.ShapeDtypeStruct((B, L, D_A), BF16),
        jax.ShapeDtypeStruct((B, LC, N_QKV), BF16),
        jax.ShapeDtypeStruct((B, LC, D_B), BF16),
        jax.ShapeDtypeStruct((B, LC, N_AB), F32),
        jax.ShapeDtypeStruct((B, N_AB, LC), F32),
    ]
    out_specs = [
        pl.BlockSpec((NB_IN, TM_IN, D_A), lambda b, j: (b, jnp.maximum(j - 1, 0), 0)),
        pl.BlockSpec((NB_IN, TM_IN, N_QKV), lambda b, j: (b, j, 0)),
        pl.BlockSpec((NB_IN, TM_IN, D_B), lambda b, j: (b, j, 0)),
        pl.BlockSpec((NB_IN, TM_IN, N_AB), lambda b, j: (b, j, 0)),
        pl.BlockSpec((NB_IN, N_AB, TM_IN), lambda b, j: (b, 0, j)),
    ]
    return pl.pallas_call(
        _in_kernel,
        out_shape=out_shape,
        grid=(B // NB_IN, n_steps),
        in_specs=in_specs,
        out_specs=out_specs,
        scratch_shapes=[pltpu.VMEM((D, N_MAIN), BF16), pltpu.VMEM((D, 128), BF16), pltpu.VMEM((128, D), BF16)],
        compiler_params=pltpu.CompilerParams(dimension_semantics=("arbitrary", "arbitrary"),
                                             vmem_limit_bytes=VMEM_LIMIT),
    )(x, x, x, ctx, mod_lat, mod_ctx, g1, w_in_t, lng, lnb, ws, bst, conv, alog, dtb, alogt, dtbt,
      jnp.asarray(_CHUNK_TRIL, BF16), jnp.asarray(_CHUNK_TRIL.T, BF16))


def _delta_kernel(qf_ref, qb_ref, zf_ref, zb_ref, gf_ref, gbk_ref, gtf_ref, gtb_ref, on_ref, wgu_ref, wdn_ref,
                  y_ref, wgu_bf_ref, wdn_bf_ref, s_ref, oacc_ref, *, n_ctx, n_lat):
    s = pl.program_id(1)

    wgu_bf_ref[...] = wgu_ref[...].astype(BF16)
    wdn_bf_ref[...] = wdn_ref[...].astype(BF16)

    @pl.when(s == 0)
    def _():
        s_ref[...] = jnp.zeros_like(s_ref)
        oacc_ref[...] = jnp.zeros_like(oacc_ref)

    row = lax.broadcasted_iota(jnp.int32, (CHUNK, CHUNK), 0)
    col = lax.broadcasted_iota(jnp.int32, (CHUNK, CHUNK), 1)
    low = row > col
    upp = row < col
    same_blk = (row // 16) == (col // 16)
    eye = jnp.where(row == col, 1.0, 0.0).astype(BF16)
    zero = jnp.zeros((CHUNK, CHUNK), BF16)
    onorm = on_ref[...]
    half = n_ctx + n_lat // 2
    second = s >= half
    g_refs = (gf_ref, gbk_ref)
    gt_refs = (gtf_ref, gtb_ref)
    qkv_refs = (qf_ref, qb_ref)
    z_refs = (zf_ref, zb_ref)
    nb = qf_ref.shape[0]
    ps = range(nb * HEADS)

    def halves(xc, unit):
        xb = xc.astype(BF16)
        fill = eye if unit else zero
        return jnp.where(low, xb, fill), jnp.where(upp, xb, fill)

    def as_lhs(hv):
        return jnp.concatenate(hv, axis=1)

    def as_rhs(*hvs):
        cols_ = [jnp.concatenate(hv, axis=0) for hv in hvs]
        return cols_[0] if len(cols_) == 1 else jnp.concatenate(cols_, axis=1)

    def load(d, p, part):
        bb, h = divmod(p, HEADS)
        return qkv_refs[d][bb, :, part * D_B + h * HD:part * D_B + (h + 1) * HD]

    def gcol(d, p, base):
        bb, h = divmod(p, HEADS)
        c = base + d * HEADS + h
        return g_refs[d][bb, :, c:c + 1]

    def grow(d, p, base):
        bb, h = divmod(p, HEADS)
        r = base + d * HEADS + h
        return gt_refs[d][bb, r:r + 1, :]

    def lanes(col):
        return jnp.broadcast_to(col, (CHUNK, HD))

    q = [[load(d, p, 0) for p in ps] for d in range(2)]
    k = [[load(d, p, 1) for p in ps] for d in range(2)]
    v = [[load(d, p, 2) for p in ps] for d in range(2)]
    gcl = [[lanes(gcol(d, p, 0)) for p in ps] for d in range(2)]
    betal = [[lanes(gcol(d, p, 2 * HEADS)) for p in ps] for d in range(2)]
    gr = [[grow(d, p, 0) for p in ps] for d in range(2)]
    betar = [[grow(d, p, 2 * HEADS) for p in ps] for d in range(2)]
    glast = [[gr[0][p][:, CHUNK - 1:CHUNK] for p in ps], [gr[1][p][:, 0:1] for p in ps]]

    gram = [[_dot_nt(jnp.concatenate([q[d][p], k[d][p]], axis=0), k[d][p]) for p in ps] for d in range(2)]
    dec = [jnp.exp(jnp.where(low, gcl[0][p] - gr[0][p], jnp.where(upp, gcl[1][p] - gr[1][p], 0.0))) for p in ps]
    lc = [jnp.where(low, gram[0][p][CHUNK:] * betar[0][p], jnp.where(upp, gram[1][p][CHUNK:] * betar[1][p], 0.0))
          * dec[p] for p in ps]
    qk = [[jnp.where(upp, 0.0, gram[0][p][:CHUNK] * dec[p]).astype(BF16) for p in ps],
          [jnp.where(low, 0.0, gram[1][p][:CHUNK] * dec[p]).astype(BF16) for p in ps]]

    dg = [jnp.where(same_blk, lc[p], 0.0) for p in ps]
    ob = [lc[p] - dg[p] for p in ps]
    d1h = [halves(dg[p], False) for p in ps]
    d2 = [_dot(as_lhs(d1h[p]), as_rhs(d1h[p])) for p in ps]
    p0s = [-dg[p] for p in ps]
    d2h = [halves(d2[p], False) for p in ps]
    p0h = [halves(p0s[p], True) for p in ps]
    o2 = [_dot(as_lhs(d2h[p]), as_rhs(d2h[p], p0h[p])) for p in ps]
    p1s = [p0s[p] + o2[p][:, CHUNK:] for p in ps]
    d4h = [halves(o2[p][:, :CHUNK], False) for p in ps]
    p1h = [halves(p1s[p], True) for p in ps]
    o3 = [_dot(as_lhs(d4h[p]), as_rhs(d4h[p], p1h[p])) for p in ps]
    p2s = [p1s[p] + o3[p][:, CHUNK:] for p in ps]
    d8h = [halves(o3[p][:, :CHUNK], False) for p in ps]
    p2h = [halves(p2s[p], True) for p in ps]
    p3s = [p2s[p] + _dot(as_lhs(d8h[p]), as_rhs(p2h[p])) for p in ps]
    p3h = [halves(p3s[p], True) for p in ps]
    obh = [halves(ob[p], False) for p in ps]
    n1h = [halves(_dot(as_lhs(p3h[p]), as_rhs(obh[p])), False) for p in ps]
    o6 = [_dot(as_lhs(n1h[p]), as_rhs(n1h[p], p3h[p])) for p in ps]
    r0s = [p3s[p] - o6[p][:, CHUNK:] for p in ps]
    n2h = [halves(o6[p][:, :CHUNK], False) for p in ps]
    r0h = [halves(r0s[p], True) for p in ps]
    o7 = [_dot(as_lhs(n2h[p]), as_rhs(n2h[p], r0h[p])) for p in ps]
    r1s = [r0s[p] + o7[p][:, CHUNK:] for p in ps]
    n4h = [halves(o7[p][:, :CHUNK], False) for p in ps]
    r1h = [halves(r1s[p], True) for p in ps]
    tinv = [halves(r1s[p] + _dot(as_lhs(n4h[p]), as_rhs(r1h[p])), True) for p in ps]

    offs = []
    for d in range(2):
        lat_chunk = (s - n_ctx) if d == 0 else (n_ctx + n_lat - 1 - s)
        off = pl.multiple_of(jnp.clip(lat_chunk, 0, n_lat - 1) * CHUNK, CHUNK)
        sidx = [(p // HEADS * 2 + d) * HEADS + p % HEADS for p in ps]
        egc = [jnp.exp(gcl[d][p]) for p in ps]
        kf = [k[d][p].astype(F32) for p in ps]
        rhs = [jnp.concatenate([v[d][p], (kf[p] * egc[p]).astype(BF16)], axis=1) for p in ps]
        uw = [_dot(tinv[p][d], rhs[p]) for p in ps]
        qd = [q[d][p].astype(F32) * egc[p] for p in ps]
        kd = [(kf[p] * jnp.exp(glast[d][p] - gcl[d][p])).astype(BF16) for p in ps]
        st = [s_ref[sidx[p]] for p in ps]
        a1 = [_dot(jnp.concatenate([uw[p][:, HD:] * betal[d][p], qd[p]], axis=0).astype(BF16), st[p].astype(BF16))
              for p in ps]
        vnew = [(uw[p][:, :HD] * betal[d][p] - a1[p][:CHUNK]).astype(BF16) for p in ps]
        o = [a1[p][CHUNK:] + _dot(qk[d][p], vnew[p]) for p in ps]
        for p in ps:
            s_ref[sidx[p]] = st[p] * jnp.exp(glast[d][p]) + _dot_tn(kd[p], vnew[p])
        for p in ps:
            bb, h = divmod(p, HEADS)
            cols = slice(h * HD, (h + 1) * HD)
            oacc_ref[bb, pl.ds(off, CHUNK), cols] = (
                jnp.where(second, oacc_ref[bb, pl.ds(off, CHUNK), cols], 0.0) + o[p])
        offs.append(off)

    @pl.when(second)
    def _():
        for d in range(2):
            for p in ps:
                bb, h = divmod(p, HEADS)
                cols = slice(h * HD, (h + 1) * HD)
                tot = oacc_ref[bb, pl.ds(offs[d], CHUNK), cols]
                ms = jnp.mean(tot * tot, axis=-1, keepdims=True)
                zz = z_refs[d][bb, :, cols].astype(F32)
                y_ref[bb, pl.ds(offs[d], CHUNK), cols] = (
                    tot * lax.rsqrt(ms + EPS) * onorm * _silu(zz)).astype(BF16)


def _delta(qkv, z, gb, gbt, onorm, w_gu, w_dn, L):
    B, LC, _ = qkv.shape
    n_all = LC // CHUNK
    n_lat = L // CHUNK
    n_ctx = n_all - n_lat

    def cf(s):
        return s

    def cb(s):
        return jnp.where(s < n_ctx, n_ctx - 1 - s, n_all + n_ctx - 1 - s)

    in_specs = [
        pl.BlockSpec((NB_DELTA, CHUNK, N_QKV), lambda b, s: (b, cf(s), 0)),
        pl.BlockSpec((NB_DELTA, CHUNK, N_QKV), lambda b, s: (b, cb(s), 0)),
        pl.BlockSpec((NB_DELTA, CHUNK, D_B), lambda b, s: (b, cf(s), 0)),
        pl.BlockSpec((NB_DELTA, CHUNK, D_B), lambda b, s: (b, cb(s), 0)),
        pl.BlockSpec((NB_DELTA, CHUNK, N_AB), lambda b, s: (b, cf(s), 0)),
        pl.BlockSpec((NB_DELTA, CHUNK, N_AB), lambda b, s: (b, cb(s), 0)),
        pl.BlockSpec((NB_DELTA, N_AB, CHUNK), lambda b, s: (b, 0, cf(s))),
        pl.BlockSpec((NB_DELTA, N_AB, CHUNK), lambda b, s: (b, 0, cb(s))),
        pl.BlockSpec((1, HD), lambda b, s: (0, 0)),
    ]
    n_steps = (B // NB_DELTA) * n_all
    n_exp, gu_rows, gu_cols = w_gu.shape
    _, dn_rows, dn_cols = w_dn.shape
    assert n_steps >= n_exp
    per_exp = 1
    while 2 * per_exp * n_exp <= n_steps and dn_rows % (16 * per_exp) == 0:
        per_exp *= 2
    n_slabs = n_exp * per_exp

    def slab(b, s):
        t = jnp.minimum(b * n_all + s, n_slabs - 1)
        return t // per_exp, t % per_exp, 0

    in_specs += [pl.BlockSpec((1, gu_rows // per_exp, gu_cols), slab),
                 pl.BlockSpec((1, dn_rows // per_exp, dn_cols), slab)]
    return pl.pallas_call(
        functools.partial(_delta_kernel, n_ctx=n_ctx, n_lat=n_lat),
        out_shape=[jax.ShapeDtypeStruct((B, L, D_B), BF16),
                   jax.ShapeDtypeStruct(w_gu.shape, BF16),
                   jax.ShapeDtypeStruct(w_dn.shape, BF16)],
        grid=(B // NB_DELTA, n_all),
        in_specs=in_specs,
        out_specs=[pl.BlockSpec((NB_DELTA, L, D_B), lambda b, s: (b, 0, 0)),
                   pl.BlockSpec((1, gu_rows // per_exp, gu_cols), slab),
                   pl.BlockSpec((1, dn_rows // per_exp, dn_cols), slab)],
        scratch_shapes=[pltpu.VMEM((NB_DELTA * 2 * HEADS, HD, HD), F32), pltpu.VMEM((NB_DELTA, L, D_B), F32)],
        compiler_params=pltpu.CompilerParams(dimension_semantics=("arbitrary", "arbitrary"),
                                             vmem_limit_bytes=VMEM_LIMIT),
    )(qkv, qkv, z, z, gb, gb, gbt, gbt, onorm, w_gu, w_dn)


def _out_kernel(x_ref, ya_ref, yb_ref, mod_ref, wo_ref, g2_ref, wrt_ref, brt_ref, tri_ref,
                x2_ref, h_ref, route_ref, cnt_ref, base_ref, wbf_ref):
    i = pl.program_id(0)

    @pl.when(i == 0)
    def _():
        base_ref[...] = jnp.zeros_like(base_ref)
        wbf_ref[...] = wo_ref[...].astype(BF16)

    mod = mod_ref[0]
    gt1 = mod[:, 2 * D:3 * D]
    sh2 = mod[:, 3 * D:4 * D]
    sc2 = mod[:, 4 * D:5 * D]
    scale2 = g2_ref[...] * (1.0 + sc2)
    sub = TM_OUT // OUT_SPLIT
    hbs = []
    for r in range(OUT_SPLIT):
        rows = slice(r * sub, (r + 1) * sub)
        mix = _dot(ya_ref[rows, :], wbf_ref[0:D_A, :]) + _dot(yb_ref[rows, :], wbf_ref[D_A:, :])
        x2 = x_ref[rows, :] + gt1 * mix
        x2_ref[rows, :] = x2
        ms = jnp.mean(x2 * x2, axis=-1, keepdims=True)
        hv = x2 * lax.rsqrt(ms + EPS) * scale2 + sh2
        hbs.append(hv.astype(BF16))
        h_ref[rows, 0:D_PACK] = _pack_bf16_pairs(hv)
    hb = jnp.concatenate(hbs, axis=0)

    lt = _dot_nt(wrt_ref[...], hb) + brt_ref[...]
    gl = [lt[r:r + 1, :] for r in range(N_GROUPS)]
    gmax = jnp.maximum(jnp.maximum(gl[0], gl[1]), jnp.maximum(gl[2], gl[3]))
    gsel = jnp.where(gl[0] == gmax, 0, jnp.where(gl[1] == gmax, 1, jnp.where(gl[2] == gmax, 2, 3)))
    p_g = 1.0 / (jnp.exp(gl[0] - gmax) + jnp.exp(gl[1] - gmax) + jnp.exp(gl[2] - gmax) + jnp.exp(gl[3] - gmax))
    el = []
    for e in range(EPG):
        r = [lt[N_GROUPS + g * EPG + e:N_GROUPS + g * EPG + e + 1, :] for g in range(N_GROUPS)]
        el.append(jnp.where(gsel == 0, r[0], jnp.where(gsel == 1, r[1], jnp.where(gsel == 2, r[2], r[3]))))
    m1 = jnp.maximum(jnp.maximum(el[0], el[1]), jnp.maximum(el[2], el[3]))
    i1 = jnp.where(el[0] == m1, 0, jnp.where(el[1] == m1, 1, jnp.where(el[2] == m1, 2, 3)))
    neg = jnp.float32(-jnp.inf)
    el2 = [jnp.where(i1 == e, neg, el[e]) for e in range(EPG)]
    m2 = jnp.maximum(jnp.maximum(el2[0], el2[1]), jnp.maximum(el2[2], el2[3]))
    i2 = jnp.where(jnp.logical_and(el2[0] == m2, i1 != 0), 0,
                   jnp.where(jnp.logical_and(el2[1] == m2, i1 != 1), 1,
                             jnp.where(jnp.logical_and(el2[2] == m2, i1 != 2), 2, 3)))
    t = jnp.exp(m2 - m1)
    w1 = p_g / (1.0 + t)
    w2 = p_g * t / (1.0 + t)
    first_low = i1 < i2
    ea = jnp.where(first_low, i1, i2)
    eb = jnp.where(first_low, i2, i1)
    w_a = jnp.where(first_low, w1, w2)
    w_b = jnp.where(first_low, w2, w1)
    pair = jnp.where(ea == 0, eb - 1, jnp.where(ea == 1, jnp.where(eb == 3, 3, 4), 5))
    swap = pair == 5
    w_a, w_b = jnp.where(swap, w_b, w_a), jnp.where(swap, w_a, w_b)
    bucket = gsel * N_PAIRS + pair

    tm = bucket.shape[1]
    rows = lax.broadcasted_iota(jnp.int32, (32, tm), 0)
    onehot = jnp.where(rows == bucket, 1.0, 0.0).astype(F32)
    prefix = _dot(onehot.astype(BF16), tri_ref[...])
    base = base_ref[:, 0:1]
    rank = jnp.sum(onehot * (prefix - 1.0 + base), axis=0, keepdims=True)
    newbase = base + prefix[:, tm - 1:tm]
    base_ref[...] = jnp.broadcast_to(newbase, base_ref.shape)
    cnt_ref[...] = jnp.broadcast_to(newbase, cnt_ref.shape)
    route = jnp.concatenate([bucket.astype(F32), rank, w_a, w_b, jnp.zeros((4, tm), F32)], axis=0)
    route_ref[0] = route
    route_t = jnp.transpose(jnp.concatenate([route, jnp.zeros((120, tm), F32)], axis=0))
    h_ref[:, D_PACK:D_ROW] = lax.bitcast_convert_type(route_t, jnp.uint32)


def _out_proj(x2d, ya, yb, mod3, w_o, g2, wrt, brt, L):
    T = x2d.shape[0]
    nb = T // TM_OUT
    per_b = L // TM_OUT
    out_shape = [
        jax.ShapeDtypeStruct((T, D), F32),
        jax.ShapeDtypeStruct((T, D_ROW), jnp.uint32),
        jax.ShapeDtypeStruct((nb, 8, TM_OUT), F32),
        jax.ShapeDtypeStruct((32, 128), F32),
    ]
    return pl.pallas_call(
        _out_kernel,
        out_shape=out_shape,
        grid=(nb,),
        in_specs=[
            pl.BlockSpec((TM_OUT, D), lambda i: (i, 0)),
            pl.BlockSpec((TM_OUT, D_A), lambda i: (i, 0)),
            pl.BlockSpec((TM_OUT, D_B), lambda i: (i, 0)),
            pl.BlockSpec((1, 1, 6 * D), lambda i: (i // per_b, 0, 0)),
            pl.BlockSpec((D, D), lambda i: (0, 0), pipeline_mode=pl.Buffered(1)),
            pl.BlockSpec((1, D), lambda i: (0, 0)),
            pl.BlockSpec((32, D), lambda i: (0, 0)),
            pl.BlockSpec((32, 1), lambda i: (0, 0)),
            pl.BlockSpec((TM_OUT, TM_OUT), lambda i: (0, 0), pipeline_mode=pl.Buffered(1)),
        ],
        out_specs=[
            pl.BlockSpec((TM_OUT, D), lambda i: (i, 0)),
            pl.BlockSpec((TM_OUT, D_ROW), lambda i: (i, 0)),
            pl.BlockSpec((1, 8, TM_OUT), lambda i: (i, 0, 0)),
            pl.BlockSpec((32, 128), lambda i: (0, 0)),
        ],
        scratch_shapes=[pltpu.VMEM((32, 128), F32), pltpu.VMEM((D, D), BF16)],
        compiler_params=pltpu.CompilerParams(dimension_semantics=("arbitrary",),
                                             vmem_limit_bytes=VMEM_LIMIT),
    )(x2d, ya, yb, mod3, w_o, g2, wrt, brt, jnp.asarray(np.triu(np.ones((TM_OUT, TM_OUT), np.float32)), BF16))


def _sc_mesh_and_split(n_rows):
    n_workers = SC_CORES * SC_SUBCORES
    per_worker = n_rows // n_workers
    n_chunks = per_worker // SC_WINDOW
    assert per_worker * n_workers == n_rows and n_chunks * SC_WINDOW == per_worker and n_chunks % 2 == 0
    mesh = plsc.VectorSubcoreMesh(core_axis_name="c", subcore_axis_name="s")
    return mesh, n_workers, per_worker, n_chunks


def _sc_two_buffer_loop(n_chunks, first, second):
    first(0, 0).start()

    @pl.loop(0, n_chunks, step=2)
    def _(j):
        for b in range(2):
            jj = j + b
            first(jj, b).wait()

            @pl.when(jj + 1 < n_chunks)
            def _():
                @pl.when(jj >= 1)
                def _():
                    second(jj - 1, 1 - b).wait()
                first(jj + 1, 1 - b).start()

            second(jj, b).start()

    second(n_chunks - 2, 0).wait()
    second(n_chunks - 1, 1).wait()


def _sc_row_gather(table, idx):
    n_rows = idx.shape[0]
    width = table.shape[1]
    mesh, n_workers, per_worker, n_chunks = _sc_mesh_and_split(n_rows)

    @functools.partial(
        pl.kernel, mesh=mesh,
        out_type=jax.ShapeDtypeStruct((n_rows, width), table.dtype),
        scratch_types=[pltpu.VMEM((n_chunks, SC_WINDOW), jnp.int32),
                       pltpu.VMEM((2, SC_WINDOW, width), table.dtype),
                       pltpu.SemaphoreType.DMA((2,)),
                       pltpu.SemaphoreType.DMA((2,))],
    )
    def gather(table_hbm, idx_hbm, out_hbm, idx_v, rows_v, sem_in, sem_out):
        wid = lax.axis_index("s") * SC_CORES + lax.axis_index("c")
        base = wid * per_worker
        pltpu.sync_copy(idx_hbm.at[wid], idx_v)

        def fetch(j, b):
            return pltpu.make_async_copy(table_hbm.at[idx_v.at[j]], rows_v.at[b], sem_in.at[b])

        def put(j, b):
            off = pl.multiple_of(base + j * SC_WINDOW, 8)
            return pltpu.make_async_copy(rows_v.at[b], out_hbm.at[pl.ds(off, SC_WINDOW)], sem_out.at[b])

        _sc_two_buffer_loop(n_chunks, fetch, put)

    return gather(table, idx.reshape(n_workers, n_chunks, SC_WINDOW))


def _sc_row_scatter(rows, pos, n_out):
    n_rows, width = rows.shape
    mesh, n_workers, per_worker, n_chunks = _sc_mesh_and_split(n_rows)

    @functools.partial(
        pl.kernel, mesh=mesh,
        out_type=jax.ShapeDtypeStruct((n_out, width), rows.dtype),
        scratch_types=[pltpu.VMEM((n_chunks, SC_WINDOW), jnp.int32),
                       pltpu.VMEM((2, SC_WINDOW, width), rows.dtype),
                       pltpu.SemaphoreType.DMA((2,)),
                       pltpu.SemaphoreType.DMA((2,))],
    )
    def scatter(rows_hbm, pos_hbm, out_hbm, idx_v, rows_v, sem_in, sem_out):
        wid = lax.axis_index("s") * SC_CORES + lax.axis_index("c")
        base = wid * per_worker
        pltpu.sync_copy(pos_hbm.at[wid], idx_v)

        def fetch(j, b):
            off = pl.multiple_of(base + j * SC_WINDOW, 8)
            return pltpu.make_async_copy(rows_hbm.at[pl.ds(off, SC_WINDOW)], rows_v.at[b], sem_in.at[b])

        def put(j, b):
            return pltpu.make_async_copy(rows_v.at[b], out_hbm.at[idx_v.at[j]], sem_out.at[b])

        _sc_two_buffer_loop(n_chunks, fetch, put)

    return scatter(rows, pos.reshape(n_workers, n_chunks, SC_WINDOW))


def _moe_kernel(ea_ref, eb_ref, nv_ref, hs_ref, wga_ref, wgb_ref, wda_ref, wdb_ref, o_ref):
    i = pl.program_id(0)
    nv = nv_ref[i]

    def experts(n_rows):
        live = lax.broadcasted_iota(jnp.int32, (n_rows, 1), 0) < nv
        xb = _unpack_bf16_pairs(jnp.where(live, hs_ref[0:n_rows, 0:D_PACK], jnp.uint32(0)))
        w_ab = lax.bitcast_convert_type(hs_ref[0:n_rows, D_PACK:D_ROW], F32)
        w_a = jnp.where(live, w_ab[:, 2:3], 0.0)
        w_b = jnp.where(live, w_ab[:, 3:4], 0.0)
        ga = _dot(xb, wga_ref[0])
        act_a = (_silu(ga[:, :D_EXPERT]) * ga[:, D_EXPERT:] * w_a).astype(BF16)
        gb = _dot(xb, wgb_ref[0])
        act_b = (_silu(gb[:, :D_EXPERT]) * gb[:, D_EXPERT:] * w_b).astype(BF16)
        return _pack_bf16_pairs(_dot(act_a, wda_ref[0]) + _dot(act_b, wdb_ref[0]))

    quarter = BM_MOE // 4
    for q in range(1, 5):
        n_rows = q * quarter

        @pl.when(jnp.logical_and(nv > n_rows - quarter, nv <= n_rows))
        def _(n_rows=n_rows):
            o_ref[0:n_rows, :] = experts(n_rows)
            if n_rows < BM_MOE:
                o_ref[n_rows:, :] = jnp.zeros((BM_MOE - n_rows, D_PACK), jnp.uint32)

    @pl.when(nv <= 0)
    def _():
        o_ref[...] = jnp.zeros_like(o_ref)


def _moe(ea, eb, nvalid, hs, w_gu, w_dn):
    nblk = ea.shape[0]
    S = nblk * BM_MOE
    grid_spec = pltpu.PrefetchScalarGridSpec(
        num_scalar_prefetch=3,
        grid=(nblk,),
        in_specs=[
            pl.BlockSpec((BM_MOE, D_ROW), lambda i, ea, eb, nv: (i, 0)),
            pl.BlockSpec((1, D, 2 * D_EXPERT), lambda i, ea, eb, nv: (ea[i], 0, 0)),
            pl.BlockSpec((1, D, 2 * D_EXPERT), lambda i, ea, eb, nv: (eb[i], 0, 0)),
            pl.BlockSpec((1, D_EXPERT, D), lambda i, ea, eb, nv: (ea[i], 0, 0)),
            pl.BlockSpec((1, D_EXPERT, D), lambda i, ea, eb, nv: (eb[i], 0, 0)),
        ],
        out_specs=pl.BlockSpec((BM_MOE, D_PACK), lambda i, ea, eb, nv: (i, 0)),
    )
    return pl.pallas_call(
        _moe_kernel,
        out_shape=jax.ShapeDtypeStruct((S, D_PACK), jnp.uint32),
        grid_spec=grid_spec,
        compiler_params=pltpu.CompilerParams(dimension_semantics=("arbitrary",),
                                             vmem_limit_bytes=VMEM_LIMIT),
    )(ea, eb, nvalid, hs, w_gu, w_gu, w_dn, w_dn)


def _final_kernel(x2_ref, m_ref, mod_ref, fg_ref, *rest):
    o_ref = rest[-1]
    gt2 = mod_ref[0][:, 5 * D:6 * D]
    y = x2_ref[...] + gt2 * _unpack_bf16_pairs(m_ref[...]).astype(F32)
    ms = jnp.mean(y * y, axis=-1, keepdims=True)
    o_ref[...] = y * lax.rsqrt(ms + EPS) * fg_ref[...]


def _final(x2, m_part, mod3, fg, L, part, prev_out):
    T = x2.shape[0]
    steps = m_part.shape[0] // TM_FIN
    off = part * steps
    per_b = L // TM_FIN
    in_specs = [
        pl.BlockSpec((TM_FIN, D), lambda i: (i + off, 0)),
        pl.BlockSpec((TM_FIN, D_PACK), lambda i: (i, 0)),
        pl.BlockSpec((1, 1, 6 * D), lambda i: ((i + off) // per_b, 0, 0)),
        pl.BlockSpec((1, D), lambda i: (0, 0)),
    ]
    args = [x2, m_part, mod3, fg]
    aliases = {}
    if prev_out is not None:
        in_specs.append(pl.BlockSpec(memory_space=pl.ANY))
        args.append(prev_out)
        aliases = {len(args) - 1: 0}
    return pl.pallas_call(
        _final_kernel,
        out_shape=jax.ShapeDtypeStruct((T, D), F32),
        grid=(steps,),
        in_specs=in_specs,
        out_specs=pl.BlockSpec((TM_FIN, D), lambda i: (i + off, 0)),
        input_output_aliases=aliases,
        compiler_params=pltpu.CompilerParams(dimension_semantics=("arbitrary",),
                                             vmem_limit_bytes=VMEM_LIMIT),
    )(*args)


def kernel(x, c, ctx, c_ctx, w_ada, b_ada, norm1_g, w_in, ln_a_g, ln_a_b, w_spatial, b_spatial, conv_qkv, a_log,
           dt_bias, onorm_g, w_out, norm2_g, w_group, b_group, w_router, b_router, w_gate_up, w_down, final_g):
    B, L, _ = x.shape
    T = B * L
    assert w_ada.shape[0] == 1 and ctx.shape[1] == TM_IN and L % TM_OUT == 0 and T % (N_COMBINE_PARTS * TM_FIN) == 0

    cond = jnp.concatenate([c, c_ctx[None, :], jnp.zeros((7, D), F32)], axis=0)
    mod = _modulation(cond, w_ada[0], b_ada[0][None, :])
    mod_lat = mod[:B].reshape(B, 1, 6 * D)
    mod_ctx = mod[B:B + 1]

    alog = a_log[0].reshape(1, 2 * HEADS)
    dtb = dt_bias[0].reshape(1, 2 * HEADS)
    alog_row = jnp.pad(alog, ((0, 0), (0, 128 - 2 * HEADS)))
    dtb_row = jnp.pad(dtb, ((0, 0), (0, 128 - 2 * HEADS)))
    alog_col = jnp.pad(alog, ((0, 0), (0, N_AB - 2 * HEADS))).T
    dtb_col = jnp.pad(dtb, ((0, 0), (0, N_AB - 2 * HEADS))).T

    ya, qkv, z, gb, gbt = _in_proj(
        x, ctx, mod_lat, mod_ctx, norm1_g, jnp.swapaxes(w_in, 1, 2), ln_a_g, ln_a_b,
        w_spatial[0].astype(BF16), b_spatial[0].T, conv_qkv[0], alog_row, dtb_row, alog_col, dtb_col)

    yb, w_gu_bf, w_dn_bf = _delta(qkv, z, gb, gbt, onorm_g, w_gate_up[0], w_down[0], L)

    wrt = jnp.concatenate([w_group[0].T, w_router[0].T, jnp.zeros((32 - N_GROUPS - N_EXPERTS, D), F32)], axis=0)
    brt = jnp.concatenate([b_group[0], b_router[0], jnp.zeros((32 - N_GROUPS - N_EXPERTS,), F32)])[:, None]
    x2, h, route, cnt = _out_proj(x.reshape(T, D), ya.reshape(T, D_A), yb.reshape(T, D_B), mod_lat,
                                  w_out[0], norm2_g, wrt.astype(BF16), brt, L)

    bucket = route[:, 0, :].reshape(T).astype(jnp.int32)
    rank = route[:, 1, :].reshape(T).astype(jnp.int32)
    counts = cnt[:N_BUCKETS, 0].astype(jnp.int32)
    nblk_b = (counts + BM_MOE - 1) // BM_MOE
    blk_end = jnp.cumsum(nblk_b)
    blk_start = blk_end - nblk_b
    kk = jnp.arange(N_BUCKETS, dtype=jnp.int32)

    def pick(idx, table):
        return jnp.sum(jnp.where(idx[:, None] == kk[None, :], table[None, :], 0), axis=1)

    pos = pick(bucket, blk_start) * BM_MOE + rank
    n_blocks = T // BM_MOE + N_BUCKETS
    S = n_blocks * BM_MOE
    blk = jnp.arange(n_blocks, dtype=jnp.int32)
    used = blk < blk_end[-1]
    bkt = jnp.sum((jnp.minimum(blk, blk_end[-1] - 1)[:, None] >= blk_end[None, :]).astype(jnp.int32), axis=1)
    nvalid = jnp.where(used, jnp.clip(pick(bkt, counts) - (blk - pick(bkt, blk_start)) * BM_MOE, 0, BM_MOE),
                       0).astype(jnp.int32)
    ea = pick(bkt, jnp.asarray(_SLOT_A_EXPERT))
    eb = pick(bkt, jnp.asarray(_SLOT_B_EXPERT))

    hs = _sc_row_scatter(h, pos, S)
    ms = _moe(ea, eb, nvalid, hs, w_gu_bf, w_dn_bf)
    pos_parts = pos.reshape(N_COMBINE_PARTS, T // N_COMBINE_PARTS)
    m_parts = [_sc_row_gather(ms, pos_parts[q]) for q in range(N_COMBINE_PARTS)]
    out = None
    for q in range(N_COMBINE_PARTS):
        out = _final(x2, m_parts[q], mod_lat, final_g[None, :], L, q, out)
    return out.reshape(B, L, D)
```

```python
import functools

import jax
import jax.numpy as jnp
import numpy as np
from jax import lax
from jax.experimental import pallas as pl
from jax.experimental.pallas import tpu as pltpu
from jax.experimental.pallas import tpu_sc as plsc

F32 = jnp.float32
BF16 = jnp.bfloat16
EPS = 1e-6

D = 1024
D_A = 512
D_B = 512
HEADS = 4
HD = 128
CHUNK = 128
CONV_W = 5
N_QKV = 3 * D_B
N_MAIN = 2 * D_A + 4 * D_B
N_AB = 16
N_GROUPS = 4
EPG = 4
N_EXPERTS = 16
D_EXPERT = 512
N_PAIRS = 6
N_BUCKETS = N_GROUPS * N_PAIRS
D_PACK = D // 2
D_ROW = D_PACK + 128
PAIR_A = (0, 0, 0, 1, 1, 3)
PAIR_B = (1, 2, 3, 3, 2, 2)
_SLOT_A_EXPERT = np.array([g * EPG + PAIR_A[p] for g in range(N_GROUPS) for p in range(N_PAIRS)], np.int32)
_SLOT_B_EXPERT = np.array([g * EPG + PAIR_B[p] for g in range(N_GROUPS) for p in range(N_PAIRS)], np.int32)

TM_IN = 256
HALO = 8
NB_IN = 2
NB_DELTA = 2
TM_OUT = 1024
OUT_SPLIT = 4
BM_MOE = 512
TM_FIN = 1024
N_COMBINE_PARTS = 4
VMEM_LIMIT = 56 * 1024 * 1024
SC_CORES = 2
SC_SUBCORES = 16
SC_WINDOW = 64

HI = lax.Precision.HIGHEST

_CHUNK_TRIL = np.kron(np.eye(TM_IN // CHUNK, dtype=np.float32), np.tril(np.ones((CHUNK, CHUNK), np.float32)))


def _dot(a, b, precision=None):
    return jnp.dot(a, b, preferred_element_type=F32, precision=precision)


def _dot_nt(a, b):
    return lax.dot_general(a, b, (((1,), (1,)), ((), ())), preferred_element_type=F32)


def _dot_tn(a, b):
    return lax.dot_general(a, b, (((0,), (0,)), ((), ())), preferred_element_type=F32)


def _sigmoid(x):
    return 0.5 + 0.5 * jnp.tanh(0.5 * x)


def _silu(x):
    h = 0.5 * x
    return h + h * jnp.tanh(h)


def _softplus(x):
    return jnp.maximum(x, 0.0) + jnp.log(1.0 + jnp.exp(-jnp.abs(x)))


def _pack_bf16_pairs(x):
    bits = lax.bitcast_convert_type(x.astype(BF16).astype(F32), jnp.uint32)
    return (bits[:, D_PACK:] & jnp.uint32(0xFFFF0000)) | (bits[:, :D_PACK] >> 16)


def _unpack_bf16_pairs(w):
    bits = w
    lo = lax.bitcast_convert_type(bits << 16, F32)
    hi = lax.bitcast_convert_type(bits & jnp.uint32(0xFFFF0000), F32)
    return jnp.concatenate([lo, hi], axis=1).astype(BF16)


def _gelu_tanh(x):
    return 0.5 * x * (1.0 + jnp.tanh(np.sqrt(2.0 / np.pi).astype(np.float32) * (x + 0.044715 * (x * x * x))))


def _mod_kernel(c_ref, w_ref, b_ref, o_ref):
    c = c_ref[...]
    o_ref[...] = _dot(_silu(c), w_ref[...], precision=HI) + b_ref[...]


def _modulation(cond, w_ada, b_ada):
    rows = cond.shape[0]
    tn = 1536
    return pl.pallas_call(
        _mod_kernel,
        out_shape=jax.ShapeDtypeStruct((rows, 6 * D), F32),
        grid=(6 * D // tn,),
        in_specs=[pl.BlockSpec((rows, D), lambda i: (0, 0)),
                  pl.BlockSpec((D, tn), lambda i: (0, i)),
                  pl.BlockSpec((1, tn), lambda i: (0, i))],
        out_specs=pl.BlockSpec((rows, tn), lambda i: (0, i)),
        compiler_params=pltpu.CompilerParams(dimension_semantics=("arbitrary",),
                                             vmem_limit_bytes=VMEM_LIMIT),
    )(cond, w_ada, b_ada)


def _in_kernel(x_ref, xp_ref, xn_ref, ctx_ref, mod_ref, cmod_ref, g1_ref, wt_ref,
               lng_ref, lnb_ref, ws_ref, bst_ref, conv_ref, alog_ref, dtb_ref, alogt_ref, dtbt_ref, tril_ref, triu_ref,
               ya_ref, qkv_ref, z_ref, gb_ref, gbt_ref, wbf_ref, wab_ref, wabt_ref):
    j = pl.program_id(1)

    @pl.when(jnp.logical_and(pl.program_id(0) == 0, j == 0))
    def _():
        for c0 in range(0, N_MAIN, 512):
            wbf_ref[:, c0:c0 + 512] = jnp.transpose(wt_ref[0, c0:c0 + 512, :]).astype(BF16)
        tail = jnp.concatenate([wt_ref[0, N_MAIN:N_MAIN + N_AB, :], jnp.zeros((128 - N_AB, D), F32)], axis=0)
        wabt_ref[...] = tail.astype(BF16)
        wab_ref[...] = jnp.transpose(tail).astype(BF16)

    is_ctx = j == 0
    n_lat_blocks = pl.num_programs(1) - 1

    def one_batch_element(bb):
        mod = mod_ref[bb]
        cm = cmod_ref[...]
        sh = jnp.where(is_ctx, cm[:, 0:D], mod[:, 0:D])
        sc = jnp.where(is_ctx, cm[:, D:2 * D], mod[:, D:2 * D])
        scale = g1_ref[...] * (1.0 + sc)

        xmain = jnp.where(is_ctx, ctx_ref[bb], x_ref[bb])
        xv = jnp.concatenate([xp_ref[bb], xmain, xn_ref[bb]], axis=0)
        xnorm = xv * lax.rsqrt(jnp.mean(xv * xv, axis=-1, keepdims=True) + EPS) * scale + sh
        xe = xnorm.astype(BF16)
        xb = xnorm[HALO:HALO + TM_IN].astype(BF16)

        rid = lax.broadcasted_iota(jnp.int32, (TM_IN + 2 * HALO, 1), 0)
        prev_ok = j >= 2
        next_ok = jnp.logical_and(j >= 1, j < n_lat_blocks)
        valid = jnp.logical_or(jnp.logical_and(rid >= HALO, rid < HALO + TM_IN),
                               jnp.logical_or(jnp.logical_and(rid < HALO, prev_ok),
                                              jnp.logical_and(rid >= HALO + TM_IN, next_ok)))
        pad = (CONV_W - 1) // 2
        c_qkv = 2 * D_A

        def proj(c0, width, halo=True):
            return _dot(xe if halo else xb, wbf_ref[:, c0:c0 + width])

        def conv_act(pq, c0):
            groups = (TM_IN + 2 * HALO) // 8
            x3 = jnp.where(valid, pq, 0.0).reshape(groups, 8, D_B)
            sub = lax.broadcasted_iota(jnp.int32, (1, 8, 1), 1)
            lo, hi = HALO // 8, HALO // 8 + TM_IN // 8
            acc = conv_ref[pad:pad + 1, c0:c0 + D_B] * x3[lo:hi]
            for t in range(CONV_W):
                s = t - pad
                if s == 0:
                    continue
                r = pltpu.roll(x3, (-s) % 8, axis=1)
                if s > 0:
                    sh = jnp.where(sub < 8 - s, r[lo:hi], r[lo + 1:hi + 1])
                else:
                    sh = jnp.where(sub >= -s, r[lo:hi], r[lo - 1:hi - 1])
                acc = acc + conv_ref[t:t + 1, c0:c0 + D_B] * sh
            return _silu(acc.reshape(TM_IN, D_B))

        def store_unit_heads(act, c0, gain):
            for h in range(HEADS):
                t = act[:, h * HD:(h + 1) * HD]
                nrm = lax.rsqrt(jnp.sum(t * t, axis=-1, keepdims=True) + EPS) * gain
                qkv_ref[bb, :, c0 + h * HD:c0 + (h + 1) * HD] = (t * nrm).astype(BF16)

        pq_q = proj(c_qkv, D_B)
        pq_k = proj(c_qkv + D_B, D_B)
        store_unit_heads(conv_act(pq_q, 0), 0, HD ** -0.5)
        pq_v = proj(c_qkv + 2 * D_B, D_B)
        store_unit_heads(conv_act(pq_k, D_B), D_B, 1.0)
        pa_u = proj(0, D_A, halo=False)
        qkv_ref[bb, :, 2 * D_B:] = conv_act(pq_v, 2 * D_B).astype(BF16)
        pa_v = proj(D_A, D_A, halo=False)
        u = _gelu_tanh(pa_u)
        pz = proj(c_qkv + N_QKV, D_B, halo=False)
        v = _gelu_tanh(pa_v)
        mu = jnp.mean(v, axis=-1, keepdims=True)
        vc = v - mu
        var = jnp.mean(vc * vc, axis=-1, keepdims=True)
        vn = (vc * lax.rsqrt(var + EPS) * lng_ref[...] + lnb_ref[...]).astype(BF16)
        z_ref[bb] = pz.astype(BF16)

        bst = bst_ref[...]
        for n in range(TM_IN // CHUNK):
            rows = slice(n * CHUNK, (n + 1) * CHUNK)
            for h in range(HEADS):
                cols = slice(h * HD, (h + 1) * HD)
                s = _dot(ws_ref[h], vn[rows, cols]) + bst[:, h:h + 1]
                ya_ref[bb, rows, cols] = (u[rows, cols] * s).astype(BF16)

        tri_l = tril_ref[...]
        tri_u = triu_ref[...]

        def split3(g):
            hi = g.astype(BF16)
            r1 = g - hi.astype(F32)
            mid = r1.astype(BF16)
            return hi, mid, (r1 - mid.astype(F32)).astype(BF16)

        ab = _dot(xb, wab_ref[...])
        g3 = split3(-jnp.exp(alog_ref[...]) * _softplus(ab + dtb_ref[...]))
        lane = lax.broadcasted_iota(jnp.int32, ab.shape, 1)
        gb = jnp.where(lane < HEADS, _dot(tri_l, g3[0]) + _dot(tri_l, g3[1]) + _dot(tri_l, g3[2]),
                       jnp.where(lane < 2 * HEADS, _dot(tri_u, g3[0]) + _dot(tri_u, g3[1]) + _dot(tri_u, g3[2]),
                                 _sigmoid(ab)))
        gb_ref[bb] = gb[:, 0:N_AB]

        abt = _dot_nt(wabt_ref[0:N_AB, :], xb)
        t3 = split3(-jnp.exp(alogt_ref[...]) * _softplus(abt + dtbt_ref[...]))
        row = lax.broadcasted_iota(jnp.int32, abt.shape, 0)
        gbt_ref[bb] = jnp.where(row < HEADS, _dot(t3[0], tri_u) + _dot(t3[1], tri_u) + _dot(t3[2], tri_u),
                               jnp.where(row < 2 * HEADS, _dot(t3[0], tri_l) + _dot(t3[1], tri_l) + _dot(t3[2], tri_l),
                                         _sigmoid(abt)))

    for bb in range(x_ref.shape[0]):
        one_batch_element(bb)


def _in_proj(x, ctx, mod_lat, mod_ctx, g1, w_in_t, lng, lnb, ws, bst, conv, alog, dtb, alogt, dtbt):
    B, L, _ = x.shape
    n_lat = L // TM_IN
    n_steps = n_lat + 1
    LC = L + TM_IN
    hb = TM_IN // HALO

    def full(shape):
        return pl.BlockSpec(shape, lambda b, j: (0,) * len(shape))

    in_specs = [
        pl.BlockSpec((NB_IN, TM_IN, D), lambda b, j: (b, jnp.maximum(j - 1, 0), 0)),
        pl.BlockSpec((NB_IN, HALO, D), lambda b, j: (b, jnp.clip((j - 1) * hb - 1, 0, L // HALO - 1), 0)),
        pl.BlockSpec((NB_IN, HALO, D), lambda b, j: (b, jnp.clip(j * hb, 0, L // HALO - 1), 0)),
        pl.BlockSpec((NB_IN, TM_IN, D), lambda b, j: (b, 0, 0)),
        pl.BlockSpec((NB_IN, 1, 6 * D), lambda b, j: (b, 0, 0)),
        full((1, 6 * D)), full((1, D)),
        pl.BlockSpec((1, N_MAIN + N_AB, D), lambda b, j: (0, 0, 0), pipeline_mode=pl.Buffered(1)),
        full((1, D_A)), full((1, D_A)), full((HEADS, CHUNK, CHUNK)), full((CHUNK, HEADS)),
        full((CONV_W, N_QKV)), full((1, 128)), full((1, 128)), full((N_AB, 1)), full((N_AB, 1)),
        full((TM_IN, TM_IN)), full((TM_IN, TM_IN)),
    ]
    out_shape = [
        jax.ShapeDtypeStruct((B, L, D_A), BF16),
        jax.ShapeDtypeStruct((B, LC, N_QKV), BF16),
        jax.ShapeDtypeStruct((B, LC, D_B), BF16),
        jax.ShapeDtypeStruct((B, LC, N_AB), F32),
        jax.ShapeDtypeStruct((B, N_AB, LC), F32),
    ]
    out_specs = [
        pl.BlockSpec((NB_IN, TM_IN, D_A), lambda b, j: (b, jnp.maximum(j - 1, 0), 0)),
        pl.BlockSpec((NB_IN, TM_IN, N_QKV), lambda b, j: (b, j, 0)),
        pl.BlockSpec((NB_IN, TM_IN, D_B), lambda b, j: (b, j, 0)),
        pl.BlockSpec((NB_IN, TM_IN, N_AB), lambda b, j: (b, j, 0)),
        pl.BlockSpec((NB_IN, N_AB, TM_IN), lambda b, j: (b, 0, j)),
    ]
    return pl.pallas_call(
        _in_kernel,
        out_shape=out_shape,
        grid=(B // NB_IN, n_steps),
        in_specs=in_specs,
        out_specs=out_specs,
        scratch_shapes=[pltpu.VMEM((D, N_MAIN), BF16), pltpu.VMEM((D, 128), BF16), pltpu.VMEM((128, D), BF16)],
        compiler_params=pltpu.CompilerParams(dimension_semantics=("arbitrary", "arbitrary"),
                                             vmem_limit_bytes=VMEM_LIMIT),
    )(x, x, x, ctx, mod_lat, mod_ctx, g1, w_in_t, lng, lnb, ws, bst, conv, alog, dtb, alogt, dtbt,
      jnp.asarray(_CHUNK_TRIL, BF16), jnp.asarray(_CHUNK_TRIL.T, BF16))


def _delta_kernel(qf_ref, qb_ref, zf_ref, zb_ref, gf_ref, gbk_ref, gtf_ref, gtb_ref, on_ref, wgu_ref, wdn_ref,
                  y_ref, wgu_bf_ref, wdn_bf_ref, s_ref, oacc_ref, *, n_ctx, n_lat):
    s = pl.program_id(1)

    wgu_bf_ref[...] = wgu_ref[...].astype(BF16)
    wdn_bf_ref[...] = wdn_ref[...].astype(BF16)

    @pl.when(s == 0)
    def _():
        s_ref[...] = jnp.zeros_like(s_ref)
        oacc_ref[...] = jnp.zeros_like(oacc_ref)

    row = lax.broadcasted_iota(jnp.int32, (CHUNK, CHUNK), 0)
    col = lax.broadcasted_iota(jnp.int32, (CHUNK, CHUNK), 1)
    low = row > col
    upp = row < col
    same_blk = (row // 16) == (col // 16)
    eye = jnp.where(row == col, 1.0, 0.0).astype(BF16)
    zero = jnp.zeros((CHUNK, CHUNK), BF16)
    onorm = on_ref[...]
    half = n_ctx + n_lat // 2
    second = s >= half
    g_refs = (gf_ref, gbk_ref)
    gt_refs = (gtf_ref, gtb_ref)
    qkv_refs = (qf_ref, qb_ref)
    z_refs = (zf_ref, zb_ref)
    nb = qf_ref.shape[0]
    ps = range(nb * HEADS)

    def halves(xc, unit):
        xb = xc.astype(BF16)
        fill = eye if unit else zero
        return jnp.where(low, xb, fill), jnp.where(upp, xb, fill)

    def as_lhs(hv):
        return jnp.concatenate(hv, axis=1)

    def as_rhs(*hvs):
        cols_ = [jnp.concatenate(hv, axis=0) for hv in hvs]
        return cols_[0] if len(cols_) == 1 else jnp.concatenate(cols_, axis=1)

    def load(d, p, part):
        bb, h = divmod(p, HEADS)
        return qkv_refs[d][bb, :, part * D_B + h * HD:part * D_B + (h + 1) * HD]

    def gcol(d, p, base):
        bb, h = divmod(p, HEADS)
        c = base + d * HEADS + h
        return g_refs[d][bb, :, c:c + 1]

    def grow(d, p, base):
        bb, h = divmod(p, HEADS)
        r = base + d * HEADS + h
        return gt_refs[d][bb, r:r + 1, :]

    def lanes(col):
        return jnp.broadcast_to(col, (CHUNK, HD))

    q = [[load(d, p, 0) for p in ps] for d in range(2)]
    k = [[load(d, p, 1) for p in ps] for d in range(2)]
    v = [[load(d, p, 2) for p in ps] for d in range(2)]
    gcl = [[lanes(gcol(d, p, 0)) for p in ps] for d in range(2)]
    betal = [[lanes(gcol(d, p, 2 * HEADS)) for p in ps] for d in range(2)]
    gr = [[grow(d, p, 0) for p in ps] for d in range(2)]
    betar = [[grow(d, p, 2 * HEADS) for p in ps] for d in range(2)]
    glast = [[gr[0][p][:, CHUNK - 1:CHUNK] for p in ps], [gr[1][p][:, 0:1] for p in ps]]

    gram = [[_dot_nt(jnp.concatenate([q[d][p], k[d][p]], axis=0), k[d][p]) for p in ps] for d in range(2)]
    dec = [jnp.exp(jnp.where(low, gcl[0][p] - gr[0][p], jnp.where(upp, gcl[1][p] - gr[1][p], 0.0))) for p in ps]
    lc = [jnp.where(low, gram[0][p][CHUNK:] * betar[0][p], jnp.where(upp, gram[1][p][CHUNK:] * betar[1][p], 0.0))
          * dec[p] for p in ps]
    qk = [[jnp.where(upp, 0.0, gram[0][p][:CHUNK] * dec[p]).astype(BF16) for p in ps],
          [jnp.where(low, 0.0, gram[1][p][:CHUNK] * dec[p]).astype(BF16) for p in ps]]

    dg = [jnp.where(same_blk, lc[p], 0.0) for p in ps]
    ob = [lc[p] - dg[p] for p in ps]
    d1h = [halves(dg[p], False) for p in ps]
    d2 = [_dot(as_lhs(d1h[p]), as_rhs(d1h[p])) for p in ps]
    p0s = [-dg[p] for p in ps]
    d2h = [halves(d2[p], False) for p in ps]
    p0h = [halves(p0s[p], True) for p in ps]
    o2 = [_dot(as_lhs(d2h[p]), as_rhs(d2h[p], p0h[p])) for p in ps]
    p1s = [p0s[p] + o2[p][:, CHUNK:] for p in ps]
    d4h = [halves(o2[p][:, :CHUNK], False) for p in ps]
    p1h = [halves(p1s[p], True) for p in ps]
    o3 = [_dot(as_lhs(d4h[p]), as_rhs(d4h[p], p1h[p])) for p in ps]
    p2s = [p1s[p] + o3[p][:, CHUNK:] for p in ps]
    d8h = [halves(o3[p][:, :CHUNK], False) for p in ps]
    p2h = [halves(p2s[p], True) for p in ps]
    p3s = [p2s[p] + _dot(as_lhs(d8h[p]), as_rhs(p2h[p])) for p in ps]
    p3h = [halves(p3s[p], True) for p in ps]
    obh = [halves(ob[p], False) for p in ps]
    n1h = [halves(_dot(as_lhs(p3h[p]), as_rhs(obh[p])), False) for p in ps]
    o6 = [_dot(as_lhs(n1h[p]), as_rhs(n1h[p], p3h[p])) for p in ps]
    r0s = [p3s[p] - o6[p][:, CHUNK:] for p in ps]
    n2h = [halves(o6[p][:, :CHUNK], False) for p in ps]
    r0h = [halves(r0s[p], True) for p in ps]
    o7 = [_dot(as_lhs(n2h[p]), as_rhs(n2h[p], r0h[p])) for p in ps]
    r1s = [r0s[p] + o7[p][:, CHUNK:] for p in ps]
    n4h = [halves(o7[p][:, :CHUNK], False) for p in ps]
    r1h = [halves(r1s[p], True) for p in ps]
    tinv = [halves(r1s[p] + _dot(as_lhs(n4h[p]), as_rhs(r1h[p])), True) for p in ps]

    offs = []
    for d in range(2):
        lat_chunk = (s - n_ctx) if d == 0 else (n_ctx + n_lat - 1 - s)
        off = pl.multiple_of(jnp.clip(lat_chunk, 0, n_lat - 1) * CHUNK, CHUNK)
        sidx = [(p // HEADS * 2 + d) * HEADS + p % HEADS for p in ps]
        egc = [jnp.exp(gcl[d][p]) for p in ps]
        kf = [k[d][p].astype(F32) for p in ps]
        rhs = [jnp.concatenate([v[d][p], (kf[p] * egc[p]).astype(BF16)], axis=1) for p in ps]
        uw = [_dot(tinv[p][d], rhs[p]) for p in ps]
        qd = [q[d][p].astype(F32) * egc[p] for p in ps]
        kd = [(kf[p] * jnp.exp(glast[d][p] - gcl[d][p])).astype(BF16) for p in ps]
        st = [s_ref[sidx[p]] for p in ps]
        a1 = [_dot(jnp.concatenate([uw[p][:, HD:] * betal[d][p], qd[p]], axis=0).astype(BF16), st[p].astype(BF16))
              for p in ps]
        vnew = [(uw[p][:, :HD] * betal[d][p] - a1[p][:CHUNK]).astype(BF16) for p in ps]
        o = [a1[p][CHUNK:] + _dot(qk[d][p], vnew[p]) for p in ps]
        for p in ps:
            s_ref[sidx[p]] = st[p] * jnp.exp(glast[d][p]) + _dot_tn(kd[p], vnew[p])
        for p in ps:
            bb, h = divmod(p, HEADS)
            cols = slice(h * HD, (h + 1) * HD)
            oacc_ref[bb, pl.ds(off, CHUNK), cols] = (
                jnp.where(second, oacc_ref[bb, pl.ds(off, CHUNK), cols], 0.0) + o[p])
        offs.append(off)

    @pl.when(second)
    def _():
        for d in range(2):
            for p in ps:
                bb, h = divmod(p, HEADS)
                cols = slice(h * HD, (h + 1) * HD)
                tot = oacc_ref[bb, pl.ds(offs[d], CHUNK), cols]
                ms = jnp.mean(tot * tot, axis=-1, keepdims=True)
                zz = z_refs[d][bb, :, cols].astype(F32)
                y_ref[bb, pl.ds(offs[d], CHUNK), cols] = (
                    tot * lax.rsqrt(ms + EPS) * onorm * _silu(zz)).astype(BF16)


def _delta(qkv, z, gb, gbt, onorm, w_gu, w_dn, L):
    B, LC, _ = qkv.shape
    n_all = LC // CHUNK
    n_lat = L // CHUNK
    n_ctx = n_all - n_lat

    def cf(s):
        return s

    def cb(s):
        return jnp.where(s < n_ctx, n_ctx - 1 - s, n_all + n_ctx - 1 - s)

    in_specs = [
        pl.BlockSpec((NB_DELTA, CHUNK, N_QKV), lambda b, s: (b, cf(s), 0)),
        pl.BlockSpec((NB_DELTA, CHUNK, N_QKV), lambda b, s: (b, cb(s), 0)),
        pl.BlockSpec((NB_DELTA, CHUNK, D_B), lambda b, s: (b, cf(s), 0)),
        pl.BlockSpec((NB_DELTA, CHUNK, D_B), lambda b, s: (b, cb(s), 0)),
        pl.BlockSpec((NB_DELTA, CHUNK, N_AB), lambda b, s: (b, cf(s), 0)),
        pl.BlockSpec((NB_DELTA, CHUNK, N_AB), lambda b, s: (b, cb(s), 0)),
        pl.BlockSpec((NB_DELTA, N_AB, CHUNK), lambda b, s: (b, 0, cf(s))),
        pl.BlockSpec((NB_DELTA, N_AB, CHUNK), lambda b, s: (b, 0, cb(s))),
        pl.BlockSpec((1, HD), lambda b, s: (0, 0)),
    ]
    n_steps = (B // NB_DELTA) * n_all
    n_exp, gu_rows, gu_cols = w_gu.shape
    _, dn_rows, dn_cols = w_dn.shape
    assert n_steps >= n_exp
    per_exp = 1
    while 2 * per_exp * n_exp <= n_steps and dn_rows % (16 * per_exp) == 0:
        per_exp *= 2
    n_slabs = n_exp * per_exp

    def slab(b, s):
        t = jnp.minimum(b * n_all + s, n_slabs - 1)
        return t // per_exp, t % per_exp, 0

    in_specs += [pl.BlockSpec((1, gu_rows // per_exp, gu_cols), slab),
                 pl.BlockSpec((1, dn_rows // per_exp, dn_cols), slab)]
    return pl.pallas_call(
        functools.partial(_delta_kernel, n_ctx=n_ctx, n_lat=n_lat),
        out_shape=[jax.ShapeDtypeStruct((B, L, D_B), BF16),
                   jax.ShapeDtypeStruct(w_gu.shape, BF16),
                   jax.ShapeDtypeStruct(w_dn.shape, BF16)],
        grid=(B // NB_DELTA, n_all),
        in_specs=in_specs,
        out_specs=[pl.BlockSpec((NB_DELTA, L, D_B), lambda b, s: (b, 0, 0)),
                   pl.BlockSpec((1, gu_rows // per_exp, gu_cols), slab),
                   pl.BlockSpec((1, dn_rows // per_exp, dn_cols), slab)],
        scratch_shapes=[pltpu.VMEM((NB_DELTA * 2 * HEADS, HD, HD), F32), pltpu.VMEM((NB_DELTA, L, D_B), F32)],
        compiler_params=pltpu.CompilerParams(dimension_semantics=("arbitrary", "arbitrary"),
                                             vmem_limit_bytes=VMEM_LIMIT),
    )(qkv, qkv, z, z, gb, gb, gbt, gbt, onorm, w_gu, w_dn)


def _out_kernel(x_ref, ya_ref, yb_ref, mod_ref, wo_ref, g2_ref, wrt_ref, brt_ref, tri_ref,
                x2_ref, h_ref, route_ref, cnt_ref, base_ref, wbf_ref):
    i = pl.program_id(0)

    @pl.when(i == 0)
    def _():
        base_ref[...] = jnp.zeros_like(base_ref)
        wbf_ref[...] = wo_ref[...].astype(BF16)

    mod = mod_ref[0]
    gt1 = mod[:, 2 * D:3 * D]
    sh2 = mod[:, 3 * D:4 * D]
    sc2 = mod[:, 4 * D:5 * D]
    scale2 = g2_ref[...] * (1.0 + sc2)
    sub = TM_OUT // OUT_SPLIT
    hbs = []
    for r in range(OUT_SPLIT):
        rows = slice(r * sub, (r + 1) * sub)
        mix = _dot(ya_ref[rows, :], wbf_ref[0:D_A, :]) + _dot(yb_ref[rows, :], wbf_ref[D_A:, :])
        x2 = x_ref[rows, :] + gt1 * mix
        x2_ref[rows, :] = x2
        ms = jnp.mean(x2 * x2, axis=-1, keepdims=True)
        hv = x2 * lax.rsqrt(ms + EPS) * scale2 + sh2
        hbs.append(hv.astype(BF16))
        h_ref[rows, 0:D_PACK] = _pack_bf16_pairs(hv)
    hb = jnp.concatenate(hbs, axis=0)

    lt = _dot_nt(wrt_ref[...], hb) + brt_ref[...]
    gl = [lt[r:r + 1, :] for r in range(N_GROUPS)]
    gmax = jnp.maximum(jnp.maximum(gl[0], gl[1]), jnp.maximum(gl[2], gl[3]))
    gsel = jnp.where(gl[0] == gmax, 0, jnp.where(gl[1] == gmax, 1, jnp.where(gl[2] == gmax, 2, 3)))
    p_g = 1.0 / (jnp.exp(gl[0] - gmax) + jnp.exp(gl[1] - gmax) + jnp.exp(gl[2] - gmax) + jnp.exp(gl[3] - gmax))
    el = []
    for e in range(EPG):
        r = [lt[N_GROUPS + g * EPG + e:N_GROUPS + g * EPG + e + 1, :] for g in range(N_GROUPS)]
        el.append(jnp.where(gsel == 0, r[0], jnp.where(gsel == 1, r[1], jnp.where(gsel == 2, r[2], r[3]))))
    m1 = jnp.maximum(jnp.maximum(el[0], el[1]), jnp.maximum(el[2], el[3]))
    i1 = jnp.where(el[0] == m1, 0, jnp.where(el[1] == m1, 1, jnp.where(el[2] == m1, 2, 3)))
    neg = jnp.float32(-jnp.inf)
    el2 = [jnp.where(i1 == e, neg, el[e]) for e in range(EPG)]
    m2 = jnp.maximum(jnp.maximum(el2[0], el2[1]), jnp.maximum(el2[2], el2[3]))
    i2 = jnp.where(jnp.logical_and(el2[0] == m2, i1 != 0), 0,
                   jnp.where(jnp.logical_and(el2[1] == m2, i1 != 1), 1,
                             jnp.where(jnp.logical_and(el2[2] == m2, i1 != 2), 2, 3)))
    t = jnp.exp(m2 - m1)
    w1 = p_g / (1.0 + t)
    w2 = p_g * t / (1.0 + t)
    first_low = i1 < i2
    ea = jnp.where(first_low, i1, i2)
    eb = jnp.where(first_low, i2, i1)
    w_a = jnp.where(first_low, w1, w2)
    w_b = jnp.where(first_low, w2, w1)
    pair = jnp.where(ea == 0, eb - 1, jnp.where(ea == 1, jnp.where(eb == 3, 3, 4), 5))
    swap = pair == 5
    w_a, w_b = jnp.where(swap, w_b, w_a), jnp.where(swap, w_a, w_b)
    bucket = gsel * N_PAIRS + pair

    tm = bucket.shape[1]
    rows = lax.broadcasted_iota(jnp.int32, (32, tm), 0)
    onehot = jnp.where(rows == bucket, 1.0, 0.0).astype(F32)
    prefix = _dot(onehot.astype(BF16), tri_ref[...])
    base = base_ref[:, 0:1]
    rank = jnp.sum(onehot * (prefix - 1.0 + base), axis=0, keepdims=True)
    newbase = base + prefix[:, tm - 1:tm]
    base_ref[...] = jnp.broadcast_to(newbase, base_ref.shape)
    cnt_ref[...] = jnp.broadcast_to(newbase, cnt_ref.shape)
    route = jnp.concatenate([bucket.astype(F32), rank, w_a, w_b, jnp.zeros((4, tm), F32)], axis=0)
    route_ref[0] = route
    route_t = jnp.transpose(jnp.concatenate([route, jnp.zeros((120, tm), F32)], axis=0))
    h_ref[:, D_PACK:D_ROW] = lax.bitcast_convert_type(route_t, jnp.uint32)


def _out_proj(x2d, ya, yb, mod3, w_o, g2, wrt, brt, L):
    T = x2d.shape[0]
    nb = T // TM_OUT
    per_b = L // TM_OUT
    out_shape = [
        jax.ShapeDtypeStruct((T, D), F32),
        jax.ShapeDtypeStruct((T, D_ROW), jnp.uint32),
        jax.ShapeDtypeStruct((nb, 8, TM_OUT), F32),
        jax.ShapeDtypeStruct((32, 128), F32),
    ]
    return pl.pallas_call(
        _out_kernel,
        out_shape=out_shape,
        grid=(nb,),
        in_specs=[
            pl.BlockSpec((TM_OUT, D), lambda i: (i, 0)),
            pl.BlockSpec((TM_OUT, D_A), lambda i: (i, 0)),
            pl.BlockSpec((TM_OUT, D_B), lambda i: (i, 0)),
            pl.BlockSpec((1, 1, 6 * D), lambda i: (i // per_b, 0, 0)),
            pl.BlockSpec((D, D), lambda i: (0, 0), pipeline_mode=pl.Buffered(1)),
            pl.BlockSpec((1, D), lambda i: (0, 0)),
            pl.BlockSpec((32, D), lambda i: (0, 0)),
            pl.BlockSpec((32, 1), lambda i: (0, 0)),
            pl.BlockSpec((TM_OUT, TM_OUT), lambda i: (0, 0), pipeline_mode=pl.Buffered(1)),
        ],
        out_specs=[
            pl.BlockSpec((TM_OUT, D), lambda i: (i, 0)),
            pl.BlockSpec((TM_OUT, D_ROW), lambda i: (i, 0)),
            pl.BlockSpec((1, 8, TM_OUT), lambda i: (i, 0, 0)),
            pl.BlockSpec((32, 128), lambda i: (0, 0)),
        ],
        scratch_shapes=[pltpu.VMEM((32, 128), F32), pltpu.VMEM((D, D), BF16)],
        compiler_params=pltpu.CompilerParams(dimension_semantics=("arbitrary",),
                                             vmem_limit_bytes=VMEM_LIMIT),
    )(x2d, ya, yb, mod3, w_o, g2, wrt, brt, jnp.asarray(np.triu(np.ones((TM_OUT, TM_OUT), np.float32)), BF16))


def _sc_mesh_and_split(n_rows):
    n_workers = SC_CORES * SC_SUBCORES
    per_worker = n_rows // n_workers
    n_chunks = per_worker // SC_WINDOW
    assert per_worker * n_workers == n_rows and n_chunks * SC_WINDOW == per_worker and n_chunks % 2 == 0
    mesh = plsc.VectorSubcoreMesh(core_axis_name="c", subcore_axis_name="s")
    return mesh, n_workers, per_worker, n_chunks


def _sc_two_buffer_loop(n_chunks, first, second):
    first(0, 0).start()

    @pl.loop(0, n_chunks, step=2)
    def _(j):
        for b in range(2):
            jj = j + b
            first(jj, b).wait()

            @pl.when(jj + 1 < n_chunks)
            def _():
                @pl.when(jj >= 1)
                def _():
                    second(jj - 1, 1 - b).wait()
                first(jj + 1, 1 - b).start()

            second(jj, b).start()

    second(n_chunks - 2, 0).wait()
    second(n_chunks - 1, 1).wait()


def _sc_row_gather(table, idx):
    n_rows = idx.shape[0]
    width = table.shape[1]
    mesh, n_workers, per_worker, n_chunks = _sc_mesh_and_split(n_rows)

    @functools.partial(
        pl.kernel, mesh=mesh,
        out_type=jax.ShapeDtypeStruct((n_rows, width), table.dtype),
        scratch_types=[pltpu.VMEM((n_chunks, SC_WINDOW), jnp.int32),
                       pltpu.VMEM((2, SC_WINDOW, width), table.dtype),
                       pltpu.SemaphoreType.DMA((2,)),
                       pltpu.SemaphoreType.DMA((2,))],
    )
    def gather(table_hbm, idx_hbm, out_hbm, idx_v, rows_v, sem_in, sem_out):
        wid = lax.axis_index("s") * SC_CORES + lax.axis_index("c")
        base = wid * per_worker
        pltpu.sync_copy(idx_hbm.at[wid], idx_v)

        def fetch(j, b):
            return pltpu.make_async_copy(table_hbm.at[idx_v.at[j]], rows_v.at[b], sem_in.at[b])

        def put(j, b):
            off = pl.multiple_of(base + j * SC_WINDOW, 8)
            return pltpu.make_async_copy(rows_v.at[b], out_hbm.at[pl.ds(off, SC_WINDOW)], sem_out.at[b])

        _sc_two_buffer_loop(n_chunks, fetch, put)

    return gather(table, idx.reshape(n_workers, n_chunks, SC_WINDOW))


def _sc_row_scatter(rows, pos, n_out):
    n_rows, width = rows.shape
    mesh, n_workers, per_worker, n_chunks = _sc_mesh_and_split(n_rows)

    @functools.partial(
        pl.kernel, mesh=mesh,
        out_type=jax.ShapeDtypeStruct((n_out, width), rows.dtype),
        scratch_types=[pltpu.VMEM((n_chunks, SC_WINDOW), jnp.int32),
                       pltpu.VMEM((2, SC_WINDOW, width), rows.dtype),
                       pltpu.SemaphoreType.DMA((2,)),
                       pltpu.SemaphoreType.DMA((2,))],
    )
    def scatter(rows_hbm, pos_hbm, out_hbm, idx_v, rows_v, sem_in, sem_out):
        wid = lax.axis_index("s") * SC_CORES + lax.axis_index("c")
        base = wid * per_worker
        pltpu.sync_copy(pos_hbm.at[wid], idx_v)

        def fetch(j, b):
            off = pl.multiple_of(base + j * SC_WINDOW, 8)
            return pltpu.make_async_copy(rows_hbm.at[pl.ds(off, SC_WINDOW)], rows_v.at[b], sem_in.at[b])

        def put(j, b):
            return pltpu.make_async_copy(rows_v.at[b], out_hbm.at[idx_v.at[j]], sem_out.at[b])

        _sc_two_buffer_loop(n_chunks, fetch, put)

    return scatter(rows, pos.reshape(n_workers, n_chunks, SC_WINDOW))


def _moe_kernel(ea_ref, eb_ref, nv_ref, hs_ref, wga_ref, wgb_ref, wda_ref, wdb_ref, o_ref):
    i = pl.program_id(0)
    nv = nv_ref[i]

    def experts(n_rows):
        live = lax.broadcasted_iota(jnp.int32, (n_rows, 1), 0) < nv
        xb = _unpack_bf16_pairs(jnp.where(live, hs_ref[0:n_rows, 0:D_PACK], jnp.uint32(0)))
        w_ab = lax.bitcast_convert_type(hs_ref[0:n_rows, D_PACK:D_ROW], F32)
        w_a = jnp.where(live, w_ab[:, 2:3], 0.0)
        w_b = jnp.where(live, w_ab[:, 3:4], 0.0)
        ga = _dot(xb, wga_ref[0])
        act_a = (_silu(ga[:, :D_EXPERT]) * ga[:, D_EXPERT:] * w_a).astype(BF16)
        gb = _dot(xb, wgb_ref[0])
        act_b = (_silu(gb[:, :D_EXPERT]) * gb[:, D_EXPERT:] * w_b).astype(BF16)
        return _pack_bf16_pairs(_dot(act_a, wda_ref[0]) + _dot(act_b, wdb_ref[0]))

    quarter = BM_MOE // 4
    for q in range(1, 5):
        n_rows = q * quarter

        @pl.when(jnp.logical_and(nv > n_rows - quarter, nv <= n_rows))
        def _(n_rows=n_rows):
            o_ref[0:n_rows, :] = experts(n_rows)
            if n_rows < BM_MOE:
                o_ref[n_rows:, :] = jnp.zeros((BM_MOE - n_rows, D_PACK), jnp.uint32)

    @pl.when(nv <= 0)
    def _():
        o_ref[...] = jnp.zeros_like(o_ref)


def _moe(ea, eb, nvalid, hs, w_gu, w_dn):
    nblk = ea.shape[0]
    S = nblk * BM_MOE
    grid_spec = pltpu.PrefetchScalarGridSpec(
        num_scalar_prefetch=3,
        grid=(nblk,),
        in_specs=[
            pl.BlockSpec((BM_MOE, D_ROW), lambda i, ea, eb, nv: (i, 0)),
            pl.BlockSpec((1, D, 2 * D_EXPERT), lambda i, ea, eb, nv: (ea[i], 0, 0)),
            pl.BlockSpec((1, D, 2 * D_EXPERT), lambda i, ea, eb, nv: (eb[i], 0, 0)),
            pl.BlockSpec((1, D_EXPERT, D), lambda i, ea, eb, nv: (ea[i], 0, 0)),
            pl.BlockSpec((1, D_EXPERT, D), lambda i, ea, eb, nv: (eb[i], 0, 0)),
        ],
        out_specs=pl.BlockSpec((BM_MOE, D_PACK), lambda i, ea, eb, nv: (i, 0)),
    )
    return pl.pallas_call(
        _moe_kernel,
        out_shape=jax.ShapeDtypeStruct((S, D_PACK), jnp.uint32),
        grid_spec=grid_spec,
        compiler_params=pltpu.CompilerParams(dimension_semantics=("arbitrary",),
                                             vmem_limit_bytes=VMEM_LIMIT),
    )(ea, eb, nvalid, hs, w_gu, w_gu, w_dn, w_dn)


def _final_kernel(x2_ref, m_ref, mod_ref, fg_ref, *rest):
    o_ref = rest[-1]
    gt2 = mod_ref[0][:, 5 * D:6 * D]
    y = x2_ref[...] + gt2 * _unpack_bf16_pairs(m_ref[...]).astype(F32)
    ms = jnp.mean(y * y, axis=-1, keepdims=True)
    o_ref[...] = y * lax.rsqrt(ms + EPS) * fg_ref[...]


def _final(x2, m_part, mod3, fg, L, part, prev_out):
    T = x2.shape[0]
    steps = m_part.shape[0] // TM_FIN
    off = part * steps
    per_b = L // TM_FIN
    in_specs = [
        pl.BlockSpec((TM_FIN, D), lambda i: (i + off, 0)),
        pl.BlockSpec((TM_FIN, D_PACK), lambda i: (i, 0)),
        pl.BlockSpec((1, 1, 6 * D), lambda i: ((i + off) // per_b, 0, 0)),
        pl.BlockSpec((1, D), lambda i: (0, 0)),
    ]
    args = [x2, m_part, mod3, fg]
    aliases = {}
    if prev_out is not None:
        in_specs.append(pl.BlockSpec(memory_space=pl.ANY))
        args.append(prev_out)
        aliases = {len(args) - 1: 0}
    return pl.pallas_call(
        _final_kernel,
        out_shape=jax.ShapeDtypeStruct((T, D), F32),
        grid=(steps,),
        in_specs=in_specs,
        out_specs=pl.BlockSpec((TM_FIN, D), lambda i: (i + off, 0)),
        input_output_aliases=aliases,
        compiler_params=pltpu.CompilerParams(dimension_semantics=("arbitrary",),
                                             vmem_limit_bytes=VMEM_LIMIT),
    )(*args)


def kernel(x, c, ctx, c_ctx, w_ada, b_ada, norm1_g, w_in, ln_a_g, ln_a_b, w_spatial, b_spatial, conv_qkv, a_log,
           dt_bias, onorm_g, w_out, norm2_g, w_group, b_group, w_router, b_router, w_gate_up, w_down, final_g):
    B, L, _ = x.shape
    T = B * L
    assert w_ada.shape[0] == 1 and ctx.shape[1] == TM_IN and L % TM_OUT == 0 and T % (N_COMBINE_PARTS * TM_FIN) == 0

    cond = jnp.concatenate([c, c_ctx[None, :], jnp.zeros((7, D), F32)], axis=0)
    mod = _modulation(cond, w_ada[0], b_ada[0][None, :])
    mod_lat = mod[:B].reshape(B, 1, 6 * D)
    mod_ctx = mod[B:B + 1]

    alog = a_log[0].reshape(1, 2 * HEADS)
    dtb = dt_bias[0].reshape(1, 2 * HEADS)
    alog_row = jnp.pad(alog, ((0, 0), (0, 128 - 2 * HEADS)))
    dtb_row = jnp.pad(dtb, ((0, 0), (0, 128 - 2 * HEADS)))
    alog_col = jnp.pad(alog, ((0, 0), (0, N_AB - 2 * HEADS))).T
    dtb_col = jnp.pad(dtb, ((0, 0), (0, N_AB - 2 * HEADS))).T

    ya, qkv, z, gb, gbt = _in_proj(
        x, ctx, mod_lat, mod_ctx, norm1_g, jnp.swapaxes(w_in, 1, 2), ln_a_g, ln_a_b,
        w_spatial[0].astype(BF16), b_spatial[0].T, conv_qkv[0], alog_row, dtb_row, alog_col, dtb_col)

    yb, w_gu_bf, w_dn_bf = _delta(qkv, z, gb, gbt, onorm_g, w_gate_up[0], w_down[0], L)

    wrt = jnp.concatenate([w_group[0].T, w_router[0].T, jnp.zeros((32 - N_GROUPS - N_EXPERTS, D), F32)], axis=0)
    brt = jnp.concatenate([b_group[0], b_router[0], jnp.zeros((32 - N_GROUPS - N_EXPERTS,), F32)])[:, None]
    x2, h, route, cnt = _out_proj(x.reshape(T, D), ya.reshape(T, D_A), yb.reshape(T, D_B), mod_lat,
                                  w_out[0], norm2_g, wrt.astype(BF16), brt, L)

    bucket = route[:, 0, :].reshape(T).astype(jnp.int32)
    rank = route[:, 1, :].reshape(T).astype(jnp.int32)
    counts = cnt[:N_BUCKETS, 0].astype(jnp.int32)
    nblk_b = (counts + BM_MOE - 1) // BM_MOE
    blk_end = jnp.cumsum(nblk_b)
    blk_start = blk_end - nblk_b
    kk = jnp.arange(N_BUCKETS, dtype=jnp.int32)

    def pick(idx, table):
        return jnp.sum(jnp.where(idx[:, None] == kk[None, :], table[None, :], 0), axis=1)

    pos = pick(bucket, blk_start) * BM_MOE + rank
    n_blocks = T // BM_MOE + N_BUCKETS
    S = n_blocks * BM_MOE
    blk = jnp.arange(n_blocks, dtype=jnp.int32)
    used = blk < blk_end[-1]
    bkt = jnp.sum((jnp.minimum(blk, blk_end[-1] - 1)[:, None] >= blk_end[None, :]).astype(jnp.int32), axis=1)
    nvalid = jnp.where(used, jnp.clip(pick(bkt, counts) - (blk - pick(bkt, blk_start)) * BM_MOE, 0, BM_MOE),
                       0).astype(jnp.int32)
    ea = pick(bkt, jnp.asarray(_SLOT_A_EXPERT))
    eb = pick(bkt, jnp.asarray(_SLOT_B_EXPERT))

    hs = _sc_row_scatter(h, pos, S)
    ms = _moe(ea, eb, nvalid, hs, w_gu_bf, w_dn_bf)
    pos_parts = pos.reshape(N_COMBINE_PARTS, T // N_COMBINE_PARTS)
    m_parts = [_sc_row_gather(ms, pos_parts[q]) for q in range(N_COMBINE_PARTS)]
    out = None
    for q in range(N_COMBINE_PARTS):
        out = _final(x2, m_parts[q], mod_lat, final_g[None, :], L, q, out)
    return out.reshape(B, L, D)
```

```python
import functools

import jax
import jax.numpy as jnp
import numpy as np
from jax import lax
from jax.experimental import pallas as pl
from jax.experimental.pallas import tpu as pltpu
from jax.experimental.pallas import tpu_sc as plsc

F32 = jnp.float32
BF16 = jnp.bfloat16
EPS = 1e-6

D = 1024
D_A = 512
D_B = 512
HEADS = 4
HD = 128
CHUNK = 128
CONV_W = 5
N_QKV = 3 * D_B
N_MAIN = 2 * D_A + 4 * D_B
N_AB = 16
N_GROUPS = 4
EPG = 4
N_EXPERTS = 16
D_EXPERT = 512
N_PAIRS = 6
N_BUCKETS = N_GROUPS * N_PAIRS
D_PACK = D // 2
D_ROW = D_PACK + 128
PAIR_A = (0, 0, 0, 1, 1, 3)
PAIR_B = (1, 2, 3, 3, 2, 2)
_SLOT_A_EXPERT = np.array([g * EPG + PAIR_A[p] for g in range(N_GROUPS) for p in range(N_PAIRS)], np.int32)
_SLOT_B_EXPERT = np.array([g * EPG + PAIR_B[p] for g in range(N_GROUPS) for p in range(N_PAIRS)], np.int32)

TM_IN = 256
HALO = 8
NB_IN = 2
NB_DELTA = 2
TM_OUT = 1024
OUT_SPLIT = 4
BM_MOE = 512
TM_FIN = 2048
N_COMBINE_PARTS = 4
VMEM_LIMIT = 56 * 1024 * 1024
SC_CORES = 2
SC_SUBCORES = 16
SC_WINDOW = 64

HI = lax.Precision.HIGHEST

_CHUNK_TRIL = np.kron(np.eye(TM_IN // CHUNK, dtype=np.float32), np.tril(np.ones((CHUNK, CHUNK), np.float32)))


def _dot(a, b, precision=None):
    return jnp.dot(a, b, preferred_element_type=F32, precision=precision)


def _dot_nt(a, b):
    return lax.dot_general(a, b, (((1,), (1,)), ((), ())), preferred_element_type=F32)


def _dot_tn(a, b):
    return lax.dot_general(a, b, (((0,), (0,)), ((), ())), preferred_element_type=F32)


def _sigmoid(x):
    return 0.5 + 0.5 * jnp.tanh(0.5 * x)


def _silu(x):
    h = 0.5 * x
    return h + h * jnp.tanh(h)


def _softplus(x):
    return jnp.maximum(x, 0.0) + jnp.log(1.0 + jnp.exp(-jnp.abs(x)))


def _pack_bf16_pairs(x):
    bits = lax.bitcast_convert_type(x.astype(BF16).astype(F32), jnp.uint32)
    return (bits[:, D_PACK:] & jnp.uint32(0xFFFF0000)) | (bits[:, :D_PACK] >> 16)


def _unpack_bf16_pairs(w):
    bits = w
    lo = lax.bitcast_convert_type(bits << 16, F32)
    hi = lax.bitcast_convert_type(bits & jnp.uint32(0xFFFF0000), F32)
    return jnp.concatenate([lo, hi], axis=1).astype(BF16)


def _gelu_tanh(x):
    return 0.5 * x * (1.0 + jnp.tanh(np.sqrt(2.0 / np.pi).astype(np.float32) * (x + 0.044715 * (x * x * x))))


def _mod_kernel(c_ref, w_ref, b_ref, o_ref):
    c = c_ref[...]
    o_ref[...] = _dot(_silu(c), w_ref[...], precision=HI) + b_ref[...]


def _modulation(cond, w_ada, b_ada):
    rows = cond.shape[0]
    tn = 1536
    return pl.pallas_call(
        _mod_kernel,
        out_shape=jax.ShapeDtypeStruct((rows, 6 * D), F32),
        grid=(6 * D // tn,),
        in_specs=[pl.BlockSpec((rows, D), lambda i: (0, 0)),
                  pl.BlockSpec((D, tn), lambda i: (0, i)),
                  pl.BlockSpec((1, tn), lambda i: (0, i))],
        out_specs=pl.BlockSpec((rows, tn), lambda i: (0, i)),
        compiler_params=pltpu.CompilerParams(dimension_semantics=("arbitrary",),
                                             vmem_limit_bytes=VMEM_LIMIT),
    )(cond, w_ada, b_ada)


def _in_kernel(x_ref, xp_ref, xn_ref, ctx_ref, mod_ref, cmod_ref, g1_ref, wt_ref,
               lng_ref, lnb_ref, ws_ref, bst_ref, conv_ref, alog_ref, dtb_ref, alogt_ref, dtbt_ref, tril_ref, triu_ref,
               ya_ref, qkv_ref, z_ref, gb_ref, gbt_ref, wbf_ref, wab_ref, wabt_ref):
    j = pl.program_id(1)

    @pl.when(jnp.logical_and(pl.program_id(0) == 0, j == 0))
    def _():
        for c0 in range(0, N_MAIN, 512):
            wbf_ref[:, c0:c0 + 512] = jnp.transpose(wt_ref[0, c0:c0 + 512, :]).astype(BF16)
        tail = jnp.concatenate([wt_ref[0, N_MAIN:N_MAIN + N_AB, :], jnp.zeros((128 - N_AB, D), F32)], axis=0)
        wabt_ref[...] = tail.astype(BF16)
        wab_ref[...] = jnp.transpose(tail).astype(BF16)

    is_ctx = j == 0
    n_lat_blocks = pl.num_programs(1) - 1

    def one_batch_element(bb):
        mod = mod_ref[bb]
        cm = cmod_ref[...]
        sh = jnp.where(is_ctx, cm[:, 0:D], mod[:, 0:D])
        sc = jnp.where(is_ctx, cm[:, D:2 * D], mod[:, D:2 * D])
        scale = g1_ref[...] * (1.0 + sc)

        xmain = jnp.where(is_ctx, ctx_ref[bb], x_ref[bb])
        xv = jnp.concatenate([xp_ref[bb], xmain, xn_ref[bb]], axis=0)
        xnorm = xv * lax.rsqrt(jnp.mean(xv * xv, axis=-1, keepdims=True) + EPS) * scale + sh
        xe = xnorm.astype(BF16)
        xb = xnorm[HALO:HALO + TM_IN].astype(BF16)

        rid = lax.broadcasted_iota(jnp.int32, (TM_IN + 2 * HALO, 1), 0)
        prev_ok = j >= 2
        next_ok = jnp.logical_and(j >= 1, j < n_lat_blocks)
        valid = jnp.logical_or(jnp.logical_and(rid >= HALO, rid < HALO + TM_IN),
                               jnp.logical_or(jnp.logical_and(rid < HALO, prev_ok),
                                              jnp.logical_and(rid >= HALO + TM_IN, next_ok)))
        pad = (CONV_W - 1) // 2
        c_qkv = 2 * D_A

        def proj(c0, width, halo=True):
            return _dot(xe if halo else xb, wbf_ref[:, c0:c0 + width])

        def conv_act(pq, c0):
            groups = (TM_IN + 2 * HALO) // 8
            x3 = jnp.where(valid, pq, 0.0).reshape(groups, 8, D_B)
            sub = lax.broadcasted_iota(jnp.int32, (1, 8, 1), 1)
            lo, hi = HALO // 8, HALO // 8 + TM_IN // 8
            acc = conv_ref[pad:pad + 1, c0:c0 + D_B] * x3[lo:hi]
            for t in range(CONV_W):
                s = t - pad
                if s == 0:
                    continue
                r = pltpu.roll(x3, (-s) % 8, axis=1)
                if s > 0:
                    sh = jnp.where(sub < 8 - s, r[lo:hi], r[lo + 1:hi + 1])
                else:
                    sh = jnp.where(sub >= -s, r[lo:hi], r[lo - 1:hi - 1])
                acc = acc + conv_ref[t:t + 1, c0:c0 + D_B] * sh
            return _silu(acc.reshape(TM_IN, D_B))

        def store_unit_heads(act, c0, gain):
            for h in range(HEADS):
                t = act[:, h * HD:(h + 1) * HD]
                nrm = lax.rsqrt(jnp.sum(t * t, axis=-1, keepdims=True) + EPS) * gain
                qkv_ref[bb, :, c0 + h * HD:c0 + (h + 1) * HD] = (t * nrm).astype(BF16)

        pq_q = proj(c_qkv, D_B)
        pq_k = proj(c_qkv + D_B, D_B)
        store_unit_heads(conv_act(pq_q, 0), 0, HD ** -0.5)
        pq_v = proj(c_qkv + 2 * D_B, D_B)
        store_unit_heads(conv_act(pq_k, D_B), D_B, 1.0)
        pa_u = proj(0, D_A, halo=False)
        qkv_ref[bb, :, 2 * D_B:] = conv_act(pq_v, 2 * D_B).astype(BF16)
        pa_v = proj(D_A, D_A, halo=False)
        u = _gelu_tanh(pa_u)
        pz = proj(c_qkv + N_QKV, D_B, halo=False)
        v = _gelu_tanh(pa_v)
        mu = jnp.mean(v, axis=-1, keepdims=True)
        vc = v - mu
        var = jnp.mean(vc * vc, axis=-1, keepdims=True)
        vn = (vc * lax.rsqrt(var + EPS) * lng_ref[...] + lnb_ref[...]).astype(BF16)
        z_ref[bb] = pz.astype(BF16)

        bst = bst_ref[...]
        for n in range(TM_IN // CHUNK):
            rows = slice(n * CHUNK, (n + 1) * CHUNK)
            for h in range(HEADS):
                cols = slice(h * HD, (h + 1) * HD)
                s = _dot(ws_ref[h], vn[rows, cols]) + bst[:, h:h + 1]
                ya_ref[bb, rows, cols] = (u[rows, cols] * s).astype(BF16)

        tri_l = tril_ref[...]
        tri_u = triu_ref[...]

        def split3(g):
            hi = g.astype(BF16)
            r1 = g - hi.astype(F32)
            mid = r1.astype(BF16)
            return hi, mid, (r1 - mid.astype(F32)).astype(BF16)

        ab = _dot(xb, wab_ref[...])
        g3 = split3(-jnp.exp(alog_ref[...]) * _softplus(ab + dtb_ref[...]))
        lane = lax.broadcasted_iota(jnp.int32, ab.shape, 1)
        gb = jnp.where(lane < HEADS, _dot(tri_l, g3[0]) + _dot(tri_l, g3[1]) + _dot(tri_l, g3[2]),
                       jnp.where(lane < 2 * HEADS, _dot(tri_u, g3[0]) + _dot(tri_u, g3[1]) + _dot(tri_u, g3[2]),
                                 _sigmoid(ab)))
        gb_ref[bb] = gb[:, 0:N_AB]

        abt = _dot_nt(wabt_ref[0:N_AB, :], xb)
        t3 = split3(-jnp.exp(alogt_ref[...]) * _softplus(abt + dtbt_ref[...]))
        row = lax.broadcasted_iota(jnp.int32, abt.shape, 0)
        gbt_ref[bb] = jnp.where(row < HEADS, _dot(t3[0], tri_u) + _dot(t3[1], tri_u) + _dot(t3[2], tri_u),
                               jnp.where(row < 2 * HEADS, _dot(t3[0], tri_l) + _dot(t3[1], tri_l) + _dot(t3[2], tri_l),
                                         _sigmoid(abt)))

    for bb in range(x_ref.shape[0]):
        one_batch_element(bb)


def _in_proj(x, ctx, mod_lat, mod_ctx, g1, w_in_t, lng, lnb, ws, bst, conv, alog, dtb, alogt, dtbt):
    B, L, _ = x.shape
    n_lat = L // TM_IN
    n_steps = n_lat + 1
    LC = L + TM_IN
    hb = TM_IN // HALO

    def full(shape):
        return pl.BlockSpec(shape, lambda b, j: (0,) * len(shape))

    in_specs = [
        pl.BlockSpec((NB_IN, TM_IN, D), lambda b, j: (b, jnp.maximum(j - 1, 0), 0)),
        pl.BlockSpec((NB_IN, HALO, D), lambda b, j: (b, jnp.clip((j - 1) * hb - 1, 0, L // HALO - 1), 0)),
        pl.BlockSpec((NB_IN, HALO, D), lambda b, j: (b, jnp.clip(j * hb, 0, L // HALO - 1), 0)),
        pl.BlockSpec((NB_IN, TM_IN, D), lambda b, j: (b, 0, 0)),
        pl.BlockSpec((NB_IN, 1, 6 * D), lambda b, j: (b, 0, 0)),
        full((1, 6 * D)), full((1, D)),
        pl.BlockSpec((1, N_MAIN + N_AB, D), lambda b, j: (0, 0, 0), pipeline_mode=pl.Buffered(1)),
        full((1, D_A)), full((1, D_A)), full((HEADS, CHUNK, CHUNK)), full((CHUNK, HEADS)),
        full((CONV_W, N_QKV)), full((1, 128)), full((1, 128)), full((N_AB, 1)), full((N_AB, 1)),
        full((TM_IN, TM_IN)), full((TM_IN, TM_IN)),
    ]
    out_shape = [
        jax.ShapeDtypeStruct((B, L, D_A), BF16),
        jax.ShapeDtypeStruct((B, LC, N_QKV), BF16),
        jax.ShapeDtypeStruct((B, LC, D_B), BF16),
        jax.ShapeDtypeStruct((B, LC, N_AB), F32),
        jax.ShapeDtypeStruct((B, N_AB, LC), F32),
    ]
    out_specs = [
        pl.BlockSpec((NB_IN, TM_IN, D_A), lambda b, j: (b, jnp.maximum(j - 1, 0), 0)),
        pl.BlockSpec((NB_IN, TM_IN, N_QKV), lambda b, j: (b, j, 0)),
        pl.BlockSpec((NB_IN, TM_IN, D_B), lambda b, j: (b, j, 0)),
        pl.BlockSpec((NB_IN, TM_IN, N_AB), lambda b, j: (b, j, 0)),
        pl.BlockSpec((NB_IN, N_AB, TM_IN), lambda b, j: (b, 0, j)),
    ]
    return pl.pallas_call(
        _in_kernel,
        out_shape=out_shape,
        grid=(B // NB_IN, n_steps),
        in_specs=in_specs,
        out_specs=out_specs,
        scratch_shapes=[pltpu.VMEM((D, N_MAIN), BF16), pltpu.VMEM((D, 128), BF16), pltpu.VMEM((128, D), BF16)],
        compiler_params=pltpu.CompilerParams(dimension_semantics=("arbitrary", "arbitrary"),
                                             vmem_limit_bytes=VMEM_LIMIT),
    )(x, x, x, ctx, mod_lat, mod_ctx, g1, w_in_t, lng, lnb, ws, bst, conv, alog, dtb, alogt, dtbt,
      jnp.asarray(_CHUNK_TRIL, BF16), jnp.asarray(_CHUNK_TRIL.T, BF16))


def _delta_kernel(qf_ref, qb_ref, zf_ref, zb_ref, gf_ref, gbk_ref, gtf_ref, gtb_ref, on_ref, wgu_ref, wdn_ref,
                  y_ref, wgu_bf_ref, wdn_bf_ref, s_ref, oacc_ref, *, n_ctx, n_lat):
    s = pl.program_id(1)

    wgu_bf_ref[...] = wgu_ref[...].astype(BF16)
    wdn_bf_ref[...] = wdn_ref[...].astype(BF16)

    @pl.when(s == 0)
    def _():
        s_ref[...] = jnp.zeros_like(s_ref)
        oacc_ref[...] = jnp.zeros_like(oacc_ref)

    row = lax.broadcasted_iota(jnp.int32, (CHUNK, CHUNK), 0)
    col = lax.broadcasted_iota(jnp.int32, (CHUNK, CHUNK), 1)
    low = row > col
    upp = row < col
    same_blk = (row // 16) == (col // 16)
    eye = jnp.where(row == col, 1.0, 0.0).astype(BF16)
    zero = jnp.zeros((CHUNK, CHUNK), BF16)
    onorm = on_ref[...]
    half = n_ctx + n_lat // 2
    second = s >= half
    g_refs = (gf_ref, gbk_ref)
    gt_refs = (gtf_ref, gtb_ref)
    qkv_refs = (qf_ref, qb_ref)
    z_refs = (zf_ref, zb_ref)
    nb = qf_ref.shape[0]
    ps = range(nb * HEADS)

    def halves(xc, unit):
        xb = xc.astype(BF16)
        fill = eye if unit else zero
        return jnp.where(low, xb, fill), jnp.where(upp, xb, fill)

    def as_lhs(hv):
        return jnp.concatenate(hv, axis=1)

    def as_rhs(*hvs):
        cols_ = [jnp.concatenate(hv, axis=0) for hv in hvs]
        return cols_[0] if len(cols_) == 1 else jnp.concatenate(cols_, axis=1)

    def load(d, p, part):
        bb, h = divmod(p, HEADS)
        return qkv_refs[d][bb, :, part * D_B + h * HD:part * D_B + (h + 1) * HD]

    def gcol(d, p, base):
        bb, h = divmod(p, HEADS)
        c = base + d * HEADS + h
        return g_refs[d][bb, :, c:c + 1]

    def grow(d, p, base):
        bb, h = divmod(p, HEADS)
        r = base + d * HEADS + h
        return gt_refs[d][bb, r:r + 1, :]

    def lanes(col):
        return jnp.broadcast_to(col, (CHUNK, HD))

    q = [[load(d, p, 0) for p in ps] for d in range(2)]
    k = [[load(d, p, 1) for p in ps] for d in range(2)]
    v = [[load(d, p, 2) for p in ps] for d in range(2)]
    gcl = [[lanes(gcol(d, p, 0)) for p in ps] for d in range(2)]
    betal = [[lanes(gcol(d, p, 2 * HEADS)) for p in ps] for d in range(2)]
    gr = [[grow(d, p, 0) for p in ps] for d in range(2)]
    betar = [[grow(d, p, 2 * HEADS) for p in ps] for d in range(2)]
    glast = [[gr[0][p][:, CHUNK - 1:CHUNK] for p in ps], [gr[1][p][:, 0:1] for p in ps]]

    gram = [[_dot_nt(jnp.concatenate([q[d][p], k[d][p]], axis=0), k[d][p]) for p in ps] for d in range(2)]
    dec = [jnp.exp(jnp.where(low, gcl[0][p] - gr[0][p], jnp.where(upp, gcl[1][p] - gr[1][p], 0.0))) for p in ps]
    lc = [jnp.where(low, gram[0][p][CHUNK:] * betar[0][p], jnp.where(upp, gram[1][p][CHUNK:] * betar[1][p], 0.0))
          * dec[p] for p in ps]
    qk = [[jnp.where(upp, 0.0, gram[0][p][:CHUNK] * dec[p]).astype(BF16) for p in ps],
          [jnp.where(low, 0.0, gram[1][p][:CHUNK] * dec[p]).astype(BF16) for p in ps]]

    dg = [jnp.where(same_blk, lc[p], 0.0) for p in ps]
    ob = [lc[p] - dg[p] for p in ps]
    d1h = [halves(dg[p], False) for p in ps]
    d2 = [_dot(as_lhs(d1h[p]), as_rhs(d1h[p])) for p in ps]
    p0s = [-dg[p] for p in ps]
    d2h = [halves(d2[p], False) for p in ps]
    p0h = [halves(p0s[p], True) for p in ps]
    o2 = [_dot(as_lhs(d2h[p]), as_rhs(d2h[p], p0h[p])) for p in ps]
    p1s = [p0s[p] + o2[p][:, CHUNK:] for p in ps]
    d4h = [halves(o2[p][:, :CHUNK], False) for p in ps]
    p1h = [halves(p1s[p], True) for p in ps]
    o3 = [_dot(as_lhs(d4h[p]), as_rhs(d4h[p], p1h[p])) for p in ps]
    p2s = [p1s[p] + o3[p][:, CHUNK:] for p in ps]
    d8h = [halves(o3[p][:, :CHUNK], False) for p in ps]
    p2h = [halves(p2s[p], True) for p in ps]
    p3s = [p2s[p] + _dot(as_lhs(d8h[p]), as_rhs(p2h[p])) for p in ps]
    p3h = [halves(p3s[p], True) for p in ps]
    obh = [halves(ob[p], False) for p in ps]
    n1h = [halves(_dot(as_lhs(p3h[p]), as_rhs(obh[p])), False) for p in ps]
    o6 = [_dot(as_lhs(n1h[p]), as_rhs(n1h[p], p3h[p])) for p in ps]
    r0s = [p3s[p] - o6[p][:, CHUNK:] for p in ps]
    n2h = [halves(o6[p][:, :CHUNK], False) for p in ps]
    r0h = [halves(r0s[p], True) for p in ps]
    o7 = [_dot(as_lhs(n2h[p]), as_rhs(n2h[p], r0h[p])) for p in ps]
    r1s = [r0s[p] + o7[p][:, CHUNK:] for p in ps]
    n4h = [halves(o7[p][:, :CHUNK], False) for p in ps]
    r1h = [halves(r1s[p], True) for p in ps]
    tinv = [halves(r1s[p] + _dot(as_lhs(n4h[p]), as_rhs(r1h[p])), True) for p in ps]

    offs = []
    for d in range(2):
        lat_chunk = (s - n_ctx) if d == 0 else (n_ctx + n_lat - 1 - s)
        off = pl.multiple_of(jnp.clip(lat_chunk, 0, n_lat - 1) * CHUNK, CHUNK)
        sidx = [(p // HEADS * 2 + d) * HEADS + p % HEADS for p in ps]
        egc = [jnp.exp(gcl[d][p]) for p in ps]
        kf = [k[d][p].astype(F32) for p in ps]
        rhs = [jnp.concatenate([v[d][p], (kf[p] * egc[p]).astype(BF16)], axis=1) for p in ps]
        uw = [_dot(tinv[p][d], rhs[p]) for p in ps]
        qd = [q[d][p].astype(F32) * egc[p] for p in ps]
        kd = [(kf[p] * jnp.exp(glast[d][p] - gcl[d][p])).astype(BF16) for p in ps]
        st = [s_ref[sidx[p]] for p in ps]
        a1 = [_dot(jnp.concatenate([uw[p][:, HD:] * betal[d][p], qd[p]], axis=0).astype(BF16), st[p].astype(BF16))
              for p in ps]
        vnew = [(uw[p][:, :HD] * betal[d][p] - a1[p][:CHUNK]).astype(BF16) for p in ps]
        o = [a1[p][CHUNK:] + _dot(qk[d][p], vnew[p]) for p in ps]
        for p in ps:
            s_ref[sidx[p]] = st[p] * jnp.exp(glast[d][p]) + _dot_tn(kd[p], vnew[p])
        for p in ps:
            bb, h = divmod(p, HEADS)
            cols = slice(h * HD, (h + 1) * HD)
            oacc_ref[bb, pl.ds(off, CHUNK), cols] = (
                jnp.where(second, oacc_ref[bb, pl.ds(off, CHUNK), cols], 0.0) + o[p])
        offs.append(off)

    @pl.when(second)
    def _():
        for d in range(2):
            for p in ps:
                bb, h = divmod(p, HEADS)
                cols = slice(h * HD, (h + 1) * HD)
                tot = oacc_ref[bb, pl.ds(offs[d], CHUNK), cols]
                ms = jnp.mean(tot * tot, axis=-1, keepdims=True)
                zz = z_refs[d][bb, :, cols].astype(F32)
                y_ref[bb, pl.ds(offs[d], CHUNK), cols] = (
                    tot * lax.rsqrt(ms + EPS) * onorm * _silu(zz)).astype(BF16)


def _delta(qkv, z, gb, gbt, onorm, w_gu, w_dn, L):
    B, LC, _ = qkv.shape
    n_all = LC // CHUNK
    n_lat = L // CHUNK
    n_ctx = n_all - n_lat

    def cf(s):
        return s

    def cb(s):
        return jnp.where(s < n_ctx, n_ctx - 1 - s, n_all + n_ctx - 1 - s)

    in_specs = [
        pl.BlockSpec((NB_DELTA, CHUNK, N_QKV), lambda b, s: (b, cf(s), 0)),
        pl.BlockSpec((NB_DELTA, CHUNK, N_QKV), lambda b, s: (b, cb(s), 0)),
        pl.BlockSpec((NB_DELTA, CHUNK, D_B), lambda b, s: (b, cf(s), 0)),
        pl.BlockSpec((NB_DELTA, CHUNK, D_B), lambda b, s: (b, cb(s), 0)),
        pl.BlockSpec((NB_DELTA, CHUNK, N_AB), lambda b, s: (b, cf(s), 0)),
        pl.BlockSpec((NB_DELTA, CHUNK, N_AB), lambda b, s: (b, cb(s), 0)),
        pl.BlockSpec((NB_DELTA, N_AB, CHUNK), lambda b, s: (b, 0, cf(s))),
        pl.BlockSpec((NB_DELTA, N_AB, CHUNK), lambda b, s: (b, 0, cb(s))),
        pl.BlockSpec((1, HD), lambda b, s: (0, 0)),
    ]
    n_steps = (B // NB_DELTA) * n_all
    n_exp, gu_rows, gu_cols = w_gu.shape
    _, dn_rows, dn_cols = w_dn.shape
    assert n_steps >= n_exp
    per_exp = 1
    while 2 * per_exp * n_exp <= n_steps and dn_rows % (16 * per_exp) == 0:
        per_exp *= 2
    n_slabs = n_exp * per_exp

    def slab(b, s):
        t = jnp.minimum(b * n_all + s, n_slabs - 1)
        return t // per_exp, t % per_exp, 0

    in_specs += [pl.BlockSpec((1, gu_rows // per_exp, gu_cols), slab),
                 pl.BlockSpec((1, dn_rows // per_exp, dn_cols), slab)]
    return pl.pallas_call(
        functools.partial(_delta_kernel, n_ctx=n_ctx, n_lat=n_lat),
        out_shape=[jax.ShapeDtypeStruct((B, L, D_B), BF16),
                   jax.ShapeDtypeStruct(w_gu.shape, BF16),
                   jax.ShapeDtypeStruct(w_dn.shape, BF16)],
        grid=(B // NB_DELTA, n_all),
        in_specs=in_specs,
        out_specs=[pl.BlockSpec((NB_DELTA, L, D_B), lambda b, s: (b, 0, 0)),
                   pl.BlockSpec((1, gu_rows // per_exp, gu_cols), slab),
                   pl.BlockSpec((1, dn_rows // per_exp, dn_cols), slab)],
        scratch_shapes=[pltpu.VMEM((NB_DELTA * 2 * HEADS, HD, HD), F32), pltpu.VMEM((NB_DELTA, L, D_B), F32)],
        compiler_params=pltpu.CompilerParams(dimension_semantics=("arbitrary", "arbitrary"),
                                             vmem_limit_bytes=VMEM_LIMIT),
    )(qkv, qkv, z, z, gb, gb, gbt, gbt, onorm, w_gu, w_dn)


def _out_kernel(x_ref, ya_ref, yb_ref, mod_ref, wo_ref, g2_ref, wrt_ref, brt_ref, tri_ref,
                x2_ref, h_ref, route_ref, cnt_ref, base_ref, wbf_ref):
    i = pl.program_id(0)

    @pl.when(i == 0)
    def _():
        base_ref[...] = jnp.zeros_like(base_ref)
        wbf_ref[...] = wo_ref[...].astype(BF16)

    mod = mod_ref[0]
    gt1 = mod[:, 2 * D:3 * D]
    sh2 = mod[:, 3 * D:4 * D]
    sc2 = mod[:, 4 * D:5 * D]
    scale2 = g2_ref[...] * (1.0 + sc2)
    sub = TM_OUT // OUT_SPLIT
    hbs = []
    for r in range(OUT_SPLIT):
        rows = slice(r * sub, (r + 1) * sub)
        mix = _dot(ya_ref[rows, :], wbf_ref[0:D_A, :]) + _dot(yb_ref[rows, :], wbf_ref[D_A:, :])
        x2 = x_ref[rows, :] + gt1 * mix
        x2_ref[rows, :] = x2
        ms = jnp.mean(x2 * x2, axis=-1, keepdims=True)
        hv = x2 * lax.rsqrt(ms + EPS) * scale2 + sh2
        hbs.append(hv.astype(BF16))
        h_ref[rows, 0:D_PACK] = _pack_bf16_pairs(hv)
    hb = jnp.concatenate(hbs, axis=0)

    lt = _dot_nt(wrt_ref[...], hb) + brt_ref[...]
    gl = [lt[r:r + 1, :] for r in range(N_GROUPS)]
    gmax = jnp.maximum(jnp.maximum(gl[0], gl[1]), jnp.maximum(gl[2], gl[3]))
    gsel = jnp.where(gl[0] == gmax, 0, jnp.where(gl[1] == gmax, 1, jnp.where(gl[2] == gmax, 2, 3)))
    p_g = 1.0 / (jnp.exp(gl[0] - gmax) + jnp.exp(gl[1] - gmax) + jnp.exp(gl[2] - gmax) + jnp.exp(gl[3] - gmax))
    el = []
    for e in range(EPG):
        r = [lt[N_GROUPS + g * EPG + e:N_GROUPS + g * EPG + e + 1, :] for g in range(N_GROUPS)]
        el.append(jnp.where(gsel == 0, r[0], jnp.where(gsel == 1, r[1], jnp.where(gsel == 2, r[2], r[3]))))
    m1 = jnp.maximum(jnp.maximum(el[0], el[1]), jnp.maximum(el[2], el[3]))
    i1 = jnp.where(el[0] == m1, 0, jnp.where(el[1] == m1, 1, jnp.where(el[2] == m1, 2, 3)))
    neg = jnp.float32(-jnp.inf)
    el2 = [jnp.where(i1 == e, neg, el[e]) for e in range(EPG)]
    m2 = jnp.maximum(jnp.maximum(el2[0], el2[1]), jnp.maximum(el2[2], el2[3]))
    i2 = jnp.where(jnp.logical_and(el2[0] == m2, i1 != 0), 0,
                   jnp.where(jnp.logical_and(el2[1] == m2, i1 != 1), 1,
                             jnp.where(jnp.logical_and(el2[2] == m2, i1 != 2), 2, 3)))
    t = jnp.exp(m2 - m1)
    w1 = p_g / (1.0 + t)
    w2 = p_g * t / (1.0 + t)
    first_low = i1 < i2
    ea = jnp.where(first_low, i1, i2)
    eb = jnp.where(first_low, i2, i1)
    w_a = jnp.where(first_low, w1, w2)
    w_b = jnp.where(first_low, w2, w1)
    pair = jnp.where(ea == 0, eb - 1, jnp.where(ea == 1, jnp.where(eb == 3, 3, 4), 5))
    swap = pair == 5
    w_a, w_b = jnp.where(swap, w_b, w_a), jnp.where(swap, w_a, w_b)
    bucket = gsel * N_PAIRS + pair

    tm = bucket.shape[1]
    rows = lax.broadcasted_iota(jnp.int32, (32, tm), 0)
    onehot = jnp.where(rows == bucket, 1.0, 0.0).astype(F32)
    prefix = _dot(onehot.astype(BF16), tri_ref[...])
    base = base_ref[:, 0:1]
    rank = jnp.sum(onehot * (prefix - 1.0 + base), axis=0, keepdims=True)
    newbase = base + prefix[:, tm - 1:tm]
    base_ref[...] = jnp.broadcast_to(newbase, base_ref.shape)
    cnt_ref[...] = jnp.broadcast_to(newbase, cnt_ref.shape)
    route = jnp.concatenate([bucket.astype(F32), rank, w_a, w_b, jnp.zeros((4, tm), F32)], axis=0)
    route_ref[0] = route
    route_t = jnp.transpose(jnp.concatenate([route, jnp.zeros((120, tm), F32)], axis=0))
    h_ref[:, D_PACK:D_ROW] = lax.bitcast_convert_type(route_t, jnp.uint32)


def _out_proj(x2d, ya, yb, mod3, w_o, g2, wrt, brt, L):
    T = x2d.shape[0]
    nb = T // TM_OUT
    per_b = L // TM_OUT
    out_shape = [
        jax.ShapeDtypeStruct((T, D), F32),
        jax.ShapeDtypeStruct((T, D_ROW), jnp.uint32),
        jax.ShapeDtypeStruct((nb, 8, TM_OUT), F32),
        jax.ShapeDtypeStruct((32, 128), F32),
    ]
    return pl.pallas_call(
        _out_kernel,
        out_shape=out_shape,
        grid=(nb,),
        in_specs=[
            pl.BlockSpec((TM_OUT, D), lambda i: (i, 0)),
            pl.BlockSpec((TM_OUT, D_A), lambda i: (i, 0)),
            pl.BlockSpec((TM_OUT, D_B), lambda i: (i, 0)),
            pl.BlockSpec((1, 1, 6 * D), lambda i: (i // per_b, 0, 0)),
            pl.BlockSpec((D, D), lambda i: (0, 0), pipeline_mode=pl.Buffered(1)),
            pl.BlockSpec((1, D), lambda i: (0, 0)),
            pl.BlockSpec((32, D), lambda i: (0, 0)),
            pl.BlockSpec((32, 1), lambda i: (0, 0)),
            pl.BlockSpec((TM_OUT, TM_OUT), lambda i: (0, 0), pipeline_mode=pl.Buffered(1)),
        ],
        out_specs=[
            pl.BlockSpec((TM_OUT, D), lambda i: (i, 0)),
            pl.BlockSpec((TM_OUT, D_ROW), lambda i: (i, 0)),
            pl.BlockSpec((1, 8, TM_OUT), lambda i: (i, 0, 0)),
            pl.BlockSpec((32, 128), lambda i: (0, 0)),
        ],
        scratch_shapes=[pltpu.VMEM((32, 128), F32), pltpu.VMEM((D, D), BF16)],
        compiler_params=pltpu.CompilerParams(dimension_semantics=("arbitrary",),
                                             vmem_limit_bytes=VMEM_LIMIT),
    )(x2d, ya, yb, mod3, w_o, g2, wrt, brt, jnp.asarray(np.triu(np.ones((TM_OUT, TM_OUT), np.float32)), BF16))


def _sc_mesh_and_split(n_rows):
    n_workers = SC_CORES * SC_SUBCORES
    per_worker = n_rows // n_workers
    n_chunks = per_worker // SC_WINDOW
    assert per_worker * n_workers == n_rows and n_chunks * SC_WINDOW == per_worker and n_chunks % 2 == 0
    mesh = plsc.VectorSubcoreMesh(core_axis_name="c", subcore_axis_name="s")
    return mesh, n_workers, per_worker, n_chunks


def _sc_two_buffer_loop(n_chunks, first, second):
    first(0, 0).start()

    @pl.loop(0, n_chunks, step=2)
    def _(j):
        for b in range(2):
            jj = j + b
            first(jj, b).wait()

            @pl.when(jj + 1 < n_chunks)
            def _():
                @pl.when(jj >= 1)
                def _():
                    second(jj - 1, 1 - b).wait()
                first(jj + 1, 1 - b).start()

            second(jj, b).start()

    second(n_chunks - 2, 0).wait()
    second(n_chunks - 1, 1).wait()


def _sc_row_gather(table, idx):
    n_rows = idx.shape[0]
    width = table.shape[1]
    mesh, n_workers, per_worker, n_chunks = _sc_mesh_and_split(n_rows)

    @functools.partial(
        pl.kernel, mesh=mesh,
        out_type=jax.ShapeDtypeStruct((n_rows, width), table.dtype),
        scratch_types=[pltpu.VMEM((n_chunks, SC_WINDOW), jnp.int32),
                       pltpu.VMEM((2, SC_WINDOW, width), table.dtype),
                       pltpu.SemaphoreType.DMA((2,)),
                       pltpu.SemaphoreType.DMA((2,))],
    )
    def gather(table_hbm, idx_hbm, out_hbm, idx_v, rows_v, sem_in, sem_out):
        wid = lax.axis_index("s") * SC_CORES + lax.axis_index("c")
        base = wid * per_worker
        pltpu.sync_copy(idx_hbm.at[wid], idx_v)

        def fetch(j, b):
            return pltpu.make_async_copy(table_hbm.at[idx_v.at[j]], rows_v.at[b], sem_in.at[b])

        def put(j, b):
            off = pl.multiple_of(base + j * SC_WINDOW, 8)
            return pltpu.make_async_copy(rows_v.at[b], out_hbm.at[pl.ds(off, SC_WINDOW)], sem_out.at[b])

        _sc_two_buffer_loop(n_chunks, fetch, put)

    return gather(table, idx.reshape(n_workers, n_chunks, SC_WINDOW))


def _sc_row_scatter(rows, pos, n_out):
    n_rows, width = rows.shape
    mesh, n_workers, per_worker, n_chunks = _sc_mesh_and_split(n_rows)

    @functools.partial(
        pl.kernel, mesh=mesh,
        out_type=jax.ShapeDtypeStruct((n_out, width), rows.dtype),
        scratch_types=[pltpu.VMEM((n_chunks, SC_WINDOW), jnp.int32),
                       pltpu.VMEM((2, SC_WINDOW, width), rows.dtype),
                       pltpu.SemaphoreType.DMA((2,)),
                       pltpu.SemaphoreType.DMA((2,))],
    )
    def scatter(rows_hbm, pos_hbm, out_hbm, idx_v, rows_v, sem_in, sem_out):
        wid = lax.axis_index("s") * SC_CORES + lax.axis_index("c")
        base = wid * per_worker
        pltpu.sync_copy(pos_hbm.at[wid], idx_v)

        def fetch(j, b):
            off = pl.multiple_of(base + j * SC_WINDOW, 8)
            return pltpu.make_async_copy(rows_hbm.at[pl.ds(off, SC_WINDOW)], rows_v.at[b], sem_in.at[b])

        def put(j, b):
            return pltpu.make_async_copy(rows_v.at[b], out_hbm.at[idx_v.at[j]], sem_out.at[b])

        _sc_two_buffer_loop(n_chunks, fetch, put)

    return scatter(rows, pos.reshape(n_workers, n_chunks, SC_WINDOW))


def _moe_kernel(ea_ref, eb_ref, nv_ref, hs_ref, wga_ref, wgb_ref, wda_ref, wdb_ref, o_ref):
    i = pl.program_id(0)
    nv = nv_ref[i]

    def experts(n_rows):
        live = lax.broadcasted_iota(jnp.int32, (n_rows, 1), 0) < nv
        xb = _unpack_bf16_pairs(jnp.where(live, hs_ref[0:n_rows, 0:D_PACK], jnp.uint32(0)))
        w_ab = lax.bitcast_convert_type(hs_ref[0:n_rows, D_PACK:D_ROW], F32)
        w_a = jnp.where(live, w_ab[:, 2:3], 0.0)
        w_b = jnp.where(live, w_ab[:, 3:4], 0.0)
        ga = _dot(xb, wga_ref[0])
        act_a = (_silu(ga[:, :D_EXPERT]) * ga[:, D_EXPERT:] * w_a).astype(BF16)
        gb = _dot(xb, wgb_ref[0])
        act_b = (_silu(gb[:, :D_EXPERT]) * gb[:, D_EXPERT:] * w_b).astype(BF16)
        return _pack_bf16_pairs(_dot(act_a, wda_ref[0]) + _dot(act_b, wdb_ref[0]))

    quarter = BM_MOE // 4
    for q in range(1, 5):
        n_rows = q * quarter

        @pl.when(jnp.logical_and(nv > n_rows - quarter, nv <= n_rows))
        def _(n_rows=n_rows):
            o_ref[0:n_rows, :] = experts(n_rows)
            if n_rows < BM_MOE:
                o_ref[n_rows:, :] = jnp.zeros((BM_MOE - n_rows, D_PACK), jnp.uint32)

    @pl.when(nv <= 0)
    def _():
        o_ref[...] = jnp.zeros_like(o_ref)


def _moe(ea, eb, nvalid, hs, w_gu, w_dn):
    nblk = ea.shape[0]
    S = nblk * BM_MOE
    grid_spec = pltpu.PrefetchScalarGridSpec(
        num_scalar_prefetch=3,
        grid=(nblk,),
        in_specs=[
            pl.BlockSpec((BM_MOE, D_ROW), lambda i, ea, eb, nv: (i, 0)),
            pl.BlockSpec((1, D, 2 * D_EXPERT), lambda i, ea, eb, nv: (ea[i], 0, 0)),
            pl.BlockSpec((1, D, 2 * D_EXPERT), lambda i, ea, eb, nv: (eb[i], 0, 0)),
            pl.BlockSpec((1, D_EXPERT, D), lambda i, ea, eb, nv: (ea[i], 0, 0)),
            pl.BlockSpec((1, D_EXPERT, D), lambda i, ea, eb, nv: (eb[i], 0, 0)),
        ],
        out_specs=pl.BlockSpec((BM_MOE, D_PACK), lambda i, ea, eb, nv: (i, 0)),
    )
    return pl.pallas_call(
        _moe_kernel,
        out_shape=jax.ShapeDtypeStruct((S, D_PACK), jnp.uint32),
        grid_spec=grid_spec,
        compiler_params=pltpu.CompilerParams(dimension_semantics=("arbitrary",),
                                             vmem_limit_bytes=VMEM_LIMIT),
    )(ea, eb, nvalid, hs, w_gu, w_gu, w_dn, w_dn)


def _final_kernel(x2_ref, m_ref, mod_ref, fg_ref, *rest):
    o_ref = rest[-1]
    gt2 = mod_ref[0][:, 5 * D:6 * D]
    y = x2_ref[...] + gt2 * _unpack_bf16_pairs(m_ref[...]).astype(F32)
    ms = jnp.mean(y * y, axis=-1, keepdims=True)
    o_ref[...] = y * lax.rsqrt(ms + EPS) * fg_ref[...]


def _final(x2, m_part, mod3, fg, L, part, prev_out):
    T = x2.shape[0]
    steps = m_part.shape[0] // TM_FIN
    off = part * steps
    per_b = L // TM_FIN
    in_specs = [
        pl.BlockSpec((TM_FIN, D), lambda i: (i + off, 0)),
        pl.BlockSpec((TM_FIN, D_PACK), lambda i: (i, 0)),
        pl.BlockSpec((1, 1, 6 * D), lambda i: ((i + off) // per_b, 0, 0)),
        pl.BlockSpec((1, D), lambda i: (0, 0)),
    ]
    args = [x2, m_part, mod3, fg]
    aliases = {}
    if prev_out is not None:
        in_specs.append(pl.BlockSpec(memory_space=pl.ANY))
        args.append(prev_out)
        aliases = {len(args) - 1: 0}
    return pl.pallas_call(
        _final_kernel,
        out_shape=jax.ShapeDtypeStruct((T, D), F32),
        grid=(steps,),
        in_specs=in_specs,
        out_specs=pl.BlockSpec((TM_FIN, D), lambda i: (i + off, 0)),
        input_output_aliases=aliases,
        compiler_params=pltpu.CompilerParams(dimension_semantics=("arbitrary",),
                                             vmem_limit_bytes=VMEM_LIMIT),
    )(*args)


def kernel(x, c, ctx, c_ctx, w_ada, b_ada, norm1_g, w_in, ln_a_g, ln_a_b, w_spatial, b_spatial, conv_qkv, a_log,
           dt_bias, onorm_g, w_out, norm2_g, w_group, b_group, w_router, b_router, w_gate_up, w_down, final_g):
    B, L, _ = x.shape
    T = B * L
    assert w_ada.shape[0] == 1 and ctx.shape[1] == TM_IN and L % TM_OUT == 0 and T % (N_COMBINE_PARTS * TM_FIN) == 0

    cond = jnp.concatenate([c, c_ctx[None, :], jnp.zeros((7, D), F32)], axis=0)
    mod = _modulation(cond, w_ada[0], b_ada[0][None, :])
    mod_lat = mod[:B].reshape(B, 1, 6 * D)
    mod_ctx = mod[B:B + 1]

    alog = a_log[0].reshape(1, 2 * HEADS)
    dtb = dt_bias[0].reshape(1, 2 * HEADS)
    alog_row = jnp.pad(alog, ((0, 0), (0, 128 - 2 * HEADS)))
    dtb_row = jnp.pad(dtb, ((0, 0), (0, 128 - 2 * HEADS)))
    alog_col = jnp.pad(alog, ((0, 0), (0, N_AB - 2 * HEADS))).T
    dtb_col = jnp.pad(dtb, ((0, 0), (0, N_AB - 2 * HEADS))).T

    ya, qkv, z, gb, gbt = _in_proj(
        x, ctx, mod_lat, mod_ctx, norm1_g, jnp.swapaxes(w_in, 1, 2), ln_a_g, ln_a_b,
        w_spatial[0].astype(BF16), b_spatial[0].T, conv_qkv[0], alog_row, dtb_row, alog_col, dtb_col)

    yb, w_gu_bf, w_dn_bf = _delta(qkv, z, gb, gbt, onorm_g, w_gate_up[0], w_down[0], L)

    wrt = jnp.concatenate([w_group[0].T, w_router[0].T, jnp.zeros((32 - N_GROUPS - N_EXPERTS, D), F32)], axis=0)
    brt = jnp.concatenate([b_group[0], b_router[0], jnp.zeros((32 - N_GROUPS - N_EXPERTS,), F32)])[:, None]
    x2, h, route, cnt = _out_proj(x.reshape(T, D), ya.reshape(T, D_A), yb.reshape(T, D_B), mod_lat,
                                  w_out[0], norm2_g, wrt.astype(BF16), brt, L)

    bucket = route[:, 0, :].reshape(T).astype(jnp.int32)
    rank = route[:, 1, :].reshape(T).astype(jnp.int32)
    counts = cnt[:N_BUCKETS, 0].astype(jnp.int32)
    nblk_b = (counts + BM_MOE - 1) // BM_MOE
    blk_end = jnp.cumsum(nblk_b)
    blk_start = blk_end - nblk_b
    kk = jnp.arange(N_BUCKETS, dtype=jnp.int32)

    def pick(idx, table):
        return jnp.sum(jnp.where(idx[:, None] == kk[None, :], table[None, :], 0), axis=1)

    pos = pick(bucket, blk_start) * BM_MOE + rank
    n_blocks = T // BM_MOE + N_BUCKETS
    S = n_blocks * BM_MOE
    blk = jnp.arange(n_blocks, dtype=jnp.int32)
    used = blk < blk_end[-1]
    bkt = jnp.sum((jnp.minimum(blk, blk_end[-1] - 1)[:, None] >= blk_end[None, :]).astype(jnp.int32), axis=1)
    nvalid = jnp.where(used, jnp.clip(pick(bkt, counts) - (blk - pick(bkt, blk_start)) * BM_MOE, 0, BM_MOE),
                       0).astype(jnp.int32)
    ea = pick(bkt, jnp.asarray(_SLOT_A_EXPERT))
    eb = pick(bkt, jnp.asarray(_SLOT_B_EXPERT))

    hs = _sc_row_scatter(h, pos, S)
    ms = _moe(ea, eb, nvalid, hs, w_gu_bf, w_dn_bf)
    pos_parts = pos.reshape(N_COMBINE_PARTS, T // N_COMBINE_PARTS)
    m_parts = [_sc_row_gather(ms, pos_parts[q]) for q in range(N_COMBINE_PARTS)]
    out = None
    for q in range(N_COMBINE_PARTS):
        out = _final(x2, m_parts[q], mod_lat, final_g[None, :], L, q, out)
    return out.reshape(B, L, D)
```
